```python
import math
import jax
import jax.numpy as jnp
from jax import lax
import numpy as np

D_MODEL = 1024
BATCH = 4
SEQ = 4096
DEPTH = 4

GRID_W = 64
CTX_LEN = 256
N_EVEN = (DEPTH + 1) // 2
N_ODD = DEPTH // 2
MIX_WIDTH = D_MODEL
HGRN_WIDTH = MIX_WIDTH // 2
HGRN_HEAD_DIM = 128
HGRN_HEADS = HGRN_WIDTH // HGRN_HEAD_DIM
CHUNK = 64
DIFF_WIDTH = MIX_WIDTH - HGRN_WIDTH
DIFF_HEAD_DIM = 64
DIFF_HEADS = DIFF_WIDTH // (2 * DIFF_HEAD_DIM)
Q_BLOCK = 128
ROPE_THETA = 10000.0
EVEN_IN = 5 * HGRN_WIDTH + 3 * DIFF_WIDTH
CONV_WIDTH = 3
N_EXPERTS = 16
EXPERT_FF = 1024
EC_CAPACITY = 2
MOD_CHUNKS = 6
EPS = 1e-6

kernel_name = "hybrid_hgrn2_diffattn_shortconv_ecmoe_dit"


def rmsnorm(x, g):
    xf = x.astype(jnp.float32)
    y = xf * lax.rsqrt(jnp.mean(xf * xf, axis=-1, keepdims=True) + EPS)
    return y.astype(x.dtype) * g


def modulate(h, shift, scale):
    return h * (1 + scale) + shift


def lambda_init(layer):
    return 0.8 - 0.6 * math.exp(-0.3 * layer)


def _heads(a, n_heads):
    B, T, W = a.shape
    return a.reshape(B, T, n_heads, W // n_heads).transpose(0, 2, 1, 3)


def _rope_axis(x, pos):
    n = x.shape[-1]
    inv = 1.0 / (ROPE_THETA ** (jnp.arange(0, n, 2, dtype=jnp.float32) / n))
    ang = pos.astype(jnp.float32)[:, None] * inv[None, :]
    cos, sin = jnp.cos(ang).astype(x.dtype), jnp.sin(ang).astype(x.dtype)
    x1, x2 = x[..., : n // 2], x[..., n // 2:]
    return jnp.concatenate([x1 * cos - x2 * sin, x2 * cos + x1 * sin], axis=-1)


def rope_2d(x):
    T = x.shape[-2]
    n_rows = T // GRID_W
    rows = jnp.repeat(jnp.arange(n_rows), GRID_W)
    cols = jnp.tile(jnp.arange(GRID_W), n_rows)
    half = x.shape[-1] // 2
    return jnp.concatenate([_rope_axis(x[..., :half], rows), _rope_axis(x[..., half:], cols)], axis=-1)


def gla_chunk_scan(q, k, v, log_f, s0, with_output=True):
    B, H, T, dk = q.shape
    dv = v.shape[-1]
    n_chunks = T // CHUNK

    def to_chunks(a):
        return jnp.moveaxis(a.reshape(B, H, n_chunks, CHUNK, a.shape[-1]), 2, 0)

    lower = jnp.tril(jnp.ones((CHUNK, CHUNK), dtype=bool))[:, :, None]

    def step(s, inp):
        qi, ki, vi, gi = inp
        b = jnp.cumsum(gi, axis=2)
        b_last = b[:, :, -1:, :]
        s_new = (jnp.exp(b_last)[:, :, 0, :, None] * s
                 + jnp.einsum('bhsd,bhse->bhde', ki * jnp.exp(b_last - b), vi))
        if not with_output:
            return s_new, None
        rel = b[:, :, :, None, :] - b[:, :, None, :, :]
        decay = jnp.where(lower, jnp.exp(jnp.where(lower, rel, 0.0)), 0.0)
        scores = jnp.einsum('bhtd,bhtsd,bhsd->bhts', qi, decay, ki)
        o = (jnp.einsum('bhld,bhde->bhle', qi * jnp.exp(b), s)
             + jnp.einsum('bhts,bhse->bhte', scores, vi))
        return s_new, o

    s_final, oc = lax.scan(step, s0, tuple(to_chunks(a) for a in (q, k, v, log_f)))
    if not with_output:
        return None, s_final
    return jnp.moveaxis(oc, 0, 2).reshape(B, H, T, dv), s_final


def hgrn2_direction(q, z, v, lb, qc, zc, vc, reverse, ctx_out):
    def prep(qq, zz, vv):
        zf = zz.astype(jnp.float32)
        log_f = jnp.log(lb + (1 - lb) * jax.nn.sigmoid(zf))
        key = (1 - lb) * jax.nn.sigmoid(-zf)
        arrs = [_heads(a, HGRN_HEADS) for a in (qq.astype(jnp.float32), key, vv.astype(jnp.float32), log_f)]
        return [jnp.flip(a, axis=2) for a in arrs] if reverse else arrs

    B = q.shape[0]
    s0 = jnp.zeros((B, HGRN_HEADS, HGRN_HEAD_DIM, HGRN_HEAD_DIM), jnp.float32)
    oc, s_ctx = gla_chunk_scan(*prep(qc, zc, vc), s0, with_output=ctx_out)
    o, _ = gla_chunk_scan(*prep(q, z, v), s_ctx)
    if reverse:
        o = jnp.flip(o, axis=2)
        oc = jnp.flip(oc, axis=2) if ctx_out else None
    return o, oc


def hgrn2_readout(o, gate, g):
    B, H, T, dv = o.shape
    o = rmsnorm(o, g[:, None, :]).transpose(0, 2, 1, 3).reshape(B, T, H * dv)
    return o.astype(gate.dtype) * jax.nn.silu(gate)


def qk_heads(a, g, rotary):
    B, T, _ = a.shape
    a = rmsnorm(a.reshape(B, T, DIFF_HEADS, 2, DIFF_HEAD_DIM).transpose(0, 2, 3, 1, 4), g)
    return rope_2d(a) if rotary else a


def diff_attend(q, k, v, lam):
    s = jnp.einsum('bhmqd,bhmkd->bhmqk', q, k).astype(jnp.float32)
    p = jax.nn.softmax(s, axis=-1)
    w = p[:, :, 0] - lam * p[:, :, 1]
    return jnp.einsum('bhqk,bhke->bhqe', w.astype(v.dtype), v)


def blocked_diff_attention(q, k_all, v_all, lam):
    B, H, _, T, dh = q.shape
    nb = T // Q_BLOCK
    qb = jnp.moveaxis(q.reshape(B, H, 2, nb, Q_BLOCK, dh), 3, 0)
    ob = lax.map(lambda qi: diff_attend(qi, k_all, v_all, lam), qb)
    return jnp.moveaxis(ob, 0, 2).reshape(B, H, T, 2 * DIFF_HEAD_DIM)


def diff_readout(o, g, lam_init):
    B, H, T, e = o.shape
    o = rmsnorm(o, g) * (1 - lam_init)
    return o.transpose(0, 2, 1, 3).reshape(B, T, H * e)


def even_mixer(h, hc, w_in, w_out, lb, hgrn_g, qn_g, kn_g, lam_vec, subln_g, lam_init, ctx_out):
    cuts = [HGRN_WIDTH * i for i in range(1, 6)] + [5 * HGRN_WIDTH + DIFF_WIDTH * i for i in range(1, 3)]
    hq, hzf, hzb, hv, hgate, dq, dk, dv = jnp.split(h @ w_in, cuts, axis=-1)
    cq, czf, czb, cv, cgate, cdq, cdk, cdv = jnp.split(hc @ w_in, cuts, axis=-1)
    o_f, oc_f = hgrn2_direction(hq, hzf, hv, lb[0], cq, czf, cv, False, ctx_out)
    o_b, oc_b = hgrn2_direction(hq, hzb, hv, lb[1], cq, czb, cv, True, ctx_out)
    y_h = hgrn2_readout(o_f + o_b, hgate, hgrn_g)
    lv = lam_vec.astype(jnp.float32)
    lam = jnp.exp(jnp.sum(lv[0] * lv[1])) - jnp.exp(jnp.sum(lv[2] * lv[3])) + lam_init
    scale = DIFF_HEAD_DIM ** -0.5
    q = qk_heads(dq, qn_g, True) * scale
    k = qk_heads(dk, kn_g, True)
    kc = qk_heads(cdk, kn_g, False)
    v, vc = _heads(dv, DIFF_HEADS), _heads(cdv, DIFF_HEADS)
    k_all = jnp.concatenate([k, kc], axis=3)
    v_all = jnp.concatenate([v, vc], axis=2)
    y_d = diff_readout(blocked_diff_attention(q, k_all, v_all, lam), subln_g, lam_init)
    y = jnp.concatenate([y_h, y_d], axis=-1) @ w_out
    if not ctx_out:
        return y, None
    yc_h = hgrn2_readout(oc_f + oc_b, cgate, hgrn_g)
    qc = qk_heads(cdq, qn_g, False) * scale
    yc_d = diff_readout(diff_attend(qc, kc, vc, lam), subln_g, lam_init)
    yc = jnp.concatenate([yc_h, yc_d], axis=-1) @ w_out
    return y, yc


def short_conv_mixer(h, w_in, conv_w, w_out):
    b_gate, c_gate, v = jnp.split(h @ w_in, 3, axis=-1)
    u = c_gate * v
    y = lax.conv_general_dilated(u, conv_w[:, None, :].astype(u.dtype), window_strides=(1,),
                                 padding=((CONV_WIDTH // 2, CONV_WIDTH // 2),),
                                 dimension_numbers=('NWC', 'WIO', 'NWC'),
                                 feature_group_count=u.shape[-1])
    return (b_gate * y) @ w_out


def expert_choice_ffn(h, router_w, w_gate, w_up, w_down):
    B, N, D = h.shape
    cap = EC_CAPACITY * N // N_EXPERTS
    affinity = jax.nn.softmax((h @ router_w).astype(jnp.float32), axis=-1)
    gate, idx = lax.top_k(jnp.swapaxes(affinity, 1, 2), cap)
    idx_flat = idx.reshape(B, N_EXPERTS * cap)
    xe = jax.vmap(lambda hb, ib: hb[ib])(h, idx_flat).reshape(B, N_EXPERTS, cap, D)
    a = jnp.einsum('becd,edf->becf', xe, w_gate)
    u = jnp.einsum('becd,edf->becf', xe, w_up)
    ye = jnp.einsum('becf,efd->becd', jax.nn.silu(a) * u, w_down) * gate[..., None].astype(h.dtype)
    return jax.vmap(lambda yb, ib: jnp.zeros((N, D), h.dtype).at[ib].add(yb))(
        ye.reshape(B, N_EXPERTS * cap, D), idx_flat)


def setup_inputs(seed: int = 0) -> dict:
    key = jax.random.key(seed)
    ks = jax.random.split(key, 24)
    D = D_MODEL

    def nrm(k, shape, scale):
        return jax.random.normal(k, shape, jnp.float32) * scale

    return {
        "x": nrm(ks[0], (BATCH, SEQ, D), 1.0),
        "c": nrm(ks[1], (BATCH, D), 1.0),
        "ctx": nrm(ks[2], (BATCH, CTX_LEN, D), 1.0),
        "c_ctx": nrm(ks[3], (D,), 1.0),
        "mod_w": nrm(ks[4], (DEPTH, D, MOD_CHUNKS * D), 0.3 * D ** -0.5),
        "mod_b": nrm(ks[5], (DEPTH, MOD_CHUNKS * D), 0.02),
        "norm1_g": 1.0 + nrm(ks[6], (DEPTH, D), 0.02),
        "norm2_g": 1.0 + nrm(ks[7], (DEPTH, D), 0.02),
        "even_w_in": nrm(ks[8], (N_EVEN, D, EVEN_IN), D ** -0.5),
        "even_w_out": nrm(ks[9], (N_EVEN, MIX_WIDTH, D), MIX_WIDTH ** -0.5),
        "hgrn_lb_logits": nrm(ks[10], (N_EVEN, 2, HGRN_WIDTH), 1.0),
        "hgrn_norm_g": 1.0 + nrm(ks[11], (N_EVEN, HGRN_HEADS, HGRN_HEAD_DIM), 0.02),
        "diff_qnorm_g": 1.0 + nrm(ks[12], (N_EVEN, DIFF_HEAD_DIM), 0.02),
        "diff_knorm_g": 1.0 + nrm(ks[13], (N_EVEN, DIFF_HEAD_DIM), 0.02),
        "diff_lambda": nrm(ks[14], (N_EVEN, 4, DIFF_HEAD_DIM), 0.1),
        "diff_subln_g": 1.0 + nrm(ks[15], (N_EVEN, 2 * DIFF_HEAD_DIM), 0.02),
        "conv_w_in": nrm(ks[16], (N_ODD, D, 3 * D), D ** -0.5),
        "conv_w": nrm(ks[17], (N_ODD, CONV_WIDTH, D), CONV_WIDTH ** -0.5),
        "conv_w_out": nrm(ks[18], (N_ODD, D, D), D ** -0.5),
        "router_w": nrm(ks[19], (DEPTH, D, N_EXPERTS), D ** -0.5),
        "exp_w_gate": nrm(ks[20], (DEPTH, N_EXPERTS, D, EXPERT_FF), D ** -0.5),
        "exp_w_up": nrm(ks[21], (DEPTH, N_EXPERTS, D, EXPERT_FF), D ** -0.5),
        "exp_w_down": nrm(ks[22], (DEPTH, N_EXPERTS, EXPERT_FF, D), EXPERT_FF ** -0.5),
    }


def reference(x, c, ctx, c_ctx, mod_w, mod_b, norm1_g, norm2_g, even_w_in, even_w_out,
              hgrn_lb_logits, hgrn_norm_g, diff_qnorm_g, diff_knorm_g, diff_lambda, diff_subln_g,
              conv_w_in, conv_w, conv_w_out, router_w, exp_w_gate, exp_w_up, exp_w_down):
    lb_soft = jax.nn.softmax(hgrn_lb_logits.astype(jnp.float32), axis=0)
    lower_bounds = jnp.cumsum(lb_soft, axis=0) - lb_soft[:1]
    last_ctx_layer = 2 * ((DEPTH - 1) // 2)
    silu_c = jax.nn.silu(c)
    silu_cc = jax.nn.silu(c_ctx)
    xc = ctx
    for l in range(DEPTH):
        read_ctx = l <= last_ctx_layer
        ctx_out = l < last_ctx_layer
        sh1, sc1, g1, sh2, sc2, g2 = jnp.split((silu_c @ mod_w[l] + mod_b[l])[:, None, :], MOD_CHUNKS, axis=-1)
        h = modulate(rmsnorm(x, norm1_g[l]), sh1, sc1)
        if read_ctx:
            csh1, csc1, cg1, csh2, csc2, cg2 = jnp.split(silu_cc @ mod_w[l] + mod_b[l], MOD_CHUNKS, axis=-1)
            hc = modulate(rmsnorm(xc, norm1_g[l]), csh1, csc1)
        if l % 2 == 0:
            e = l // 2
            y, yc = even_mixer(h, hc, even_w_in[e], even_w_out[e], lower_bounds[e], hgrn_norm_g[e],
                               diff_qnorm_g[e], diff_knorm_g[e], diff_lambda[e], diff_subln_g[e],
                               lambda_init(l), ctx_out)
        else:
            j = l // 2
            y = short_conv_mixer(h, conv_w_in[j], conv_w[j], conv_w_out[j])
            yc = short_conv_mixer(hc, conv_w_in[j], conv_w[j], conv_w_out[j]) if ctx_out else None
        x = x + g1 * y
        x = x + g2 * expert_choice_ffn(modulate(rmsnorm(x, norm2_g[l]), sh2, sc2),
                                       router_w[l], exp_w_gate[l], exp_w_up[l], exp_w_down[l])
        if ctx_out:
            xc = xc + cg1 * yc
            xc = xc + cg2 * expert_choice_ffn(modulate(rmsnorm(xc, norm2_g[l]), csh2, csc2),
                                              router_w[l], exp_w_gate[l], exp_w_up[l], exp_w_down[l])
    return x
```

```python
import functools
import math

import jax
import jax.numpy as jnp
from jax import lax
from jax.experimental import pallas as pl
from jax.experimental.pallas import tpu as pltpu

F32 = jnp.float32
BF16 = jnp.bfloat16

EPS = 1e-6
GRID_W = 64
ROPE_THETA = 10000.0
HGRN_HEAD_DIM = 128
HGRN_HEADS = 4
HGRN_WIDTH = HGRN_HEADS * HGRN_HEAD_DIM
DIFF_HEAD_DIM = 64
DIFF_HEADS = 4
DIFF_WIDTH = DIFF_HEADS * 2 * DIFF_HEAD_DIM
N_EXPERTS = 16
EC_CAPACITY = 2
MOD_CHUNKS = 6
SCAN_BLOCK = 16
LANES = 128
VMEM_LIMIT = 56 * 1024 * 1024


def _cparams(*sem):
    return pltpu.CompilerParams(dimension_semantics=sem, vmem_limit_bytes=VMEM_LIMIT)


def _row_tile(t, want):
    return want if t % want == 0 else t


def _split_dot(a_exact, x):
    hi = x.astype(BF16)
    r1 = x - hi.astype(F32)
    mid = r1.astype(BF16)
    lo = (r1 - mid.astype(F32)).astype(BF16)
    d = lambda p: jnp.dot(a_exact, p, preferred_element_type=F32)
    return d(hi) + d(mid) + d(lo)


def _mod_kernel(s_ref, w_ref, b_ref, o_ref):
    s = s_ref[...]
    w = w_ref[0]
    s_hi = s.astype(BF16)
    s_lo = (s - s_hi.astype(F32)).astype(BF16)
    w_hi = w.astype(BF16)
    w_lo = (w - w_hi.astype(F32)).astype(BF16)
    d = lambda a, b: jnp.dot(a, b, preferred_element_type=F32)
    o_ref[0] = d(s_hi, w_hi) + d(s_hi, w_lo) + d(s_lo, w_hi) + b_ref[0]


def modulation(s, mod_w, mod_b, *, tn=512):
    R, D = s.shape
    depth, _, N = mod_w.shape
    return pl.pallas_call(
        _mod_kernel,
        grid=(depth, N // tn),
        in_specs=[pl.BlockSpec((R, D), lambda l, j: (0, 0)),
                  pl.BlockSpec((1, D, tn), lambda l, j: (l, 0, j)),
                  pl.BlockSpec((1, 1, tn), lambda l, j: (l, 0, j))],
        out_specs=pl.BlockSpec((1, R, tn), lambda l, j: (l, 0, j)),
        out_shape=jax.ShapeDtypeStruct((depth, R, N), F32),
        compiler_params=_cparams("parallel", "parallel"),
        name="modulation",
    )(s, mod_w, mod_b.reshape(depth, 1, N))


def _nmm_kernel(x_ref, g_ref, sh_ref, sc_ref, w_ref, o_ref):
    x = x_ref[0]
    ms = jnp.mean(x * x, axis=-1, keepdims=True)
    h = (x * lax.rsqrt(ms + EPS)) * g_ref[...]
    h = h * (1.0 + sc_ref[0]) + sh_ref[0]
    o_ref[0] = jnp.dot(h.astype(BF16), w_ref[...], preferred_element_type=F32).astype(o_ref.dtype)


def norm_mod_matmul(x, g, shift, scale, w, *, tm=512, tn=1024, out_dtype=F32):
    B, T, D = x.shape
    N = w.shape[1]
    tm = _row_tile(T, tm)
    tn = _row_tile(N, tn)
    return pl.pallas_call(
        _nmm_kernel,
        grid=(B, T // tm, N // tn),
        in_specs=[
            pl.BlockSpec((1, tm, D), lambda b, i, j: (b, i, 0)),
            pl.BlockSpec((1, D), lambda b, i, j: (0, 0)),
            pl.BlockSpec((1, 1, D), lambda b, i, j: (b, 0, 0)),
            pl.BlockSpec((1, 1, D), lambda b, i, j: (b, 0, 0)),
            pl.BlockSpec((D, tn), lambda b, i, j: (0, j)),
        ],
        out_specs=pl.BlockSpec((1, tm, tn), lambda b, i, j: (b, i, j)),
        out_shape=jax.ShapeDtypeStruct((B, T, N), out_dtype),
        compiler_params=_cparams("parallel", "parallel", "arbitrary"),
        name="norm_mod_matmul",
    )(x, g.reshape(1, D), shift, scale, w)


def _scan_block(q, z, v, lb, st, tri, sel, rev):
    L = SCAN_BLOCK
    g = jnp.log(lb + (1.0 - lb) * jax.nn.sigmoid(z))
    k = (1.0 - lb) * jax.nn.sigmoid(-z)
    b = _split_dot(tri, g)
    e = b[0:1, :] if rev else b[L - 1:L, :]
    qt = q * jnp.exp(b)
    kt = k * jnp.exp(e - b)
    p = jnp.concatenate([q * jnp.exp(jnp.minimum(b - b[s:s + 1, :], 0.0)) for s in range(L)], axis=0)
    r = lax.dot_general(p.astype(BF16), k.astype(BF16), (((1,), (1,)), ((), ())),
                        preferred_element_type=F32)
    scores = jnp.sum(jnp.where(sel, r.reshape(L, L, L), 0.0), axis=0)
    o = lax.dot_general(qt.astype(BF16), st.astype(BF16), (((1,), (1,)), ((), ())),
                        preferred_element_type=F32)
    o = o + jnp.dot(scores.astype(BF16), v.astype(BF16), preferred_element_type=F32)
    upd = lax.dot_general(v.astype(BF16), kt.astype(BF16), (((0,), (0,)), ((), ())),
                          preferred_element_type=F32)
    return o, st * jnp.exp(e) + upd


def _scan_kernel(qf_ref, zf_ref, vf_ref, qb_ref, zb_ref, vb_ref, lbf_ref, lbb_ref, s0_ref,
                 of_ref, ob_ref, sT_ref, st_scr, *, n_blocks):
    c = pl.program_id(2)
    L = SCAN_BLOCK

    @pl.when(c == 0)
    def _():
        st_scr[...] = s0_ref[0, 0]

    row = lax.broadcasted_iota(jnp.int32, (L, L), 0)
    col = lax.broadcasted_iota(jnp.int32, (L, L), 1)
    tri_f = jnp.where(col <= row, 1.0, 0.0).astype(BF16)
    tri_b = jnp.where(col >= row, 1.0, 0.0).astype(BF16)
    s_i = lax.broadcasted_iota(jnp.int32, (L, L, L), 0)
    t_i = lax.broadcasted_iota(jnp.int32, (L, L, L), 1)
    c_i = lax.broadcasted_iota(jnp.int32, (L, L, L), 2)
    sel_f = (c_i == s_i) & (s_i <= t_i)
    sel_b = (c_i == s_i) & (s_i >= t_i)
    lbf = lbf_ref[...]
    lbb = lbb_ref[...]

    def body(i, carry):
        rf = pl.multiple_of(i * L, L)
        rb = pl.multiple_of((n_blocks - 1 - i) * L, L)
        o, sf = _scan_block(qf_ref[0, pl.ds(rf, L), :], zf_ref[0, pl.ds(rf, L), :], vf_ref[0, pl.ds(rf, L), :],
                            lbf, st_scr[0], tri_f, sel_f, False)
        of_ref[0, pl.ds(rf, L), :] = o
        st_scr[0] = sf
        o, sb = _scan_block(qb_ref[0, pl.ds(rb, L), :], zb_ref[0, pl.ds(rb, L), :], vb_ref[0, pl.ds(rb, L), :],
                            lbb, st_scr[1], tri_b, sel_b, True)
        ob_ref[0, pl.ds(rb, L), :] = o
        st_scr[1] = sb
        return carry

    lax.fori_loop(0, n_blocks, body, 0)

    @pl.when(c == pl.num_programs(2) - 1)
    def _():
        sT_ref[0, 0] = st_scr[...]


def hgrn_scan(proj, lb, s0, *, tb=512):
    B, T, _ = proj.shape
    tb = _row_tile(T, tb)
    nc = T // tb
    H = HGRN_HEADS
    hd = HGRN_HEAD_DIM
    fwd = lambda grp: pl.BlockSpec((1, tb, hd), lambda b, h, c: (b, c, grp * H + h))
    bwd = lambda grp: pl.BlockSpec((1, tb, hd), lambda b, h, c: (b, nc - 1 - c, grp * H + h))
    kern = functools.partial(_scan_kernel, n_blocks=tb // SCAN_BLOCK)
    return pl.pallas_call(
        kern,
        grid=(B, H, nc),
        in_specs=[fwd(0), fwd(1), fwd(3), bwd(0), bwd(2), bwd(3),
                  pl.BlockSpec((1, hd), lambda b, h, c: (0, h)),
                  pl.BlockSpec((1, hd), lambda b, h, c: (0, h)),
                  pl.BlockSpec((1, 1, 2, hd, hd), lambda b, h, c: (b, h, 0, 0, 0))],
        out_specs=[pl.BlockSpec((1, tb, hd), lambda b, h, c: (b, c, h)),
                   pl.BlockSpec((1, tb, hd), lambda b, h, c: (b, nc - 1 - c, h)),
                   pl.BlockSpec((1, 1, 2, hd, hd), lambda b, h, c: (b, h, 0, 0, 0))],
        out_shape=[jax.ShapeDtypeStruct((B, T, HGRN_WIDTH), F32),
                   jax.ShapeDtypeStruct((B, T, HGRN_WIDTH), F32),
                   jax.ShapeDtypeStruct((B, H, 2, hd, hd), F32)],
        scratch_shapes=[pltpu.VMEM((2, hd, hd), F32)],
        compiler_params=_cparams("parallel", "parallel", "arbitrary"),
        name="hgrn_scan",
    )(proj, proj, proj, proj, proj, proj, lb[0:1], lb[1:2], s0)


def _group_mean_sq(x, gmat):
    sq = x * x
    hi = sq.astype(BF16)
    lo = (sq - hi.astype(F32)).astype(BF16)
    return (jnp.dot(hi, gmat, preferred_element_type=F32) + jnp.dot(lo, gmat, preferred_element_type=F32))


def _qkv_prep_kernel(q_ref, k_ref, v_ref, qg_ref, kg_ref, cos_ref, sin_ref, qo_ref, ko_ref, vo_ref, *, rotary):
    W = LANES
    r_i = lax.broadcasted_iota(jnp.int32, (W, W), 0) // DIFF_HEAD_DIM
    c_i = lax.broadcasted_iota(jnp.int32, (W, W), 1) // DIFF_HEAD_DIM
    gmat = jnp.where(r_i == c_i, 1.0 / DIFF_HEAD_DIM, 0.0).astype(BF16)
    lane = lax.broadcasted_iota(jnp.int32, (1, W), 1)
    first = (lane % 32) < 16

    def prep(x, g, scale):
        y = (x * lax.rsqrt(_group_mean_sq(x, gmat) + EPS)) * g
        if rotary:
            partner = jnp.where(first, pltpu.roll(y, W - 16, axis=1), pltpu.roll(y, 16, axis=1))
            y = y * cos_ref[...] + partner * sin_ref[...]
        if scale != 1.0:
            y = y * scale
        return y.astype(BF16)

    for h in range(DIFF_HEADS):
        sl = slice(h * W, (h + 1) * W)
        qo_ref[0, :, sl] = prep(q_ref[0, :, sl], qg_ref[...], DIFF_HEAD_DIM ** -0.5)
        ko_ref[0, :, sl] = prep(k_ref[0, :, sl], kg_ref[...], 1.0)
    vo_ref[0] = v_ref[0].astype(BF16)


def qkv_prep(proj, qg, kg, cos, sin, *, rotary, tm=512):
    B, T, _ = proj.shape
    tm = _row_tile(T, tm)
    Wd = DIFF_WIDTH
    col = lambda j: pl.BlockSpec((1, tm, Wd), lambda b, i: (b, i, j))
    vec = pl.BlockSpec((1, LANES), lambda b, i: (0, 0))
    tab = pl.BlockSpec((tm, LANES), lambda b, i: (i, 0))
    out = pl.BlockSpec((1, tm, Wd), lambda b, i: (b, i, 0))
    sds = jax.ShapeDtypeStruct((B, T, Wd), BF16)
    return pl.pallas_call(
        functools.partial(_qkv_prep_kernel, rotary=rotary),
        grid=(B, T // tm),
        in_specs=[col(5), col(6), col(7), vec, vec, tab, tab],
        out_specs=[out, out, out],
        out_shape=[sds, sds, sds],
        compiler_params=_cparams("parallel", "parallel"),
        name="qkv_prep",
    )(proj, proj, proj, jnp.tile(qg, 2).reshape(1, LANES), jnp.tile(kg, 2).reshape(1, LANES), cos, sin)


def rope_tables(T):
    n = DIFF_HEAD_DIM // 2
    inv = 1.0 / (ROPE_THETA ** (jnp.arange(0, n, 2, dtype=F32) / n))
    t = jnp.arange(T)
    ang_r = (t // GRID_W).astype(F32)[:, None] * inv[None, :]
    ang_c = (t % GRID_W).astype(F32)[:, None] * inv[None, :]
    cos = jnp.concatenate([jnp.cos(ang_r)] * 2 + [jnp.cos(ang_c)] * 2, axis=-1)
    sin = jnp.concatenate([-jnp.sin(ang_r), jnp.sin(ang_r), -jnp.sin(ang_c), jnp.sin(ang_c)], axis=-1)
    return jnp.tile(cos, (1, 2)), jnp.tile(sin, (1, 2))


def _diff_attn_kernel(lam_ref, q_ref, k_ref, v_ref, g_ref, o_ref, *, out_scale):
    q = q_ref[0]
    tq = q.shape[0]
    lane = lax.broadcasted_iota(jnp.int32, (1, LANES), 1)
    zero = jnp.zeros_like(q)
    qq = jnp.concatenate([jnp.where(lane < DIFF_HEAD_DIM, q, zero),
                          jnp.where(lane >= DIFF_HEAD_DIM, q, zero)], axis=0)
    s = lax.dot_general(qq, k_ref[0], (((1,), (1,)), ((), ())), preferred_element_type=F32)
    m = jnp.max(s, axis=-1, keepdims=True)
    p = jnp.exp(s - m)
    inv = 1.0 / jnp.sum(p, axis=-1, keepdims=True)
    w = p[:tq] * inv[:tq] - (lam_ref[0] * inv[tq:]) * p[tq:]
    o = jnp.dot(w.astype(BF16), v_ref[0], preferred_element_type=F32)
    ms = jnp.mean(o * o, axis=-1, keepdims=True)
    o_ref[0] = (o * lax.rsqrt(ms + EPS)) * g_ref[...] * out_scale


def diff_attention(q, k, v, lam, subln_g, out_scale, *, tq=256):
    B, T, Wd = q.shape
    S = k.shape[1]
    tq = _row_tile(T, tq)
    return pl.pallas_call(
        functools.partial(_diff_attn_kernel, out_scale=out_scale),
        grid=(B, DIFF_HEADS, T // tq),
        in_specs=[pl.BlockSpec(memory_space=pltpu.SMEM),
                  pl.BlockSpec((1, tq, LANES), lambda b, h, i: (b, i, h)),
                  pl.BlockSpec((1, S, LANES), lambda b, h, i: (b, 0, h)),
                  pl.BlockSpec((1, S, LANES), lambda b, h, i: (b, 0, h)),
                  pl.BlockSpec((1, LANES), lambda b, h, i: (0, 0))],
        out_specs=pl.BlockSpec((1, tq, LANES), lambda b, h, i: (b, i, h)),
        out_shape=jax.ShapeDtypeStruct((B, T, Wd), F32),
        compiler_params=_cparams("parallel", "parallel", "arbitrary"),
        name="diff_attention",
    )(lam.reshape(1), q, k, v, subln_g.reshape(1, LANES))


def _even_out_kernel(x_ref, of_ref, ob_ref, gate_ref, yd_ref, hg_ref, g1_ref, w_ref, o_ref):
    acc = jnp.dot(yd_ref[0].astype(BF16), w_ref[HGRN_WIDTH:, :], preferred_element_type=F32)
    for h in range(HGRN_HEADS):
        sl = slice(h * HGRN_HEAD_DIM, (h + 1) * HGRN_HEAD_DIM)
        o = of_ref[0, :, sl] + ob_ref[0, :, sl]
        ms = jnp.mean(o * o, axis=-1, keepdims=True)
        gate = gate_ref[0, :, sl]
        yh = (o * lax.rsqrt(ms + EPS)) * hg_ref[:, sl] * (gate * jax.nn.sigmoid(gate))
        acc = acc + jnp.dot(yh.astype(BF16), w_ref[sl, :], preferred_element_type=F32)
    o_ref[0] = x_ref[0] + g1_ref[0] * acc


def even_out(x, o_f, o_b, proj, y_d, hgrn_g, g1, w_out, *, tm=512):
    B, T, D = x.shape
    tm = _row_tile(T, tm)
    Wd = HGRN_WIDTH
    row = lambda w, j: pl.BlockSpec((1, tm, w), lambda b, i: (b, i, j))
    return pl.pallas_call(
        _even_out_kernel,
        grid=(B, T // tm),
        in_specs=[row(D, 0), row(Wd, 0), row(Wd, 0), row(Wd, 4), row(Wd, 0),
                  pl.BlockSpec((1, Wd), lambda b, i: (0, 0)),
                  pl.BlockSpec((1, 1, D), lambda b, i: (b, 0, 0)),
                  pl.BlockSpec(w_out.shape, lambda b, i: (0, 0))],
        out_specs=row(D, 0),
        out_shape=jax.ShapeDtypeStruct((B, T, D), F32),
        compiler_params=_cparams("parallel", "parallel"),
        name="even_out",
    )(x, o_f, o_b, proj, y_d, hgrn_g.reshape(1, Wd), g1, w_out)


def _conv_out_kernel(x_ref, bg_ref, cg_ref, v_ref, cp_ref, vp_ref, cn_ref, vn_ref, cw_ref, g1_ref, w_ref, o_ref):
    i = pl.program_id(1)
    n = pl.num_programs(1)
    u = cg_ref[0] * v_ref[0]
    tm = u.shape[0]
    u_prev_row = jnp.where(i > 0, cp_ref[0, 7:8, :] * vp_ref[0, 7:8, :], 0.0)
    u_next_row = jnp.where(i < n - 1, cn_ref[0, 0:1, :] * vn_ref[0, 0:1, :], 0.0)
    ridx = lax.broadcasted_iota(jnp.int32, (tm, 1), 0)
    u_prev = jnp.where(ridx == 0, u_prev_row, pltpu.roll(u, 1, axis=0))
    u_next = jnp.where(ridx == tm - 1, u_next_row, pltpu.roll(u, tm - 1, axis=0))
    y = cw_ref[0:1, :] * u_prev + cw_ref[1:2, :] * u + cw_ref[2:3, :] * u_next
    acc = jnp.dot((bg_ref[0] * y).astype(BF16), w_ref[...], preferred_element_type=F32)
    o_ref[0] = x_ref[0] + g1_ref[0] * acc


def conv_out(x, proj, conv_w, g1, w_out, *, tm=512):
    B, T, D = x.shape
    tm = _row_tile(T, tm)
    r8 = tm // 8
    last8 = T // 8 - 1
    row = lambda j: pl.BlockSpec((1, tm, D), lambda b, i: (b, i, j))
    prev = lambda j: pl.BlockSpec((1, 8, D), lambda b, i: (b, jnp.maximum(i * r8 - 1, 0), j))
    nxt = lambda j: pl.BlockSpec((1, 8, D), lambda b, i: (b, jnp.minimum((i + 1) * r8, last8), j))
    return pl.pallas_call(
        _conv_out_kernel,
        grid=(B, T // tm),
        in_specs=[row(0), row(0), row(1), row(2), prev(1), prev(2), nxt(1), nxt(2),
                  pl.BlockSpec((8, D), lambda b, i: (0, 0)),
                  pl.BlockSpec((1, 1, D), lambda b, i: (b, 0, 0)),
                  pl.BlockSpec(w_out.shape, lambda b, i: (0, 0))],
        out_specs=row(0),
        out_shape=jax.ShapeDtypeStruct((B, T, D), F32),
        compiler_params=_cparams("parallel", "parallel"),
        name="conv_out",
    )(x, proj, proj, proj, proj, proj, proj, proj,
      jnp.concatenate([conv_w, jnp.zeros((8 - conv_w.shape[0], D), conv_w.dtype)], axis=0), g1, w_out)


def _router_kernel(x_ref, g_ref, sh_ref, sc_ref, rw_ref, h_ref, lg_ref):
    x = x_ref[0]
    ms = jnp.mean(x * x, axis=-1, keepdims=True)
    h = (x * lax.rsqrt(ms + EPS)) * g_ref[...]
    h = h * (1.0 + sc_ref[0]) + sh_ref[0]
    h_ref[0] = h.astype(BF16)
    rw = rw_ref[...]
    h_hi = h.astype(BF16)
    h_lo = (h - h_hi.astype(F32)).astype(BF16)
    w_hi = rw.astype(BF16)
    w_lo = (rw - w_hi.astype(F32)).astype(BF16)
    d = lambda a, b: jnp.dot(a, b, preferred_element_type=F32)
    lg_ref[0] = d(h_hi, w_hi) + d(h_hi, w_lo) + d(h_lo, w_hi)


def norm_mod_router(x, g, shift, scale, router_w, *, tm=512):
    B, T, D = x.shape
    E = router_w.shape[1]
    tm = _row_tile(T, tm)
    return pl.pallas_call(
        _router_kernel,
        grid=(B, T // tm),
        in_specs=[pl.BlockSpec((1, tm, D), lambda b, i: (b, i, 0)),
                  pl.BlockSpec((1, D), lambda b, i: (0, 0)),
                  pl.BlockSpec((1, 1, D), lambda b, i: (b, 0, 0)),
                  pl.BlockSpec((1, 1, D), lambda b, i: (b, 0, 0)),
                  pl.BlockSpec((D, E), lambda b, i: (0, 0))],
        out_specs=[pl.BlockSpec((1, tm, D), lambda b, i: (b, i, 0)),
                   pl.BlockSpec((1, tm, E), lambda b, i: (b, i, 0))],
        out_shape=[jax.ShapeDtypeStruct((B, T, D), BF16), jax.ShapeDtypeStruct((B, T, E), F32)],
        compiler_params=_cparams("parallel", "parallel"),
        name="norm_mod_router",
    )(x, g.reshape(1, D), shift, scale, router_w)


def _expert_kernel(xe_ref, gate_ref, wg_ref, wu_ref, wd_ref, o_ref):
    xe = xe_ref[0, 0]
    a = jnp.dot(xe, wg_ref[0], preferred_element_type=F32)
    u = jnp.dot(xe, wu_ref[0], preferred_element_type=F32)
    hid = (a * jax.nn.sigmoid(a)) * u
    y = jnp.dot(hid.astype(BF16), wd_ref[0], preferred_element_type=F32)
    o_ref[0, 0] = y * gate_ref[0, 0]


def expert_ffn(xe, gate, w_gate, w_up, w_down):
    B, E, cap, D = xe.shape
    FF = w_gate.shape[2]
    return pl.pallas_call(
        _expert_kernel,
        grid=(E, B),
        in_specs=[pl.BlockSpec((1, 1, cap, D), lambda e, b: (b, e, 0, 0)),
                  pl.BlockSpec((1, 1, cap, 1), lambda e, b: (b, e, 0, 0)),
                  pl.BlockSpec((1, D, FF), lambda e, b: (e, 0, 0)),
                  pl.BlockSpec((1, D, FF), lambda e, b: (e, 0, 0)),
                  pl.BlockSpec((1, FF, D), lambda e, b: (e, 0, 0))],
        out_specs=pl.BlockSpec((1, 1, cap, D), lambda e, b: (b, e, 0, 0)),
        out_shape=jax.ShapeDtypeStruct((B, E, cap, D), F32),
        compiler_params=_cparams("parallel", "arbitrary"),
        name="expert_ffn",
    )(xe, gate[..., None], w_gate, w_up, w_down)


def moe_residual(x, g, shift, scale, gate2, router_w, w_gate, w_up, w_down):
    B, N, D = x.shape
    cap = EC_CAPACITY * N // N_EXPERTS
    h, logits = norm_mod_router(x, g, shift, scale, router_w)
    affinity = jax.nn.softmax(logits, axis=-1)
    gate, idx = lax.top_k(jnp.swapaxes(affinity, 1, 2), cap)
    idx_flat = idx.reshape(B, N_EXPERTS * cap)
    xe = jax.vmap(lambda hb, ib: hb[ib])(h, idx_flat).reshape(B, N_EXPERTS, cap, D)
    ye = expert_ffn(xe, gate, w_gate, w_up, w_down)
    y = jax.vmap(lambda yb, ib: jnp.zeros((N, D), F32).at[ib].add(yb))(
        ye.reshape(B, N_EXPERTS * cap, D), idx_flat)
    return x + gate2 * y


def lambda_init(layer):
    return 0.8 - 0.6 * math.exp(-0.3 * layer)


def even_layer(x, xc, mods, cmods, norm1_g, w_in, w_out, lb, hgrn_g, qn_g, kn_g, lam_vec, subln_g,
               lam_init, ctx_out, tables):
    sh1, sc1, g1 = mods
    csh1, csc1, cg1 = cmods
    B = x.shape[0]
    proj = norm_mod_matmul(x, norm1_g, sh1, sc1, w_in)
    projc = norm_mod_matmul(xc, norm1_g, csh1, csc1, w_in)
    s0 = jnp.zeros((B, HGRN_HEADS, 2, HGRN_HEAD_DIM, HGRN_HEAD_DIM), F32)
    oc_f, oc_b, s_ctx = hgrn_scan(projc, lb, s0)
    o_f, o_b, _ = hgrn_scan(proj, lb, s_ctx)
    lv = lam_vec.astype(F32)
    lam = jnp.exp(jnp.sum(lv[0] * lv[1])) - jnp.exp(jnp.sum(lv[2] * lv[3])) + lam_init
    cos, sin = tables
    q, k, v = qkv_prep(proj, qn_g, kn_g, cos, sin, rotary=True)
    qc, kc, vc = qkv_prep(projc, qn_g, kn_g, cos[:xc.shape[1]], sin[:xc.shape[1]], rotary=False)
    k_all = jnp.concatenate([k, kc], axis=1)
    v_all = jnp.concatenate([v, vc], axis=1)
    y_d = diff_attention(q, k_all, v_all, lam, subln_g, 1.0 - lam_init)
    x_new = even_out(x, o_f, o_b, proj, y_d, hgrn_g.reshape(-1), g1, w_out)
    if not ctx_out:
        return x_new, None
    yc_d = diff_attention(qc, kc, vc, lam, subln_g, 1.0 - lam_init)
    xc_new = even_out(xc, oc_f, oc_b, projc, yc_d, hgrn_g.reshape(-1), cg1, w_out)
    return x_new, xc_new


def conv_layer(x, mods, norm1_g, w_in, conv_w, w_out):
    sh1, sc1, g1 = mods
    proj = norm_mod_matmul(x, norm1_g, sh1, sc1, w_in)
    return conv_out(x, proj, conv_w, g1, w_out)


def kernel(x, c, ctx, c_ctx, mod_w, mod_b, norm1_g, norm2_g, even_w_in, even_w_out, hgrn_lb_logits, hgrn_norm_g,
           diff_qnorm_g, diff_knorm_g, diff_lambda, diff_subln_g, conv_w_in, conv_w, conv_w_out, router_w,
           exp_w_gate, exp_w_up, exp_w_down):
    depth = mod_w.shape[0]
    B, T, D = x.shape
    lb_soft = jax.nn.softmax(hgrn_lb_logits.astype(F32), axis=0)
    lower_bounds = jnp.cumsum(lb_soft, axis=0) - lb_soft[:1]
    last_ctx_layer = 2 * ((depth - 1) // 2)
    cond = jnp.concatenate([c, c_ctx[None, :], jnp.zeros((8 - (B + 1) % 8, D), F32)], axis=0)
    mods = modulation(jax.nn.silu(cond), mod_w, mod_b)
    tables = rope_tables(T)
    xc = ctx
    for l in range(depth):
        read_ctx = l <= last_ctx_layer
        ctx_out = l < last_ctx_layer
        sh1, sc1, g1, sh2, sc2, g2 = [m[:, None, :] for m in jnp.split(mods[l, :B], MOD_CHUNKS, axis=-1)]
        if read_ctx:
            csh1, csc1, cg1, csh2, csc2, cg2 = [
                jnp.broadcast_to(m[None, None, :], (B, 1, D)) for m in jnp.split(mods[l, B], MOD_CHUNKS, axis=-1)]
        ew = (exp_w_gate[l].astype(BF16), exp_w_up[l].astype(BF16), exp_w_down[l].astype(BF16))
        if l % 2 == 0:
            e = l // 2
            x, xc_new = even_layer(x, xc, (sh1, sc1, g1), (csh1, csc1, cg1), norm1_g[l],
                                   even_w_in[e].astype(BF16), even_w_out[e].astype(BF16), lower_bounds[e],
                                   hgrn_norm_g[e], diff_qnorm_g[e], diff_knorm_g[e], diff_lambda[e],
                                   diff_subln_g[e], lambda_init(l), ctx_out, tables)
        else:
            j = l // 2
            wi, wo = conv_w_in[j].astype(BF16), conv_w_out[j].astype(BF16)
            x = conv_layer(x, (sh1, sc1, g1), norm1_g[l], wi, conv_w[j], wo)
            xc_new = conv_layer(xc, (csh1, csc1, cg1), norm1_g[l], wi, conv_w[j], wo) if ctx_out else None
        x = moe_residual(x, norm2_g[l], sh2, sc2, g2, router_w[l], *ew)
        if ctx_out:
            xc = moe_residual(xc_new, norm2_g[l], csh2, csc2, cg2, router_w[l], *ew)
    return x
```

```python
import functools
import math

import numpy as np
import jax
import jax.numpy as jnp
from jax import lax
from jax.experimental import pallas as pl
from jax.experimental.pallas import tpu as pltpu

F32 = jnp.float32
BF16 = jnp.bfloat16

EPS = 1e-6
GRID_W = 64
ROPE_THETA = 10000.0
HGRN_HEAD_DIM = 128
HGRN_HEADS = 4
HGRN_WIDTH = HGRN_HEADS * HGRN_HEAD_DIM
DIFF_HEAD_DIM = 64
DIFF_HEADS = 4
DIFF_WIDTH = DIFF_HEADS * 2 * DIFF_HEAD_DIM
N_EXPERTS = 16
EC_CAPACITY = 2
MOD_CHUNKS = 6
SCAN_CHUNK = 64
SCAN_LEVELS = (32, 16, 8, 4, 2, 1)
LANES = 128
VMEM_LIMIT = 56 * 1024 * 1024


def _cparams(*sem):
    return pltpu.CompilerParams(dimension_semantics=sem, vmem_limit_bytes=VMEM_LIMIT)


def _row_tile(t, want):
    return want if t % want == 0 else t


def _mod_kernel(s_ref, w_ref, b_ref, o_ref):
    s = s_ref[...]
    w = w_ref[0]
    s_hi = s.astype(BF16)
    s_lo = (s - s_hi.astype(F32)).astype(BF16)
    w_hi = w.astype(BF16)
    w_lo = (w - w_hi.astype(F32)).astype(BF16)
    d = lambda a, b: jnp.dot(a, b, preferred_element_type=F32)
    o_ref[0] = d(s_hi, w_hi) + d(s_hi, w_lo) + d(s_lo, w_hi) + b_ref[0]


def modulation(s, mod_w, mod_b, *, tn=512):
    R, D = s.shape
    depth, _, N = mod_w.shape
    return pl.pallas_call(
        _mod_kernel,
        grid=(depth, N // tn),
        in_specs=[pl.BlockSpec((R, D), lambda l, j: (0, 0)),
                  pl.BlockSpec((1, D, tn), lambda l, j: (l, 0, j)),
                  pl.BlockSpec((1, 1, tn), lambda l, j: (l, 0, j))],
        out_specs=pl.BlockSpec((1, R, tn), lambda l, j: (l, 0, j)),
        out_shape=jax.ShapeDtypeStruct((depth, R, N), F32),
        compiler_params=_cparams("parallel", "parallel"),
        name="modulation",
    )(s, mod_w, mod_b.reshape(depth, 1, N))


def _nmm_kernel(x_ref, g_ref, sh_ref, sc_ref, w_ref, o_ref):
    x = x_ref[0]
    ms = jnp.mean(x * x, axis=-1, keepdims=True)
    h = (x * lax.rsqrt(ms + EPS)) * g_ref[...]
    h = h * (1.0 + sc_ref[0]) + sh_ref[0]
    o_ref[0] = jnp.dot(h.astype(BF16), w_ref[...], preferred_element_type=F32).astype(o_ref.dtype)


def norm_mod_matmul(x, g, shift, scale, w, *, tm=512, tn=1024, out_dtype=F32):
    B, T, D = x.shape
    N = w.shape[1]
    tm = _row_tile(T, tm)
    tn = _row_tile(N, tn)
    return pl.pallas_call(
        _nmm_kernel,
        grid=(B, T // tm, N // tn),
        in_specs=[
            pl.BlockSpec((1, tm, D), lambda b, i, j: (b, i, 0)),
            pl.BlockSpec((1, D), lambda b, i, j: (0, 0)),
            pl.BlockSpec((1, 1, D), lambda b, i, j: (b, 0, 0)),
            pl.BlockSpec((1, 1, D), lambda b, i, j: (b, 0, 0)),
            pl.BlockSpec((D, tn), lambda b, i, j: (0, j)),
        ],
        out_specs=pl.BlockSpec((1, tm, tn), lambda b, i, j: (b, i, j)),
        out_shape=jax.ShapeDtypeStruct((B, T, N), out_dtype),
        compiler_params=_cparams("parallel", "parallel", "arbitrary"),
        name="norm_mod_matmul",
    )(x, g.reshape(1, D), shift, scale, w)


def _scan_constants():
    C = SCAN_CHUNK
    t = np.arange(C)[:, None]
    u = np.arange(C)[None, :]
    mats = [u <= t, u > t]
    masks = []
    for w in SCAN_LEVELS:
        m = (t // (2 * w)) * 2 * w + w - 1
        later = (t // w) % 2 == 1
        mats.append(np.where(later, (u > m) & (u <= t), (u > t) & (u <= m)))
        masks.append(later & ((u // w) % 2 == 0) & (u // (2 * w) == t // (2 * w)))
    masks.append(t == u)
    a_f = np.stack(mats).astype(np.float32)
    m_f = np.stack(masks).astype(np.float32)
    a = np.stack([a_f, a_f[:, ::-1, ::-1]]).reshape(2, -1, C)
    return np.concatenate([a, a], axis=2), np.stack([m_f, m_f[:, ::-1, ::-1]])


def _scan_pair(q, z, v, lb, st, a2, mask_ref, d, later):
    C = SCAN_CHUNK
    W = HGRN_HEAD_DIM
    nl = len(SCAN_LEVELS)
    nt = lambda x, y: lax.dot_general(x, y, (((1,), (1,)), ((), ())), preferred_element_type=F32)
    nn = lambda x, y: jnp.dot(x, y, preferred_element_type=F32)
    e_abs = jnp.exp(-jnp.abs(z))
    r = 1.0 / (1.0 + e_abs)
    er = e_abs * r
    pos = z >= 0.0
    g2 = jnp.log2(lb + (1.0 - lb) * jnp.where(pos, r, er))
    k = (1.0 - lb) * jnp.where(pos, er, r)
    hi = g2.astype(BF16)
    lo = (g2 - hi.astype(F32)).astype(BF16)
    gs = jnp.concatenate([jnp.concatenate([hi[:C], hi[C:]], axis=1),
                          jnp.concatenate([lo[:C], lo[C:]], axis=1)], axis=0)
    x = jnp.exp2(nn(a2, gs))
    first, second = (1, 0) if d else (0, 1)
    vb = v.astype(BF16)
    sc, qt, kt, dec = [], [], [], []
    for c in (0, 1):
        qc, kc, xs = q[c * C:(c + 1) * C], k[c * C:(c + 1) * C], x[:, c * W:(c + 1) * W]
        s = mask_ref[d, nl] * nt(qc.astype(BF16), kc.astype(BF16))
        for i in range(nl):
            qk = (jnp.where(later[i], qc, kc) * xs[(2 + i) * C:(3 + i) * C]).astype(BF16)
            s = s + mask_ref[d, i] * nt(qk, qk)
        sc.append(s.astype(BF16))
        qt.append(qc * xs[0:C])
        kt.append(kc * xs[C:2 * C])
        dec.append(xs[0:1, :] if d else xs[C - 1:C, :])
    cross = nt(qt[second].astype(BF16), kt[first].astype(BF16)).astype(BF16)
    qt[second] = qt[second] * dec[first]
    qtb = [a.astype(BF16) for a in qt]
    kt[first] = kt[first] * dec[second]
    o_st = nt(jnp.concatenate(qtb, axis=0), st.astype(BF16))
    o = [None, None]
    o[first] = o_st[first * C:(first + 1) * C] + nn(sc[first], vb[first * C:(first + 1) * C])
    o[second] = (o_st[second * C:(second + 1) * C] + nn(cross, vb[first * C:(first + 1) * C])
                 + nn(sc[second], vb[second * C:(second + 1) * C]))
    upd = lax.dot_general(vb, jnp.concatenate(kt, axis=0).astype(BF16), (((0,), (0,)), ((), ())),
                          preferred_element_type=F32)
    return jnp.concatenate(o, axis=0), st * (dec[0] * dec[1]) + upd


def _scan_kernel(qf_ref, zf_ref, vf_ref, qb_ref, zb_ref, vb_ref, lbf_ref, lbb_ref, s0_ref, a_ref, mask_ref,
                 of_ref, ob_ref, sT_ref, st_scr, *, n_chunks):
    c = pl.program_id(2)
    C = SCAN_CHUNK

    @pl.when(c == 0)
    def _():
        st_scr[...] = s0_ref[0, 0]

    row = lax.broadcasted_iota(jnp.int32, (C, HGRN_HEAD_DIM), 0)
    later_f = [(row // w) % 2 == 1 for w in SCAN_LEVELS]
    later_b = [((C - 1 - row) // w) % 2 == 1 for w in SCAN_LEVELS]
    lbf = lbf_ref[...]
    lbb = lbb_ref[...]

    n_pairs = n_chunks // 2

    def body(i, carry):
        rf = pl.multiple_of(i * 2 * C, 2 * C)
        rb = pl.multiple_of((n_pairs - 1 - i) * 2 * C, 2 * C)
        fwd_in = (qf_ref[0, pl.ds(rf, 2 * C), :], zf_ref[0, pl.ds(rf, 2 * C), :], vf_ref[0, pl.ds(rf, 2 * C), :])
        bwd_in = (qb_ref[0, pl.ds(rb, 2 * C), :], zb_ref[0, pl.ds(rb, 2 * C), :], vb_ref[0, pl.ds(rb, 2 * C), :])
        sf, sb = st_scr[0], st_scr[1]
        o_f, sf = _scan_pair(*fwd_in, lbf, sf, a_ref[0], mask_ref, 0, later_f)
        o_b, sb = _scan_pair(*bwd_in, lbb, sb, a_ref[1], mask_ref, 1, later_b)
        of_ref[0, pl.ds(rf, 2 * C), :] = o_f
        ob_ref[0, pl.ds(rb, 2 * C), :] = o_b
        st_scr[0] = sf
        st_scr[1] = sb
        return carry

    lax.fori_loop(0, n_pairs, body, 0)

    @pl.when(c == pl.num_programs(2) - 1)
    def _():
        sT_ref[0, 0] = st_scr[...]


def hgrn_scan(proj, lb, s0, *, tb=512):
    B, T, _ = proj.shape
    tb = _row_tile(T, tb)
    nc = T // tb
    H = HGRN_HEADS
    hd = HGRN_HEAD_DIM
    fwd = lambda grp: pl.BlockSpec((1, tb, hd), lambda b, h, c: (b, c, grp * H + h))
    bwd = lambda grp: pl.BlockSpec((1, tb, hd), lambda b, h, c: (b, nc - 1 - c, grp * H + h))
    kern = functools.partial(_scan_kernel, n_chunks=tb // SCAN_CHUNK)
    a2, masks = _scan_constants()
    return pl.pallas_call(
        kern,
        grid=(B, H, nc),
        in_specs=[fwd(0), fwd(1), fwd(3), bwd(0), bwd(2), bwd(3),
                  pl.BlockSpec((1, hd), lambda b, h, c: (0, h)),
                  pl.BlockSpec((1, hd), lambda b, h, c: (0, h)),
                  pl.BlockSpec((1, 1, 2, hd, hd), lambda b, h, c: (b, h, 0, 0, 0)),
                  pl.BlockSpec(a2.shape, lambda b, h, c: (0, 0, 0)),
                  pl.BlockSpec(masks.shape, lambda b, h, c: (0, 0, 0, 0))],
        out_specs=[pl.BlockSpec((1, tb, hd), lambda b, h, c: (b, c, h)),
                   pl.BlockSpec((1, tb, hd), lambda b, h, c: (b, nc - 1 - c, h)),
                   pl.BlockSpec((1, 1, 2, hd, hd), lambda b, h, c: (b, h, 0, 0, 0))],
        out_shape=[jax.ShapeDtypeStruct((B, T, HGRN_WIDTH), F32),
                   jax.ShapeDtypeStruct((B, T, HGRN_WIDTH), F32),
                   jax.ShapeDtypeStruct((B, H, 2, hd, hd), F32)],
        scratch_shapes=[pltpu.VMEM((2, hd, hd), F32)],
        compiler_params=_cparams("parallel", "parallel", "arbitrary"),
        name="hgrn_scan",
    )(proj, proj, proj, proj, proj, proj, lb[0:1], lb[1:2], s0, jnp.asarray(a2, BF16), jnp.asarray(masks, F32))


def _group_mean_sq(x, gmat):
    sq = x * x
    hi = sq.astype(BF16)
    lo = (sq - hi.astype(F32)).astype(BF16)
    return (jnp.dot(hi, gmat, preferred_element_type=F32) + jnp.dot(lo, gmat, preferred_element_type=F32))


def _qkv_prep_kernel(q_ref, k_ref, v_ref, qg_ref, kg_ref, cos_ref, sin_ref, qo_ref, ko_ref, vo_ref, *, rotary):
    W = LANES
    r_i = lax.broadcasted_iota(jnp.int32, (W, W), 0) // DIFF_HEAD_DIM
    c_i = lax.broadcasted_iota(jnp.int32, (W, W), 1) // DIFF_HEAD_DIM
    gmat = jnp.where(r_i == c_i, 1.0 / DIFF_HEAD_DIM, 0.0).astype(BF16)
    lane = lax.broadcasted_iota(jnp.int32, (1, W), 1)
    first = (lane % 32) < 16

    def prep(x, g, scale):
        y = (x * lax.rsqrt(_group_mean_sq(x, gmat) + EPS)) * g
        if rotary:
            partner = jnp.where(first, pltpu.roll(y, W - 16, axis=1), pltpu.roll(y, 16, axis=1))
            y = y * cos_ref[...] + partner * sin_ref[...]
        if scale != 1.0:
            y = y * scale
        return y.astype(BF16)

    for h in range(DIFF_HEADS):
        sl = slice(h * W, (h + 1) * W)
        qo_ref[0, :, sl] = prep(q_ref[0, :, sl], qg_ref[...], DIFF_HEAD_DIM ** -0.5)
        ko_ref[0, :, sl] = prep(k_ref[0, :, sl], kg_ref[...], 1.0)
    vo_ref[0] = v_ref[0].astype(BF16)


def qkv_prep(proj, qg, kg, cos, sin, *, rotary, tm=512):
    B, T, _ = proj.shape
    tm = _row_tile(T, tm)
    Wd = DIFF_WIDTH
    col = lambda j: pl.BlockSpec((1, tm, Wd), lambda b, i: (b, i, j))
    vec = pl.BlockSpec((1, LANES), lambda b, i: (0, 0))
    tab = pl.BlockSpec((tm, LANES), lambda b, i: (i, 0))
    out = pl.BlockSpec((1, tm, Wd), lambda b, i: (b, i, 0))
    sds = jax.ShapeDtypeStruct((B, T, Wd), BF16)
    return pl.pallas_call(
        functools.partial(_qkv_prep_kernel, rotary=rotary),
        grid=(B, T // tm),
        in_specs=[col(5), col(6), col(7), vec, vec, tab, tab],
        out_specs=[out, out, out],
        out_shape=[sds, sds, sds],
        compiler_params=_cparams("parallel", "parallel"),
        name="qkv_prep",
    )(proj, proj, proj, jnp.tile(qg, 2).reshape(1, LANES), jnp.tile(kg, 2).reshape(1, LANES), cos, sin)


def rope_tables(T):
    n = DIFF_HEAD_DIM // 2
    inv = 1.0 / (ROPE_THETA ** (jnp.arange(0, n, 2, dtype=F32) / n))
    t = jnp.arange(T)
    ang_r = (t // GRID_W).astype(F32)[:, None] * inv[None, :]
    ang_c = (t % GRID_W).astype(F32)[:, None] * inv[None, :]
    cos = jnp.concatenate([jnp.cos(ang_r)] * 2 + [jnp.cos(ang_c)] * 2, axis=-1)
    sin = jnp.concatenate([-jnp.sin(ang_r), jnp.sin(ang_r), -jnp.sin(ang_c), jnp.sin(ang_c)], axis=-1)
    return jnp.tile(cos, (1, 2)), jnp.tile(sin, (1, 2))


def _diff_attn_kernel(lam_ref, q_ref, k_ref, v_ref, g_ref, o_ref, *, out_scale):
    q = q_ref[0]
    tq = q.shape[0]
    lane = lax.broadcasted_iota(jnp.int32, (1, LANES), 1)
    zero = jnp.zeros_like(q)
    qq = jnp.concatenate([jnp.where(lane < DIFF_HEAD_DIM, q, zero),
                          jnp.where(lane >= DIFF_HEAD_DIM, q, zero)], axis=0)
    s = lax.dot_general(qq, k_ref[0], (((1,), (1,)), ((), ())), preferred_element_type=F32)
    m = jnp.max(s, axis=-1, keepdims=True)
    p = jnp.exp(s - m)
    inv = 1.0 / jnp.sum(p, axis=-1, keepdims=True)
    w = p[:tq] * inv[:tq] - (lam_ref[0] * inv[tq:]) * p[tq:]
    o = jnp.dot(w.astype(BF16), v_ref[0], preferred_element_type=F32)
    ms = jnp.mean(o * o, axis=-1, keepdims=True)
    o_ref[0] = (o * lax.rsqrt(ms + EPS)) * g_ref[...] * out_scale


def diff_attention(q, k, v, lam, subln_g, out_scale, *, tq=256):
    B, T, Wd = q.shape
    S = k.shape[1]
    tq = _row_tile(T, tq)
    return pl.pallas_call(
        functools.partial(_diff_attn_kernel, out_scale=out_scale),
        grid=(B, DIFF_HEADS, T // tq),
        in_specs=[pl.BlockSpec(memory_space=pltpu.SMEM),
                  pl.BlockSpec((1, tq, LANES), lambda b, h, i: (b, i, h)),
                  pl.BlockSpec((1, S, LANES), lambda b, h, i: (b, 0, h)),
                  pl.BlockSpec((1, S, LANES), lambda b, h, i: (b, 0, h)),
                  pl.BlockSpec((1, LANES), lambda b, h, i: (0, 0))],
        out_specs=pl.BlockSpec((1, tq, LANES), lambda b, h, i: (b, i, h)),
        out_shape=jax.ShapeDtypeStruct((B, T, Wd), F32),
        compiler_params=_cparams("parallel", "parallel", "arbitrary"),
        name="diff_attention",
    )(lam.reshape(1), q, k, v, subln_g.reshape(1, LANES))


def _even_out_kernel(x_ref, of_ref, ob_ref, gate_ref, yd_ref, hg_ref, g1_ref, w_ref, o_ref):
    acc = jnp.dot(yd_ref[0].astype(BF16), w_ref[HGRN_WIDTH:, :], preferred_element_type=F32)
    for h in range(HGRN_HEADS):
        sl = slice(h * HGRN_HEAD_DIM, (h + 1) * HGRN_HEAD_DIM)
        o = of_ref[0, :, sl] + ob_ref[0, :, sl]
        ms = jnp.mean(o * o, axis=-1, keepdims=True)
        gate = gate_ref[0, :, sl]
        yh = (o * lax.rsqrt(ms + EPS)) * hg_ref[:, sl] * (gate * jax.nn.sigmoid(gate))
        acc = acc + jnp.dot(yh.astype(BF16), w_ref[sl, :], preferred_element_type=F32)
    o_ref[0] = x_ref[0] + g1_ref[0] * acc


def even_out(x, o_f, o_b, proj, y_d, hgrn_g, g1, w_out, *, tm=512):
    B, T, D = x.shape
    tm = _row_tile(T, tm)
    Wd = HGRN_WIDTH
    row = lambda w, j: pl.BlockSpec((1, tm, w), lambda b, i: (b, i, j))
    return pl.pallas_call(
        _even_out_kernel,
        grid=(B, T // tm),
        in_specs=[row(D, 0), row(Wd, 0), row(Wd, 0), row(Wd, 4), row(Wd, 0),
                  pl.BlockSpec((1, Wd), lambda b, i: (0, 0)),
                  pl.BlockSpec((1, 1, D), lambda b, i: (b, 0, 0)),
                  pl.BlockSpec(w_out.shape, lambda b, i: (0, 0))],
        out_specs=row(D, 0),
        out_shape=jax.ShapeDtypeStruct((B, T, D), F32),
        compiler_params=_cparams("parallel", "parallel"),
        name="even_out",
    )(x, o_f, o_b, proj, y_d, hgrn_g.reshape(1, Wd), g1, w_out)


def _conv_out_kernel(x_ref, bg_ref, cg_ref, v_ref, cp_ref, vp_ref, cn_ref, vn_ref, cw_ref, g1_ref, w_ref, o_ref):
    i = pl.program_id(1)
    n = pl.num_programs(1)
    u = cg_ref[0] * v_ref[0]
    tm = u.shape[0]
    u_prev_row = jnp.where(i > 0, cp_ref[0, 7:8, :] * vp_ref[0, 7:8, :], 0.0)
    u_next_row = jnp.where(i < n - 1, cn_ref[0, 0:1, :] * vn_ref[0, 0:1, :], 0.0)
    ridx = lax.broadcasted_iota(jnp.int32, (tm, 1), 0)
    u_prev = jnp.where(ridx == 0, u_prev_row, pltpu.roll(u, 1, axis=0))
    u_next = jnp.where(ridx == tm - 1, u_next_row, pltpu.roll(u, tm - 1, axis=0))
    y = cw_ref[0:1, :] * u_prev + cw_ref[1:2, :] * u + cw_ref[2:3, :] * u_next
    acc = jnp.dot((bg_ref[0] * y).astype(BF16), w_ref[...], preferred_element_type=F32)
    o_ref[0] = x_ref[0] + g1_ref[0] * acc


def conv_out(x, proj, conv_w, g1, w_out, *, tm=512):
    B, T, D = x.shape
    tm = _row_tile(T, tm)
    r8 = tm // 8
    last8 = T // 8 - 1
    row = lambda j: pl.BlockSpec((1, tm, D), lambda b, i: (b, i, j))
    prev = lambda j: pl.BlockSpec((1, 8, D), lambda b, i: (b, jnp.maximum(i * r8 - 1, 0), j))
    nxt = lambda j: pl.BlockSpec((1, 8, D), lambda b, i: (b, jnp.minimum((i + 1) * r8, last8), j))
    return pl.pallas_call(
        _conv_out_kernel,
        grid=(B, T // tm),
        in_specs=[row(0), row(0), row(1), row(2), prev(1), prev(2), nxt(1), nxt(2),
                  pl.BlockSpec((8, D), lambda b, i: (0, 0)),
                  pl.BlockSpec((1, 1, D), lambda b, i: (b, 0, 0)),
                  pl.BlockSpec(w_out.shape, lambda b, i: (0, 0))],
        out_specs=row(0),
        out_shape=jax.ShapeDtypeStruct((B, T, D), F32),
        compiler_params=_cparams("parallel", "parallel"),
        name="conv_out",
    )(x, proj, proj, proj, proj, proj, proj, proj,
      jnp.concatenate([conv_w, jnp.zeros((8 - conv_w.shape[0], D), conv_w.dtype)], axis=0), g1, w_out)


def _router_kernel(x_ref, g_ref, sh_ref, sc_ref, rw_ref, h_ref, lg_ref):
    x = x_ref[0]
    ms = jnp.mean(x * x, axis=-1, keepdims=True)
    h = (x * lax.rsqrt(ms + EPS)) * g_ref[...]
    h = h * (1.0 + sc_ref[0]) + sh_ref[0]
    h_ref[0] = h.astype(BF16)
    rw = rw_ref[...]
    h_hi = h.astype(BF16)
    h_lo = (h - h_hi.astype(F32)).astype(BF16)
    w_hi = rw.astype(BF16)
    w_lo = (rw - w_hi.astype(F32)).astype(BF16)
    d = lambda a, b: jnp.dot(a, b, preferred_element_type=F32)
    lg_ref[0] = d(h_hi, w_hi) + d(h_hi, w_lo) + d(h_lo, w_hi)


def norm_mod_router(x, g, shift, scale, router_w, *, tm=512):
    B, T, D = x.shape
    E = router_w.shape[1]
    tm = _row_tile(T, tm)
    return pl.pallas_call(
        _router_kernel,
        grid=(B, T // tm),
        in_specs=[pl.BlockSpec((1, tm, D), lambda b, i: (b, i, 0)),
                  pl.BlockSpec((1, D), lambda b, i: (0, 0)),
                  pl.BlockSpec((1, 1, D), lambda b, i: (b, 0, 0)),
                  pl.BlockSpec((1, 1, D), lambda b, i: (b, 0, 0)),
                  pl.BlockSpec((D, E), lambda b, i: (0, 0))],
        out_specs=[pl.BlockSpec((1, tm, D), lambda b, i: (b, i, 0)),
                   pl.BlockSpec((1, tm, E), lambda b, i: (b, i, 0))],
        out_shape=[jax.ShapeDtypeStruct((B, T, D), BF16), jax.ShapeDtypeStruct((B, T, E), F32)],
        compiler_params=_cparams("parallel", "parallel"),
        name="norm_mod_router",
    )(x, g.reshape(1, D), shift, scale, router_w)


def _expert_kernel(xe_ref, gate_ref, wg_ref, wu_ref, wd_ref, o_ref):
    xe = xe_ref[0, 0]
    a = jnp.dot(xe, wg_ref[0], preferred_element_type=F32)
    u = jnp.dot(xe, wu_ref[0], preferred_element_type=F32)
    hid = (a * jax.nn.sigmoid(a)) * u
    y = jnp.dot(hid.astype(BF16), wd_ref[0], preferred_element_type=F32)
    o_ref[0, 0] = y * gate_ref[0, 0]


def expert_ffn(xe, gate, w_gate, w_up, w_down):
    B, E, cap, D = xe.shape
    FF = w_gate.shape[2]
    return pl.pallas_call(
        _expert_kernel,
        grid=(E, B),
        in_specs=[pl.BlockSpec((1, 1, cap, D), lambda e, b: (b, e, 0, 0)),
                  pl.BlockSpec((1, 1, cap, 1), lambda e, b: (b, e, 0, 0)),
                  pl.BlockSpec((1, D, FF), lambda e, b: (e, 0, 0)),
                  pl.BlockSpec((1, D, FF), lambda e, b: (e, 0, 0)),
                  pl.BlockSpec((1, FF, D), lambda e, b: (e, 0, 0))],
        out_specs=pl.BlockSpec((1, 1, cap, D), lambda e, b: (b, e, 0, 0)),
        out_shape=jax.ShapeDtypeStruct((B, E, cap, D), F32),
        compiler_params=_cparams("parallel", "arbitrary"),
        name="expert_ffn",
    )(xe, gate[..., None], w_gate, w_up, w_down)


def moe_residual(x, g, shift, scale, gate2, router_w, w_gate, w_up, w_down):
    B, N, D = x.shape
    cap = EC_CAPACITY * N // N_EXPERTS
    h, logits = norm_mod_router(x, g, shift, scale, router_w)
    affinity = jax.nn.softmax(logits, axis=-1)
    gate, idx = lax.top_k(jnp.swapaxes(affinity, 1, 2), cap)
    idx_flat = idx.reshape(B, N_EXPERTS * cap)
    xe = jax.vmap(lambda hb, ib: hb[ib])(h, idx_flat).reshape(B, N_EXPERTS, cap, D)
    ye = expert_ffn(xe, gate, w_gate, w_up, w_down)
    y = jax.vmap(lambda yb, ib: jnp.zeros((N, D), F32).at[ib].add(yb))(
        ye.reshape(B, N_EXPERTS * cap, D), idx_flat)
    return x + gate2 * y


def lambda_init(layer):
    return 0.8 - 0.6 * math.exp(-0.3 * layer)


def even_layer(x, xc, mods, cmods, norm1_g, w_in, w_out, lb, hgrn_g, qn_g, kn_g, lam_vec, subln_g,
               lam_init, ctx_out, tables):
    sh1, sc1, g1 = mods
    csh1, csc1, cg1 = cmods
    B = x.shape[0]
    proj = norm_mod_matmul(x, norm1_g, sh1, sc1, w_in)
    projc = norm_mod_matmul(xc, norm1_g, csh1, csc1, w_in)
    s0 = jnp.zeros((B, HGRN_HEADS, 2, HGRN_HEAD_DIM, HGRN_HEAD_DIM), F32)
    oc_f, oc_b, s_ctx = hgrn_scan(projc, lb, s0)
    o_f, o_b, _ = hgrn_scan(proj, lb, s_ctx)
    lv = lam_vec.astype(F32)
    lam = jnp.exp(jnp.sum(lv[0] * lv[1])) - jnp.exp(jnp.sum(lv[2] * lv[3])) + lam_init
    cos, sin = tables
    q, k, v = qkv_prep(proj, qn_g, kn_g, cos, sin, rotary=True)
    qc, kc, vc = qkv_prep(projc, qn_g, kn_g, cos[:xc.shape[1]], sin[:xc.shape[1]], rotary=False)
    k_all = jnp.concatenate([k, kc], axis=1)
    v_all = jnp.concatenate([v, vc], axis=1)
    y_d = diff_attention(q, k_all, v_all, lam, subln_g, 1.0 - lam_init)
    x_new = even_out(x, o_f, o_b, proj, y_d, hgrn_g.reshape(-1), g1, w_out)
    if not ctx_out:
        return x_new, None
    yc_d = diff_attention(qc, kc, vc, lam, subln_g, 1.0 - lam_init)
    xc_new = even_out(xc, oc_f, oc_b, projc, yc_d, hgrn_g.reshape(-1), cg1, w_out)
    return x_new, xc_new


def conv_layer(x, mods, norm1_g, w_in, conv_w, w_out):
    sh1, sc1, g1 = mods
    proj = norm_mod_matmul(x, norm1_g, sh1, sc1, w_in)
    return conv_out(x, proj, conv_w, g1, w_out)


def kernel(x, c, ctx, c_ctx, mod_w, mod_b, norm1_g, norm2_g, even_w_in, even_w_out, hgrn_lb_logits, hgrn_norm_g,
           diff_qnorm_g, diff_knorm_g, diff_lambda, diff_subln_g, conv_w_in, conv_w, conv_w_out, router_w,
           exp_w_gate, exp_w_up, exp_w_down):
    depth = mod_w.shape[0]
    B, T, D = x.shape
    lb_soft = jax.nn.softmax(hgrn_lb_logits.astype(F32), axis=0)
    lower_bounds = jnp.cumsum(lb_soft, axis=0) - lb_soft[:1]
    last_ctx_layer = 2 * ((depth - 1) // 2)
    cond = jnp.concatenate([c, c_ctx[None, :], jnp.zeros((8 - (B + 1) % 8, D), F32)], axis=0)
    mods = modulation(jax.nn.silu(cond), mod_w, mod_b)
    tables = rope_tables(T)
    xc = ctx
    for l in range(depth):
        read_ctx = l <= last_ctx_layer
        ctx_out = l < last_ctx_layer
        sh1, sc1, g1, sh2, sc2, g2 = [m[:, None, :] for m in jnp.split(mods[l, :B], MOD_CHUNKS, axis=-1)]
        if read_ctx:
            csh1, csc1, cg1, csh2, csc2, cg2 = [
                jnp.broadcast_to(m[None, None, :], (B, 1, D)) for m in jnp.split(mods[l, B], MOD_CHUNKS, axis=-1)]
        ew = (exp_w_gate[l].astype(BF16), exp_w_up[l].astype(BF16), exp_w_down[l].astype(BF16))
        if l % 2 == 0:
            e = l // 2
            x, xc_new = even_layer(x, xc, (sh1, sc1, g1), (csh1, csc1, cg1), norm1_g[l],
                                   even_w_in[e].astype(BF16), even_w_out[e].astype(BF16), lower_bounds[e],
                                   hgrn_norm_g[e], diff_qnorm_g[e], diff_knorm_g[e], diff_lambda[e],
                                   diff_subln_g[e], lambda_init(l), ctx_out, tables)
        else:
            j = l // 2
            wi, wo = conv_w_in[j].astype(BF16), conv_w_out[j].astype(BF16)
            x = conv_layer(x, (sh1, sc1, g1), norm1_g[l], wi, conv_w[j], wo)
            xc_new = conv_layer(xc, (csh1, csc1, cg1), norm1_g[l], wi, conv_w[j], wo) if ctx_out else None
        x = moe_residual(x, norm2_g[l], sh2, sc2, g2, router_w[l], *ew)
        if ctx_out:
            xc = moe_residual(xc_new, norm2_g[l], csh2, csc2, cg2, router_w[l], *ew)
    return x
```

```python
import functools
import math

import numpy as np
import jax
import jax.numpy as jnp
from jax import lax
from jax.experimental import pallas as pl
from jax.experimental.pallas import tpu as pltpu

F32 = jnp.float32
BF16 = jnp.bfloat16

EPS = 1e-6
GRID_W = 64
ROPE_THETA = 10000.0
HGRN_HEAD_DIM = 128
HGRN_HEADS = 4
HGRN_WIDTH = HGRN_HEADS * HGRN_HEAD_DIM
DIFF_HEAD_DIM = 64
DIFF_HEADS = 4
DIFF_WIDTH = DIFF_HEADS * 2 * DIFF_HEAD_DIM
N_EXPERTS = 16
EC_CAPACITY = 2
MOD_CHUNKS = 6
SCAN_CHUNK = 64
SCAN_LEVELS = (32, 16, 8, 4, 2, 1)
LANES = 128
VMEM_LIMIT = 56 * 1024 * 1024


def _cparams(*sem):
    return pltpu.CompilerParams(dimension_semantics=sem, vmem_limit_bytes=VMEM_LIMIT)


def _row_tile(t, want):
    return want if t % want == 0 else t


def _mod_kernel(s_ref, w_ref, b_ref, o_ref):
    s = s_ref[...]
    w = w_ref[0]
    s_hi = s.astype(BF16)
    s_lo = (s - s_hi.astype(F32)).astype(BF16)
    w_hi = w.astype(BF16)
    w_lo = (w - w_hi.astype(F32)).astype(BF16)
    d = lambda a, b: jnp.dot(a, b, preferred_element_type=F32)
    o_ref[0] = d(s_hi, w_hi) + d(s_hi, w_lo) + d(s_lo, w_hi) + b_ref[0]


def modulation(s, mod_w, mod_b, *, tn=512):
    R, D = s.shape
    depth, _, N = mod_w.shape
    return pl.pallas_call(
        _mod_kernel,
        grid=(depth, N // tn),
        in_specs=[pl.BlockSpec((R, D), lambda l, j: (0, 0)),
                  pl.BlockSpec((1, D, tn), lambda l, j: (l, 0, j)),
                  pl.BlockSpec((1, 1, tn), lambda l, j: (l, 0, j))],
        out_specs=pl.BlockSpec((1, R, tn), lambda l, j: (l, 0, j)),
        out_shape=jax.ShapeDtypeStruct((depth, R, N), F32),
        compiler_params=_cparams("parallel", "parallel"),
        name="modulation",
    )(s, mod_w, mod_b.reshape(depth, 1, N))


def _nmm_kernel(x_ref, g_ref, sh_ref, sc_ref, w_ref, o_ref):
    x = x_ref[0]
    ms = jnp.mean(x * x, axis=-1, keepdims=True)
    h = (x * lax.rsqrt(ms + EPS)) * g_ref[...]
    h = h * (1.0 + sc_ref[0]) + sh_ref[0]
    o_ref[0] = jnp.dot(h.astype(BF16), w_ref[0].astype(BF16), preferred_element_type=F32).astype(o_ref.dtype)


def norm_mod_matmul(x, g, shift, scale, w, *, tm=512, tn=1024, out_dtype=F32):
    B, T, D = x.shape
    w, li = w
    N = w.shape[2]
    tm = _row_tile(T, tm)
    tn = _row_tile(N, tn)
    return pl.pallas_call(
        _nmm_kernel,
        grid=(B, T // tm, N // tn),
        in_specs=[
            pl.BlockSpec((1, tm, D), lambda b, i, j: (b, i, 0)),
            pl.BlockSpec((1, D), lambda b, i, j: (0, 0)),
            pl.BlockSpec((1, 1, D), lambda b, i, j: (b, 0, 0)),
            pl.BlockSpec((1, 1, D), lambda b, i, j: (b, 0, 0)),
            pl.BlockSpec((1, D, tn), lambda b, i, j: (li, 0, j)),
        ],
        out_specs=pl.BlockSpec((1, tm, tn), lambda b, i, j: (b, i, j)),
        out_shape=jax.ShapeDtypeStruct((B, T, N), out_dtype),
        compiler_params=_cparams("parallel", "parallel", "arbitrary"),
        name="norm_mod_matmul",
    )(x, g.reshape(1, D), shift, scale, w)


def _scan_constants():
    C = SCAN_CHUNK
    t = np.arange(C)[:, None]
    u = np.arange(C)[None, :]
    mats = [u <= t, u > t]
    masks = []
    for w in SCAN_LEVELS:
        m = (t // (2 * w)) * 2 * w + w - 1
        later = (t // w) % 2 == 1
        mats.append(np.where(later, (u > m) & (u <= t), (u > t) & (u <= m)))
        masks.append(later & ((u // w) % 2 == 0) & (u // (2 * w) == t // (2 * w)))
    masks.append(t == u)
    a_f = np.stack(mats).astype(np.float32)
    m_f = np.stack(masks).astype(np.float32)
    a = np.stack([a_f, a_f[:, ::-1, ::-1]]).reshape(2, -1, C)
    return np.concatenate([a, a], axis=2), np.stack([m_f, m_f[:, ::-1, ::-1]])


def _scan_pair(q, z, v, lb, st, a2, mask_ref, d, later):
    C = SCAN_CHUNK
    W = HGRN_HEAD_DIM
    nl = len(SCAN_LEVELS)
    nt = lambda x, y: lax.dot_general(x, y, (((1,), (1,)), ((), ())), preferred_element_type=F32)
    nn = lambda x, y: jnp.dot(x, y, preferred_element_type=F32)
    e_abs = jnp.exp(-jnp.abs(z))
    r = 1.0 / (1.0 + e_abs)
    er = e_abs * r
    pos = z >= 0.0
    g2 = jnp.log2(lb + (1.0 - lb) * jnp.where(pos, r, er))
    k = (1.0 - lb) * jnp.where(pos, er, r)
    hi = g2.astype(BF16)
    lo = (g2 - hi.astype(F32)).astype(BF16)
    gs = jnp.concatenate([jnp.concatenate([hi[:C], hi[C:]], axis=1),
                          jnp.concatenate([lo[:C], lo[C:]], axis=1)], axis=0)
    x = jnp.exp2(nn(a2, gs))
    first, second = (1, 0) if d else (0, 1)
    vb = v.astype(BF16)
    sc, qt, kt, dec = [], [], [], []
    for c in (0, 1):
        qc, kc, xs = q[c * C:(c + 1) * C], k[c * C:(c + 1) * C], x[:, c * W:(c + 1) * W]
        s = mask_ref[d, nl] * nt(qc.astype(BF16), kc.astype(BF16))
        for i in range(nl):
            qk = (jnp.where(later[i], qc, kc) * xs[(2 + i) * C:(3 + i) * C]).astype(BF16)
            s = s + mask_ref[d, i] * nt(qk, qk)
        sc.append(s.astype(BF16))
        qt.append(qc * xs[0:C])
        kt.append(kc * xs[C:2 * C])
        dec.append(xs[0:1, :] if d else xs[C - 1:C, :])
    cross = nt(qt[second].astype(BF16), kt[first].astype(BF16)).astype(BF16)
    qt[second] = qt[second] * dec[first]
    qtb = [a.astype(BF16) for a in qt]
    kt[first] = kt[first] * dec[second]
    o_st = nt(jnp.concatenate(qtb, axis=0), st.astype(BF16))
    o = [None, None]
    o[first] = o_st[first * C:(first + 1) * C] + nn(sc[first], vb[first * C:(first + 1) * C])
    o[second] = (o_st[second * C:(second + 1) * C] + nn(cross, vb[first * C:(first + 1) * C])
                 + nn(sc[second], vb[second * C:(second + 1) * C]))
    upd = lax.dot_general(vb, jnp.concatenate(kt, axis=0).astype(BF16), (((0,), (0,)), ((), ())),
                          preferred_element_type=F32)
    return jnp.concatenate(o, axis=0), st * (dec[0] * dec[1]) + upd


def _scan_kernel(qf_ref, zf_ref, vf_ref, qb_ref, zb_ref, vb_ref, lbf_ref, lbb_ref, s0_ref, a_ref, mask_ref,
                 of_ref, ob_ref, sT_ref, st_scr, *, n_chunks):
    c = pl.program_id(2)
    C = SCAN_CHUNK

    @pl.when(c == 0)
    def _():
        st_scr[...] = s0_ref[0, 0]

    row = lax.broadcasted_iota(jnp.int32, (C, HGRN_HEAD_DIM), 0)
    later_f = [(row // w) % 2 == 1 for w in SCAN_LEVELS]
    later_b = [((C - 1 - row) // w) % 2 == 1 for w in SCAN_LEVELS]
    lbf = lbf_ref[...]
    lbb = lbb_ref[...]

    n_pairs = n_chunks // 2

    def body(i, carry):
        rf = pl.multiple_of(i * 2 * C, 2 * C)
        rb = pl.multiple_of((n_pairs - 1 - i) * 2 * C, 2 * C)
        fwd_in = (qf_ref[0, pl.ds(rf, 2 * C), :], zf_ref[0, pl.ds(rf, 2 * C), :], vf_ref[0, pl.ds(rf, 2 * C), :])
        bwd_in = (qb_ref[0, pl.ds(rb, 2 * C), :], zb_ref[0, pl.ds(rb, 2 * C), :], vb_ref[0, pl.ds(rb, 2 * C), :])
        sf, sb = st_scr[0], st_scr[1]
        o_f, sf = _scan_pair(*fwd_in, lbf, sf, a_ref[0], mask_ref, 0, later_f)
        o_b, sb = _scan_pair(*bwd_in, lbb, sb, a_ref[1], mask_ref, 1, later_b)
        of_ref[0, pl.ds(rf, 2 * C), :] = o_f
        ob_ref[0, pl.ds(rb, 2 * C), :] = o_b
        st_scr[0] = sf
        st_scr[1] = sb
        return carry

    lax.fori_loop(0, n_pairs, body, 0)

    @pl.when(c == pl.num_programs(2) - 1)
    def _():
        sT_ref[0, 0] = st_scr[...]


def hgrn_scan(proj, lb, s0, *, tb=512):
    B, T, _ = proj.shape
    tb = _row_tile(T, tb)
    nc = T // tb
    H = HGRN_HEADS
    hd = HGRN_HEAD_DIM
    fwd = lambda grp: pl.BlockSpec((1, tb, hd), lambda b, h, c: (b, c, grp * H + h))
    bwd = lambda grp: pl.BlockSpec((1, tb, hd), lambda b, h, c: (b, nc - 1 - c, grp * H + h))
    kern = functools.partial(_scan_kernel, n_chunks=tb // SCAN_CHUNK)
    a2, masks = _scan_constants()
    return pl.pallas_call(
        kern,
        grid=(B, H, nc),
        in_specs=[fwd(0), fwd(1), fwd(3), bwd(0), bwd(2), bwd(3),
                  pl.BlockSpec((1, hd), lambda b, h, c: (0, h)),
                  pl.BlockSpec((1, hd), lambda b, h, c: (0, h)),
                  pl.BlockSpec((1, 1, 2, hd, hd), lambda b, h, c: (b, h, 0, 0, 0)),
                  pl.BlockSpec(a2.shape, lambda b, h, c: (0, 0, 0)),
                  pl.BlockSpec(masks.shape, lambda b, h, c: (0, 0, 0, 0))],
        out_specs=[pl.BlockSpec((1, tb, hd), lambda b, h, c: (b, c, h)),
                   pl.BlockSpec((1, tb, hd), lambda b, h, c: (b, nc - 1 - c, h)),
                   pl.BlockSpec((1, 1, 2, hd, hd), lambda b, h, c: (b, h, 0, 0, 0))],
        out_shape=[jax.ShapeDtypeStruct((B, T, HGRN_WIDTH), F32),
                   jax.ShapeDtypeStruct((B, T, HGRN_WIDTH), F32),
                   jax.ShapeDtypeStruct((B, H, 2, hd, hd), F32)],
        scratch_shapes=[pltpu.VMEM((2, hd, hd), F32)],
        compiler_params=_cparams("parallel", "parallel", "arbitrary"),
        name="hgrn_scan",
    )(proj, proj, proj, proj, proj, proj, lb[0:1], lb[1:2], s0, jnp.asarray(a2, BF16), jnp.asarray(masks, F32))


def _group_mean_sq(x, gmat):
    sq = x * x
    hi = sq.astype(BF16)
    lo = (sq - hi.astype(F32)).astype(BF16)
    return (jnp.dot(hi, gmat, preferred_element_type=F32) + jnp.dot(lo, gmat, preferred_element_type=F32))


def _qkv_prep_kernel(q_ref, k_ref, v_ref, qg_ref, kg_ref, cos_ref, sin_ref, qo_ref, ko_ref, vo_ref, *, rotary):
    W = LANES
    r_i = lax.broadcasted_iota(jnp.int32, (W, W), 0) // DIFF_HEAD_DIM
    c_i = lax.broadcasted_iota(jnp.int32, (W, W), 1) // DIFF_HEAD_DIM
    gmat = jnp.where(r_i == c_i, 1.0 / DIFF_HEAD_DIM, 0.0).astype(BF16)
    lane = lax.broadcasted_iota(jnp.int32, (1, W), 1)
    first = (lane % 32) < 16

    def prep(x, g, scale):
        y = (x * lax.rsqrt(_group_mean_sq(x, gmat) + EPS)) * g
        if rotary:
            partner = jnp.where(first, pltpu.roll(y, W - 16, axis=1), pltpu.roll(y, 16, axis=1))
            y = y * cos_ref[...] + partner * sin_ref[...]
        if scale != 1.0:
            y = y * scale
        return y.astype(BF16)

    for h in range(DIFF_HEADS):
        sl = slice(h * W, (h + 1) * W)
        qo_ref[0, :, sl] = prep(q_ref[0, :, sl], qg_ref[...], DIFF_HEAD_DIM ** -0.5)
        ko_ref[0, :, sl] = prep(k_ref[0, :, sl], kg_ref[...], 1.0)
    vo_ref[0] = v_ref[0].astype(BF16)


def qkv_prep(proj, qg, kg, cos, sin, *, rotary, tm=512):
    B, T, _ = proj.shape
    tm = _row_tile(T, tm)
    Wd = DIFF_WIDTH
    col = lambda j: pl.BlockSpec((1, tm, Wd), lambda b, i: (b, i, j))
    vec = pl.BlockSpec((1, LANES), lambda b, i: (0, 0))
    tab = pl.BlockSpec((tm, LANES), lambda b, i: (i, 0))
    out = pl.BlockSpec((1, tm, Wd), lambda b, i: (b, i, 0))
    sds = jax.ShapeDtypeStruct((B, T, Wd), BF16)
    return pl.pallas_call(
        functools.partial(_qkv_prep_kernel, rotary=rotary),
        grid=(B, T // tm),
        in_specs=[col(5), col(6), col(7), vec, vec, tab, tab],
        out_specs=[out, out, out],
        out_shape=[sds, sds, sds],
        compiler_params=_cparams("parallel", "parallel"),
        name="qkv_prep",
    )(proj, proj, proj, jnp.tile(qg, 2).reshape(1, LANES), jnp.tile(kg, 2).reshape(1, LANES), cos, sin)


def rope_tables(T):
    n = DIFF_HEAD_DIM // 2
    inv = 1.0 / (ROPE_THETA ** (jnp.arange(0, n, 2, dtype=F32) / n))
    t = jnp.arange(T)
    ang_r = (t // GRID_W).astype(F32)[:, None] * inv[None, :]
    ang_c = (t % GRID_W).astype(F32)[:, None] * inv[None, :]
    cos = jnp.concatenate([jnp.cos(ang_r)] * 2 + [jnp.cos(ang_c)] * 2, axis=-1)
    sin = jnp.concatenate([-jnp.sin(ang_r), jnp.sin(ang_r), -jnp.sin(ang_c), jnp.sin(ang_c)], axis=-1)
    return jnp.tile(cos, (1, 2)), jnp.tile(sin, (1, 2))


def _diff_attn_kernel(lam_ref, q_ref, k_ref, v_ref, g_ref, o_ref, *, out_scale):
    q = q_ref[0]
    tq = q.shape[0]
    lane = lax.broadcasted_iota(jnp.int32, (1, LANES), 1)
    zero = jnp.zeros_like(q)
    qq = jnp.concatenate([jnp.where(lane < DIFF_HEAD_DIM, q, zero),
                          jnp.where(lane >= DIFF_HEAD_DIM, q, zero)], axis=0)
    s = lax.dot_general(qq, k_ref[0], (((1,), (1,)), ((), ())), preferred_element_type=F32)
    m = jnp.max(s, axis=-1, keepdims=True)
    p = jnp.exp(s - m)
    inv = 1.0 / jnp.sum(p, axis=-1, keepdims=True)
    w = p[:tq] * inv[:tq] - (lam_ref[0] * inv[tq:]) * p[tq:]
    o = jnp.dot(w.astype(BF16), v_ref[0], preferred_element_type=F32)
    ms = jnp.mean(o * o, axis=-1, keepdims=True)
    o_ref[0] = (o * lax.rsqrt(ms + EPS)) * g_ref[...] * out_scale


def diff_attention(q, k, v, lam, subln_g, out_scale, *, tq=256):
    B, T, Wd = q.shape
    S = k.shape[1]
    tq = _row_tile(T, tq)
    return pl.pallas_call(
        functools.partial(_diff_attn_kernel, out_scale=out_scale),
        grid=(B, DIFF_HEADS, T // tq),
        in_specs=[pl.BlockSpec(memory_space=pltpu.SMEM),
                  pl.BlockSpec((1, tq, LANES), lambda b, h, i: (b, i, h)),
                  pl.BlockSpec((1, S, LANES), lambda b, h, i: (b, 0, h)),
                  pl.BlockSpec((1, S, LANES), lambda b, h, i: (b, 0, h)),
                  pl.BlockSpec((1, LANES), lambda b, h, i: (0, 0))],
        out_specs=pl.BlockSpec((1, tq, LANES), lambda b, h, i: (b, i, h)),
        out_shape=jax.ShapeDtypeStruct((B, T, Wd), F32),
        compiler_params=_cparams("parallel", "parallel", "arbitrary"),
        name="diff_attention",
    )(lam.reshape(1), q, k, v, subln_g.reshape(1, LANES))


def _even_out_kernel(x_ref, of_ref, ob_ref, gate_ref, yd_ref, hg_ref, g1_ref, w_ref, o_ref):
    acc = jnp.dot(yd_ref[0].astype(BF16), w_ref[0, HGRN_WIDTH:, :].astype(BF16), preferred_element_type=F32)
    for h in range(HGRN_HEADS):
        sl = slice(h * HGRN_HEAD_DIM, (h + 1) * HGRN_HEAD_DIM)
        o = of_ref[0, :, sl] + ob_ref[0, :, sl]
        ms = jnp.mean(o * o, axis=-1, keepdims=True)
        gate = gate_ref[0, :, sl]
        yh = (o * lax.rsqrt(ms + EPS)) * hg_ref[:, sl] * (gate * jax.nn.sigmoid(gate))
        acc = acc + jnp.dot(yh.astype(BF16), w_ref[0, sl, :].astype(BF16), preferred_element_type=F32)
    o_ref[0] = x_ref[0] + g1_ref[0] * acc


def even_out(x, o_f, o_b, proj, y_d, hgrn_g, g1, w_out, *, tm=512):
    B, T, D = x.shape
    w_out, li = w_out
    tm = _row_tile(T, tm)
    Wd = HGRN_WIDTH
    row = lambda w, j: pl.BlockSpec((1, tm, w), lambda b, i: (b, i, j))
    return pl.pallas_call(
        _even_out_kernel,
        grid=(B, T // tm),
        in_specs=[row(D, 0), row(Wd, 0), row(Wd, 0), row(Wd, 4), row(Wd, 0),
                  pl.BlockSpec((1, Wd), lambda b, i: (0, 0)),
                  pl.BlockSpec((1, 1, D), lambda b, i: (b, 0, 0)),
                  pl.BlockSpec((1,) + w_out.shape[1:], lambda b, i: (li, 0, 0))],
        out_specs=row(D, 0),
        out_shape=jax.ShapeDtypeStruct((B, T, D), F32),
        compiler_params=_cparams("parallel", "parallel"),
        name="even_out",
    )(x, o_f, o_b, proj, y_d, hgrn_g.reshape(1, Wd), g1, w_out)


def _conv_out_kernel(x_ref, bg_ref, cg_ref, v_ref, cp_ref, vp_ref, cn_ref, vn_ref, cw_ref, g1_ref, w_ref, o_ref):
    i = pl.program_id(1)
    n = pl.num_programs(1)
    u = cg_ref[0] * v_ref[0]
    tm = u.shape[0]
    u_prev_row = jnp.where(i > 0, cp_ref[0, 7:8, :] * vp_ref[0, 7:8, :], 0.0)
    u_next_row = jnp.where(i < n - 1, cn_ref[0, 0:1, :] * vn_ref[0, 0:1, :], 0.0)
    ridx = lax.broadcasted_iota(jnp.int32, (tm, 1), 0)
    u_prev = jnp.where(ridx == 0, u_prev_row, pltpu.roll(u, 1, axis=0))
    u_next = jnp.where(ridx == tm - 1, u_next_row, pltpu.roll(u, tm - 1, axis=0))
    y = cw_ref[0:1, :] * u_prev + cw_ref[1:2, :] * u + cw_ref[2:3, :] * u_next
    acc = jnp.dot((bg_ref[0] * y).astype(BF16), w_ref[0].astype(BF16), preferred_element_type=F32)
    o_ref[0] = x_ref[0] + g1_ref[0] * acc


def conv_out(x, proj, conv_w, g1, w_out, *, tm=512):
    B, T, D = x.shape
    w_out, li = w_out
    tm = _row_tile(T, tm)
    r8 = tm // 8
    last8 = T // 8 - 1
    row = lambda j: pl.BlockSpec((1, tm, D), lambda b, i: (b, i, j))
    prev = lambda j: pl.BlockSpec((1, 8, D), lambda b, i: (b, jnp.maximum(i * r8 - 1, 0), j))
    nxt = lambda j: pl.BlockSpec((1, 8, D), lambda b, i: (b, jnp.minimum((i + 1) * r8, last8), j))
    return pl.pallas_call(
        _conv_out_kernel,
        grid=(B, T // tm),
        in_specs=[row(0), row(0), row(1), row(2), prev(1), prev(2), nxt(1), nxt(2),
                  pl.BlockSpec((8, D), lambda b, i: (0, 0)),
                  pl.BlockSpec((1, 1, D), lambda b, i: (b, 0, 0)),
                  pl.BlockSpec((1,) + w_out.shape[1:], lambda b, i: (li, 0, 0))],
        out_specs=row(0),
        out_shape=jax.ShapeDtypeStruct((B, T, D), F32),
        compiler_params=_cparams("parallel", "parallel"),
        name="conv_out",
    )(x, proj, proj, proj, proj, proj, proj, proj,
      jnp.concatenate([conv_w, jnp.zeros((8 - conv_w.shape[0], D), conv_w.dtype)], axis=0), g1, w_out)


def _router_kernel(x_ref, g_ref, sh_ref, sc_ref, rw_ref, h_ref, lg_ref):
    x = x_ref[0]
    ms = jnp.mean(x * x, axis=-1, keepdims=True)
    h = (x * lax.rsqrt(ms + EPS)) * g_ref[...]
    h = h * (1.0 + sc_ref[0]) + sh_ref[0]
    h_ref[0] = h.astype(BF16)
    rw = rw_ref[...]
    h_hi = h.astype(BF16)
    h_lo = (h - h_hi.astype(F32)).astype(BF16)
    w_hi = rw.astype(BF16)
    w_lo = (rw - w_hi.astype(F32)).astype(BF16)
    d = lambda a, b: jnp.dot(a, b, preferred_element_type=F32)
    lg_ref[0] = d(h_hi, w_hi) + d(h_hi, w_lo) + d(h_lo, w_hi)


def norm_mod_router(x, g, shift, scale, router_w, *, tm=512):
    B, T, D = x.shape
    E = router_w.shape[1]
    tm = _row_tile(T, tm)
    return pl.pallas_call(
        _router_kernel,
        grid=(B, T // tm),
        in_specs=[pl.BlockSpec((1, tm, D), lambda b, i: (b, i, 0)),
                  pl.BlockSpec((1, D), lambda b, i: (0, 0)),
                  pl.BlockSpec((1, 1, D), lambda b, i: (b, 0, 0)),
                  pl.BlockSpec((1, 1, D), lambda b, i: (b, 0, 0)),
                  pl.BlockSpec((D, E), lambda b, i: (0, 0))],
        out_specs=[pl.BlockSpec((1, tm, D), lambda b, i: (b, i, 0)),
                   pl.BlockSpec((1, tm, E), lambda b, i: (b, i, 0))],
        out_shape=[jax.ShapeDtypeStruct((B, T, D), BF16), jax.ShapeDtypeStruct((B, T, E), F32)],
        compiler_params=_cparams("parallel", "parallel"),
        name="norm_mod_router",
    )(x, g.reshape(1, D), shift, scale, router_w)


def _lane_prefix(flags, tri_tot):
    E, N = flags.shape
    carries = [jnp.zeros((E, LANES), F32)]
    out = []
    for j in range(N // LANES):
        r = jnp.dot(flags[:, j * LANES:(j + 1) * LANES].astype(BF16), tri_tot, preferred_element_type=F32)
        out.append(r[:, :LANES] + carries[-1])
        carries.append(carries[-1] + r[:, LANES:])
    return jnp.concatenate(out, axis=1), carries


def _route_kernel(lg_ref, aff_ref, rank_ref, ts_ref, *, cap, tt):
    lg = lg_ref[0]
    E, N = lg.shape
    p = jnp.exp(lg - jnp.max(lg, axis=0, keepdims=True))
    aff = p / jnp.sum(p, axis=0, keepdims=True)
    aff_ref[0] = aff
    bits = pltpu.bitcast(aff, jnp.int32)
    count = lambda m: jnp.sum(jnp.where(m, 1.0, 0.0), axis=1, keepdims=True)

    def refine(i, thr):
        cand = thr | jnp.left_shift(jnp.int32(1), 30 - i)
        return jnp.where(count(bits >= cand) >= cap, cand, thr)

    thr = lax.fori_loop(0, 31, refine, jnp.zeros((E, 1), jnp.int32))
    gt = bits > thr
    eq = bits == thr
    r_i = lax.broadcasted_iota(jnp.int32, (LANES, 2 * LANES), 0)
    c_i = lax.broadcasted_iota(jnp.int32, (LANES, 2 * LANES), 1)
    tri_tot = jnp.where((r_i < c_i) | (c_i >= LANES), 1.0, 0.0).astype(BF16)
    eq_rank, _ = _lane_prefix(jnp.where(eq, 1.0, 0.0), tri_tot)
    sel = gt | (eq & (eq_rank < cap - count(gt)))
    rank, before = _lane_prefix(jnp.where(sel, 1.0, 0.0), tri_tot)
    rank_ref[0] = jnp.where(sel, rank, -1.0).astype(jnp.int32)
    lane = lax.broadcasted_iota(jnp.int32, (E, LANES), 1)
    ts = jnp.zeros((E, LANES), F32)
    for k in range(N // tt + 1):
        ts = jnp.where(lane == k, before[k * tt // LANES], ts)
    ts_ref[0] = ts.astype(jnp.int32)


def route(logits_t, cap, tt):
    B, E, N = logits_t.shape
    blk = pl.BlockSpec((1, E, N), lambda b: (b, 0, 0))
    return pl.pallas_call(
        functools.partial(_route_kernel, cap=cap, tt=tt),
        grid=(B,),
        in_specs=[blk],
        out_specs=[blk, blk, pl.BlockSpec((1, E, LANES), lambda b: (b, 0, 0))],
        out_shape=[jax.ShapeDtypeStruct((B, E, N), F32), jax.ShapeDtypeStruct((B, E, N), jnp.int32),
                   jax.ShapeDtypeStruct((B, E, LANES), jnp.int32)],
        compiler_params=_cparams("parallel"),
        name="route",
    )(logits_t)


def _moe_gather_kernel(ts_ref, rank_ref, h_ref, xe_ref, acc_ref, *, rb, tt):
    b, e = pl.program_id(0), pl.program_id(1)
    N = h_ref.shape[1]
    cap = xe_ref.shape[2]
    nt = N // tt
    base = (b * pl.num_programs(1) + e) * (nt + 1)
    acc_ref[...] = jnp.zeros_like(acc_ref)
    row = lax.broadcasted_iota(jnp.int32, (rb, tt), 0)

    def tile(k, carry):
        lo, hi = ts_ref[base + k], ts_ref[base + k + 1]
        c0 = pl.multiple_of(k * tt, tt)

        def block(j, carry2):
            @pl.when((lo < (j + 1) * rb) & (hi > j * rb))
            def _():
                r0 = pl.multiple_of(j * rb, rb)
                onehot = jnp.where(rank_ref[0, 0, :, pl.ds(c0, tt)] == row + j * rb, 1.0, 0.0).astype(BF16)
                acc_ref[pl.ds(r0, rb), :] += jnp.dot(onehot, h_ref[0, pl.ds(c0, tt), :],
                                                     preferred_element_type=F32)
            return carry2

        return lax.fori_loop(0, cap // rb, block, carry)

    lax.fori_loop(0, nt, tile, 0)
    xe_ref[0, 0] = acc_ref[...].astype(BF16)


def moe_gather(ts_flat, rank, h, cap, *, tt):
    B, E, N = rank.shape
    D = h.shape[2]
    rb = min(LANES, cap)
    return pl.pallas_call(
        functools.partial(_moe_gather_kernel, rb=rb, tt=tt),
        grid_spec=pltpu.PrefetchScalarGridSpec(
            num_scalar_prefetch=1,
            grid=(B, E),
            in_specs=[pl.BlockSpec((1, 1, 1, N), lambda b, e, ts: (b, e, 0, 0)),
                      pl.BlockSpec((1, N, D), lambda b, e, ts: (b, 0, 0))],
            out_specs=pl.BlockSpec((1, 1, cap, D), lambda b, e, ts: (b, e, 0, 0)),
            scratch_shapes=[pltpu.VMEM((cap, D), F32)]),
        out_shape=jax.ShapeDtypeStruct((B, E, cap, D), BF16),
        compiler_params=_cparams("parallel", "arbitrary"),
        name="moe_gather",
    )(ts_flat, rank.reshape(B, E, 1, N), h)


def _expert_kernel(xe_ref, wg_ref, wu_ref, wd_ref, o_ref, wg_scr, wu_scr, wd_scr):
    @pl.when(pl.program_id(1) == 0)
    def _():
        wg_scr[...] = wg_ref[0, 0].astype(BF16)
        wu_scr[...] = wu_ref[0, 0].astype(BF16)
        wd_scr[...] = wd_ref[0, 0].astype(BF16)

    xe = xe_ref[0, 0]
    a = jnp.dot(xe, wg_scr[...], preferred_element_type=F32)
    u = jnp.dot(xe, wu_scr[...], preferred_element_type=F32)
    hid = (a * jax.nn.sigmoid(a)) * u
    o_ref[0, 0] = jnp.dot(hid.astype(BF16), wd_scr[...], preferred_element_type=F32).astype(BF16)


def expert_ffn(xe, w_gate, w_up, w_down, li):
    B, E, cap, D = xe.shape
    FF = w_gate.shape[3]
    return pl.pallas_call(
        _expert_kernel,
        grid=(E, B),
        in_specs=[pl.BlockSpec((1, 1, cap, D), lambda e, b: (b, e, 0, 0)),
                  pl.BlockSpec((1, 1, D, FF), lambda e, b: (li, e, 0, 0)),
                  pl.BlockSpec((1, 1, D, FF), lambda e, b: (li, e, 0, 0)),
                  pl.BlockSpec((1, 1, FF, D), lambda e, b: (li, e, 0, 0))],
        out_specs=pl.BlockSpec((1, 1, cap, D), lambda e, b: (b, e, 0, 0)),
        out_shape=jax.ShapeDtypeStruct((B, E, cap, D), BF16),
        scratch_shapes=[pltpu.VMEM((D, FF), BF16), pltpu.VMEM((D, FF), BF16), pltpu.VMEM((FF, D), BF16)],
        compiler_params=_cparams("arbitrary", "arbitrary"),
        name="expert_ffn",
    )(xe, w_gate, w_up, w_down)


def _moe_combine_kernel(ts_ref, rank_ref, aff_ref, ye_ref, x_ref, g2_ref, o_ref, acc_ref, *, rb):
    b, k = pl.program_id(0), pl.program_id(1)
    E, cap = ye_ref.shape[1], ye_ref.shape[2]
    tt = x_ref.shape[1]
    nt = pl.num_programs(1)
    acc_ref[...] = jnp.zeros_like(acc_ref)
    col = lax.broadcasted_iota(jnp.int32, (tt, rb), 1)
    for e in range(E):
        base = (b * E + e) * (nt + 1) + k
        lo, hi = ts_ref[base], ts_ref[base + 1]
        rank_col = rank_ref[0, :, e:e + 1]
        gate_col = aff_ref[0, :, e:e + 1]
        for j in range(cap // rb):
            @pl.when((lo < (j + 1) * rb) & (hi > j * rb))
            def _():
                onehot = jnp.where(rank_col == col + j * rb, 1.0, 0.0).astype(BF16)
                acc_ref[...] += gate_col * jnp.dot(onehot, ye_ref[0, e, j * rb:(j + 1) * rb, :],
                                                   preferred_element_type=F32)
    o_ref[0] = x_ref[0] + g2_ref[0] * acc_ref[...]


def moe_combine(ts_flat, rank_t, aff_t, ye, x, g2, *, tt):
    B, N, D = x.shape
    E, cap = ye.shape[1], ye.shape[2]
    rb = min(2 * LANES, cap)
    tok = lambda w: pl.BlockSpec((1, tt, w), lambda b, k, ts: (b, k, 0))
    return pl.pallas_call(
        functools.partial(_moe_combine_kernel, rb=rb),
        grid_spec=pltpu.PrefetchScalarGridSpec(
            num_scalar_prefetch=1,
            grid=(B, N // tt),
            in_specs=[tok(E), tok(E),
                      pl.BlockSpec((1, E, cap, D), lambda b, k, ts: (b, 0, 0, 0)),
                      tok(D),
                      pl.BlockSpec((1, 1, D), lambda b, k, ts: (b, 0, 0))],
            out_specs=tok(D),
            scratch_shapes=[pltpu.VMEM((tt, D), F32)]),
        out_shape=jax.ShapeDtypeStruct((B, N, D), F32),
        compiler_params=_cparams("parallel", "arbitrary"),
        name="moe_combine",
    )(ts_flat, rank_t, aff_t, ye, x, g2)


def moe_residual(x, g, shift, scale, gate2, router_w, w_gate, w_up, w_down, li):
    B, N, D = x.shape
    cap = EC_CAPACITY * N // N_EXPERTS
    tt = min(4 * LANES, N)
    h, logits = norm_mod_router(x, g, shift, scale, router_w)
    aff, rank, ts = route(jnp.swapaxes(logits, 1, 2), cap, tt)
    ts_flat = ts[:, :, :N // tt + 1].reshape(-1)
    xe = moe_gather(ts_flat, rank, h, cap, tt=tt)
    ye = expert_ffn(xe, w_gate, w_up, w_down, li)
    return moe_combine(ts_flat, jnp.swapaxes(rank, 1, 2), jnp.swapaxes(aff, 1, 2), ye, x, gate2, tt=tt)


def lambda_init(layer):
    return 0.8 - 0.6 * math.exp(-0.3 * layer)


def even_layer(x, xc, mods, cmods, norm1_g, w_in, w_out, lb, hgrn_g, qn_g, kn_g, lam_vec, subln_g,
               lam_init, ctx_out, tables):
    sh1, sc1, g1 = mods
    csh1, csc1, cg1 = cmods
    B = x.shape[0]
    proj = norm_mod_matmul(x, norm1_g, sh1, sc1, w_in)
    projc = norm_mod_matmul(xc, norm1_g, csh1, csc1, w_in)
    s0 = jnp.zeros((B, HGRN_HEADS, 2, HGRN_HEAD_DIM, HGRN_HEAD_DIM), F32)
    oc_f, oc_b, s_ctx = hgrn_scan(projc, lb, s0)
    o_f, o_b, _ = hgrn_scan(proj, lb, s_ctx)
    lv = lam_vec.astype(F32)
    lam = jnp.exp(jnp.sum(lv[0] * lv[1])) - jnp.exp(jnp.sum(lv[2] * lv[3])) + lam_init
    cos, sin = tables
    q, k, v = qkv_prep(proj, qn_g, kn_g, cos, sin, rotary=True)
    qc, kc, vc = qkv_prep(projc, qn_g, kn_g, cos[:xc.shape[1]], sin[:xc.shape[1]], rotary=False)
    k_all = jnp.concatenate([k, kc], axis=1)
    v_all = jnp.concatenate([v, vc], axis=1)
    y_d = diff_attention(q, k_all, v_all, lam, subln_g, 1.0 - lam_init)
    x_new = even_out(x, o_f, o_b, proj, y_d, hgrn_g.reshape(-1), g1, w_out)
    if not ctx_out:
        return x_new, None
    yc_d = diff_attention(qc, kc, vc, lam, subln_g, 1.0 - lam_init)
    xc_new = even_out(xc, oc_f, oc_b, projc, yc_d, hgrn_g.reshape(-1), cg1, w_out)
    return x_new, xc_new


def conv_layer(x, mods, norm1_g, w_in, conv_w, w_out):
    sh1, sc1, g1 = mods
    proj = norm_mod_matmul(x, norm1_g, sh1, sc1, w_in)
    return conv_out(x, proj, conv_w, g1, w_out)


def kernel(x, c, ctx, c_ctx, mod_w, mod_b, norm1_g, norm2_g, even_w_in, even_w_out, hgrn_lb_logits, hgrn_norm_g,
           diff_qnorm_g, diff_knorm_g, diff_lambda, diff_subln_g, conv_w_in, conv_w, conv_w_out, router_w,
           exp_w_gate, exp_w_up, exp_w_down):
    depth = mod_w.shape[0]
    B, T, D = x.shape
    lb_soft = jax.nn.softmax(hgrn_lb_logits.astype(F32), axis=0)
    lower_bounds = jnp.cumsum(lb_soft, axis=0) - lb_soft[:1]
    last_ctx_layer = 2 * ((depth - 1) // 2)
    cond = jnp.concatenate([c, c_ctx[None, :], jnp.zeros((8 - (B + 1) % 8, D), F32)], axis=0)
    mods = modulation(jax.nn.silu(cond), mod_w, mod_b)
    tables = rope_tables(T)
    xc = ctx
    for l in range(depth):
        read_ctx = l <= last_ctx_layer
        ctx_out = l < last_ctx_layer
        sh1, sc1, g1, sh2, sc2, g2 = [m[:, None, :] for m in jnp.split(mods[l, :B], MOD_CHUNKS, axis=-1)]
        if read_ctx:
            csh1, csc1, cg1, csh2, csc2, cg2 = [
                jnp.broadcast_to(m[None, None, :], (B, 1, D)) for m in jnp.split(mods[l, B], MOD_CHUNKS, axis=-1)]
        ew = (exp_w_gate, exp_w_up, exp_w_down, l)
        if l % 2 == 0:
            e = l // 2
            x, xc_new = even_layer(x, xc, (sh1, sc1, g1), (csh1, csc1, cg1), norm1_g[l],
                                   (even_w_in, e), (even_w_out, e), lower_bounds[e],
                                   hgrn_norm_g[e], diff_qnorm_g[e], diff_knorm_g[e], diff_lambda[e],
                                   diff_subln_g[e], lambda_init(l), ctx_out, tables)
        else:
            j = l // 2
            wi, wo = (conv_w_in, j), (conv_w_out, j)
            x = conv_layer(x, (sh1, sc1, g1), norm1_g[l], wi, conv_w[j], wo)
            xc_new = conv_layer(xc, (csh1, csc1, cg1), norm1_g[l], wi, conv_w[j], wo) if ctx_out else None
        x = moe_residual(x, norm2_g[l], sh2, sc2, g2, router_w[l], *ew)
        if ctx_out:
            xc = moe_residual(xc_new, norm2_g[l], csh2, csc2, cg2, router_w[l], *ew)
    return x
```

```python
import functools
import math

import numpy as np
import jax
import jax.numpy as jnp
from jax import lax
from jax.experimental import pallas as pl
from jax.experimental.pallas import tpu as pltpu

F32 = jnp.float32
BF16 = jnp.bfloat16

EPS = 1e-6
GRID_W = 64
ROPE_THETA = 10000.0
HGRN_HEAD_DIM = 128
HGRN_HEADS = 4
HGRN_WIDTH = HGRN_HEADS * HGRN_HEAD_DIM
DIFF_HEAD_DIM = 64
DIFF_HEADS = 4
DIFF_WIDTH = DIFF_HEADS * 2 * DIFF_HEAD_DIM
N_EXPERTS = 16
EC_CAPACITY = 2
MOD_CHUNKS = 6
SCAN_CHUNK = 64
SCAN_LEVELS = (32, 16, 8, 4, 2, 1)
LANES = 128
VMEM_LIMIT = 56 * 1024 * 1024


def _cparams(*sem):
    return pltpu.CompilerParams(dimension_semantics=sem, vmem_limit_bytes=VMEM_LIMIT)


def _row_tile(t, want):
    return want if t % want == 0 else t


def _mod_kernel(s_ref, w_ref, b_ref, o_ref):
    s = s_ref[...]
    w = w_ref[0]
    s_hi = s.astype(BF16)
    s_lo = (s - s_hi.astype(F32)).astype(BF16)
    w_hi = w.astype(BF16)
    w_lo = (w - w_hi.astype(F32)).astype(BF16)
    d = lambda a, b: jnp.dot(a, b, preferred_element_type=F32)
    o_ref[0] = d(s_hi, w_hi) + d(s_hi, w_lo) + d(s_lo, w_hi) + b_ref[0]


def modulation(s, mod_w, mod_b, *, tn=512):
    R, D = s.shape
    depth, _, N = mod_w.shape
    return pl.pallas_call(
        _mod_kernel,
        grid=(depth, N // tn),
        in_specs=[pl.BlockSpec((R, D), lambda l, j: (0, 0)),
                  pl.BlockSpec((1, D, tn), lambda l, j: (l, 0, j)),
                  pl.BlockSpec((1, 1, tn), lambda l, j: (l, 0, j))],
        out_specs=pl.BlockSpec((1, R, tn), lambda l, j: (l, 0, j)),
        out_shape=jax.ShapeDtypeStruct((depth, R, N), F32),
        compiler_params=_cparams("parallel", "parallel"),
        name="modulation",
    )(s, mod_w, mod_b.reshape(depth, 1, N))


def _nmm_kernel(x_ref, g_ref, sh_ref, sc_ref, w_ref, o_ref, w_scr):
    @pl.when((pl.program_id(1) == 0) & (pl.program_id(2) == 0))
    def _():
        w_scr[...] = w_ref[0].astype(BF16)

    x = x_ref[0]
    ms = jnp.mean(x * x, axis=-1, keepdims=True)
    h = (x * lax.rsqrt(ms + EPS)) * g_ref[...]
    h = h * (1.0 + sc_ref[0]) + sh_ref[0]
    o_ref[0] = jnp.dot(h.astype(BF16), w_scr[...], preferred_element_type=F32).astype(o_ref.dtype)


def norm_mod_matmul(x, g, shift, scale, w, *, tm=512, tn=1024, out_dtype=F32):
    B, T, D = x.shape
    w, li = w
    N = w.shape[2]
    tm = _row_tile(T, tm)
    tn = _row_tile(N, tn)
    return pl.pallas_call(
        _nmm_kernel,
        grid=(N // tn, B, T // tm),
        in_specs=[
            pl.BlockSpec((1, tm, D), lambda j, b, i: (b, i, 0)),
            pl.BlockSpec((1, D), lambda j, b, i: (0, 0)),
            pl.BlockSpec((1, 1, D), lambda j, b, i: (b, 0, 0)),
            pl.BlockSpec((1, 1, D), lambda j, b, i: (b, 0, 0)),
            pl.BlockSpec((1, D, tn), lambda j, b, i: (li, 0, j)),
        ],
        out_specs=pl.BlockSpec((1, tm, tn), lambda j, b, i: (b, i, j)),
        out_shape=jax.ShapeDtypeStruct((B, T, N), out_dtype),
        scratch_shapes=[pltpu.VMEM((D, tn), BF16)],
        compiler_params=_cparams("arbitrary", "arbitrary", "arbitrary"),
        name="norm_mod_matmul",
    )(x, g.reshape(1, D), shift, scale, w)


def _scan_constants():
    C = SCAN_CHUNK
    t = np.arange(C)[:, None]
    u = np.arange(C)[None, :]
    mats = [u <= t, u > t]
    masks = []
    for w in SCAN_LEVELS:
        m = (t // (2 * w)) * 2 * w + w - 1
        later = (t // w) % 2 == 1
        mats.append(np.where(later, (u > m) & (u <= t), (u > t) & (u <= m)))
        masks.append(later & ((u // w) % 2 == 0) & (u // (2 * w) == t // (2 * w)))
    masks.append(t == u)
    a_f = np.stack(mats).astype(np.float32)
    m_f = np.stack(masks).astype(np.float32)
    a = np.stack([a_f, a_f[:, ::-1, ::-1]]).reshape(2, -1, C)
    return np.concatenate([a, a], axis=2), np.stack([m_f, m_f[:, ::-1, ::-1]])


def _scan_pair(q, z, v, lb, st, a2, mask_ref, d, later):
    C = SCAN_CHUNK
    W = HGRN_HEAD_DIM
    nl = len(SCAN_LEVELS)
    nt = lambda x, y: lax.dot_general(x, y, (((1,), (1,)), ((), ())), preferred_element_type=F32)
    nn = lambda x, y: jnp.dot(x, y, preferred_element_type=F32)
    e_abs = jnp.exp(-jnp.abs(z))
    r = 1.0 / (1.0 + e_abs)
    er = e_abs * r
    pos = z >= 0.0
    g2 = jnp.log2(lb + (1.0 - lb) * jnp.where(pos, r, er))
    k = (1.0 - lb) * jnp.where(pos, er, r)
    hi = g2.astype(BF16)
    lo = (g2 - hi.astype(F32)).astype(BF16)
    gs = jnp.concatenate([jnp.concatenate([hi[:C], hi[C:]], axis=1),
                          jnp.concatenate([lo[:C], lo[C:]], axis=1)], axis=0)
    x = jnp.exp2(nn(a2, gs))
    first, second = (1, 0) if d else (0, 1)
    vb = v.astype(BF16)
    sc, qt, kt, dec = [], [], [], []
    for c in (0, 1):
        qc, kc, xs = q[c * C:(c + 1) * C], k[c * C:(c + 1) * C], x[:, c * W:(c + 1) * W]
        s = mask_ref[d, nl] * nt(qc.astype(BF16), kc.astype(BF16))
        for i in range(nl):
            qk = (jnp.where(later[i], qc, kc) * xs[(2 + i) * C:(3 + i) * C]).astype(BF16)
            s = s + mask_ref[d, i] * nt(qk, qk)
        sc.append(s.astype(BF16))
        qt.append(qc * xs[0:C])
        kt.append(kc * xs[C:2 * C])
        dec.append(xs[0:1, :] if d else xs[C - 1:C, :])
    cross = nt(qt[second].astype(BF16), kt[first].astype(BF16)).astype(BF16)
    qt[second] = qt[second] * dec[first]
    qtb = [a.astype(BF16) for a in qt]
    kt[first] = kt[first] * dec[second]
    o_st = nt(jnp.concatenate(qtb, axis=0), st.astype(BF16))
    o = [None, None]
    o[first] = o_st[first * C:(first + 1) * C] + nn(sc[first], vb[first * C:(first + 1) * C])
    o[second] = (o_st[second * C:(second + 1) * C] + nn(cross, vb[first * C:(first + 1) * C])
                 + nn(sc[second], vb[second * C:(second + 1) * C]))
    upd = lax.dot_general(vb, jnp.concatenate(kt, axis=0).astype(BF16), (((0,), (0,)), ((), ())),
                          preferred_element_type=F32)
    return jnp.concatenate(o, axis=0), st * (dec[0] * dec[1]) + upd


def _scan_kernel(qf_ref, zf_ref, vf_ref, qb_ref, zb_ref, vb_ref, lbf_ref, lbb_ref, s0_ref, a_ref, mask_ref,
                 of_ref, ob_ref, sT_ref, st_scr, *, n_chunks):
    c = pl.program_id(2)
    C = SCAN_CHUNK

    @pl.when(c == 0)
    def _():
        st_scr[...] = s0_ref[0, 0]

    row = lax.broadcasted_iota(jnp.int32, (C, HGRN_HEAD_DIM), 0)
    later_f = [(row // w) % 2 == 1 for w in SCAN_LEVELS]
    later_b = [((C - 1 - row) // w) % 2 == 1 for w in SCAN_LEVELS]
    lbf = lbf_ref[...]
    lbb = lbb_ref[...]

    n_pairs = n_chunks // 2

    def body(i, carry):
        rf = pl.multiple_of(i * 2 * C, 2 * C)
        rb = pl.multiple_of((n_pairs - 1 - i) * 2 * C, 2 * C)
        fwd_in = (qf_ref[0, pl.ds(rf, 2 * C), :], zf_ref[0, pl.ds(rf, 2 * C), :], vf_ref[0, pl.ds(rf, 2 * C), :])
        bwd_in = (qb_ref[0, pl.ds(rb, 2 * C), :], zb_ref[0, pl.ds(rb, 2 * C), :], vb_ref[0, pl.ds(rb, 2 * C), :])
        sf, sb = st_scr[0], st_scr[1]
        o_f, sf = _scan_pair(*fwd_in, lbf, sf, a_ref[0], mask_ref, 0, later_f)
        o_b, sb = _scan_pair(*bwd_in, lbb, sb, a_ref[1], mask_ref, 1, later_b)
        of_ref[0, pl.ds(rf, 2 * C), :] = o_f
        ob_ref[0, pl.ds(rb, 2 * C), :] = o_b
        st_scr[0] = sf
        st_scr[1] = sb
        return carry

    lax.fori_loop(0, n_pairs, body, 0)

    @pl.when(c == pl.num_programs(2) - 1)
    def _():
        sT_ref[0, 0] = st_scr[...]


def hgrn_scan(proj, lb, s0, *, tb=512):
    B, T, _ = proj.shape
    tb = _row_tile(T, tb)
    nc = T // tb
    H = HGRN_HEADS
    hd = HGRN_HEAD_DIM
    fwd = lambda grp: pl.BlockSpec((1, tb, hd), lambda b, h, c: (b, c, grp * H + h))
    bwd = lambda grp: pl.BlockSpec((1, tb, hd), lambda b, h, c: (b, nc - 1 - c, grp * H + h))
    kern = functools.partial(_scan_kernel, n_chunks=tb // SCAN_CHUNK)
    a2, masks = _scan_constants()
    return pl.pallas_call(
        kern,
        grid=(B, H, nc),
        in_specs=[fwd(0), fwd(1), fwd(3), bwd(0), bwd(2), bwd(3),
                  pl.BlockSpec((1, hd), lambda b, h, c: (0, h)),
                  pl.BlockSpec((1, hd), lambda b, h, c: (0, h)),
                  pl.BlockSpec((1, 1, 2, hd, hd), lambda b, h, c: (b, h, 0, 0, 0)),
                  pl.BlockSpec(a2.shape, lambda b, h, c: (0, 0, 0)),
                  pl.BlockSpec(masks.shape, lambda b, h, c: (0, 0, 0, 0))],
        out_specs=[pl.BlockSpec((1, tb, hd), lambda b, h, c: (b, c, h)),
                   pl.BlockSpec((1, tb, hd), lambda b, h, c: (b, nc - 1 - c, h)),
                   pl.BlockSpec((1, 1, 2, hd, hd), lambda b, h, c: (b, h, 0, 0, 0))],
        out_shape=[jax.ShapeDtypeStruct((B, T, HGRN_WIDTH), F32),
                   jax.ShapeDtypeStruct((B, T, HGRN_WIDTH), F32),
                   jax.ShapeDtypeStruct((B, H, 2, hd, hd), F32)],
        scratch_shapes=[pltpu.VMEM((2, hd, hd), F32)],
        compiler_params=_cparams("parallel", "parallel", "arbitrary"),
        name="hgrn_scan",
    )(proj, proj, proj, proj, proj, proj, lb[0:1], lb[1:2], s0, jnp.asarray(a2, BF16), jnp.asarray(masks, F32))


def _group_mean_sq(x, gmat):
    sq = x * x
    hi = sq.astype(BF16)
    lo = (sq - hi.astype(F32)).astype(BF16)
    return (jnp.dot(hi, gmat, preferred_element_type=F32) + jnp.dot(lo, gmat, preferred_element_type=F32))


def _qkv_prep_kernel(q_ref, k_ref, v_ref, qg_ref, kg_ref, cos_ref, sin_ref, qo_ref, ko_ref, vo_ref, *, rotary):
    W = LANES
    r_i = lax.broadcasted_iota(jnp.int32, (W, W), 0) // DIFF_HEAD_DIM
    c_i = lax.broadcasted_iota(jnp.int32, (W, W), 1) // DIFF_HEAD_DIM
    gmat = jnp.where(r_i == c_i, 1.0 / DIFF_HEAD_DIM, 0.0).astype(BF16)
    lane = lax.broadcasted_iota(jnp.int32, (1, W), 1)
    first = (lane % 32) < 16

    def prep(x, g, scale):
        y = (x * lax.rsqrt(_group_mean_sq(x, gmat) + EPS)) * g
        if rotary:
            partner = jnp.where(first, pltpu.roll(y, W - 16, axis=1), pltpu.roll(y, 16, axis=1))
            y = y * cos_ref[...] + partner * sin_ref[...]
        if scale != 1.0:
            y = y * scale
        return y

    for h in range(DIFF_HEADS):
        sl = slice(h * W, (h + 1) * W)
        qo_ref[0, sl, :] = prep(q_ref[0, :, sl], qg_ref[...], DIFF_HEAD_DIM ** -0.5).T.astype(BF16)
        ko_ref[0, :, sl] = prep(k_ref[0, :, sl], kg_ref[...], 1.0).astype(BF16)
        vo_ref[0, sl, :] = v_ref[0, :, sl].T.astype(BF16)


def qkv_prep(proj, qg, kg, cos, sin, *, rotary, tm=512):
    B, T, _ = proj.shape
    tm = _row_tile(T, tm)
    Wd = DIFF_WIDTH
    col = lambda j: pl.BlockSpec((1, tm, Wd), lambda b, i: (b, i, j))
    vec = pl.BlockSpec((1, LANES), lambda b, i: (0, 0))
    tab = pl.BlockSpec((tm, LANES), lambda b, i: (i, 0))
    rows = pl.BlockSpec((1, tm, Wd), lambda b, i: (b, i, 0))
    cols = pl.BlockSpec((1, Wd, tm), lambda b, i: (b, 0, i))
    return pl.pallas_call(
        functools.partial(_qkv_prep_kernel, rotary=rotary),
        grid=(B, T // tm),
        in_specs=[col(5), col(6), col(7), vec, vec, tab, tab],
        out_specs=[cols, rows, cols],
        out_shape=[jax.ShapeDtypeStruct((B, Wd, T), BF16), jax.ShapeDtypeStruct((B, T, Wd), BF16),
                   jax.ShapeDtypeStruct((B, Wd, T), BF16)],
        compiler_params=_cparams("parallel", "parallel"),
        name="qkv_prep",
    )(proj, proj, proj, jnp.tile(qg, 2).reshape(1, LANES), jnp.tile(kg, 2).reshape(1, LANES), cos, sin)


def rope_tables(T):
    n = DIFF_HEAD_DIM // 2
    inv = 1.0 / (ROPE_THETA ** (jnp.arange(0, n, 2, dtype=F32) / n))
    t = jnp.arange(T)
    ang_r = (t // GRID_W).astype(F32)[:, None] * inv[None, :]
    ang_c = (t % GRID_W).astype(F32)[:, None] * inv[None, :]
    cos = jnp.concatenate([jnp.cos(ang_r)] * 2 + [jnp.cos(ang_c)] * 2, axis=-1)
    sin = jnp.concatenate([-jnp.sin(ang_r), jnp.sin(ang_r), -jnp.sin(ang_c), jnp.sin(ang_c)], axis=-1)
    return jnp.tile(cos, (1, 2)), jnp.tile(sin, (1, 2))


def _diff_attn_kernel(lam_ref, q_ref, k_ref, v_ref, g_ref, o_ref, *, out_scale, key_chunk):
    qt = q_ref[0]
    tq = qt.shape[1]
    S = k_ref.shape[1]
    row = lax.broadcasted_iota(jnp.int32, (LANES, 1), 0)
    zero = jnp.zeros_like(qt)
    qq = jnp.concatenate([jnp.where(row < DIFF_HEAD_DIM, qt, zero),
                          jnp.where(row >= DIFF_HEAD_DIM, qt, zero)], axis=1)
    m = jnp.full((1, 2 * tq), -jnp.inf, F32)
    acc = [jnp.zeros((LANES + 16, tq), F32), jnp.zeros((LANES + 16, tq), F32)]
    ones = jnp.ones((16, key_chunk), BF16)
    chunks = [(c0, min(c0 + key_chunk, S)) for c0 in range(0, S, key_chunk)]
    scores = lambda c: jnp.dot(k_ref[0, c[0]:c[1], :], qq, preferred_element_type=F32)
    s_next = scores(chunks[0])
    for n, (c0, c1) in enumerate(chunks):
        s = s_next
        if n + 1 < len(chunks):
            s_next = scores(chunks[n + 1])
        m_new = jnp.maximum(m, jnp.max(s, axis=0, keepdims=True))
        alpha = jnp.exp(m - m_new)
        pb = jnp.exp((s - m_new).astype(BF16))
        vt1 = jnp.concatenate([v_ref[0, :, c0:c1], ones[:, :c1 - c0]], axis=0)
        for i in range(2):
            acc[i] = acc[i] * alpha[:, i * tq:(i + 1) * tq] + jnp.dot(
                vt1, pb[:, i * tq:(i + 1) * tq], preferred_element_type=F32)
        m = m_new
    inv = [1.0 / a[LANES:LANES + 1] for a in acc]
    o = acc[0][:LANES] * inv[0] - acc[1][:LANES] * (lam_ref[0] * inv[1])
    ms = jnp.mean(o * o, axis=0, keepdims=True)
    o_ref[0] = ((o * lax.rsqrt(ms + EPS)) * g_ref[...] * out_scale).T


def diff_attention(qt, k, vt, lam, subln_g, out_scale, *, tq=256, key_chunk=512):
    B, Wd, T = qt.shape
    S = k.shape[1]
    tq = _row_tile(T, tq)
    return pl.pallas_call(
        functools.partial(_diff_attn_kernel, out_scale=out_scale, key_chunk=key_chunk),
        grid=(B, DIFF_HEADS, T // tq),
        in_specs=[pl.BlockSpec(memory_space=pltpu.SMEM),
                  pl.BlockSpec((1, LANES, tq), lambda b, h, i: (b, h, i)),
                  pl.BlockSpec((1, S, LANES), lambda b, h, i: (b, 0, h)),
                  pl.BlockSpec((1, LANES, S), lambda b, h, i: (b, h, 0)),
                  pl.BlockSpec((LANES, 1), lambda b, h, i: (0, 0))],
        out_specs=pl.BlockSpec((1, tq, LANES), lambda b, h, i: (b, i, h)),
        out_shape=jax.ShapeDtypeStruct((B, T, Wd), F32),
        compiler_params=_cparams("parallel", "parallel", "arbitrary"),
        name="diff_attention",
    )(lam.reshape(1), qt, k, vt, subln_g.reshape(LANES, 1))


def _even_out_kernel(x_ref, of_ref, ob_ref, gate_ref, yd_ref, hg_ref, g1_ref, w_ref, o_ref):
    acc = jnp.dot(yd_ref[0].astype(BF16), w_ref[0, HGRN_WIDTH:, :].astype(BF16), preferred_element_type=F32)
    for h in range(HGRN_HEADS):
        sl = slice(h * HGRN_HEAD_DIM, (h + 1) * HGRN_HEAD_DIM)
        o = of_ref[0, :, sl] + ob_ref[0, :, sl]
        ms = jnp.mean(o * o, axis=-1, keepdims=True)
        gate = gate_ref[0, :, sl]
        yh = (o * lax.rsqrt(ms + EPS)) * hg_ref[:, sl] * (gate * jax.nn.sigmoid(gate))
        acc = acc + jnp.dot(yh.astype(BF16), w_ref[0, sl, :].astype(BF16), preferred_element_type=F32)
    o_ref[0] = x_ref[0] + g1_ref[0] * acc


def even_out(x, o_f, o_b, proj, y_d, hgrn_g, g1, w_out, *, tm=512):
    B, T, D = x.shape
    w_out, li = w_out
    tm = _row_tile(T, tm)
    Wd = HGRN_WIDTH
    row = lambda w, j: pl.BlockSpec((1, tm, w), lambda b, i: (b, i, j))
    return pl.pallas_call(
        _even_out_kernel,
        grid=(B, T // tm),
        in_specs=[row(D, 0), row(Wd, 0), row(Wd, 0), row(Wd, 4), row(Wd, 0),
                  pl.BlockSpec((1, Wd), lambda b, i: (0, 0)),
                  pl.BlockSpec((1, 1, D), lambda b, i: (b, 0, 0)),
                  pl.BlockSpec((1,) + w_out.shape[1:], lambda b, i: (li, 0, 0))],
        out_specs=row(D, 0),
        out_shape=jax.ShapeDtypeStruct((B, T, D), F32),
        compiler_params=_cparams("parallel", "parallel"),
        name="even_out",
    )(x, o_f, o_b, proj, y_d, hgrn_g.reshape(1, Wd), g1, w_out)


def _conv_out_kernel(x_ref, bg_ref, cg_ref, v_ref, cp_ref, vp_ref, cn_ref, vn_ref, cw_ref, g1_ref, w_ref, o_ref):
    i = pl.program_id(1)
    n = pl.num_programs(1)
    u = cg_ref[0] * v_ref[0]
    tm = u.shape[0]
    u_prev_row = jnp.where(i > 0, cp_ref[0, 7:8, :] * vp_ref[0, 7:8, :], 0.0)
    u_next_row = jnp.where(i < n - 1, cn_ref[0, 0:1, :] * vn_ref[0, 0:1, :], 0.0)
    ridx = lax.broadcasted_iota(jnp.int32, (tm, 1), 0)
    u_prev = jnp.where(ridx == 0, u_prev_row, pltpu.roll(u, 1, axis=0))
    u_next = jnp.where(ridx == tm - 1, u_next_row, pltpu.roll(u, tm - 1, axis=0))
    y = cw_ref[0:1, :] * u_prev + cw_ref[1:2, :] * u + cw_ref[2:3, :] * u_next
    acc = jnp.dot((bg_ref[0] * y).astype(BF16), w_ref[0].astype(BF16), preferred_element_type=F32)
    o_ref[0] = x_ref[0] + g1_ref[0] * acc


def conv_out(x, proj, conv_w, g1, w_out, *, tm=512):
    B, T, D = x.shape
    w_out, li = w_out
    tm = _row_tile(T, tm)
    r8 = tm // 8
    last8 = T // 8 - 1
    row = lambda j: pl.BlockSpec((1, tm, D), lambda b, i: (b, i, j))
    prev = lambda j: pl.BlockSpec((1, 8, D), lambda b, i: (b, jnp.maximum(i * r8 - 1, 0), j))
    nxt = lambda j: pl.BlockSpec((1, 8, D), lambda b, i: (b, jnp.minimum((i + 1) * r8, last8), j))
    return pl.pallas_call(
        _conv_out_kernel,
        grid=(B, T // tm),
        in_specs=[row(0), row(0), row(1), row(2), prev(1), prev(2), nxt(1), nxt(2),
                  pl.BlockSpec((8, D), lambda b, i: (0, 0)),
                  pl.BlockSpec((1, 1, D), lambda b, i: (b, 0, 0)),
                  pl.BlockSpec((1,) + w_out.shape[1:], lambda b, i: (li, 0, 0))],
        out_specs=row(0),
        out_shape=jax.ShapeDtypeStruct((B, T, D), F32),
        compiler_params=_cparams("parallel", "parallel"),
        name="conv_out",
    )(x, proj, proj, proj, proj, proj, proj, proj,
      jnp.concatenate([conv_w, jnp.zeros((8 - conv_w.shape[0], D), conv_w.dtype)], axis=0), g1, w_out)


def _router_kernel(x_ref, g_ref, sh_ref, sc_ref, rw_ref, h_ref, lg_ref):
    x = x_ref[0]
    ms = jnp.mean(x * x, axis=-1, keepdims=True)
    h = (x * lax.rsqrt(ms + EPS)) * g_ref[...]
    h = h * (1.0 + sc_ref[0]) + sh_ref[0]
    h_ref[0] = h.astype(BF16)
    rw = rw_ref[...]
    h_hi = h.astype(BF16)
    h_lo = (h - h_hi.astype(F32)).astype(BF16)
    w_hi = rw.astype(BF16)
    w_lo = (rw - w_hi.astype(F32)).astype(BF16)
    d = lambda a, b: jnp.dot(a, b, preferred_element_type=F32)
    lg_ref[0] = d(h_hi, w_hi) + d(h_hi, w_lo) + d(h_lo, w_hi)


def norm_mod_router(x, g, shift, scale, router_w, *, tm=512):
    B, T, D = x.shape
    E = router_w.shape[1]
    tm = _row_tile(T, tm)
    return pl.pallas_call(
        _router_kernel,
        grid=(B, T // tm),
        in_specs=[pl.BlockSpec((1, tm, D), lambda b, i: (b, i, 0)),
                  pl.BlockSpec((1, D), lambda b, i: (0, 0)),
                  pl.BlockSpec((1, 1, D), lambda b, i: (b, 0, 0)),
                  pl.BlockSpec((1, 1, D), lambda b, i: (b, 0, 0)),
                  pl.BlockSpec((D, E), lambda b, i: (0, 0))],
        out_specs=[pl.BlockSpec((1, tm, D), lambda b, i: (b, i, 0)),
                   pl.BlockSpec((1, tm, E), lambda b, i: (b, i, 0))],
        out_shape=[jax.ShapeDtypeStruct((B, T, D), BF16), jax.ShapeDtypeStruct((B, T, E), F32)],
        compiler_params=_cparams("parallel", "parallel"),
        name="norm_mod_router",
    )(x, g.reshape(1, D), shift, scale, router_w)


def _lane_prefix(flags, tri_tot):
    E, N = flags.shape
    carries = [jnp.zeros((E, LANES), F32)]
    out = []
    for j in range(N // LANES):
        r = jnp.dot(flags[:, j * LANES:(j + 1) * LANES].astype(BF16), tri_tot, preferred_element_type=F32)
        out.append(r[:, :LANES] + carries[-1])
        carries.append(carries[-1] + r[:, LANES:])
    return jnp.concatenate(out, axis=1), carries


def _route_kernel(lg_ref, aff_ref, rank_ref, ts_ref, *, cap, tt):
    lg = lg_ref[0]
    E, N = lg.shape
    p = jnp.exp(lg - jnp.max(lg, axis=0, keepdims=True))
    aff = p / jnp.sum(p, axis=0, keepdims=True)
    aff_ref[0] = aff
    count = lambda m: jnp.sum(jnp.where(m, 1.0, 0.0), axis=1, keepdims=True)
    as_float = lambda i: pltpu.bitcast(i, F32)

    def refine_bits(i, thr):
        cand = thr | jnp.left_shift(jnp.int32(1), 30 - i)
        return jnp.where(count(aff >= as_float(cand)) >= cap, cand, thr)

    thr = lax.fori_loop(0, 31, refine_bits, jnp.zeros((E, 1), jnp.int32))

    def refine_mid(i, lo_hi):
        lo, hi = lo_hi
        mid = 0.5 * (lo + hi)
        up = count(aff >= mid) >= cap
        return jnp.where(up, mid, lo), jnp.where(up, hi, mid)

    lo, hi = lax.fori_loop(0, 24, refine_mid, (as_float(thr), as_float(jnp.maximum(thr + 1, 0x00800000))))
    gt = aff >= hi
    eq = (aff >= lo) & (aff < hi)
    r_i = lax.broadcasted_iota(jnp.int32, (LANES, 2 * LANES), 0)
    c_i = lax.broadcasted_iota(jnp.int32, (LANES, 2 * LANES), 1)
    tri_tot = jnp.where((r_i < c_i) | (c_i >= LANES), 1.0, 0.0).astype(BF16)
    eq_rank, _ = _lane_prefix(jnp.where(eq, 1.0, 0.0), tri_tot)
    sel = gt | (eq & (eq_rank < cap - count(gt)))
    rank, before = _lane_prefix(jnp.where(sel, 1.0, 0.0), tri_tot)
    rank_ref[0] = jnp.where(sel, rank, -1.0).astype(jnp.int32)
    lane = lax.broadcasted_iota(jnp.int32, (E, LANES), 1)
    ts = jnp.zeros((E, LANES), F32)
    for k in range(N // tt + 1):
        ts = jnp.where(lane == k, before[k * tt // LANES], ts)
    ts_ref[0] = ts.astype(jnp.int32)


def route(logits_t, cap, tt):
    B, E, N = logits_t.shape
    blk = pl.BlockSpec((1, E, N), lambda b: (b, 0, 0))
    return pl.pallas_call(
        functools.partial(_route_kernel, cap=cap, tt=tt),
        grid=(B,),
        in_specs=[blk],
        out_specs=[blk, blk, pl.BlockSpec((1, E, LANES), lambda b: (b, 0, 0))],
        out_shape=[jax.ShapeDtypeStruct((B, E, N), F32), jax.ShapeDtypeStruct((B, E, N), jnp.int32),
                   jax.ShapeDtypeStruct((B, E, LANES), jnp.int32)],
        compiler_params=_cparams("parallel"),
        name="route",
    )(logits_t)


def _moe_gather_kernel(ts_ref, rank_ref, h_ref, xe_ref, acc_ref, *, rb, tt):
    b, e = pl.program_id(0), pl.program_id(1)
    N = h_ref.shape[1]
    cap = xe_ref.shape[2]
    nt = N // tt
    base = (b * pl.num_programs(1) + e) * (nt + 1)
    acc_ref[...] = jnp.zeros_like(acc_ref)
    row = lax.broadcasted_iota(jnp.int32, (rb, tt), 0)

    def tile(k, carry):
        lo, hi = ts_ref[base + k], ts_ref[base + k + 1]
        c0 = pl.multiple_of(k * tt, tt)

        def block(j, carry2):
            @pl.when((lo < (j + 1) * rb) & (hi > j * rb))
            def _():
                r0 = pl.multiple_of(j * rb, rb)
                onehot = jnp.where(rank_ref[0, 0, :, pl.ds(c0, tt)] == row + j * rb, 1.0, 0.0).astype(BF16)
                acc_ref[pl.ds(r0, rb), :] += jnp.dot(onehot, h_ref[0, pl.ds(c0, tt), :],
                                                     preferred_element_type=F32)
            return carry2

        return lax.fori_loop(0, cap // rb, block, carry)

    lax.fori_loop(0, nt, tile, 0)
    xe_ref[0, 0] = acc_ref[...].astype(BF16)


def moe_gather(ts_flat, rank, h, cap, *, tt):
    B, E, N = rank.shape
    D = h.shape[2]
    rb = min(LANES, cap)
    return pl.pallas_call(
        functools.partial(_moe_gather_kernel, rb=rb, tt=tt),
        grid_spec=pltpu.PrefetchScalarGridSpec(
            num_scalar_prefetch=1,
            grid=(B, E),
            in_specs=[pl.BlockSpec((1, 1, 1, N), lambda b, e, ts: (b, e, 0, 0)),
                      pl.BlockSpec((1, N, D), lambda b, e, ts: (b, 0, 0))],
            out_specs=pl.BlockSpec((1, 1, cap, D), lambda b, e, ts: (b, e, 0, 0)),
            scratch_shapes=[pltpu.VMEM((cap, D), F32)]),
        out_shape=jax.ShapeDtypeStruct((B, E, cap, D), BF16),
        compiler_params=_cparams("parallel", "arbitrary"),
        name="moe_gather",
    )(ts_flat, rank.reshape(B, E, 1, N), h)


def _expert_kernel(xe_ref, wg_ref, wu_ref, wd_ref, o_ref, wg_scr, wu_scr, wd_scr):
    @pl.when(pl.program_id(1) == 0)
    def _():
        wg_scr[...] = wg_ref[0, 0].astype(BF16)
        wu_scr[...] = wu_ref[0, 0].astype(BF16)
        wd_scr[...] = wd_ref[0, 0].astype(BF16)

    xe = xe_ref[0, 0]
    a = jnp.dot(xe, wg_scr[...], preferred_element_type=F32)
    u = jnp.dot(xe, wu_scr[...], preferred_element_type=F32)
    hid = (a * jax.nn.sigmoid(a)) * u
    o_ref[0, 0] = jnp.dot(hid.astype(BF16), wd_scr[...], preferred_element_type=F32).astype(BF16)


def expert_ffn(xe, w_gate, w_up, w_down, li):
    B, E, cap, D = xe.shape
    FF = w_gate.shape[3]
    return pl.pallas_call(
        _expert_kernel,
        grid=(E, B),
        in_specs=[pl.BlockSpec((1, 1, cap, D), lambda e, b: (b, e, 0, 0)),
                  pl.BlockSpec((1, 1, D, FF), lambda e, b: (li, e, 0, 0)),
                  pl.BlockSpec((1, 1, D, FF), lambda e, b: (li, e, 0, 0)),
                  pl.BlockSpec((1, 1, FF, D), lambda e, b: (li, e, 0, 0))],
        out_specs=pl.BlockSpec((1, 1, cap, D), lambda e, b: (b, e, 0, 0)),
        out_shape=jax.ShapeDtypeStruct((B, E, cap, D), BF16),
        scratch_shapes=[pltpu.VMEM((D, FF), BF16), pltpu.VMEM((D, FF), BF16), pltpu.VMEM((FF, D), BF16)],
        compiler_params=_cparams("arbitrary", "arbitrary"),
        name="expert_ffn",
    )(xe, w_gate, w_up, w_down)


def _moe_combine_kernel(ts_ref, rank_ref, aff_ref, ye_ref, x_ref, g2_ref, o_ref, acc_ref, *, rb):
    b, k = pl.program_id(0), pl.program_id(1)
    E, cap = ye_ref.shape[1], ye_ref.shape[2]
    tt = x_ref.shape[1]
    nt = pl.num_programs(1)
    acc_ref[...] = jnp.zeros_like(acc_ref)
    col = lax.broadcasted_iota(jnp.int32, (tt, rb), 1)
    for e in range(E):
        base = (b * E + e) * (nt + 1) + k
        lo, hi = ts_ref[base], ts_ref[base + 1]
        rank_col = rank_ref[0, :, e:e + 1]
        gate_col = aff_ref[0, :, e:e + 1]
        for j in range(cap // rb):
            @pl.when((lo < (j + 1) * rb) & (hi > j * rb))
            def _():
                onehot = jnp.where(rank_col == col + j * rb, 1.0, 0.0).astype(BF16)
                acc_ref[...] += gate_col * jnp.dot(onehot, ye_ref[0, e, j * rb:(j + 1) * rb, :],
                                                   preferred_element_type=F32)
    o_ref[0] = x_ref[0] + g2_ref[0] * acc_ref[...]


def moe_combine(ts_flat, rank_t, aff_t, ye, x, g2, *, tt):
    B, N, D = x.shape
    E, cap = ye.shape[1], ye.shape[2]
    rb = min(2 * LANES, cap)
    tok = lambda w: pl.BlockSpec((1, tt, w), lambda b, k, ts: (b, k, 0))
    return pl.pallas_call(
        functools.partial(_moe_combine_kernel, rb=rb),
        grid_spec=pltpu.PrefetchScalarGridSpec(
            num_scalar_prefetch=1,
            grid=(B, N // tt),
            in_specs=[tok(E), tok(E),
                      pl.BlockSpec((1, E, cap, D), lambda b, k, ts: (b, 0, 0, 0)),
                      tok(D),
                      pl.BlockSpec((1, 1, D), lambda b, k, ts: (b, 0, 0))],
            out_specs=tok(D),
            scratch_shapes=[pltpu.VMEM((tt, D), F32)]),
        out_shape=jax.ShapeDtypeStruct((B, N, D), F32),
        compiler_params=_cparams("parallel", "arbitrary"),
        name="moe_combine",
    )(ts_flat, rank_t, aff_t, ye, x, g2)


def moe_residual(x, g, shift, scale, gate2, router_w, w_gate, w_up, w_down, li):
    B, N, D = x.shape
    cap = EC_CAPACITY * N // N_EXPERTS
    tt = min(4 * LANES, N)
    h, logits = norm_mod_router(x, g, shift, scale, router_w)
    aff, rank, ts = route(jnp.swapaxes(logits, 1, 2), cap, tt)
    ts_flat = ts[:, :, :N // tt + 1].reshape(-1)
    xe = moe_gather(ts_flat, rank, h, cap, tt=tt)
    ye = expert_ffn(xe, w_gate, w_up, w_down, li)
    return moe_combine(ts_flat, jnp.swapaxes(rank, 1, 2), jnp.swapaxes(aff, 1, 2), ye, x, gate2, tt=tt)


def lambda_init(layer):
    return 0.8 - 0.6 * math.exp(-0.3 * layer)


def even_layer(x, xc, mods, cmods, norm1_g, w_in, w_out, lb, hgrn_g, qn_g, kn_g, lam_vec, subln_g,
               lam_init, ctx_out, tables):
    sh1, sc1, g1 = mods
    csh1, csc1, cg1 = cmods
    B = x.shape[0]
    proj = norm_mod_matmul(x, norm1_g, sh1, sc1, w_in)
    projc = norm_mod_matmul(xc, norm1_g, csh1, csc1, w_in)
    s0 = jnp.zeros((B, HGRN_HEADS, 2, HGRN_HEAD_DIM, HGRN_HEAD_DIM), F32)
    oc_f, oc_b, s_ctx = hgrn_scan(projc, lb, s0)
    o_f, o_b, _ = hgrn_scan(proj, lb, s_ctx)
    lv = lam_vec.astype(F32)
    lam = jnp.exp(jnp.sum(lv[0] * lv[1])) - jnp.exp(jnp.sum(lv[2] * lv[3])) + lam_init
    cos, sin = tables
    q, k, v = qkv_prep(proj, qn_g, kn_g, cos, sin, rotary=True)
    qc, kc, vc = qkv_prep(projc, qn_g, kn_g, cos[:xc.shape[1]], sin[:xc.shape[1]], rotary=False)
    k_all = jnp.concatenate([k, kc], axis=1)
    v_all = jnp.concatenate([v, vc], axis=2)
    y_d = diff_attention(q, k_all, v_all, lam, subln_g, 1.0 - lam_init)
    x_new = even_out(x, o_f, o_b, proj, y_d, hgrn_g.reshape(-1), g1, w_out)
    if not ctx_out:
        return x_new, None
    yc_d = diff_attention(qc, kc, vc, lam, subln_g, 1.0 - lam_init)
    xc_new = even_out(xc, oc_f, oc_b, projc, yc_d, hgrn_g.reshape(-1), cg1, w_out)
    return x_new, xc_new


def conv_layer(x, mods, norm1_g, w_in, conv_w, w_out):
    sh1, sc1, g1 = mods
    proj = norm_mod_matmul(x, norm1_g, sh1, sc1, w_in)
    return conv_out(x, proj, conv_w, g1, w_out)


def kernel(x, c, ctx, c_ctx, mod_w, mod_b, norm1_g, norm2_g, even_w_in, even_w_out, hgrn_lb_logits, hgrn_norm_g,
           diff_qnorm_g, diff_knorm_g, diff_lambda, diff_subln_g, conv_w_in, conv_w, conv_w_out, router_w,
           exp_w_gate, exp_w_up, exp_w_down):
    depth = mod_w.shape[0]
    B, T, D = x.shape
    lb_soft = jax.nn.softmax(hgrn_lb_logits.astype(F32), axis=0)
    lower_bounds = jnp.cumsum(lb_soft, axis=0) - lb_soft[:1]
    last_ctx_layer = 2 * ((depth - 1) // 2)
    cond = jnp.concatenate([c, c_ctx[None, :], jnp.zeros((8 - (B + 1) % 8, D), F32)], axis=0)
    mods = modulation(jax.nn.silu(cond), mod_w, mod_b)
    tables = rope_tables(T)
    xc = ctx
    for l in range(depth):
        read_ctx = l <= last_ctx_layer
        ctx_out = l < last_ctx_layer
        sh1, sc1, g1, sh2, sc2, g2 = [m[:, None, :] for m in jnp.split(mods[l, :B], MOD_CHUNKS, axis=-1)]
        if read_ctx:
            csh1, csc1, cg1, csh2, csc2, cg2 = [
                jnp.broadcast_to(m[None, None, :], (B, 1, D)) for m in jnp.split(mods[l, B], MOD_CHUNKS, axis=-1)]
        ew = (exp_w_gate, exp_w_up, exp_w_down, l)
        if l % 2 == 0:
            e = l // 2
            x, xc_new = even_layer(x, xc, (sh1, sc1, g1), (csh1, csc1, cg1), norm1_g[l],
                                   (even_w_in, e), (even_w_out, e), lower_bounds[e],
                                   hgrn_norm_g[e], diff_qnorm_g[e], diff_knorm_g[e], diff_lambda[e],
                                   diff_subln_g[e], lambda_init(l), ctx_out, tables)
        else:
            j = l // 2
            wi, wo = (conv_w_in, j), (conv_w_out, j)
            x = conv_layer(x, (sh1, sc1, g1), norm1_g[l], wi, conv_w[j], wo)
            xc_new = conv_layer(xc, (csh1, csc1, cg1), norm1_g[l], wi, conv_w[j], wo) if ctx_out else None
        x = moe_residual(x, norm2_g[l], sh2, sc2, g2, router_w[l], *ew)
        if ctx_out:
            xc = moe_residual(xc_new, norm2_g[l], csh2, csc2, cg2, router_w[l], *ew)
    return x
```

```python
import functools
import math

import numpy as np
import jax
import jax.numpy as jnp
from jax import lax
from jax.experimental import pallas as pl
from jax.experimental.pallas import tpu as pltpu

F32 = jnp.float32
BF16 = jnp.bfloat16

EPS = 1e-6
GRID_W = 64
ROPE_THETA = 10000.0
HGRN_HEAD_DIM = 128
HGRN_HEADS = 4
HGRN_WIDTH = HGRN_HEADS * HGRN_HEAD_DIM
DIFF_HEAD_DIM = 64
DIFF_HEADS = 4
DIFF_WIDTH = DIFF_HEADS * 2 * DIFF_HEAD_DIM
N_EXPERTS = 16
EC_CAPACITY = 2
MOD_CHUNKS = 6
SCAN_CHUNK = 64
SCAN_LEVELS = (32, 16, 8, 4, 2, 1)
LANES = 128
ROW_ALIGN = 16
VMEM_LIMIT = 56 * 1024 * 1024


def _cparams(*sem):
    return pltpu.CompilerParams(dimension_semantics=sem, vmem_limit_bytes=VMEM_LIMIT)


def _row_tile(t, want):
    return want if t % want == 0 else t


def _mod_kernel(s_ref, w_ref, b_ref, o_ref):
    s = s_ref[...]
    w = w_ref[0]
    s_hi = s.astype(BF16)
    s_lo = (s - s_hi.astype(F32)).astype(BF16)
    w_hi = w.astype(BF16)
    w_lo = (w - w_hi.astype(F32)).astype(BF16)
    d = lambda a, b: jnp.dot(a, b, preferred_element_type=F32)
    o_ref[0] = d(s_hi, w_hi) + d(s_hi, w_lo) + d(s_lo, w_hi) + b_ref[0]


def modulation(s, mod_w, mod_b, *, tn=512):
    R, D = s.shape
    depth, _, N = mod_w.shape
    return pl.pallas_call(
        _mod_kernel,
        grid=(depth, N // tn),
        in_specs=[pl.BlockSpec((R, D), lambda l, j: (0, 0)),
                  pl.BlockSpec((1, D, tn), lambda l, j: (l, 0, j)),
                  pl.BlockSpec((1, 1, tn), lambda l, j: (l, 0, j))],
        out_specs=pl.BlockSpec((1, R, tn), lambda l, j: (l, 0, j)),
        out_shape=jax.ShapeDtypeStruct((depth, R, N), F32),
        compiler_params=_cparams("parallel", "parallel"),
        name="modulation",
    )(s, mod_w, mod_b.reshape(depth, 1, N))


def _nmm_kernel(x_ref, g_ref, sh_ref, sc_ref, w_ref, o_ref, w_scr):
    @pl.when((pl.program_id(1) == 0) & (pl.program_id(2) == 0))
    def _():
        w_scr[...] = w_ref[0].astype(BF16)

    x = x_ref[0]
    ms = jnp.mean(x * x, axis=-1, keepdims=True)
    h = (x * lax.rsqrt(ms + EPS)) * g_ref[...]
    h = h * (1.0 + sc_ref[0]) + sh_ref[0]
    o_ref[0] = jnp.dot(h.astype(BF16), w_scr[...], preferred_element_type=F32).astype(o_ref.dtype)


def norm_mod_matmul(x, g, shift, scale, w, *, tm=512, tn=1024, out_dtype=F32):
    B, T, D = x.shape
    w, li = w
    N = w.shape[2]
    tm = _row_tile(T, tm)
    tn = _row_tile(N, tn)
    return pl.pallas_call(
        _nmm_kernel,
        grid=(N // tn, B, T // tm),
        in_specs=[
            pl.BlockSpec((1, tm, D), lambda j, b, i: (b, i, 0)),
            pl.BlockSpec((1, D), lambda j, b, i: (0, 0)),
            pl.BlockSpec((1, 1, D), lambda j, b, i: (b, 0, 0)),
            pl.BlockSpec((1, 1, D), lambda j, b, i: (b, 0, 0)),
            pl.BlockSpec((1, D, tn), lambda j, b, i: (li, 0, j)),
        ],
        out_specs=pl.BlockSpec((1, tm, tn), lambda j, b, i: (b, i, j)),
        out_shape=jax.ShapeDtypeStruct((B, T, N), out_dtype),
        scratch_shapes=[pltpu.VMEM((D, tn), BF16)],
        compiler_params=_cparams("arbitrary", "arbitrary", "arbitrary"),
        name="norm_mod_matmul",
    )(x, g.reshape(1, D), shift, scale, w)


def _scan_constants():
    C = SCAN_CHUNK
    t = np.arange(C)[:, None]
    u = np.arange(C)[None, :]
    mats = [u <= t, u > t]
    masks = []
    for w in SCAN_LEVELS:
        m = (t // (2 * w)) * 2 * w + w - 1
        later = (t // w) % 2 == 1
        mats.append(np.where(later, (u > m) & (u <= t), (u > t) & (u <= m)))
        masks.append(later & ((u // w) % 2 == 0) & (u // (2 * w) == t // (2 * w)))
    masks.append(t == u)
    a_f = np.stack(mats).astype(np.float32)
    m_f = np.stack(masks).astype(np.float32)
    a = np.stack([a_f, a_f[:, ::-1, ::-1]]).reshape(2, -1, C)
    return np.concatenate([a, a], axis=2), np.stack([m_f, m_f[:, ::-1, ::-1]])


def _scan_pair(q, z, v, lb, st, a2, mask_ref, d, later):
    C = SCAN_CHUNK
    W = HGRN_HEAD_DIM
    nl = len(SCAN_LEVELS)
    nt = lambda x, y: lax.dot_general(x, y, (((1,), (1,)), ((), ())), preferred_element_type=F32)
    nn = lambda x, y: jnp.dot(x, y, preferred_element_type=F32)
    e_abs = jnp.exp(-jnp.abs(z))
    r = 1.0 / (1.0 + e_abs)
    er = e_abs * r
    pos = z >= 0.0
    g2 = jnp.log2(lb + (1.0 - lb) * jnp.where(pos, r, er))
    k = (1.0 - lb) * jnp.where(pos, er, r)
    hi = g2.astype(BF16)
    lo = (g2 - hi.astype(F32)).astype(BF16)
    gs = jnp.concatenate([jnp.concatenate([hi[:C], hi[C:]], axis=1),
                          jnp.concatenate([lo[:C], lo[C:]], axis=1)], axis=0)
    x = jnp.exp2(nn(a2, gs))
    first, second = (1, 0) if d else (0, 1)
    vb = v.astype(BF16)
    sc, qt, kt, dec = [], [], [], []
    for c in (0, 1):
        qc, kc, xs = q[c * C:(c + 1) * C], k[c * C:(c + 1) * C], x[:, c * W:(c + 1) * W]
        s = mask_ref[d, nl] * nt(qc.astype(BF16), kc.astype(BF16))
        for i in range(nl):
            qk = (jnp.where(later[i], qc, kc) * xs[(2 + i) * C:(3 + i) * C]).astype(BF16)
            s = s + mask_ref[d, i] * nt(qk, qk)
        sc.append(s.astype(BF16))
        qt.append(qc * xs[0:C])
        kt.append(kc * xs[C:2 * C])
        dec.append(xs[0:1, :] if d else xs[C - 1:C, :])
    cross = nt(qt[second].astype(BF16), kt[first].astype(BF16)).astype(BF16)
    qt[second] = qt[second] * dec[first]
    qtb = [a.astype(BF16) for a in qt]
    kt[first] = kt[first] * dec[second]
    o_st = nt(jnp.concatenate(qtb, axis=0), st.astype(BF16))
    o = [None, None]
    o[first] = o_st[first * C:(first + 1) * C] + nn(sc[first], vb[first * C:(first + 1) * C])
    o[second] = (o_st[second * C:(second + 1) * C] + nn(cross, vb[first * C:(first + 1) * C])
                 + nn(sc[second], vb[second * C:(second + 1) * C]))
    upd = lax.dot_general(vb, jnp.concatenate(kt, axis=0).astype(BF16), (((0,), (0,)), ((), ())),
                          preferred_element_type=F32)
    return jnp.concatenate(o, axis=0), st * (dec[0] * dec[1]) + upd


def _scan_kernel(qf_ref, zf_ref, vf_ref, qb_ref, zb_ref, vb_ref, lbf_ref, lbb_ref, s0_ref, a_ref, mask_ref,
                 of_ref, ob_ref, sT_ref, st_scr, *, n_chunks):
    c = pl.program_id(2)
    C = SCAN_CHUNK

    @pl.when(c == 0)
    def _():
        st_scr[...] = s0_ref[0, 0]

    row = lax.broadcasted_iota(jnp.int32, (C, HGRN_HEAD_DIM), 0)
    later_f = [(row // w) % 2 == 1 for w in SCAN_LEVELS]
    later_b = [((C - 1 - row) // w) % 2 == 1 for w in SCAN_LEVELS]
    lbf = lbf_ref[...]
    lbb = lbb_ref[...]

    n_pairs = n_chunks // 2

    def body(i, carry):
        rf = pl.multiple_of(i * 2 * C, 2 * C)
        rb = pl.multiple_of((n_pairs - 1 - i) * 2 * C, 2 * C)
        fwd_in = (qf_ref[0, pl.ds(rf, 2 * C), :], zf_ref[0, pl.ds(rf, 2 * C), :], vf_ref[0, pl.ds(rf, 2 * C), :])
        bwd_in = (qb_ref[0, pl.ds(rb, 2 * C), :], zb_ref[0, pl.ds(rb, 2 * C), :], vb_ref[0, pl.ds(rb, 2 * C), :])
        sf, sb = st_scr[0], st_scr[1]
        o_f, sf = _scan_pair(*fwd_in, lbf, sf, a_ref[0], mask_ref, 0, later_f)
        o_b, sb = _scan_pair(*bwd_in, lbb, sb, a_ref[1], mask_ref, 1, later_b)
        of_ref[0, pl.ds(rf, 2 * C), :] = o_f
        ob_ref[0, pl.ds(rb, 2 * C), :] = o_b
        st_scr[0] = sf
        st_scr[1] = sb
        return carry

    lax.fori_loop(0, n_pairs, body, 0)

    @pl.when(c == pl.num_programs(2) - 1)
    def _():
        sT_ref[0, 0] = st_scr[...]


def hgrn_scan(proj, lb, s0, *, tb=512):
    B, T, _ = proj.shape
    tb = _row_tile(T, tb)
    nc = T // tb
    H = HGRN_HEADS
    hd = HGRN_HEAD_DIM
    fwd = lambda grp: pl.BlockSpec((1, tb, hd), lambda b, h, c: (b, c, grp * H + h))
    bwd = lambda grp: pl.BlockSpec((1, tb, hd), lambda b, h, c: (b, nc - 1 - c, grp * H + h))
    kern = functools.partial(_scan_kernel, n_chunks=tb // SCAN_CHUNK)
    a2, masks = _scan_constants()
    return pl.pallas_call(
        kern,
        grid=(B, H, nc),
        in_specs=[fwd(0), fwd(1), fwd(3), bwd(0), bwd(2), bwd(3),
                  pl.BlockSpec((1, hd), lambda b, h, c: (0, h)),
                  pl.BlockSpec((1, hd), lambda b, h, c: (0, h)),
                  pl.BlockSpec((1, 1, 2, hd, hd), lambda b, h, c: (b, h, 0, 0, 0)),
                  pl.BlockSpec(a2.shape, lambda b, h, c: (0, 0, 0)),
                  pl.BlockSpec(masks.shape, lambda b, h, c: (0, 0, 0, 0))],
        out_specs=[pl.BlockSpec((1, tb, hd), lambda b, h, c: (b, c, h)),
                   pl.BlockSpec((1, tb, hd), lambda b, h, c: (b, nc - 1 - c, h)),
                   pl.BlockSpec((1, 1, 2, hd, hd), lambda b, h, c: (b, h, 0, 0, 0))],
        out_shape=[jax.ShapeDtypeStruct((B, T, HGRN_WIDTH), F32),
                   jax.ShapeDtypeStruct((B, T, HGRN_WIDTH), F32),
                   jax.ShapeDtypeStruct((B, H, 2, hd, hd), F32)],
        scratch_shapes=[pltpu.VMEM((2, hd, hd), F32)],
        compiler_params=_cparams("parallel", "parallel", "arbitrary"),
        name="hgrn_scan",
    )(proj, proj, proj, proj, proj, proj, lb[0:1], lb[1:2], s0, jnp.asarray(a2, BF16), jnp.asarray(masks, F32))


def _group_mean_sq(x, gmat):
    sq = x * x
    hi = sq.astype(BF16)
    lo = (sq - hi.astype(F32)).astype(BF16)
    return (jnp.dot(hi, gmat, preferred_element_type=F32) + jnp.dot(lo, gmat, preferred_element_type=F32))


def _qkv_prep_kernel(q_ref, k_ref, v_ref, qg_ref, kg_ref, cos_ref, sin_ref, qo_ref, ko_ref, vo_ref, *, rotary):
    W = LANES
    r_i = lax.broadcasted_iota(jnp.int32, (W, W), 0) // DIFF_HEAD_DIM
    c_i = lax.broadcasted_iota(jnp.int32, (W, W), 1) // DIFF_HEAD_DIM
    gmat = jnp.where(r_i == c_i, 1.0 / DIFF_HEAD_DIM, 0.0).astype(BF16)
    lane = lax.broadcasted_iota(jnp.int32, (1, W), 1)
    first = (lane % 32) < 16

    def prep(x, g, scale):
        y = (x * lax.rsqrt(_group_mean_sq(x, gmat) + EPS)) * g
        if rotary:
            partner = jnp.where(first, pltpu.roll(y, W - 16, axis=1), pltpu.roll(y, 16, axis=1))
            y = y * cos_ref[...] + partner * sin_ref[...]
        if scale != 1.0:
            y = y * scale
        return y

    for h in range(DIFF_HEADS):
        sl = slice(h * W, (h + 1) * W)
        qo_ref[0, sl, :] = prep(q_ref[0, :, sl], qg_ref[...], DIFF_HEAD_DIM ** -0.5).T.astype(BF16)
        ko_ref[0, :, sl] = prep(k_ref[0, :, sl], kg_ref[...], 1.0).astype(BF16)
        vo_ref[0, sl, :] = v_ref[0, :, sl].T.astype(BF16)


def qkv_prep(proj, qg, kg, cos, sin, *, rotary, tm=512):
    B, T, _ = proj.shape
    tm = _row_tile(T, tm)
    Wd = DIFF_WIDTH
    col = lambda j: pl.BlockSpec((1, tm, Wd), lambda b, i: (b, i, j))
    vec = pl.BlockSpec((1, LANES), lambda b, i: (0, 0))
    tab = pl.BlockSpec((tm, LANES), lambda b, i: (i, 0))
    rows = pl.BlockSpec((1, tm, Wd), lambda b, i: (b, i, 0))
    cols = pl.BlockSpec((1, Wd, tm), lambda b, i: (b, 0, i))
    return pl.pallas_call(
        functools.partial(_qkv_prep_kernel, rotary=rotary),
        grid=(B, T // tm),
        in_specs=[col(5), col(6), col(7), vec, vec, tab, tab],
        out_specs=[cols, rows, cols],
        out_shape=[jax.ShapeDtypeStruct((B, Wd, T), BF16), jax.ShapeDtypeStruct((B, T, Wd), BF16),
                   jax.ShapeDtypeStruct((B, Wd, T), BF16)],
        compiler_params=_cparams("parallel", "parallel"),
        name="qkv_prep",
    )(proj, proj, proj, jnp.tile(qg, 2).reshape(1, LANES), jnp.tile(kg, 2).reshape(1, LANES), cos, sin)


def rope_tables(T):
    n = DIFF_HEAD_DIM // 2
    inv = 1.0 / (ROPE_THETA ** (jnp.arange(0, n, 2, dtype=F32) / n))
    t = jnp.arange(T)
    ang_r = (t // GRID_W).astype(F32)[:, None] * inv[None, :]
    ang_c = (t % GRID_W).astype(F32)[:, None] * inv[None, :]
    cos = jnp.concatenate([jnp.cos(ang_r)] * 2 + [jnp.cos(ang_c)] * 2, axis=-1)
    sin = jnp.concatenate([-jnp.sin(ang_r), jnp.sin(ang_r), -jnp.sin(ang_c), jnp.sin(ang_c)], axis=-1)
    return jnp.tile(cos, (1, 2)), jnp.tile(sin, (1, 2))


def _diff_attn_kernel(lam_ref, q_ref, k_ref, v_ref, g_ref, o_ref, *, out_scale, key_chunk):
    qt = q_ref[0]
    tq = qt.shape[1]
    S = k_ref.shape[1]
    row = lax.broadcasted_iota(jnp.int32, (LANES, 1), 0)
    zero = jnp.zeros_like(qt)
    qq = jnp.concatenate([jnp.where(row < DIFF_HEAD_DIM, qt, zero),
                          jnp.where(row >= DIFF_HEAD_DIM, qt, zero)], axis=1)
    m = jnp.full((1, 2 * tq), -jnp.inf, F32)
    acc = [jnp.zeros((LANES + 16, tq), F32), jnp.zeros((LANES + 16, tq), F32)]
    ones = jnp.ones((16, key_chunk), BF16)
    chunks = [(c0, min(c0 + key_chunk, S)) for c0 in range(0, S, key_chunk)]
    scores = lambda c: jnp.dot(k_ref[0, c[0]:c[1], :], qq, preferred_element_type=F32)
    s_next = scores(chunks[0])
    for n, (c0, c1) in enumerate(chunks):
        s = s_next
        if n + 1 < len(chunks):
            s_next = scores(chunks[n + 1])
        m_new = jnp.maximum(m, jnp.max(s, axis=0, keepdims=True))
        alpha = jnp.exp(m - m_new)
        pb = jnp.exp((s - m_new).astype(BF16))
        vt1 = jnp.concatenate([v_ref[0, :, c0:c1], ones[:, :c1 - c0]], axis=0)
        for i in range(2):
            acc[i] = acc[i] * alpha[:, i * tq:(i + 1) * tq] + jnp.dot(
                vt1, pb[:, i * tq:(i + 1) * tq], preferred_element_type=F32)
        m = m_new
    inv = [1.0 / a[LANES:LANES + 1] for a in acc]
    o = acc[0][:LANES] * inv[0] - acc[1][:LANES] * (lam_ref[0] * inv[1])
    ms = jnp.mean(o * o, axis=0, keepdims=True)
    o_ref[0] = ((o * lax.rsqrt(ms + EPS)) * g_ref[...] * out_scale).T


def diff_attention(qt, k, vt, lam, subln_g, out_scale, *, tq=256, key_chunk=512):
    B, Wd, T = qt.shape
    S = k.shape[1]
    tq = _row_tile(T, tq)
    return pl.pallas_call(
        functools.partial(_diff_attn_kernel, out_scale=out_scale, key_chunk=key_chunk),
        grid=(B, DIFF_HEADS, T // tq),
        in_specs=[pl.BlockSpec(memory_space=pltpu.SMEM),
                  pl.BlockSpec((1, LANES, tq), lambda b, h, i: (b, h, i)),
                  pl.BlockSpec((1, S, LANES), lambda b, h, i: (b, 0, h)),
                  pl.BlockSpec((1, LANES, S), lambda b, h, i: (b, h, 0)),
                  pl.BlockSpec((LANES, 1), lambda b, h, i: (0, 0))],
        out_specs=pl.BlockSpec((1, tq, LANES), lambda b, h, i: (b, i, h)),
        out_shape=jax.ShapeDtypeStruct((B, T, Wd), F32),
        compiler_params=_cparams("parallel", "parallel", "arbitrary"),
        name="diff_attention",
    )(lam.reshape(1), qt, k, vt, subln_g.reshape(LANES, 1))


def _even_out_kernel(x_ref, of_ref, ob_ref, gate_ref, yd_ref, hg_ref, g1_ref, w_ref, o_ref):
    acc = jnp.dot(yd_ref[0].astype(BF16), w_ref[0, HGRN_WIDTH:, :].astype(BF16), preferred_element_type=F32)
    for h in range(HGRN_HEADS):
        sl = slice(h * HGRN_HEAD_DIM, (h + 1) * HGRN_HEAD_DIM)
        o = of_ref[0, :, sl] + ob_ref[0, :, sl]
        ms = jnp.mean(o * o, axis=-1, keepdims=True)
        gate = gate_ref[0, :, sl]
        yh = (o * lax.rsqrt(ms + EPS)) * hg_ref[:, sl] * (gate * jax.nn.sigmoid(gate))
        acc = acc + jnp.dot(yh.astype(BF16), w_ref[0, sl, :].astype(BF16), preferred_element_type=F32)
    o_ref[0] = x_ref[0] + g1_ref[0] * acc


def even_out(x, o_f, o_b, proj, y_d, hgrn_g, g1, w_out, *, tm=512):
    B, T, D = x.shape
    w_out, li = w_out
    tm = _row_tile(T, tm)
    Wd = HGRN_WIDTH
    row = lambda w, j: pl.BlockSpec((1, tm, w), lambda b, i: (b, i, j))
    return pl.pallas_call(
        _even_out_kernel,
        grid=(B, T // tm),
        in_specs=[row(D, 0), row(Wd, 0), row(Wd, 0), row(Wd, 4), row(Wd, 0),
                  pl.BlockSpec((1, Wd), lambda b, i: (0, 0)),
                  pl.BlockSpec((1, 1, D), lambda b, i: (b, 0, 0)),
                  pl.BlockSpec((1,) + w_out.shape[1:], lambda b, i: (li, 0, 0))],
        out_specs=row(D, 0),
        out_shape=jax.ShapeDtypeStruct((B, T, D), F32),
        compiler_params=_cparams("parallel", "parallel"),
        name="even_out",
    )(x, o_f, o_b, proj, y_d, hgrn_g.reshape(1, Wd), g1, w_out)


def _conv_out_kernel(x_ref, bg_ref, cg_ref, v_ref, cp_ref, vp_ref, cn_ref, vn_ref, cw_ref, g1_ref, w_ref, o_ref):
    i = pl.program_id(1)
    n = pl.num_programs(1)
    u = cg_ref[0] * v_ref[0]
    tm = u.shape[0]
    u_prev_row = jnp.where(i > 0, cp_ref[0, 7:8, :] * vp_ref[0, 7:8, :], 0.0)
    u_next_row = jnp.where(i < n - 1, cn_ref[0, 0:1, :] * vn_ref[0, 0:1, :], 0.0)
    ridx = lax.broadcasted_iota(jnp.int32, (tm, 1), 0)
    u_prev = jnp.where(ridx == 0, u_prev_row, pltpu.roll(u, 1, axis=0))
    u_next = jnp.where(ridx == tm - 1, u_next_row, pltpu.roll(u, tm - 1, axis=0))
    y = cw_ref[0:1, :] * u_prev + cw_ref[1:2, :] * u + cw_ref[2:3, :] * u_next
    acc = jnp.dot((bg_ref[0] * y).astype(BF16), w_ref[0].astype(BF16), preferred_element_type=F32)
    o_ref[0] = x_ref[0] + g1_ref[0] * acc


def conv_out(x, proj, conv_w, g1, w_out, *, tm=512):
    B, T, D = x.shape
    w_out, li = w_out
    tm = _row_tile(T, tm)
    r8 = tm // 8
    last8 = T // 8 - 1
    row = lambda j: pl.BlockSpec((1, tm, D), lambda b, i: (b, i, j))
    prev = lambda j: pl.BlockSpec((1, 8, D), lambda b, i: (b, jnp.maximum(i * r8 - 1, 0), j))
    nxt = lambda j: pl.BlockSpec((1, 8, D), lambda b, i: (b, jnp.minimum((i + 1) * r8, last8), j))
    return pl.pallas_call(
        _conv_out_kernel,
        grid=(B, T // tm),
        in_specs=[row(0), row(0), row(1), row(2), prev(1), prev(2), nxt(1), nxt(2),
                  pl.BlockSpec((8, D), lambda b, i: (0, 0)),
                  pl.BlockSpec((1, 1, D), lambda b, i: (b, 0, 0)),
                  pl.BlockSpec((1,) + w_out.shape[1:], lambda b, i: (li, 0, 0))],
        out_specs=row(0),
        out_shape=jax.ShapeDtypeStruct((B, T, D), F32),
        compiler_params=_cparams("parallel", "parallel"),
        name="conv_out",
    )(x, proj, proj, proj, proj, proj, proj, proj,
      jnp.concatenate([conv_w, jnp.zeros((8 - conv_w.shape[0], D), conv_w.dtype)], axis=0), g1, w_out)


def _router_kernel(x_ref, g_ref, sh_ref, sc_ref, rw_ref, h_ref, lg_ref):
    x = x_ref[0]
    ms = jnp.mean(x * x, axis=-1, keepdims=True)
    h = (x * lax.rsqrt(ms + EPS)) * g_ref[...]
    h = h * (1.0 + sc_ref[0]) + sh_ref[0]
    h_ref[0] = h.astype(BF16)
    rw = rw_ref[...]
    h_hi = h.astype(BF16)
    h_lo = (h - h_hi.astype(F32)).astype(BF16)
    w_hi = rw.astype(BF16)
    w_lo = (rw - w_hi.astype(F32)).astype(BF16)
    d = lambda a, b: jnp.dot(a, b, preferred_element_type=F32)
    lg_ref[0] = d(h_hi, w_hi) + d(h_hi, w_lo) + d(h_lo, w_hi)


def norm_mod_router(x, g, shift, scale, router_w, *, tm=512):
    B, T, D = x.shape
    E = router_w.shape[1]
    tm = _row_tile(T, tm)
    return pl.pallas_call(
        _router_kernel,
        grid=(B, T // tm),
        in_specs=[pl.BlockSpec((1, tm, D), lambda b, i: (b, i, 0)),
                  pl.BlockSpec((1, D), lambda b, i: (0, 0)),
                  pl.BlockSpec((1, 1, D), lambda b, i: (b, 0, 0)),
                  pl.BlockSpec((1, 1, D), lambda b, i: (b, 0, 0)),
                  pl.BlockSpec((D, E), lambda b, i: (0, 0))],
        out_specs=[pl.BlockSpec((1, tm, D), lambda b, i: (b, i, 0)),
                   pl.BlockSpec((1, tm, E), lambda b, i: (b, i, 0))],
        out_shape=[jax.ShapeDtypeStruct((B, T, D), BF16), jax.ShapeDtypeStruct((B, T, E), F32)],
        compiler_params=_cparams("parallel", "parallel"),
        name="norm_mod_router",
    )(x, g.reshape(1, D), shift, scale, router_w)


def _lane_prefix(flags, tri_tot):
    E, N = flags.shape
    carries = [jnp.zeros((E, LANES), F32)]
    out = []
    for j in range(N // LANES):
        r = jnp.dot(flags[:, j * LANES:(j + 1) * LANES].astype(BF16), tri_tot, preferred_element_type=F32)
        out.append(r[:, :LANES] + carries[-1])
        carries.append(carries[-1] + r[:, LANES:])
    return jnp.concatenate(out, axis=1), carries


def _route_kernel(lg_ref, aff_ref, rank_ref, ts_ref, *, cap, tt):
    lg = lg_ref[0]
    E, N = lg.shape
    p = jnp.exp(lg - jnp.max(lg, axis=0, keepdims=True))
    aff = p / jnp.sum(p, axis=0, keepdims=True)
    aff_ref[0] = aff
    count = lambda m: jnp.sum(jnp.where(m, 1.0, 0.0), axis=1, keepdims=True)
    as_float = lambda i: pltpu.bitcast(i, F32)

    def refine_bits(i, thr):
        cand = thr | jnp.left_shift(jnp.int32(1), 30 - i)
        return jnp.where(count(aff >= as_float(cand)) >= cap, cand, thr)

    thr = lax.fori_loop(0, 31, refine_bits, jnp.zeros((E, 1), jnp.int32))

    def refine_mid(i, lo_hi):
        lo, hi = lo_hi
        mid = 0.5 * (lo + hi)
        up = count(aff >= mid) >= cap
        return jnp.where(up, mid, lo), jnp.where(up, hi, mid)

    lo, hi = lax.fori_loop(0, 24, refine_mid, (as_float(thr), as_float(jnp.maximum(thr + 1, 0x00800000))))
    gt = aff >= hi
    eq = (aff >= lo) & (aff < hi)
    r_i = lax.broadcasted_iota(jnp.int32, (LANES, 2 * LANES), 0)
    c_i = lax.broadcasted_iota(jnp.int32, (LANES, 2 * LANES), 1)
    tri_tot = jnp.where((r_i < c_i) | (c_i >= LANES), 1.0, 0.0).astype(BF16)
    eq_rank, _ = _lane_prefix(jnp.where(eq, 1.0, 0.0), tri_tot)
    sel = gt | (eq & (eq_rank < cap - count(gt)))
    rank, before = _lane_prefix(jnp.where(sel, 1.0, 0.0), tri_tot)
    rank_ref[0] = jnp.where(sel, rank, -1.0).astype(jnp.int32)
    lane = lax.broadcasted_iota(jnp.int32, (E, LANES), 1)
    ts = jnp.zeros((E, LANES), F32)
    for k in range(N // tt + 1):
        ts = jnp.where(lane == k, before[k * tt // LANES], ts)
    ts_ref[0] = ts.astype(jnp.int32)


def route(logits_t, cap, tt):
    B, E, N = logits_t.shape
    blk = pl.BlockSpec((1, E, N), lambda b: (b, 0, 0))
    return pl.pallas_call(
        functools.partial(_route_kernel, cap=cap, tt=tt),
        grid=(B,),
        in_specs=[blk],
        out_specs=[blk, blk, pl.BlockSpec((1, E, LANES), lambda b: (b, 0, 0))],
        out_shape=[jax.ShapeDtypeStruct((B, E, N), F32), jax.ShapeDtypeStruct((B, E, N), jnp.int32),
                   jax.ShapeDtypeStruct((B, E, LANES), jnp.int32)],
        compiler_params=_cparams("parallel"),
        name="route",
    )(logits_t)


def _window(lo, w, win, cap):
    lower = (lo // ROW_ALIGN) * ROW_ALIGN + w * win
    return pl.multiple_of(jnp.minimum(lower, cap - win), ROW_ALIGN), lower


def _extra_windows(cap, tt, win):
    return -(-(min(cap, tt) + ROW_ALIGN - 1) // win) - 1


def _moe_gather_kernel(ts_ref, rank_ref, h_ref, xe_ref, acc_ref, *, win, tt):
    b, e = pl.program_id(0), pl.program_id(1)
    N = h_ref.shape[1]
    cap = xe_ref.shape[2]
    nt = N // tt
    base = (b * pl.num_programs(1) + e) * (nt + 1)
    acc_ref[...] = jnp.zeros_like(acc_ref)
    row = lax.broadcasted_iota(jnp.int32, (win, tt), 0)

    def place(k, w):
        start, lower = _window(ts_ref[base + k], w, win, cap)
        c0 = k * tt if isinstance(k, int) else pl.multiple_of(k * tt, tt)
        rk = rank_ref[0, 0, :, pl.ds(c0, tt)]
        onehot = jnp.where(jnp.where(rk >= lower, rk, -1) == row + start, 1.0, 0.0).astype(BF16)
        acc_ref[pl.ds(start, win), :] += jnp.dot(onehot, h_ref[0, pl.ds(c0, tt), :], preferred_element_type=F32)

    span = lambda k: ts_ref[base + k + 1] - (ts_ref[base + k] // ROW_ALIGN) * ROW_ALIGN
    need = jnp.int32(0)
    for k in range(nt):
        place(k, 0)
        need = jnp.maximum(need, span(k))

    n_extra = _extra_windows(cap, tt, win)

    @pl.when(need > win)
    def _():
        def extra(i, carry):
            k, w = i // n_extra, i % n_extra + 1

            @pl.when(span(k) > w * win)
            def _():
                place(k, w)
            return carry

        lax.fori_loop(0, nt * n_extra, extra, 0)

    xe_ref[0, 0] = acc_ref[...].astype(BF16)


def moe_gather(ts_flat, rank, h, cap, *, tt):
    B, E, N = rank.shape
    D = h.shape[2]
    return pl.pallas_call(
        functools.partial(_moe_gather_kernel, win=min(LANES, cap), tt=tt),
        grid_spec=pltpu.PrefetchScalarGridSpec(
            num_scalar_prefetch=1,
            grid=(B, E),
            in_specs=[pl.BlockSpec((1, 1, 1, N), lambda b, e, ts: (b, e, 0, 0)),
                      pl.BlockSpec((1, N, D), lambda b, e, ts: (b, 0, 0))],
            out_specs=pl.BlockSpec((1, 1, cap, D), lambda b, e, ts: (b, e, 0, 0)),
            scratch_shapes=[pltpu.VMEM((cap, D), F32)]),
        out_shape=jax.ShapeDtypeStruct((B, E, cap, D), BF16),
        compiler_params=_cparams("parallel", "arbitrary"),
        name="moe_gather",
    )(ts_flat, rank.reshape(B, E, 1, N), h)


def _expert_kernel(*refs, n_streams):
    xe_refs = refs[:n_streams]
    wg_ref, wu_ref, wd_ref = refs[n_streams:n_streams + 3]
    o_refs = refs[n_streams + 3:2 * n_streams + 3]
    wg_scr, wu_scr, wd_scr = refs[2 * n_streams + 3:]

    @pl.when(pl.program_id(1) == 0)
    def _():
        wg_scr[...] = wg_ref[0, 0].astype(BF16)
        wu_scr[...] = wu_ref[0, 0].astype(BF16)
        wd_scr[...] = wd_ref[0, 0].astype(BF16)

    xe = jnp.concatenate([r[0, 0] for r in xe_refs], axis=0)
    a = jnp.dot(xe, wg_scr[...], preferred_element_type=F32)
    u = jnp.dot(xe, wu_scr[...], preferred_element_type=F32)
    hid = (a * jax.nn.sigmoid(a)) * u
    y = jnp.dot(hid.astype(BF16), wd_scr[...], preferred_element_type=F32).astype(BF16)
    r0 = 0
    for o_ref in o_refs:
        o_ref[0, 0] = y[r0:r0 + o_ref.shape[2]]
        r0 += o_ref.shape[2]


def expert_ffn(xes, w_gate, w_up, w_down, li):
    B, E, _, D = xes[0].shape
    FF = w_gate.shape[3]
    rows = [pl.BlockSpec((1, 1, xe.shape[2], D), lambda e, b: (b, e, 0, 0)) for xe in xes]
    return pl.pallas_call(
        functools.partial(_expert_kernel, n_streams=len(xes)),
        grid=(E, B),
        in_specs=rows + [pl.BlockSpec((1, 1, D, FF), lambda e, b: (li, e, 0, 0)),
                         pl.BlockSpec((1, 1, D, FF), lambda e, b: (li, e, 0, 0)),
                         pl.BlockSpec((1, 1, FF, D), lambda e, b: (li, e, 0, 0))],
        out_specs=rows,
        out_shape=[jax.ShapeDtypeStruct(xe.shape, BF16) for xe in xes],
        scratch_shapes=[pltpu.VMEM((D, FF), BF16), pltpu.VMEM((D, FF), BF16), pltpu.VMEM((FF, D), BF16)],
        compiler_params=_cparams("arbitrary", "arbitrary"),
        name="expert_ffn",
    )(*xes, w_gate, w_up, w_down)


def _moe_combine_kernel(ts_ref, rank_ref, aff_ref, ye_ref, x_ref, g2_ref, o_ref, acc_ref, *, win, group):
    b, k = pl.program_id(0), pl.program_id(1)
    E, cap = ye_ref.shape[1], ye_ref.shape[2]
    tt = x_ref.shape[1]
    nt = pl.num_programs(1)
    col = lax.broadcasted_iota(jnp.int32, (tt, win), 1)
    lo = [ts_ref[(b * E + e) * (nt + 1) + k] for e in range(E)]
    hi = [ts_ref[(b * E + e) * (nt + 1) + k + 1] for e in range(E)]

    def contribution(w):
        total = None
        for g0 in range(0, E, group):
            lhs, rhs = [], []
            for e in range(g0, g0 + group):
                start, lower = _window(lo[e], w, win, cap)
                rk = rank_ref[0, :, e:e + 1]
                hit = jnp.where(rk >= lower, rk, -1) == col + start
                lhs.append(jnp.where(hit, aff_ref[0, :, e:e + 1], 0.0).astype(BF16))
                rhs.append(ye_ref[0, e, pl.ds(start, win), :])
            d = jnp.dot(jnp.concatenate(lhs, axis=1), jnp.concatenate(rhs, axis=0), preferred_element_type=F32)
            total = d if total is None else total + d
        return total

    acc_ref[...] = contribution(0)
    need = jnp.int32(0)
    for e in range(E):
        need = jnp.maximum(need, hi[e] - (lo[e] // ROW_ALIGN) * ROW_ALIGN)

    @pl.when(need > win)
    def _():
        def extra(w, carry):
            @pl.when(need > w * win)
            def _():
                acc_ref[...] += contribution(w)
            return carry

        lax.fori_loop(1, _extra_windows(cap, tt, win) + 1, extra, 0)

    o_ref[0] = x_ref[0] + g2_ref[0] * acc_ref[...]


def moe_combine(ts_flat, rank_t, aff_t, ye, x, g2, *, tt):
    B, N, D = x.shape
    E, cap = ye.shape[1], ye.shape[2]
    tok = lambda w: pl.BlockSpec((1, tt, w), lambda b, k, ts: (b, k, 0))
    return pl.pallas_call(
        functools.partial(_moe_combine_kernel, win=min(LANES, cap), group=4),
        grid_spec=pltpu.PrefetchScalarGridSpec(
            num_scalar_prefetch=1,
            grid=(B, N // tt),
            in_specs=[tok(E), tok(E),
                      pl.BlockSpec((1, E, cap, D), lambda b, k, ts: (b, 0, 0, 0)),
                      tok(D),
                      pl.BlockSpec((1, 1, D), lambda b, k, ts: (b, 0, 0))],
            out_specs=tok(D),
            scratch_shapes=[pltpu.VMEM((tt, D), F32)]),
        out_shape=jax.ShapeDtypeStruct((B, N, D), F32),
        compiler_params=_cparams("parallel", "arbitrary"),
        name="moe_combine",
    )(ts_flat, rank_t, aff_t, ye, x, g2)


def moe_residual(streams, g, router_w, w_gate, w_up, w_down, li):
    routed = []
    for x, shift, scale, _ in streams:
        B, N, D = x.shape
        cap = EC_CAPACITY * N // N_EXPERTS
        tt = min(4 * LANES, N)
        h, logits = norm_mod_router(x, g, shift, scale, router_w)
        aff, rank, ts = route(jnp.swapaxes(logits, 1, 2), cap, tt)
        ts_flat = ts[:, :, :N // tt + 1].reshape(-1)
        routed.append((moe_gather(ts_flat, rank, h, cap, tt=tt), ts_flat, rank, aff, tt))
    yes = expert_ffn([r[0] for r in routed], w_gate, w_up, w_down, li)
    return [moe_combine(ts_flat, jnp.swapaxes(rank, 1, 2), jnp.swapaxes(aff, 1, 2), ye, x, gate2, tt=tt)
            for (x, _, _, gate2), ye, (_, ts_flat, rank, aff, tt) in zip(streams, yes, routed)]


def lambda_init(layer):
    return 0.8 - 0.6 * math.exp(-0.3 * layer)


def even_layer(x, xc, mods, cmods, norm1_g, w_in, w_out, lb, hgrn_g, qn_g, kn_g, lam_vec, subln_g,
               lam_init, ctx_out, tables):
    sh1, sc1, g1 = mods
    csh1, csc1, cg1 = cmods
    B = x.shape[0]
    proj = norm_mod_matmul(x, norm1_g, sh1, sc1, w_in)
    projc = norm_mod_matmul(xc, norm1_g, csh1, csc1, w_in)
    s0 = jnp.zeros((B, HGRN_HEADS, 2, HGRN_HEAD_DIM, HGRN_HEAD_DIM), F32)
    oc_f, oc_b, s_ctx = hgrn_scan(projc, lb, s0)
    o_f, o_b, _ = hgrn_scan(proj, lb, s_ctx)
    lv = lam_vec.astype(F32)
    lam = jnp.exp(jnp.sum(lv[0] * lv[1])) - jnp.exp(jnp.sum(lv[2] * lv[3])) + lam_init
    cos, sin = tables
    q, k, v = qkv_prep(proj, qn_g, kn_g, cos, sin, rotary=True)
    qc, kc, vc = qkv_prep(projc, qn_g, kn_g, cos[:xc.shape[1]], sin[:xc.shape[1]], rotary=False)
    k_all = jnp.concatenate([k, kc], axis=1)
    v_all = jnp.concatenate([v, vc], axis=2)
    y_d = diff_attention(q, k_all, v_all, lam, subln_g, 1.0 - lam_init)
    x_new = even_out(x, o_f, o_b, proj, y_d, hgrn_g.reshape(-1), g1, w_out)
    if not ctx_out:
        return x_new, None
    yc_d = diff_attention(qc, kc, vc, lam, subln_g, 1.0 - lam_init)
    xc_new = even_out(xc, oc_f, oc_b, projc, yc_d, hgrn_g.reshape(-1), cg1, w_out)
    return x_new, xc_new


def conv_layer(x, mods, norm1_g, w_in, conv_w, w_out):
    sh1, sc1, g1 = mods
    proj = norm_mod_matmul(x, norm1_g, sh1, sc1, w_in)
    return conv_out(x, proj, conv_w, g1, w_out)


def kernel(x, c, ctx, c_ctx, mod_w, mod_b, norm1_g, norm2_g, even_w_in, even_w_out, hgrn_lb_logits, hgrn_norm_g,
           diff_qnorm_g, diff_knorm_g, diff_lambda, diff_subln_g, conv_w_in, conv_w, conv_w_out, router_w,
           exp_w_gate, exp_w_up, exp_w_down):
    depth = mod_w.shape[0]
    B, T, D = x.shape
    lb_soft = jax.nn.softmax(hgrn_lb_logits.astype(F32), axis=0)
    lower_bounds = jnp.cumsum(lb_soft, axis=0) - lb_soft[:1]
    last_ctx_layer = 2 * ((depth - 1) // 2)
    cond = jnp.concatenate([c, c_ctx[None, :], jnp.zeros((8 - (B + 1) % 8, D), F32)], axis=0)
    mods = modulation(jax.nn.silu(cond), mod_w, mod_b)
    tables = rope_tables(T)
    xc = ctx
    for l in range(depth):
        read_ctx = l <= last_ctx_layer
        ctx_out = l < last_ctx_layer
        sh1, sc1, g1, sh2, sc2, g2 = [m[:, None, :] for m in jnp.split(mods[l, :B], MOD_CHUNKS, axis=-1)]
        if read_ctx:
            csh1, csc1, cg1, csh2, csc2, cg2 = [
                jnp.broadcast_to(m[None, None, :], (B, 1, D)) for m in jnp.split(mods[l, B], MOD_CHUNKS, axis=-1)]
        ew = (exp_w_gate, exp_w_up, exp_w_down, l)
        if l % 2 == 0:
            e = l // 2
            x, xc_new = even_layer(x, xc, (sh1, sc1, g1), (csh1, csc1, cg1), norm1_g[l],
                                   (even_w_in, e), (even_w_out, e), lower_bounds[e],
                                   hgrn_norm_g[e], diff_qnorm_g[e], diff_knorm_g[e], diff_lambda[e],
                                   diff_subln_g[e], lambda_init(l), ctx_out, tables)
        else:
            j = l // 2
            wi, wo = (conv_w_in, j), (conv_w_out, j)
            x = conv_layer(x, (sh1, sc1, g1), norm1_g[l], wi, conv_w[j], wo)
            xc_new = conv_layer(xc, (csh1, csc1, cg1), norm1_g[l], wi, conv_w[j], wo) if ctx_out else None
        streams = [(x, sh2, sc2, g2)] + ([(xc_new, csh2, csc2, cg2)] if ctx_out else [])
        outs = moe_residual(streams, norm2_g[l], router_w[l], *ew)
        x = outs[0]
        if ctx_out:
            xc = outs[1]
    return x
```

```python
import functools
import math

import numpy as np
import jax
import jax.numpy as jnp
from jax import lax
from jax.experimental import pallas as pl
from jax.experimental.pallas import tpu as pltpu

F32 = jnp.float32
BF16 = jnp.bfloat16

EPS = 1e-6
GRID_W = 64
ROPE_THETA = 10000.0
HGRN_HEAD_DIM = 128
HGRN_HEADS = 4
HGRN_WIDTH = HGRN_HEADS * HGRN_HEAD_DIM
DIFF_HEAD_DIM = 64
DIFF_HEADS = 4
DIFF_WIDTH = DIFF_HEADS * 2 * DIFF_HEAD_DIM
N_EXPERTS = 16
EC_CAPACITY = 2
MOD_CHUNKS = 6
SCAN_CHUNK = 64
SCAN_LEVELS = (32, 16, 8, 4, 2, 1)
SCAN_HEADS_PER_STEP = 2
LANES = 128
ROW_ALIGN = 16
VMEM_LIMIT = 56 * 1024 * 1024


def _cparams(*sem):
    return pltpu.CompilerParams(dimension_semantics=sem, vmem_limit_bytes=VMEM_LIMIT)


def _row_tile(t, want):
    return want if t % want == 0 else t


def _mod_kernel(s_ref, w_ref, b_ref, o_ref):
    s = s_ref[...]
    w = w_ref[0]
    s_hi = s.astype(BF16)
    s_lo = (s - s_hi.astype(F32)).astype(BF16)
    w_hi = w.astype(BF16)
    w_lo = (w - w_hi.astype(F32)).astype(BF16)
    d = lambda a, b: jnp.dot(a, b, preferred_element_type=F32)
    o_ref[0] = d(s_hi, w_hi) + d(s_hi, w_lo) + d(s_lo, w_hi) + b_ref[0]


def modulation(s, mod_w, mod_b, *, tn=512):
    R, D = s.shape
    depth, _, N = mod_w.shape
    return pl.pallas_call(
        _mod_kernel,
        grid=(depth, N // tn),
        in_specs=[pl.BlockSpec((R, D), lambda l, j: (0, 0)),
                  pl.BlockSpec((1, D, tn), lambda l, j: (l, 0, j)),
                  pl.BlockSpec((1, 1, tn), lambda l, j: (l, 0, j))],
        out_specs=pl.BlockSpec((1, R, tn), lambda l, j: (l, 0, j)),
        out_shape=jax.ShapeDtypeStruct((depth, R, N), F32),
        compiler_params=_cparams("parallel", "parallel"),
        name="modulation",
    )(s, mod_w, mod_b.reshape(depth, 1, N))


def _nmm_kernel(x_ref, g_ref, sh_ref, sc_ref, w_ref, o_ref, w_scr):
    @pl.when((pl.program_id(1) == 0) & (pl.program_id(2) == 0))
    def _():
        w_scr[...] = w_ref[0].astype(BF16)

    x = x_ref[0]
    ms = jnp.mean(x * x, axis=-1, keepdims=True)
    h = (x * lax.rsqrt(ms + EPS)) * g_ref[...]
    h = h * (1.0 + sc_ref[0]) + sh_ref[0]
    o_ref[0] = jnp.dot(h.astype(BF16), w_scr[...], preferred_element_type=F32).astype(o_ref.dtype)


def norm_mod_matmul(x, g, shift, scale, w, *, tm=512, n_col_blocks=2, out_dtype=F32):
    B, T, D = x.shape
    w, li = w
    N = w.shape[2]
    tm = _row_tile(T, tm)
    tn = N // n_col_blocks
    return pl.pallas_call(
        _nmm_kernel,
        grid=(N // tn, B, T // tm),
        in_specs=[
            pl.BlockSpec((1, tm, D), lambda j, b, i: (b, i, 0)),
            pl.BlockSpec((1, D), lambda j, b, i: (0, 0)),
            pl.BlockSpec((1, 1, D), lambda j, b, i: (b, 0, 0)),
            pl.BlockSpec((1, 1, D), lambda j, b, i: (b, 0, 0)),
            pl.BlockSpec((1, D, tn), lambda j, b, i: (li, 0, j)),
        ],
        out_specs=pl.BlockSpec((1, tm, tn), lambda j, b, i: (b, i, j)),
        out_shape=jax.ShapeDtypeStruct((B, T, N), out_dtype),
        scratch_shapes=[pltpu.VMEM((D, tn), BF16)],
        compiler_params=_cparams("arbitrary", "arbitrary", "arbitrary"),
        name="norm_mod_matmul",
    )(x, g.reshape(1, D), shift, scale, w)


def _scan_constants():
    C = SCAN_CHUNK
    t = np.arange(C)[:, None]
    u = np.arange(C)[None, :]
    mats = [u <= t, u > t]
    masks = []
    for w in SCAN_LEVELS:
        m = (t // (2 * w)) * 2 * w + w - 1
        later = (t // w) % 2 == 1
        mats.append(np.where(later, (u > m) & (u <= t), (u > t) & (u <= m)))
        masks.append(later & ((u // w) % 2 == 0) & (u // (2 * w) == t // (2 * w)))
    masks = [t == u] + masks + [np.ones((C, C), bool)]
    a_f = np.stack(mats).astype(np.float32)
    m_f = np.stack(masks).astype(np.float32)
    a = np.stack([a_f, a_f[:, ::-1, ::-1]]).reshape(2, -1, C)
    m = np.stack([m_f, m_f[:, ::-1, ::-1]])
    m_pairs = np.concatenate([m[:, 0::2], m[:, 1::2]], axis=3)
    return np.concatenate([a, a], axis=2), m_pairs


def _scan_pair(q, z, v, lb, st, a2, mask_ref, d, later):
    C = SCAN_CHUNK
    W = HGRN_HEAD_DIM
    nl = len(SCAN_LEVELS)
    nt = lambda x, y: lax.dot_general(x, y, (((1,), (1,)), ((), ())), preferred_element_type=F32)
    nn = lambda x, y: jnp.dot(x, y, preferred_element_type=F32)
    e_abs = jnp.exp(-jnp.abs(z))
    r = 1.0 / (1.0 + e_abs)
    er = e_abs * r
    pos = z >= 0.0
    g2 = jnp.log2(lb + (1.0 - lb) * jnp.where(pos, r, er))
    k = (1.0 - lb) * jnp.where(pos, er, r)
    hi = g2.astype(BF16)
    lo = (g2 - hi.astype(F32)).astype(BF16)
    gs = jnp.concatenate([jnp.concatenate([hi[:C], hi[C:]], axis=1),
                          jnp.concatenate([lo[:C], lo[C:]], axis=1)], axis=0)
    x = nn(a2, gs)
    yield
    x = jnp.exp2(x)
    first, second = (1, 0) if d else (0, 1)
    vb = v.astype(BF16)
    sides, qt, kt, dec = [], [], [], []
    for c in (0, 1):
        qc, kc, xs = q[c * C:(c + 1) * C], k[c * C:(c + 1) * C], x[:, c * W:(c + 1) * W]
        ops = [(qc.astype(BF16), kc.astype(BF16))]
        for i in range(nl):
            qk = (jnp.where(later[i], qc, kc) * xs[(2 + i) * C:(3 + i) * C]).astype(BF16)
            ops.append((qk, qk))
        sides.append(ops)
        qt.append(qc * xs[0:C])
        kt.append(kc * xs[C:2 * C])
        dec.append(xs[0:1, :] if d else xs[C - 1:C, :])
    zero = jnp.zeros((C, W), BF16)
    sides[first].append((zero, zero))
    sides[second].append((qt[second].astype(BF16), kt[first].astype(BF16)))
    prods = []
    for ops in sides:
        pair = []
        for j in range(0, nl + 2, 2):
            (qa, ka), (qb, kb) = ops[j], ops[j + 1]
            rhs = jnp.concatenate([jnp.concatenate([ka, zero], axis=1), jnp.concatenate([zero, kb], axis=1)], axis=0)
            pair.append(nt(jnp.concatenate([qa, qb], axis=1), rhs))
        prods.append(pair)
        yield
    qt[second] = qt[second] * dec[first]
    kt[first] = kt[first] * dec[second]
    o_st = nt(jnp.concatenate([a.astype(BF16) for a in qt], axis=0), st.astype(BF16))
    upd = lax.dot_general(vb, jnp.concatenate(kt, axis=0).astype(BF16), (((0,), (0,)), ((), ())),
                          preferred_element_type=F32)
    yield
    o = [None, None]
    n_main = nl // 2
    for c, pair in enumerate(prods):
        main = mask_ref[d, 0] * pair[0]
        for j in range(1, n_main):
            main = main + mask_ref[d, j] * pair[j]
        last = mask_ref[d, n_main] * pair[n_main]
        vc, vf = vb[c * C:(c + 1) * C], vb[first * C:(first + 1) * C]
        o[c] = o_st[c * C:(c + 1) * C] + nn(jnp.concatenate([main, last], axis=1).astype(BF16),
                                            jnp.concatenate([vc, vc, vc, vf], axis=0))
    yield
    return jnp.concatenate(o, axis=0), st * (dec[0] * dec[1]) + upd


def _run_interleaved(gens):
    results = [None] * len(gens)
    live = list(range(len(gens)))
    while live:
        for i in list(live):
            try:
                next(gens[i])
            except StopIteration as done:
                results[i] = done.value
                live.remove(i)
    return results


def _scan_kernel(qf_ref, zf_ref, vf_ref, qb_ref, zb_ref, vb_ref, lbf_ref, lbb_ref, s0_ref, a_ref, mask_ref,
                 of_ref, ob_ref, sT_ref, st_scr, *, n_chunks):
    c = pl.program_id(2)
    C = SCAN_CHUNK

    @pl.when(c == 0)
    def _():
        st_scr[...] = s0_ref[0]

    W = HGRN_HEAD_DIM
    row = lax.broadcasted_iota(jnp.int32, (C, W), 0)
    later = [[(row // w) % 2 == 1 for w in SCAN_LEVELS], [((C - 1 - row) // w) % 2 == 1 for w in SCAN_LEVELS]]
    in_refs = [(qf_ref, zf_ref, vf_ref), (qb_ref, zb_ref, vb_ref)]
    lb_refs = [lbf_ref, lbb_ref]
    o_refs = [of_ref, ob_ref]
    n_pairs = n_chunks // 2
    heads = range(qf_ref.shape[2] // W)

    def body(i, carry):
        r0 = [pl.multiple_of(i * 2 * C, 2 * C), pl.multiple_of((n_pairs - 1 - i) * 2 * C, 2 * C)]
        chains = [(d, h) for h in heads for d in (0, 1)]
        ins = [[ref[0, pl.ds(r0[d], 2 * C), h * W:(h + 1) * W] for ref in in_refs[d]] for d, h in chains]
        sts = [st_scr[h, d] for d, h in chains]
        outs = _run_interleaved([
            _scan_pair(*x, lb_refs[d][:, h * W:(h + 1) * W], st, a_ref[d], mask_ref, d, later[d])
            for (d, h), x, st in zip(chains, ins, sts)])
        for (d, h), (o, st) in zip(chains, outs):
            o_refs[d][0, pl.ds(r0[d], 2 * C), h * W:(h + 1) * W] = o
            st_scr[h, d] = st
        return carry

    lax.fori_loop(0, n_pairs, body, 0)

    @pl.when(c == pl.num_programs(2) - 1)
    def _():
        sT_ref[0] = st_scr[...]


def hgrn_scan(proj, lb, s0, *, tb=512):
    B, T, _ = proj.shape
    tb = _row_tile(T, tb)
    nc = T // tb
    H = HGRN_HEADS
    hd = HGRN_HEAD_DIM
    hps = SCAN_HEADS_PER_STEP
    G = H // hps
    wd = hps * hd
    fwd = lambda grp: pl.BlockSpec((1, tb, wd), lambda b, h, c: (b, c, grp * G + h))
    bwd = lambda grp: pl.BlockSpec((1, tb, wd), lambda b, h, c: (b, nc - 1 - c, grp * G + h))
    kern = functools.partial(_scan_kernel, n_chunks=tb // SCAN_CHUNK)
    a2, masks = _scan_constants()
    return pl.pallas_call(
        kern,
        grid=(B, G, nc),
        in_specs=[fwd(0), fwd(1), fwd(3), bwd(0), bwd(2), bwd(3),
                  pl.BlockSpec((1, wd), lambda b, h, c: (0, h)),
                  pl.BlockSpec((1, wd), lambda b, h, c: (0, h)),
                  pl.BlockSpec((1, hps, 2, hd, hd), lambda b, h, c: (b, h, 0, 0, 0)),
                  pl.BlockSpec(a2.shape, lambda b, h, c: (0, 0, 0)),
                  pl.BlockSpec(masks.shape, lambda b, h, c: (0, 0, 0, 0))],
        out_specs=[pl.BlockSpec((1, tb, wd), lambda b, h, c: (b, c, h)),
                   pl.BlockSpec((1, tb, wd), lambda b, h, c: (b, nc - 1 - c, h)),
                   pl.BlockSpec((1, hps, 2, hd, hd), lambda b, h, c: (b, h, 0, 0, 0))],
        out_shape=[jax.ShapeDtypeStruct((B, T, HGRN_WIDTH), F32),
                   jax.ShapeDtypeStruct((B, T, HGRN_WIDTH), F32),
                   jax.ShapeDtypeStruct((B, H, 2, hd, hd), F32)],
        scratch_shapes=[pltpu.VMEM((hps, 2, hd, hd), F32)],
        compiler_params=_cparams("parallel", "parallel", "arbitrary"),
        name="hgrn_scan",
    )(proj, proj, proj, proj, proj, proj, lb[0:1], lb[1:2], s0, jnp.asarray(a2, BF16), jnp.asarray(masks, F32))


def _group_mean_sq(x, gmat):
    sq = x * x
    hi = sq.astype(BF16)
    lo = (sq - hi.astype(F32)).astype(BF16)
    return (jnp.dot(hi, gmat, preferred_element_type=F32) + jnp.dot(lo, gmat, preferred_element_type=F32))


def _qkv_prep_kernel(q_ref, k_ref, v_ref, qg_ref, kg_ref, cos_ref, sin_ref, qo_ref, ko_ref, vo_ref, *, rotary):
    W = LANES
    r_i = lax.broadcasted_iota(jnp.int32, (W, W), 0) // DIFF_HEAD_DIM
    c_i = lax.broadcasted_iota(jnp.int32, (W, W), 1) // DIFF_HEAD_DIM
    gmat = jnp.where(r_i == c_i, 1.0 / DIFF_HEAD_DIM, 0.0).astype(BF16)
    lane = lax.broadcasted_iota(jnp.int32, (1, W), 1)
    first = (lane % 32) < 16

    def prep(x, g, scale):
        y = (x * lax.rsqrt(_group_mean_sq(x, gmat) + EPS)) * g
        if rotary:
            partner = jnp.where(first, pltpu.roll(y, W - 16, axis=1), pltpu.roll(y, 16, axis=1))
            y = y * cos_ref[...] + partner * sin_ref[...]
        if scale != 1.0:
            y = y * scale
        return y

    for h in range(DIFF_HEADS):
        sl = slice(h * W, (h + 1) * W)
        qo_ref[0, sl, :] = prep(q_ref[0, :, sl], qg_ref[...], DIFF_HEAD_DIM ** -0.5).T.astype(BF16)
        ko_ref[0, :, sl] = prep(k_ref[0, :, sl], kg_ref[...], 1.0).astype(BF16)
        vo_ref[0, sl, :] = v_ref[0, :, sl].T.astype(BF16)


def qkv_prep(proj, qg, kg, cos, sin, *, rotary, tm=512):
    B, T, _ = proj.shape
    tm = _row_tile(T, tm)
    Wd = DIFF_WIDTH
    col = lambda j: pl.BlockSpec((1, tm, Wd), lambda b, i: (b, i, j))
    vec = pl.BlockSpec((1, LANES), lambda b, i: (0, 0))
    tab = pl.BlockSpec((tm, LANES), lambda b, i: (i, 0))
    rows = pl.BlockSpec((1, tm, Wd), lambda b, i: (b, i, 0))
    cols = pl.BlockSpec((1, Wd, tm), lambda b, i: (b, 0, i))
    return pl.pallas_call(
        functools.partial(_qkv_prep_kernel, rotary=rotary),
        grid=(B, T // tm),
        in_specs=[col(5), col(6), col(7), vec, vec, tab, tab],
        out_specs=[cols, rows, cols],
        out_shape=[jax.ShapeDtypeStruct((B, Wd, T), BF16), jax.ShapeDtypeStruct((B, T, Wd), BF16),
                   jax.ShapeDtypeStruct((B, Wd, T), BF16)],
        compiler_params=_cparams("parallel", "parallel"),
        name="qkv_prep",
    )(proj, proj, proj, jnp.tile(qg, 2).reshape(1, LANES), jnp.tile(kg, 2).reshape(1, LANES), cos, sin)


def rope_tables(T):
    n = DIFF_HEAD_DIM // 2
    inv = 1.0 / (ROPE_THETA ** (jnp.arange(0, n, 2, dtype=F32) / n))
    t = jnp.arange(T)
    ang_r = (t // GRID_W).astype(F32)[:, None] * inv[None, :]
    ang_c = (t % GRID_W).astype(F32)[:, None] * inv[None, :]
    cos = jnp.concatenate([jnp.cos(ang_r)] * 2 + [jnp.cos(ang_c)] * 2, axis=-1)
    sin = jnp.concatenate([-jnp.sin(ang_r), jnp.sin(ang_r), -jnp.sin(ang_c), jnp.sin(ang_c)], axis=-1)
    return jnp.tile(cos, (1, 2)), jnp.tile(sin, (1, 2))


def _attn_tile(qt, k_ref, v_ref, lam, key_chunk):
    tq = qt.shape[1]
    S = k_ref.shape[1]
    row = lax.broadcasted_iota(jnp.int32, (LANES, 1), 0)
    zero = jnp.zeros_like(qt)
    qq = jnp.concatenate([jnp.where(row < DIFF_HEAD_DIM, qt, zero),
                          jnp.where(row >= DIFF_HEAD_DIM, qt, zero)], axis=1)
    m = jnp.full((1, 2 * tq), -jnp.inf, F32)
    acc = [jnp.zeros((LANES + 16, tq), F32), jnp.zeros((LANES + 16, tq), F32)]
    ones = jnp.ones((16, key_chunk), BF16)
    chunks = [(c0, min(c0 + key_chunk, S)) for c0 in range(0, S, key_chunk)]
    scores = lambda c: jnp.dot(k_ref[0, c[0]:c[1], :], qq, preferred_element_type=F32)
    s_next = scores(chunks[0])
    yield
    for n, (c0, c1) in enumerate(chunks):
        s = s_next
        if n + 1 < len(chunks):
            s_next = scores(chunks[n + 1])
        m_new = jnp.maximum(m, jnp.max(s, axis=0, keepdims=True))
        alpha = jnp.exp(m - m_new)
        pb = jnp.exp((s - m_new).astype(BF16))
        vt1 = jnp.concatenate([v_ref[0, :, c0:c1], ones[:, :c1 - c0]], axis=0)
        for i in range(2):
            acc[i] = acc[i] * alpha[:, i * tq:(i + 1) * tq] + jnp.dot(
                vt1, pb[:, i * tq:(i + 1) * tq], preferred_element_type=F32)
        m = m_new
        yield
    inv = [1.0 / a[LANES:LANES + 1] for a in acc]
    return acc[0][:LANES] * inv[0] - acc[1][:LANES] * (lam * inv[1])


def _diff_attn_kernel(lam_ref, q_ref, k_ref, v_ref, g_ref, o_ref, *, out_scale, key_chunk, tq):
    n_tiles = q_ref.shape[2] // tq
    outs = _run_interleaved([_attn_tile(q_ref[0, :, i * tq:(i + 1) * tq], k_ref, v_ref, lam_ref[0], key_chunk)
                             for i in range(n_tiles)])
    for i, o in enumerate(outs):
        ms = jnp.mean(o * o, axis=0, keepdims=True)
        o_ref[0, i * tq:(i + 1) * tq, :] = ((o * lax.rsqrt(ms + EPS)) * g_ref[...] * out_scale).T


def diff_attention(qt, k, vt, lam, subln_g, out_scale, *, tq=256, tiles_per_step=2, key_chunk=512):
    B, Wd, T = qt.shape
    S = k.shape[1]
    tq = _row_tile(T, tq)
    ts = _row_tile(T, tq * tiles_per_step)
    return pl.pallas_call(
        functools.partial(_diff_attn_kernel, out_scale=out_scale, key_chunk=key_chunk, tq=tq),
        grid=(B, DIFF_HEADS, T // ts),
        in_specs=[pl.BlockSpec(memory_space=pltpu.SMEM),
                  pl.BlockSpec((1, LANES, ts), lambda b, h, i: (b, h, i)),
                  pl.BlockSpec((1, S, LANES), lambda b, h, i: (b, 0, h)),
                  pl.BlockSpec((1, LANES, S), lambda b, h, i: (b, h, 0)),
                  pl.BlockSpec((LANES, 1), lambda b, h, i: (0, 0))],
        out_specs=pl.BlockSpec((1, ts, LANES), lambda b, h, i: (b, i, h)),
        out_shape=jax.ShapeDtypeStruct((B, T, Wd), F32),
        compiler_params=_cparams("parallel", "parallel", "arbitrary"),
        name="diff_attention",
    )(lam.reshape(1), qt, k, vt, subln_g.reshape(LANES, 1))


def _even_out_kernel(x_ref, of_ref, ob_ref, gate_ref, yd_ref, hg_ref, g1_ref, w_ref, o_ref):
    acc = jnp.dot(yd_ref[0].astype(BF16), w_ref[0, HGRN_WIDTH:, :].astype(BF16), preferred_element_type=F32)
    for h in range(HGRN_HEADS):
        sl = slice(h * HGRN_HEAD_DIM, (h + 1) * HGRN_HEAD_DIM)
        o = of_ref[0, :, sl] + ob_ref[0, :, sl]
        ms = jnp.mean(o * o, axis=-1, keepdims=True)
        gate = gate_ref[0, :, sl]
        yh = (o * lax.rsqrt(ms + EPS)) * hg_ref[:, sl] * (gate * jax.nn.sigmoid(gate))
        acc = acc + jnp.dot(yh.astype(BF16), w_ref[0, sl, :].astype(BF16), preferred_element_type=F32)
    o_ref[0] = x_ref[0] + g1_ref[0] * acc


def even_out(x, o_f, o_b, proj, y_d, hgrn_g, g1, w_out, *, tm=512):
    B, T, D = x.shape
    w_out, li = w_out
    tm = _row_tile(T, tm)
    Wd = HGRN_WIDTH
    row = lambda w, j: pl.BlockSpec((1, tm, w), lambda b, i: (b, i, j))
    return pl.pallas_call(
        _even_out_kernel,
        grid=(B, T // tm),
        in_specs=[row(D, 0), row(Wd, 0), row(Wd, 0), row(Wd, 4), row(Wd, 0),
                  pl.BlockSpec((1, Wd), lambda b, i: (0, 0)),
                  pl.BlockSpec((1, 1, D), lambda b, i: (b, 0, 0)),
                  pl.BlockSpec((1,) + w_out.shape[1:], lambda b, i: (li, 0, 0))],
        out_specs=row(D, 0),
        out_shape=jax.ShapeDtypeStruct((B, T, D), F32),
        compiler_params=_cparams("parallel", "parallel"),
        name="even_out",
    )(x, o_f, o_b, proj, y_d, hgrn_g.reshape(1, Wd), g1, w_out)


def _conv_out_kernel(x_ref, bg_ref, cg_ref, v_ref, cp_ref, vp_ref, cn_ref, vn_ref, cw_ref, g1_ref, w_ref, o_ref):
    i = pl.program_id(1)
    n = pl.num_programs(1)
    u = cg_ref[0] * v_ref[0]
    tm = u.shape[0]
    u_prev_row = jnp.where(i > 0, cp_ref[0, 7:8, :] * vp_ref[0, 7:8, :], 0.0)
    u_next_row = jnp.where(i < n - 1, cn_ref[0, 0:1, :] * vn_ref[0, 0:1, :], 0.0)
    ridx = lax.broadcasted_iota(jnp.int32, (tm, 1), 0)
    u_prev = jnp.where(ridx == 0, u_prev_row, pltpu.roll(u, 1, axis=0))
    u_next = jnp.where(ridx == tm - 1, u_next_row, pltpu.roll(u, tm - 1, axis=0))
    y = cw_ref[0:1, :] * u_prev + cw_ref[1:2, :] * u + cw_ref[2:3, :] * u_next
    acc = jnp.dot((bg_ref[0] * y).astype(BF16), w_ref[0].astype(BF16), preferred_element_type=F32)
    o_ref[0] = x_ref[0] + g1_ref[0] * acc


def conv_out(x, proj, conv_w, g1, w_out, *, tm=512):
    B, T, D = x.shape
    w_out, li = w_out
    tm = _row_tile(T, tm)
    r8 = tm // 8
    last8 = T // 8 - 1
    row = lambda j: pl.BlockSpec((1, tm, D), lambda b, i: (b, i, j))
    prev = lambda j: pl.BlockSpec((1, 8, D), lambda b, i: (b, jnp.maximum(i * r8 - 1, 0), j))
    nxt = lambda j: pl.BlockSpec((1, 8, D), lambda b, i: (b, jnp.minimum((i + 1) * r8, last8), j))
    return pl.pallas_call(
        _conv_out_kernel,
        grid=(B, T // tm),
        in_specs=[row(0), row(0), row(1), row(2), prev(1), prev(2), nxt(1), nxt(2),
                  pl.BlockSpec((8, D), lambda b, i: (0, 0)),
                  pl.BlockSpec((1, 1, D), lambda b, i: (b, 0, 0)),
                  pl.BlockSpec((1,) + w_out.shape[1:], lambda b, i: (li, 0, 0))],
        out_specs=row(0),
        out_shape=jax.ShapeDtypeStruct((B, T, D), F32),
        compiler_params=_cparams("parallel", "parallel"),
        name="conv_out",
    )(x, proj, proj, proj, proj, proj, proj, proj,
      jnp.concatenate([conv_w, jnp.zeros((8 - conv_w.shape[0], D), conv_w.dtype)], axis=0), g1, w_out)


def _router_kernel(x_ref, g_ref, sh_ref, sc_ref, rw_ref, h_ref, lg_ref):
    x = x_ref[0]
    ms = jnp.mean(x * x, axis=-1, keepdims=True)
    h = (x * lax.rsqrt(ms + EPS)) * g_ref[...]
    h = h * (1.0 + sc_ref[0]) + sh_ref[0]
    h_ref[0] = h.astype(BF16)
    rw = rw_ref[...]
    h_hi = h.astype(BF16)
    h_lo = (h - h_hi.astype(F32)).astype(BF16)
    w_hi = rw.astype(BF16)
    w_lo = (rw - w_hi.astype(F32)).astype(BF16)
    d = lambda a, b: jnp.dot(a, b, preferred_element_type=F32)
    lg_ref[0] = d(h_hi, w_hi) + d(h_hi, w_lo) + d(h_lo, w_hi)


def norm_mod_router(x, g, shift, scale, router_w, *, tm=512):
    B, T, D = x.shape
    E = router_w.shape[1]
    tm = _row_tile(T, tm)
    return pl.pallas_call(
        _router_kernel,
        grid=(B, T // tm),
        in_specs=[pl.BlockSpec((1, tm, D), lambda b, i: (b, i, 0)),
                  pl.BlockSpec((1, D), lambda b, i: (0, 0)),
                  pl.BlockSpec((1, 1, D), lambda b, i: (b, 0, 0)),
                  pl.BlockSpec((1, 1, D), lambda b, i: (b, 0, 0)),
                  pl.BlockSpec((D, E), lambda b, i: (0, 0))],
        out_specs=[pl.BlockSpec((1, tm, D), lambda b, i: (b, i, 0)),
                   pl.BlockSpec((1, tm, E), lambda b, i: (b, i, 0))],
        out_shape=[jax.ShapeDtypeStruct((B, T, D), BF16), jax.ShapeDtypeStruct((B, T, E), F32)],
        compiler_params=_cparams("parallel", "parallel"),
        name="norm_mod_router",
    )(x, g.reshape(1, D), shift, scale, router_w)


def _lane_prefix(flags, tri_tot):
    E, N = flags.shape
    carries = [jnp.zeros((E, LANES), F32)]
    out = []
    for j in range(N // LANES):
        r = jnp.dot(flags[:, j * LANES:(j + 1) * LANES].astype(BF16), tri_tot, preferred_element_type=F32)
        out.append(r[:, :LANES] + carries[-1])
        carries.append(carries[-1] + r[:, LANES:])
    return jnp.concatenate(out, axis=1), carries


def _route_kernel(lg_ref, aff_ref, rank_ref, ts_ref, *, cap, tt):
    lg = lg_ref[0]
    E, N = lg.shape
    p = jnp.exp(lg - jnp.max(lg, axis=0, keepdims=True))
    aff = p / jnp.sum(p, axis=0, keepdims=True)
    aff_ref[0] = aff
    count = lambda m: jnp.sum(jnp.where(m, 1.0, 0.0), axis=1, keepdims=True)
    as_float = lambda i: pltpu.bitcast(i, F32)

    def refine_bits(i, thr):
        cand = thr | jnp.left_shift(jnp.int32(1), 30 - i)
        return jnp.where(count(aff >= as_float(cand)) >= cap, cand, thr)

    thr = lax.fori_loop(0, 31, refine_bits, jnp.zeros((E, 1), jnp.int32))

    def refine_mid(i, lo_hi):
        lo, hi = lo_hi
        mid = 0.5 * (lo + hi)
        up = count(aff >= mid) >= cap
        return jnp.where(up, mid, lo), jnp.where(up, hi, mid)

    lo, hi = lax.fori_loop(0, 24, refine_mid, (as_float(thr), as_float(jnp.maximum(thr + 1, 0x00800000))))
    gt = aff >= hi
    eq = (aff >= lo) & (aff < hi)
    r_i = lax.broadcasted_iota(jnp.int32, (LANES, 2 * LANES), 0)
    c_i = lax.broadcasted_iota(jnp.int32, (LANES, 2 * LANES), 1)
    tri_tot = jnp.where((r_i < c_i) | (c_i >= LANES), 1.0, 0.0).astype(BF16)
    eq_rank, _ = _lane_prefix(jnp.where(eq, 1.0, 0.0), tri_tot)
    sel = gt | (eq & (eq_rank < cap - count(gt)))
    rank, before = _lane_prefix(jnp.where(sel, 1.0, 0.0), tri_tot)
    rank_ref[0] = jnp.where(sel, rank, -1.0).astype(jnp.int32)
    lane = lax.broadcasted_iota(jnp.int32, (E, LANES), 1)
    ts = jnp.zeros((E, LANES), F32)
    for k in range(N // tt + 1):
        ts = jnp.where(lane == k, before[k * tt // LANES], ts)
    ts_ref[0] = ts.astype(jnp.int32)


def route(logits_t, cap, tt):
    B, E, N = logits_t.shape
    blk = pl.BlockSpec((1, E, N), lambda b: (b, 0, 0))
    return pl.pallas_call(
        functools.partial(_route_kernel, cap=cap, tt=tt),
        grid=(B,),
        in_specs=[blk],
        out_specs=[blk, blk, pl.BlockSpec((1, E, LANES), lambda b: (b, 0, 0))],
        out_shape=[jax.ShapeDtypeStruct((B, E, N), F32), jax.ShapeDtypeStruct((B, E, N), jnp.int32),
                   jax.ShapeDtypeStruct((B, E, LANES), jnp.int32)],
        compiler_params=_cparams("parallel"),
        name="route",
    )(logits_t)


def _window(lo, w, win, cap):
    lower = (lo // ROW_ALIGN) * ROW_ALIGN + w * win
    return pl.multiple_of(jnp.minimum(lower, cap - win), ROW_ALIGN), lower


def _extra_windows(cap, tt, win):
    return -(-(min(cap, tt) + ROW_ALIGN - 1) // win) - 1


def _moe_gather_kernel(ts_ref, rank_ref, h_ref, xe_ref, acc_ref, *, win, tt):
    b, e = pl.program_id(0), pl.program_id(1)
    N = h_ref.shape[1]
    cap = xe_ref.shape[2]
    nt = N // tt
    base = (b * pl.num_programs(1) + e) * (nt + 1)
    acc_ref[...] = jnp.zeros_like(acc_ref)
    row = lax.broadcasted_iota(jnp.int32, (win, tt), 0)

    def place(k, w):
        start, lower = _window(ts_ref[base + k], w, win, cap)
        c0 = k * tt if isinstance(k, int) else pl.multiple_of(k * tt, tt)
        rk = rank_ref[0, 0, :, pl.ds(c0, tt)]
        onehot = jnp.where(jnp.where(rk >= lower, rk, -1) == row + start, 1.0, 0.0).astype(BF16)
        acc_ref[pl.ds(start, win), :] += jnp.dot(onehot, h_ref[0, pl.ds(c0, tt), :], preferred_element_type=F32)

    span = lambda k: ts_ref[base + k + 1] - (ts_ref[base + k] // ROW_ALIGN) * ROW_ALIGN
    need = jnp.int32(0)
    for k in range(nt):
        place(k, 0)
        need = jnp.maximum(need, span(k))

    n_extra = _extra_windows(cap, tt, win)

    @pl.when(need > win)
    def _():
        def extra(i, carry):
            k, w = i // n_extra, i % n_extra + 1

            @pl.when(span(k) > w * win)
            def _():
                place(k, w)
            return carry

        lax.fori_loop(0, nt * n_extra, extra, 0)

    xe_ref[0, 0] = acc_ref[...].astype(BF16)


def moe_gather(ts_flat, rank, h, cap, *, tt):
    B, E, N = rank.shape
    D = h.shape[2]
    return pl.pallas_call(
        functools.partial(_moe_gather_kernel, win=min(LANES, cap), tt=tt),
        grid_spec=pltpu.PrefetchScalarGridSpec(
            num_scalar_prefetch=1,
            grid=(B, E),
            in_specs=[pl.BlockSpec((1, 1, 1, N), lambda b, e, ts: (b, e, 0, 0)),
                      pl.BlockSpec((1, N, D), lambda b, e, ts: (b, 0, 0))],
            out_specs=pl.BlockSpec((1, 1, cap, D), lambda b, e, ts: (b, e, 0, 0)),
            scratch_shapes=[pltpu.VMEM((cap, D), F32)]),
        out_shape=jax.ShapeDtypeStruct((B, E, cap, D), BF16),
        compiler_params=_cparams("parallel", "arbitrary"),
        name="moe_gather",
    )(ts_flat, rank.reshape(B, E, 1, N), h)


def _expert_kernel(*refs, n_streams):
    xe_refs = refs[:n_streams]
    wg_ref, wu_ref, wd_ref = refs[n_streams:n_streams + 3]
    o_refs = refs[n_streams + 3:2 * n_streams + 3]
    wg_scr, wu_scr, wd_scr = refs[2 * n_streams + 3:]

    @pl.when(pl.program_id(1) == 0)
    def _():
        wg_scr[...] = wg_ref[0, 0].astype(BF16)
        wu_scr[...] = wu_ref[0, 0].astype(BF16)
        wd_scr[...] = wd_ref[0, 0].astype(BF16)

    xe = jnp.concatenate([r[0, 0] for r in xe_refs], axis=0)
    a = jnp.dot(xe, wg_scr[...], preferred_element_type=F32)
    u = jnp.dot(xe, wu_scr[...], preferred_element_type=F32)
    hid = (a * jax.nn.sigmoid(a)) * u
    y = jnp.dot(hid.astype(BF16), wd_scr[...], preferred_element_type=F32).astype(BF16)
    r0 = 0
    for o_ref in o_refs:
        o_ref[0, 0] = y[r0:r0 + o_ref.shape[2]]
        r0 += o_ref.shape[2]


def expert_ffn(xes, w_gate, w_up, w_down, li):
    B, E, _, D = xes[0].shape
    FF = w_gate.shape[3]
    rows = [pl.BlockSpec((1, 1, xe.shape[2], D), lambda e, b: (b, e, 0, 0)) for xe in xes]
    return pl.pallas_call(
        functools.partial(_expert_kernel, n_streams=len(xes)),
        grid=(E, B),
        in_specs=rows + [pl.BlockSpec((1, 1, D, FF), lambda e, b: (li, e, 0, 0)),
                         pl.BlockSpec((1, 1, D, FF), lambda e, b: (li, e, 0, 0)),
                         pl.BlockSpec((1, 1, FF, D), lambda e, b: (li, e, 0, 0))],
        out_specs=rows,
        out_shape=[jax.ShapeDtypeStruct(xe.shape, BF16) for xe in xes],
        scratch_shapes=[pltpu.VMEM((D, FF), BF16), pltpu.VMEM((D, FF), BF16), pltpu.VMEM((FF, D), BF16)],
        compiler_params=_cparams("arbitrary", "arbitrary"),
        name="expert_ffn",
    )(*xes, w_gate, w_up, w_down)


def _moe_combine_kernel(ts_ref, rank_ref, aff_ref, ye_ref, x_ref, g2_ref, o_ref, acc_ref, *, win, group):
    b, k = pl.program_id(0), pl.program_id(1)
    E, cap = ye_ref.shape[1], ye_ref.shape[2]
    tt = x_ref.shape[1]
    nt = pl.num_programs(1)
    col = lax.broadcasted_iota(jnp.int32, (tt, win), 1)
    lo = [ts_ref[(b * E + e) * (nt + 1) + k] for e in range(E)]
    hi = [ts_ref[(b * E + e) * (nt + 1) + k + 1] for e in range(E)]

    def contribution(w):
        total = None
        for g0 in range(0, E, group):
            lhs, rhs = [], []
            for e in range(g0, g0 + group):
                start, lower = _window(lo[e], w, win, cap)
                rk = rank_ref[0, :, e:e + 1]
                hit = jnp.where(rk >= lower, rk, -1) == col + start
                lhs.append(jnp.where(hit, aff_ref[0, :, e:e + 1], 0.0).astype(BF16))
                rhs.append(ye_ref[0, e, pl.ds(start, win), :])
            d = jnp.dot(jnp.concatenate(lhs, axis=1), jnp.concatenate(rhs, axis=0), preferred_element_type=F32)
            total = d if total is None else total + d
        return total

    acc_ref[...] = contribution(0)
    need = jnp.int32(0)
    for e in range(E):
        need = jnp.maximum(need, hi[e] - (lo[e] // ROW_ALIGN) * ROW_ALIGN)

    @pl.when(need > win)
    def _():
        def extra(w, carry):
            @pl.when(need > w * win)
            def _():
                acc_ref[...] += contribution(w)
            return carry

        lax.fori_loop(1, _extra_windows(cap, tt, win) + 1, extra, 0)

    o_ref[0] = x_ref[0] + g2_ref[0] * acc_ref[...]


def moe_combine(ts_flat, rank_t, aff_t, ye, x, g2, *, tt):
    B, N, D = x.shape
    E, cap = ye.shape[1], ye.shape[2]
    tok = lambda w: pl.BlockSpec((1, tt, w), lambda b, k, ts: (b, k, 0))
    return pl.pallas_call(
        functools.partial(_moe_combine_kernel, win=min(LANES, cap), group=4),
        grid_spec=pltpu.PrefetchScalarGridSpec(
            num_scalar_prefetch=1,
            grid=(B, N // tt),
            in_specs=[tok(E), tok(E),
                      pl.BlockSpec((1, E, cap, D), lambda b, k, ts: (b, 0, 0, 0)),
                      tok(D),
                      pl.BlockSpec((1, 1, D), lambda b, k, ts: (b, 0, 0))],
            out_specs=tok(D),
            scratch_shapes=[pltpu.VMEM((tt, D), F32)]),
        out_shape=jax.ShapeDtypeStruct((B, N, D), F32),
        compiler_params=_cparams("parallel", "arbitrary"),
        name="moe_combine",
    )(ts_flat, rank_t, aff_t, ye, x, g2)


def moe_residual(streams, g, router_w, w_gate, w_up, w_down, li):
    routed = []
    for x, shift, scale, _ in streams:
        B, N, D = x.shape
        cap = EC_CAPACITY * N // N_EXPERTS
        tt = min(4 * LANES, N)
        h, logits = norm_mod_router(x, g, shift, scale, router_w)
        aff, rank, ts = route(jnp.swapaxes(logits, 1, 2), cap, tt)
        ts_flat = ts[:, :, :N // tt + 1].reshape(-1)
        routed.append((moe_gather(ts_flat, rank, h, cap, tt=tt), ts_flat, rank, aff, tt))
    yes = expert_ffn([r[0] for r in routed], w_gate, w_up, w_down, li)
    return [moe_combine(ts_flat, jnp.swapaxes(rank, 1, 2), jnp.swapaxes(aff, 1, 2), ye, x, gate2, tt=tt)
            for (x, _, _, gate2), ye, (_, ts_flat, rank, aff, tt) in zip(streams, yes, routed)]


def lambda_init(layer):
    return 0.8 - 0.6 * math.exp(-0.3 * layer)


def even_layer(x, xc, mods, cmods, norm1_g, w_in, w_out, lb, hgrn_g, qn_g, kn_g, lam_vec, subln_g,
               lam_init, ctx_out, tables):
    sh1, sc1, g1 = mods
    csh1, csc1, cg1 = cmods
    B = x.shape[0]
    proj = norm_mod_matmul(x, norm1_g, sh1, sc1, w_in)
    projc = norm_mod_matmul(xc, norm1_g, csh1, csc1, w_in)
    s0 = jnp.zeros((B, HGRN_HEADS, 2, HGRN_HEAD_DIM, HGRN_HEAD_DIM), F32)
    oc_f, oc_b, s_ctx = hgrn_scan(projc, lb, s0)
    o_f, o_b, _ = hgrn_scan(proj, lb, s_ctx)
    lv = lam_vec.astype(F32)
    lam = jnp.exp(jnp.sum(lv[0] * lv[1])) - jnp.exp(jnp.sum(lv[2] * lv[3])) + lam_init
    cos, sin = tables
    q, k, v = qkv_prep(proj, qn_g, kn_g, cos, sin, rotary=True)
    qc, kc, vc = qkv_prep(projc, qn_g, kn_g, cos[:xc.shape[1]], sin[:xc.shape[1]], rotary=False)
    k_all = jnp.concatenate([k, kc], axis=1)
    v_all = jnp.concatenate([v, vc], axis=2)
    y_d = diff_attention(q, k_all, v_all, lam, subln_g, 1.0 - lam_init)
    x_new = even_out(x, o_f, o_b, proj, y_d, hgrn_g.reshape(-1), g1, w_out)
    if not ctx_out:
        return x_new, None
    yc_d = diff_attention(qc, kc, vc, lam, subln_g, 1.0 - lam_init)
    xc_new = even_out(xc, oc_f, oc_b, projc, yc_d, hgrn_g.reshape(-1), cg1, w_out)
    return x_new, xc_new


def conv_layer(x, mods, norm1_g, w_in, conv_w, w_out):
    sh1, sc1, g1 = mods
    proj = norm_mod_matmul(x, norm1_g, sh1, sc1, w_in)
    return conv_out(x, proj, conv_w, g1, w_out)


def kernel(x, c, ctx, c_ctx, mod_w, mod_b, norm1_g, norm2_g, even_w_in, even_w_out, hgrn_lb_logits, hgrn_norm_g,
           diff_qnorm_g, diff_knorm_g, diff_lambda, diff_subln_g, conv_w_in, conv_w, conv_w_out, router_w,
           exp_w_gate, exp_w_up, exp_w_down):
    depth = mod_w.shape[0]
    B, T, D = x.shape
    lb_soft = jax.nn.softmax(hgrn_lb_logits.astype(F32), axis=0)
    lower_bounds = jnp.cumsum(lb_soft, axis=0) - lb_soft[:1]
    last_ctx_layer = 2 * ((depth - 1) // 2)
    cond = jnp.concatenate([c, c_ctx[None, :], jnp.zeros((8 - (B + 1) % 8, D), F32)], axis=0)
    mods = modulation(jax.nn.silu(cond), mod_w, mod_b)
    tables = rope_tables(T)
    xc = ctx
    for l in range(depth):
        read_ctx = l <= last_ctx_layer
        ctx_out = l < last_ctx_layer
        sh1, sc1, g1, sh2, sc2, g2 = [m[:, None, :] for m in jnp.split(mods[l, :B], MOD_CHUNKS, axis=-1)]
        if read_ctx:
            csh1, csc1, cg1, csh2, csc2, cg2 = [
                jnp.broadcast_to(m[None, None, :], (B, 1, D)) for m in jnp.split(mods[l, B], MOD_CHUNKS, axis=-1)]
        ew = (exp_w_gate, exp_w_up, exp_w_down, l)
        if l % 2 == 0:
            e = l // 2
            x, xc_new = even_layer(x, xc, (sh1, sc1, g1), (csh1, csc1, cg1), norm1_g[l],
                                   (even_w_in, e), (even_w_out, e), lower_bounds[e],
                                   hgrn_norm_g[e], diff_qnorm_g[e], diff_knorm_g[e], diff_lambda[e],
                                   diff_subln_g[e], lambda_init(l), ctx_out, tables)
        else:
            j = l // 2
            wi, wo = (conv_w_in, j), (conv_w_out, j)
            x = conv_layer(x, (sh1, sc1, g1), norm1_g[l], wi, conv_w[j], wo)
            xc_new = conv_layer(xc, (csh1, csc1, cg1), norm1_g[l], wi, conv_w[j], wo) if ctx_out else None
        streams = [(x, sh2, sc2, g2)] + ([(xc_new, csh2, csc2, cg2)] if ctx_out else [])
        outs = moe_residual(streams, norm2_g[l], router_w[l], *ew)
        x = outs[0]
        if ctx_out:
            xc = outs[1]
    return x
```

```python
import functools
import math

import numpy as np
import jax
import jax.numpy as jnp
from jax import lax
from jax.experimental import pallas as pl
from jax.experimental.pallas import tpu as pltpu

F32 = jnp.float32
BF16 = jnp.bfloat16

EPS = 1e-6
GRID_W = 64
ROPE_THETA = 10000.0
HGRN_HEAD_DIM = 128
HGRN_HEADS = 4
HGRN_WIDTH = HGRN_HEADS * HGRN_HEAD_DIM
DIFF_HEAD_DIM = 64
DIFF_HEADS = 4
DIFF_WIDTH = DIFF_HEADS * 2 * DIFF_HEAD_DIM
N_EXPERTS = 16
EC_CAPACITY = 2
MOD_CHUNKS = 6
SCAN_CHUNK = 64
SCAN_LEVELS = (32, 16, 8, 4, 2, 1)
SCAN_HEADS_PER_STEP = 2
LANES = 128
ROW_ALIGN = 16
VMEM_LIMIT = 56 * 1024 * 1024
SCORE_BOUND_MAX = 40.0


def _cparams(*sem):
    return pltpu.CompilerParams(dimension_semantics=sem, vmem_limit_bytes=VMEM_LIMIT)


def _row_tile(t, want):
    return want if t % want == 0 else t


def _mod_kernel(s_ref, w_ref, b_ref, o_ref):
    s = s_ref[...]
    w = w_ref[0]
    s_hi = s.astype(BF16)
    s_lo = (s - s_hi.astype(F32)).astype(BF16)
    w_hi = w.astype(BF16)
    w_lo = (w - w_hi.astype(F32)).astype(BF16)
    d = lambda a, b: jnp.dot(a, b, preferred_element_type=F32)
    o_ref[0] = d(s_hi, w_hi) + d(s_hi, w_lo) + d(s_lo, w_hi) + b_ref[0]


def modulation(s, mod_w, mod_b, *, tn=512):
    R, D = s.shape
    depth, _, N = mod_w.shape
    return pl.pallas_call(
        _mod_kernel,
        grid=(depth, N // tn),
        in_specs=[pl.BlockSpec((R, D), lambda l, j: (0, 0)),
                  pl.BlockSpec((1, D, tn), lambda l, j: (l, 0, j)),
                  pl.BlockSpec((1, 1, tn), lambda l, j: (l, 0, j))],
        out_specs=pl.BlockSpec((1, R, tn), lambda l, j: (l, 0, j)),
        out_shape=jax.ShapeDtypeStruct((depth, R, N), F32),
        compiler_params=_cparams("parallel", "parallel"),
        name="modulation",
    )(s, mod_w, mod_b.reshape(depth, 1, N))


def _nmm_kernel(x_ref, g_ref, sh_ref, sc_ref, w_ref, o_ref, w_scr):
    @pl.when((pl.program_id(1) == 0) & (pl.program_id(2) == 0))
    def _():
        w_scr[...] = w_ref[0].astype(BF16)

    x = x_ref[0]
    ms = jnp.mean(x * x, axis=-1, keepdims=True)
    h = (x * lax.rsqrt(ms + EPS)) * g_ref[...]
    h = h * (1.0 + sc_ref[0]) + sh_ref[0]
    o_ref[0] = jnp.dot(h.astype(BF16), w_scr[...], preferred_element_type=F32).astype(o_ref.dtype)


def norm_mod_matmul(x, g, shift, scale, w, *, tm=512, n_col_blocks=2, out_dtype=F32):
    B, T, D = x.shape
    w, li = w
    N = w.shape[2]
    tm = _row_tile(T, tm)
    tn = N // n_col_blocks
    return pl.pallas_call(
        _nmm_kernel,
        grid=(N // tn, B, T // tm),
        in_specs=[
            pl.BlockSpec((1, tm, D), lambda j, b, i: (b, i, 0)),
            pl.BlockSpec((1, D), lambda j, b, i: (0, 0)),
            pl.BlockSpec((1, 1, D), lambda j, b, i: (b, 0, 0)),
            pl.BlockSpec((1, 1, D), lambda j, b, i: (b, 0, 0)),
            pl.BlockSpec((1, D, tn), lambda j, b, i: (li, 0, j)),
        ],
        out_specs=pl.BlockSpec((1, tm, tn), lambda j, b, i: (b, i, j)),
        out_shape=jax.ShapeDtypeStruct((B, T, N), out_dtype),
        scratch_shapes=[pltpu.VMEM((D, tn), BF16)],
        compiler_params=_cparams("arbitrary", "arbitrary", "arbitrary"),
        name="norm_mod_matmul",
    )(x, g.reshape(1, D), shift, scale, w)


def _scan_constants():
    C = SCAN_CHUNK
    t = np.arange(C)[:, None]
    u = np.arange(C)[None, :]
    mats = [u <= t, u > t]
    masks = []
    for w in SCAN_LEVELS:
        m = (t // (2 * w)) * 2 * w + w - 1
        later = (t // w) % 2 == 1
        mats.append(np.where(later, (u > m) & (u <= t), (u > t) & (u <= m)))
        masks.append(later & ((u // w) % 2 == 0) & (u // (2 * w) == t // (2 * w)))
    masks = [t == u] + masks + [np.ones((C, C), bool)]
    a_f = np.stack(mats).astype(np.float32)
    m_f = np.stack(masks).astype(np.float32)
    a = np.stack([a_f, a_f[:, ::-1, ::-1]]).reshape(2, -1, C)
    m = np.stack([m_f, m_f[:, ::-1, ::-1]])
    m_pairs = np.concatenate([m[:, 0::2], m[:, 1::2]], axis=3)
    return np.concatenate([a, a], axis=2), m_pairs


def _scan_pair(q, z, v, lb, st, a2, mask_ref, d, later):
    C = SCAN_CHUNK
    W = HGRN_HEAD_DIM
    nl = len(SCAN_LEVELS)
    nt = lambda x, y: lax.dot_general(x, y, (((1,), (1,)), ((), ())), preferred_element_type=F32)
    nn = lambda x, y: jnp.dot(x, y, preferred_element_type=F32)
    e_abs = jnp.exp(-jnp.abs(z))
    r = 1.0 / (1.0 + e_abs)
    er = e_abs * r
    pos = z >= 0.0
    g2 = jnp.log2(lb + (1.0 - lb) * jnp.where(pos, r, er))
    k = (1.0 - lb) * jnp.where(pos, er, r)
    hi = g2.astype(BF16)
    lo = (g2 - hi.astype(F32)).astype(BF16)
    gs = jnp.concatenate([jnp.concatenate([hi[:C], hi[C:]], axis=1),
                          jnp.concatenate([lo[:C], lo[C:]], axis=1)], axis=0)
    x = nn(a2, gs)
    yield
    x = jnp.exp2(x)
    first, second = (1, 0) if d else (0, 1)
    vb = v.astype(BF16)
    sides, qt, kt, dec = [], [], [], []
    for c in (0, 1):
        qc, kc, xs = q[c * C:(c + 1) * C], k[c * C:(c + 1) * C], x[:, c * W:(c + 1) * W]
        ops = [(qc.astype(BF16), kc.astype(BF16))]
        for i in range(nl):
            qk = (jnp.where(later[i], qc, kc) * xs[(2 + i) * C:(3 + i) * C]).astype(BF16)
            ops.append((qk, qk))
        sides.append(ops)
        qt.append(qc * xs[0:C])
        kt.append(kc * xs[C:2 * C])
        dec.append(xs[0:1, :] if d else xs[C - 1:C, :])
    zero = jnp.zeros((C, W), BF16)
    sides[first].append((zero, zero))
    sides[second].append((qt[second].astype(BF16), kt[first].astype(BF16)))
    prods = []
    for ops in sides:
        pair = []
        for j in range(0, nl + 2, 2):
            (qa, ka), (qb, kb) = ops[j], ops[j + 1]
            rhs = jnp.concatenate([jnp.concatenate([ka, zero], axis=1), jnp.concatenate([zero, kb], axis=1)], axis=0)
            pair.append(nt(jnp.concatenate([qa, qb], axis=1), rhs))
        prods.append(pair)
        yield
    qt[second] = qt[second] * dec[first]
    kt[first] = kt[first] * dec[second]
    o_st = nt(jnp.concatenate([a.astype(BF16) for a in qt], axis=0), st.astype(BF16))
    upd = lax.dot_general(vb, jnp.concatenate(kt, axis=0).astype(BF16), (((0,), (0,)), ((), ())),
                          preferred_element_type=F32)
    yield
    o = [None, None]
    n_main = nl // 2
    for c, pair in enumerate(prods):
        main = mask_ref[d, 0] * pair[0]
        for j in range(1, n_main):
            main = main + mask_ref[d, j] * pair[j]
        last = mask_ref[d, n_main] * pair[n_main]
        vc, vf = vb[c * C:(c + 1) * C], vb[first * C:(first + 1) * C]
        o[c] = o_st[c * C:(c + 1) * C] + nn(jnp.concatenate([main, last], axis=1).astype(BF16),
                                            jnp.concatenate([vc, vc, vc, vf], axis=0))
    yield
    return jnp.concatenate(o, axis=0), st * (dec[0] * dec[1]) + upd


def _run_interleaved(gens):
    results = [None] * len(gens)
    live = list(range(len(gens)))
    while live:
        for i in list(live):
            try:
                next(gens[i])
            except StopIteration as done:
                results[i] = done.value
                live.remove(i)
    return results


def _scan_kernel(qf_ref, zf_ref, vf_ref, qb_ref, zb_ref, vb_ref, lbf_ref, lbb_ref, s0_ref, a_ref, mask_ref,
                 of_ref, ob_ref, sT_ref, st_scr, *, n_chunks):
    c = pl.program_id(2)
    C = SCAN_CHUNK

    @pl.when(c == 0)
    def _():
        st_scr[...] = s0_ref[0]

    W = HGRN_HEAD_DIM
    row = lax.broadcasted_iota(jnp.int32, (C, W), 0)
    later = [[(row // w) % 2 == 1 for w in SCAN_LEVELS], [((C - 1 - row) // w) % 2 == 1 for w in SCAN_LEVELS]]
    in_refs = [(qf_ref, zf_ref, vf_ref), (qb_ref, zb_ref, vb_ref)]
    lb_refs = [lbf_ref, lbb_ref]
    o_refs = [of_ref, ob_ref]
    n_pairs = n_chunks // 2
    heads = range(qf_ref.shape[2] // W)

    def body(i, carry):
        r0 = [pl.multiple_of(i * 2 * C, 2 * C), pl.multiple_of((n_pairs - 1 - i) * 2 * C, 2 * C)]
        chains = [(d, h) for h in heads for d in (0, 1)]
        ins = [[ref[0, pl.ds(r0[d], 2 * C), h * W:(h + 1) * W] for ref in in_refs[d]] for d, h in chains]
        sts = [st_scr[h, d] for d, h in chains]
        outs = _run_interleaved([
            _scan_pair(*x, lb_refs[d][:, h * W:(h + 1) * W], st, a_ref[d], mask_ref, d, later[d])
            for (d, h), x, st in zip(chains, ins, sts)])
        for (d, h), (o, st) in zip(chains, outs):
            o_refs[d][0, pl.ds(r0[d], 2 * C), h * W:(h + 1) * W] = o
            st_scr[h, d] = st
        return carry

    lax.fori_loop(0, n_pairs, body, 0)

    @pl.when(c == pl.num_programs(2) - 1)
    def _():
        sT_ref[0] = st_scr[...]


def hgrn_scan(proj, lb, s0, *, tb=512):
    B, T, _ = proj.shape
    tb = _row_tile(T, tb)
    nc = T // tb
    H = HGRN_HEADS
    hd = HGRN_HEAD_DIM
    hps = SCAN_HEADS_PER_STEP
    G = H // hps
    wd = hps * hd
    fwd = lambda grp: pl.BlockSpec((1, tb, wd), lambda b, h, c: (b, c, grp * G + h))
    bwd = lambda grp: pl.BlockSpec((1, tb, wd), lambda b, h, c: (b, nc - 1 - c, grp * G + h))
    kern = functools.partial(_scan_kernel, n_chunks=tb // SCAN_CHUNK)
    a2, masks = _scan_constants()
    return pl.pallas_call(
        kern,
        grid=(B, G, nc),
        in_specs=[fwd(0), fwd(1), fwd(3), bwd(0), bwd(2), bwd(3),
                  pl.BlockSpec((1, wd), lambda b, h, c: (0, h)),
                  pl.BlockSpec((1, wd), lambda b, h, c: (0, h)),
                  pl.BlockSpec((1, hps, 2, hd, hd), lambda b, h, c: (b, h, 0, 0, 0)),
                  pl.BlockSpec(a2.shape, lambda b, h, c: (0, 0, 0)),
                  pl.BlockSpec(masks.shape, lambda b, h, c: (0, 0, 0, 0))],
        out_specs=[pl.BlockSpec((1, tb, wd), lambda b, h, c: (b, c, h)),
                   pl.BlockSpec((1, tb, wd), lambda b, h, c: (b, nc - 1 - c, h)),
                   pl.BlockSpec((1, hps, 2, hd, hd), lambda b, h, c: (b, h, 0, 0, 0))],
        out_shape=[jax.ShapeDtypeStruct((B, T, HGRN_WIDTH), F32),
                   jax.ShapeDtypeStruct((B, T, HGRN_WIDTH), F32),
                   jax.ShapeDtypeStruct((B, H, 2, hd, hd), F32)],
        scratch_shapes=[pltpu.VMEM((hps, 2, hd, hd), F32)],
        compiler_params=_cparams("parallel", "parallel", "arbitrary"),
        name="hgrn_scan",
    )(proj, proj, proj, proj, proj, proj, lb[0:1], lb[1:2], s0, jnp.asarray(a2, BF16), jnp.asarray(masks, F32))


def _group_mean_sq(x, gmat):
    sq = x * x
    hi = sq.astype(BF16)
    lo = (sq - hi.astype(F32)).astype(BF16)
    return (jnp.dot(hi, gmat, preferred_element_type=F32) + jnp.dot(lo, gmat, preferred_element_type=F32))


def _qkv_prep_kernel(q_ref, k_ref, v_ref, qg_ref, kg_ref, cos_ref, sin_ref, qo_ref, ko_ref, vo_ref, *, rotary):
    W = LANES
    r_i = lax.broadcasted_iota(jnp.int32, (W, W), 0) // DIFF_HEAD_DIM
    c_i = lax.broadcasted_iota(jnp.int32, (W, W), 1) // DIFF_HEAD_DIM
    gmat = jnp.where(r_i == c_i, 1.0 / DIFF_HEAD_DIM, 0.0).astype(BF16)
    lane = lax.broadcasted_iota(jnp.int32, (1, W), 1)
    first = (lane % 32) < 16

    def prep(x, g, scale):
        y = (x * lax.rsqrt(_group_mean_sq(x, gmat) + EPS)) * g
        if rotary:
            partner = jnp.where(first, pltpu.roll(y, W - 16, axis=1), pltpu.roll(y, 16, axis=1))
            y = y * cos_ref[...] + partner * sin_ref[...]
        if scale != 1.0:
            y = y * scale
        return y

    for h in range(DIFF_HEADS):
        sl = slice(h * W, (h + 1) * W)
        qo_ref[0, sl, :] = prep(q_ref[0, :, sl], qg_ref[...], DIFF_HEAD_DIM ** -0.5).T.astype(BF16)
        ko_ref[0, :, sl] = prep(k_ref[0, :, sl], kg_ref[...], 1.0).astype(BF16)
        vo_ref[0, sl, :] = v_ref[0, :, sl].T.astype(BF16)


def qkv_prep(proj, qg, kg, cos, sin, *, rotary, tm=512):
    B, T, _ = proj.shape
    tm = _row_tile(T, tm)
    Wd = DIFF_WIDTH
    col = lambda j: pl.BlockSpec((1, tm, Wd), lambda b, i: (b, i, j))
    vec = pl.BlockSpec((1, LANES), lambda b, i: (0, 0))
    tab = pl.BlockSpec((tm, LANES), lambda b, i: (i, 0))
    rows = pl.BlockSpec((1, tm, Wd), lambda b, i: (b, i, 0))
    cols = pl.BlockSpec((1, Wd, tm), lambda b, i: (b, 0, i))
    return pl.pallas_call(
        functools.partial(_qkv_prep_kernel, rotary=rotary),
        grid=(B, T // tm),
        in_specs=[col(5), col(6), col(7), vec, vec, tab, tab],
        out_specs=[cols, rows, cols],
        out_shape=[jax.ShapeDtypeStruct((B, Wd, T), BF16), jax.ShapeDtypeStruct((B, T, Wd), BF16),
                   jax.ShapeDtypeStruct((B, Wd, T), BF16)],
        compiler_params=_cparams("parallel", "parallel"),
        name="qkv_prep",
    )(proj, proj, proj, jnp.tile(qg, 2).reshape(1, LANES), jnp.tile(kg, 2).reshape(1, LANES), cos, sin)


def rope_tables(T):
    n = DIFF_HEAD_DIM // 2
    inv = 1.0 / (ROPE_THETA ** (jnp.arange(0, n, 2, dtype=F32) / n))
    t = jnp.arange(T)
    ang_r = (t // GRID_W).astype(F32)[:, None] * inv[None, :]
    ang_c = (t % GRID_W).astype(F32)[:, None] * inv[None, :]
    cos = jnp.concatenate([jnp.cos(ang_r)] * 2 + [jnp.cos(ang_c)] * 2, axis=-1)
    sin = jnp.concatenate([-jnp.sin(ang_r), jnp.sin(ang_r), -jnp.sin(ang_c), jnp.sin(ang_c)], axis=-1)
    return jnp.tile(cos, (1, 2)), jnp.tile(sin, (1, 2))


def _attn_tile(qt, k_ref, v_ref, lam, key_chunk, bound=None):
    tq = qt.shape[1]
    S = k_ref.shape[1]
    row = lax.broadcasted_iota(jnp.int32, (LANES, 1), 0)
    zero = jnp.zeros_like(qt)
    qq = jnp.concatenate([jnp.where(row < DIFF_HEAD_DIM, qt, zero),
                          jnp.where(row >= DIFF_HEAD_DIM, qt, zero)], axis=1)
    m = jnp.full((1, 2 * tq), -jnp.inf, F32)
    acc = [jnp.zeros((LANES + 16, tq), F32), jnp.zeros((LANES + 16, tq), F32)]
    ones = jnp.ones((16, key_chunk), BF16)
    chunks = [(c0, min(c0 + key_chunk, S)) for c0 in range(0, S, key_chunk)]
    scores = lambda c: jnp.dot(k_ref[0, c[0]:c[1], :], qq, preferred_element_type=F32)
    s_next = scores(chunks[0])
    yield
    for n, (c0, c1) in enumerate(chunks):
        s = s_next
        if n + 1 < len(chunks):
            s_next = scores(chunks[n + 1])
        vt1 = jnp.concatenate([v_ref[0, :, c0:c1], ones[:, :c1 - c0]], axis=0)
        if bound is None:
            m_new = jnp.maximum(m, jnp.max(s, axis=0, keepdims=True))
            alpha = jnp.exp(m - m_new)
            pb = jnp.exp((s - m_new).astype(BF16))
            m = m_new
        else:
            alpha = None
            pb = jnp.exp(s - bound).astype(BF16)
        for i in range(2):
            pv = jnp.dot(vt1, pb[:, i * tq:(i + 1) * tq], preferred_element_type=F32)
            acc[i] = acc[i] + pv if alpha is None else acc[i] * alpha[:, i * tq:(i + 1) * tq] + pv
        yield
    inv = [1.0 / a[LANES:LANES + 1] for a in acc]
    return acc[0][:LANES] * inv[0] - acc[1][:LANES] * (lam * inv[1])


def _diff_attn_kernel(par_ref, q_ref, k_ref, v_ref, g_ref, o_ref, *, out_scale, key_chunk, tq):
    n_tiles = q_ref.shape[2] // tq

    def run(bound):
        outs = _run_interleaved([
            _attn_tile(q_ref[0, :, i * tq:(i + 1) * tq], k_ref, v_ref, par_ref[0], key_chunk, bound)
            for i in range(n_tiles)])
        for i, o in enumerate(outs):
            ms = jnp.mean(o * o, axis=0, keepdims=True)
            o_ref[0, i * tq:(i + 1) * tq, :] = ((o * lax.rsqrt(ms + EPS)) * g_ref[...] * out_scale).T

    @pl.when(par_ref[2] > 0.5)
    def _():
        run(par_ref[1])

    @pl.when(par_ref[2] <= 0.5)
    def _():
        run(None)


def diff_attention(qt, k, vt, params, subln_g, out_scale, *, tq=256, tiles_per_step=2, key_chunk=512):
    B, Wd, T = qt.shape
    S = k.shape[1]
    tq = _row_tile(T, tq)
    ts = _row_tile(T, tq * tiles_per_step)
    return pl.pallas_call(
        functools.partial(_diff_attn_kernel, out_scale=out_scale, key_chunk=key_chunk, tq=tq),
        grid=(B, DIFF_HEADS, T // ts),
        in_specs=[pl.BlockSpec(memory_space=pltpu.SMEM),
                  pl.BlockSpec((1, LANES, ts), lambda b, h, i: (b, h, i)),
                  pl.BlockSpec((1, S, LANES), lambda b, h, i: (b, 0, h)),
                  pl.BlockSpec((1, LANES, S), lambda b, h, i: (b, h, 0)),
                  pl.BlockSpec((LANES, 1), lambda b, h, i: (0, 0))],
        out_specs=pl.BlockSpec((1, ts, LANES), lambda b, h, i: (b, i, h)),
        out_shape=jax.ShapeDtypeStruct((B, T, Wd), F32),
        compiler_params=_cparams("parallel", "parallel", "arbitrary"),
        name="diff_attention",
    )(params, qt, k, vt, subln_g.reshape(LANES, 1))


def _even_out_kernel(x_ref, of_ref, ob_ref, gate_ref, yd_ref, hg_ref, g1_ref, w_ref, o_ref):
    acc = jnp.dot(yd_ref[0].astype(BF16), w_ref[0, HGRN_WIDTH:, :].astype(BF16), preferred_element_type=F32)
    for h in range(HGRN_HEADS):
        sl = slice(h * HGRN_HEAD_DIM, (h + 1) * HGRN_HEAD_DIM)
        o = of_ref[0, :, sl] + ob_ref[0, :, sl]
        ms = jnp.mean(o * o, axis=-1, keepdims=True)
        gate = gate_ref[0, :, sl]
        yh = (o * lax.rsqrt(ms + EPS)) * hg_ref[:, sl] * (gate * jax.nn.sigmoid(gate))
        acc = acc + jnp.dot(yh.astype(BF16), w_ref[0, sl, :].astype(BF16), preferred_element_type=F32)
    o_ref[0] = x_ref[0] + g1_ref[0] * acc


def even_out(x, o_f, o_b, proj, y_d, hgrn_g, g1, w_out, *, tm=512):
    B, T, D = x.shape
    w_out, li = w_out
    tm = _row_tile(T, tm)
    Wd = HGRN_WIDTH
    row = lambda w, j: pl.BlockSpec((1, tm, w), lambda b, i: (b, i, j))
    return pl.pallas_call(
        _even_out_kernel,
        grid=(B, T // tm),
        in_specs=[row(D, 0), row(Wd, 0), row(Wd, 0), row(Wd, 4), row(Wd, 0),
                  pl.BlockSpec((1, Wd), lambda b, i: (0, 0)),
                  pl.BlockSpec((1, 1, D), lambda b, i: (b, 0, 0)),
                  pl.BlockSpec((1,) + w_out.shape[1:], lambda b, i: (li, 0, 0))],
        out_specs=row(D, 0),
        out_shape=jax.ShapeDtypeStruct((B, T, D), F32),
        compiler_params=_cparams("parallel", "parallel"),
        name="even_out",
    )(x, o_f, o_b, proj, y_d, hgrn_g.reshape(1, Wd), g1, w_out)


def _conv_out_kernel(x_ref, bg_ref, cg_ref, v_ref, cp_ref, vp_ref, cn_ref, vn_ref, cw_ref, g1_ref, w_ref, o_ref):
    i = pl.program_id(1)
    n = pl.num_programs(1)
    u = cg_ref[0] * v_ref[0]
    tm = u.shape[0]
    u_prev_row = jnp.where(i > 0, cp_ref[0, 7:8, :] * vp_ref[0, 7:8, :], 0.0)
    u_next_row = jnp.where(i < n - 1, cn_ref[0, 0:1, :] * vn_ref[0, 0:1, :], 0.0)
    ridx = lax.broadcasted_iota(jnp.int32, (tm, 1), 0)
    u_prev = jnp.where(ridx == 0, u_prev_row, pltpu.roll(u, 1, axis=0))
    u_next = jnp.where(ridx == tm - 1, u_next_row, pltpu.roll(u, tm - 1, axis=0))
    y = cw_ref[0:1, :] * u_prev + cw_ref[1:2, :] * u + cw_ref[2:3, :] * u_next
    acc = jnp.dot((bg_ref[0] * y).astype(BF16), w_ref[0].astype(BF16), preferred_element_type=F32)
    o_ref[0] = x_ref[0] + g1_ref[0] * acc


def conv_out(x, proj, conv_w, g1, w_out, *, tm=512):
    B, T, D = x.shape
    w_out, li = w_out
    tm = _row_tile(T, tm)
    r8 = tm // 8
    last8 = T // 8 - 1
    row = lambda j: pl.BlockSpec((1, tm, D), lambda b, i: (b, i, j))
    prev = lambda j: pl.BlockSpec((1, 8, D), lambda b, i: (b, jnp.maximum(i * r8 - 1, 0), j))
    nxt = lambda j: pl.BlockSpec((1, 8, D), lambda b, i: (b, jnp.minimum((i + 1) * r8, last8), j))
    return pl.pallas_call(
        _conv_out_kernel,
        grid=(B, T // tm),
        in_specs=[row(0), row(0), row(1), row(2), prev(1), prev(2), nxt(1), nxt(2),
                  pl.BlockSpec((8, D), lambda b, i: (0, 0)),
                  pl.BlockSpec((1, 1, D), lambda b, i: (b, 0, 0)),
                  pl.BlockSpec((1,) + w_out.shape[1:], lambda b, i: (li, 0, 0))],
        out_specs=row(0),
        out_shape=jax.ShapeDtypeStruct((B, T, D), F32),
        compiler_params=_cparams("parallel", "parallel"),
        name="conv_out",
    )(x, proj, proj, proj, proj, proj, proj, proj,
      jnp.concatenate([conv_w, jnp.zeros((8 - conv_w.shape[0], D), conv_w.dtype)], axis=0), g1, w_out)


def _router_kernel(x_ref, g_ref, sh_ref, sc_ref, rw_ref, h_ref, lg_ref):
    x = x_ref[0]
    ms = jnp.mean(x * x, axis=-1, keepdims=True)
    h = (x * lax.rsqrt(ms + EPS)) * g_ref[...]
    h = h * (1.0 + sc_ref[0]) + sh_ref[0]
    h_ref[0] = h.astype(BF16)
    rw = rw_ref[...]
    h_hi = h.astype(BF16)
    h_lo = (h - h_hi.astype(F32)).astype(BF16)
    w_hi = rw.astype(BF16)
    w_lo = (rw - w_hi.astype(F32)).astype(BF16)
    d = lambda a, b: jnp.dot(a, b, preferred_element_type=F32)
    lg_ref[0] = d(h_hi, w_hi) + d(h_hi, w_lo) + d(h_lo, w_hi)


def norm_mod_router(x, g, shift, scale, router_w, *, tm=512):
    B, T, D = x.shape
    E = router_w.shape[1]
    tm = _row_tile(T, tm)
    return pl.pallas_call(
        _router_kernel,
        grid=(B, T // tm),
        in_specs=[pl.BlockSpec((1, tm, D), lambda b, i: (b, i, 0)),
                  pl.BlockSpec((1, D), lambda b, i: (0, 0)),
                  pl.BlockSpec((1, 1, D), lambda b, i: (b, 0, 0)),
                  pl.BlockSpec((1, 1, D), lambda b, i: (b, 0, 0)),
                  pl.BlockSpec((D, E), lambda b, i: (0, 0))],
        out_specs=[pl.BlockSpec((1, tm, D), lambda b, i: (b, i, 0)),
                   pl.BlockSpec((1, tm, E), lambda b, i: (b, i, 0))],
        out_shape=[jax.ShapeDtypeStruct((B, T, D), BF16), jax.ShapeDtypeStruct((B, T, E), F32)],
        compiler_params=_cparams("parallel", "parallel"),
        name="norm_mod_router",
    )(x, g.reshape(1, D), shift, scale, router_w)


def _lane_prefix(flags, tri_tot):
    E, N = flags.shape
    carries = [jnp.zeros((E, LANES), F32)]
    out = []
    for j in range(N // LANES):
        r = jnp.dot(flags[:, j * LANES:(j + 1) * LANES].astype(BF16), tri_tot, preferred_element_type=F32)
        out.append(r[:, :LANES] + carries[-1])
        carries.append(carries[-1] + r[:, LANES:])
    return jnp.concatenate(out, axis=1), carries


def _route_kernel(lg_ref, aff_ref, rank_ref, ts_ref, *, cap, tt):
    lg = lg_ref[0]
    E, N = lg.shape
    p = jnp.exp(lg - jnp.max(lg, axis=0, keepdims=True))
    aff = p / jnp.sum(p, axis=0, keepdims=True)
    aff_ref[0] = aff
    count = lambda m: jnp.sum(jnp.where(m, 1.0, 0.0), axis=1, keepdims=True)
    as_float = lambda i: pltpu.bitcast(i, F32)

    def refine_bits(i, thr):
        cand = thr | jnp.left_shift(jnp.int32(1), 30 - i)
        return jnp.where(count(aff >= as_float(cand)) >= cap, cand, thr)

    thr = lax.fori_loop(0, 31, refine_bits, jnp.zeros((E, 1), jnp.int32))

    def refine_mid(i, lo_hi):
        lo, hi = lo_hi
        mid = 0.5 * (lo + hi)
        up = count(aff >= mid) >= cap
        return jnp.where(up, mid, lo), jnp.where(up, hi, mid)

    lo, hi = lax.fori_loop(0, 24, refine_mid, (as_float(thr), as_float(jnp.maximum(thr + 1, 0x00800000))))
    gt = aff >= hi
    eq = (aff >= lo) & (aff < hi)
    r_i = lax.broadcasted_iota(jnp.int32, (LANES, 2 * LANES), 0)
    c_i = lax.broadcasted_iota(jnp.int32, (LANES, 2 * LANES), 1)
    tri_tot = jnp.where((r_i < c_i) | (c_i >= LANES), 1.0, 0.0).astype(BF16)
    eq_rank, _ = _lane_prefix(jnp.where(eq, 1.0, 0.0), tri_tot)
    sel = gt | (eq & (eq_rank < cap - count(gt)))
    rank, before = _lane_prefix(jnp.where(sel, 1.0, 0.0), tri_tot)
    rank_ref[0] = jnp.where(sel, rank, -1.0).astype(jnp.int32)
    lane = lax.broadcasted_iota(jnp.int32, (E, LANES), 1)
    ts = jnp.zeros((E, LANES), F32)
    for k in range(N // tt + 1):
        ts = jnp.where(lane == k, before[k * tt // LANES], ts)
    ts_ref[0] = ts.astype(jnp.int32)


def route(logits_t, cap, tt):
    B, E, N = logits_t.shape
    blk = pl.BlockSpec((1, E, N), lambda b: (b, 0, 0))
    return pl.pallas_call(
        functools.partial(_route_kernel, cap=cap, tt=tt),
        grid=(B,),
        in_specs=[blk],
        out_specs=[blk, blk, pl.BlockSpec((1, E, LANES), lambda b: (b, 0, 0))],
        out_shape=[jax.ShapeDtypeStruct((B, E, N), F32), jax.ShapeDtypeStruct((B, E, N), jnp.int32),
                   jax.ShapeDtypeStruct((B, E, LANES), jnp.int32)],
        compiler_params=_cparams("parallel"),
        name="route",
    )(logits_t)


def _window(lo, w, win, cap):
    lower = (lo // ROW_ALIGN) * ROW_ALIGN + w * win
    return pl.multiple_of(jnp.minimum(lower, cap - win), ROW_ALIGN), lower


def _extra_windows(cap, tt, win):
    return -(-(min(cap, tt) + ROW_ALIGN - 1) // win) - 1


def _moe_gather_kernel(ts_ref, rank_ref, h_ref, xe_ref, acc_ref, *, win, tt):
    b, e = pl.program_id(0), pl.program_id(1)
    N = h_ref.shape[1]
    cap = xe_ref.shape[2]
    nt = N // tt
    base = (b * pl.num_programs(1) + e) * (nt + 1)
    acc_ref[...] = jnp.zeros_like(acc_ref)
    row = lax.broadcasted_iota(jnp.int32, (win, tt), 0)

    def place(k, w):
        start, lower = _window(ts_ref[base + k], w, win, cap)
        c0 = k * tt if isinstance(k, int) else pl.multiple_of(k * tt, tt)
        rk = rank_ref[0, 0, :, pl.ds(c0, tt)]
        onehot = jnp.where(jnp.where(rk >= lower, rk, -1) == row + start, 1.0, 0.0).astype(BF16)
        acc_ref[pl.ds(start, win), :] += jnp.dot(onehot, h_ref[0, pl.ds(c0, tt), :], preferred_element_type=F32)

    span = lambda k: ts_ref[base + k + 1] - (ts_ref[base + k] // ROW_ALIGN) * ROW_ALIGN
    need = jnp.int32(0)
    for k in range(nt):
        place(k, 0)
        need = jnp.maximum(need, span(k))

    n_extra = _extra_windows(cap, tt, win)

    @pl.when(need > win)
    def _():
        def extra(i, carry):
            k, w = i // n_extra, i % n_extra + 1

            @pl.when(span(k) > w * win)
            def _():
                place(k, w)
            return carry

        lax.fori_loop(0, nt * n_extra, extra, 0)

    xe_ref[0, 0] = acc_ref[...].astype(BF16)


def moe_gather(ts_flat, rank, h, cap, *, tt):
    B, E, N = rank.shape
    D = h.shape[2]
    return pl.pallas_call(
        functools.partial(_moe_gather_kernel, win=min(LANES, cap), tt=tt),
        grid_spec=pltpu.PrefetchScalarGridSpec(
            num_scalar_prefetch=1,
            grid=(B, E),
            in_specs=[pl.BlockSpec((1, 1, 1, N), lambda b, e, ts: (b, e, 0, 0)),
                      pl.BlockSpec((1, N, D), lambda b, e, ts: (b, 0, 0))],
            out_specs=pl.BlockSpec((1, 1, cap, D), lambda b, e, ts: (b, e, 0, 0)),
            scratch_shapes=[pltpu.VMEM((cap, D), F32)]),
        out_shape=jax.ShapeDtypeStruct((B, E, cap, D), BF16),
        compiler_params=_cparams("parallel", "arbitrary"),
        name="moe_gather",
    )(ts_flat, rank.reshape(B, E, 1, N), h)


def _expert_kernel(*refs, n_streams):
    xe_refs = refs[:n_streams]
    wg_ref, wu_ref, wd_ref = refs[n_streams:n_streams + 3]
    o_refs = refs[n_streams + 3:2 * n_streams + 3]
    wg_scr, wu_scr, wd_scr = refs[2 * n_streams + 3:]

    @pl.when(pl.program_id(1) == 0)
    def _():
        wg_scr[...] = wg_ref[0, 0].astype(BF16)
        wu_scr[...] = wu_ref[0, 0].astype(BF16)
        wd_scr[...] = wd_ref[0, 0].astype(BF16)

    xe = jnp.concatenate([r[0, 0] for r in xe_refs], axis=0)
    a = jnp.dot(xe, wg_scr[...], preferred_element_type=F32)
    u = jnp.dot(xe, wu_scr[...], preferred_element_type=F32)
    hid = (a * jax.nn.sigmoid(a)) * u
    y = jnp.dot(hid.astype(BF16), wd_scr[...], preferred_element_type=F32).astype(BF16)
    r0 = 0
    for o_ref in o_refs:
        o_ref[0, 0] = y[r0:r0 + o_ref.shape[2]]
        r0 += o_ref.shape[2]


def expert_ffn(xes, w_gate, w_up, w_down, li):
    B, E, _, D = xes[0].shape
    FF = w_gate.shape[3]
    rows = [pl.BlockSpec((1, 1, xe.shape[2], D), lambda e, b: (b, e, 0, 0)) for xe in xes]
    return pl.pallas_call(
        functools.partial(_expert_kernel, n_streams=len(xes)),
        grid=(E, B),
        in_specs=rows + [pl.BlockSpec((1, 1, D, FF), lambda e, b: (li, e, 0, 0)),
                         pl.BlockSpec((1, 1, D, FF), lambda e, b: (li, e, 0, 0)),
                         pl.BlockSpec((1, 1, FF, D), lambda e, b: (li, e, 0, 0))],
        out_specs=rows,
        out_shape=[jax.ShapeDtypeStruct(xe.shape, BF16) for xe in xes],
        scratch_shapes=[pltpu.VMEM((D, FF), BF16), pltpu.VMEM((D, FF), BF16), pltpu.VMEM((FF, D), BF16)],
        compiler_params=_cparams("arbitrary", "arbitrary"),
        name="expert_ffn",
    )(*xes, w_gate, w_up, w_down)


def _moe_combine_kernel(ts_ref, rank_ref, aff_ref, ye_ref, x_ref, g2_ref, o_ref, acc_ref, *, win, group):
    b, k = pl.program_id(0), pl.program_id(1)
    E, cap = ye_ref.shape[1], ye_ref.shape[2]
    tt = x_ref.shape[1]
    nt = pl.num_programs(1)
    col = lax.broadcasted_iota(jnp.int32, (tt, win), 1)
    lo = [ts_ref[(b * E + e) * (nt + 1) + k] for e in range(E)]
    hi = [ts_ref[(b * E + e) * (nt + 1) + k + 1] for e in range(E)]

    def contribution(w):
        total = None
        for g0 in range(0, E, group):
            lhs, rhs = [], []
            for e in range(g0, g0 + group):
                start, lower = _window(lo[e], w, win, cap)
                rk = rank_ref[0, :, e:e + 1]
                hit = jnp.where(rk >= lower, rk, -1) == col + start
                lhs.append(jnp.where(hit, aff_ref[0, :, e:e + 1], 0.0).astype(BF16))
                rhs.append(ye_ref[0, e, pl.ds(start, win), :])
            d = jnp.dot(jnp.concatenate(lhs, axis=1), jnp.concatenate(rhs, axis=0), preferred_element_type=F32)
            total = d if total is None else total + d
        return total

    acc_ref[...] = contribution(0)
    need = jnp.int32(0)
    for e in range(E):
        need = jnp.maximum(need, hi[e] - (lo[e] // ROW_ALIGN) * ROW_ALIGN)

    @pl.when(need > win)
    def _():
        def extra(w, carry):
            @pl.when(need > w * win)
            def _():
                acc_ref[...] += contribution(w)
            return carry

        lax.fori_loop(1, _extra_windows(cap, tt, win) + 1, extra, 0)

    o_ref[0] = x_ref[0] + g2_ref[0] * acc_ref[...]


def moe_combine(ts_flat, rank_t, aff_t, ye, x, g2, *, tt):
    B, N, D = x.shape
    E, cap = ye.shape[1], ye.shape[2]
    tok = lambda w: pl.BlockSpec((1, tt, w), lambda b, k, ts: (b, k, 0))
    return pl.pallas_call(
        functools.partial(_moe_combine_kernel, win=min(LANES, cap), group=4),
        grid_spec=pltpu.PrefetchScalarGridSpec(
            num_scalar_prefetch=1,
            grid=(B, N // tt),
            in_specs=[tok(E), tok(E),
                      pl.BlockSpec((1, E, cap, D), lambda b, k, ts: (b, 0, 0, 0)),
                      tok(D),
                      pl.BlockSpec((1, 1, D), lambda b, k, ts: (b, 0, 0))],
            out_specs=tok(D),
            scratch_shapes=[pltpu.VMEM((tt, D), F32)]),
        out_shape=jax.ShapeDtypeStruct((B, N, D), F32),
        compiler_params=_cparams("parallel", "arbitrary"),
        name="moe_combine",
    )(ts_flat, rank_t, aff_t, ye, x, g2)


def moe_residual(streams, g, router_w, w_gate, w_up, w_down, li):
    routed = []
    for x, shift, scale, _ in streams:
        B, N, D = x.shape
        cap = EC_CAPACITY * N // N_EXPERTS
        tt = min(4 * LANES, N)
        h, logits = norm_mod_router(x, g, shift, scale, router_w)
        aff, rank, ts = route(jnp.swapaxes(logits, 1, 2), cap, tt)
        ts_flat = ts[:, :, :N // tt + 1].reshape(-1)
        routed.append((moe_gather(ts_flat, rank, h, cap, tt=tt), ts_flat, rank, aff, tt))
    yes = expert_ffn([r[0] for r in routed], w_gate, w_up, w_down, li)
    return [moe_combine(ts_flat, jnp.swapaxes(rank, 1, 2), jnp.swapaxes(aff, 1, 2), ye, x, gate2, tt=tt)
            for (x, _, _, gate2), ye, (_, ts_flat, rank, aff, tt) in zip(streams, yes, routed)]


def lambda_init(layer):
    return 0.8 - 0.6 * math.exp(-0.3 * layer)


def even_layer(x, xc, mods, cmods, norm1_g, w_in, w_out, lb, hgrn_g, qn_g, kn_g, lam_vec, subln_g,
               lam_init, ctx_out, tables):
    sh1, sc1, g1 = mods
    csh1, csc1, cg1 = cmods
    B = x.shape[0]
    proj = norm_mod_matmul(x, norm1_g, sh1, sc1, w_in)
    projc = norm_mod_matmul(xc, norm1_g, csh1, csc1, w_in)
    s0 = jnp.zeros((B, HGRN_HEADS, 2, HGRN_HEAD_DIM, HGRN_HEAD_DIM), F32)
    oc_f, oc_b, s_ctx = hgrn_scan(projc, lb, s0)
    o_f, o_b, _ = hgrn_scan(proj, lb, s_ctx)
    lv = lam_vec.astype(F32)
    lam = jnp.exp(jnp.sum(lv[0] * lv[1])) - jnp.exp(jnp.sum(lv[2] * lv[3])) + lam_init
    bound = 1.01 * math.sqrt(DIFF_HEAD_DIM) * jnp.max(jnp.abs(qn_g)) * jnp.max(jnp.abs(kn_g))
    lam = jnp.stack([lam, bound, (bound <= SCORE_BOUND_MAX).astype(F32)])
    cos, sin = tables
    q, k, v = qkv_prep(proj, qn_g, kn_g, cos, sin, rotary=True)
    qc, kc, vc = qkv_prep(projc, qn_g, kn_g, cos[:xc.shape[1]], sin[:xc.shape[1]], rotary=False)
    k_all = jnp.concatenate([k, kc], axis=1)
    v_all = jnp.concatenate([v, vc], axis=2)
    y_d = diff_attention(q, k_all, v_all, lam, subln_g, 1.0 - lam_init)
    x_new = even_out(x, o_f, o_b, proj, y_d, hgrn_g.reshape(-1), g1, w_out)
    if not ctx_out:
        return x_new, None
    yc_d = diff_attention(qc, kc, vc, lam, subln_g, 1.0 - lam_init)
    xc_new = even_out(xc, oc_f, oc_b, projc, yc_d, hgrn_g.reshape(-1), cg1, w_out)
    return x_new, xc_new


def conv_layer(x, mods, norm1_g, w_in, conv_w, w_out):
    sh1, sc1, g1 = mods
    proj = norm_mod_matmul(x, norm1_g, sh1, sc1, w_in)
    return conv_out(x, proj, conv_w, g1, w_out)


def kernel(x, c, ctx, c_ctx, mod_w, mod_b, norm1_g, norm2_g, even_w_in, even_w_out, hgrn_lb_logits, hgrn_norm_g,
           diff_qnorm_g, diff_knorm_g, diff_lambda, diff_subln_g, conv_w_in, conv_w, conv_w_out, router_w,
           exp_w_gate, exp_w_up, exp_w_down):
    depth = mod_w.shape[0]
    B, T, D = x.shape
    lb_soft = jax.nn.softmax(hgrn_lb_logits.astype(F32), axis=0)
    lower_bounds = jnp.cumsum(lb_soft, axis=0) - lb_soft[:1]
    last_ctx_layer = 2 * ((depth - 1) // 2)
    cond = jnp.concatenate([c, c_ctx[None, :], jnp.zeros((8 - (B + 1) % 8, D), F32)], axis=0)
    mods = modulation(jax.nn.silu(cond), mod_w, mod_b)
    tables = rope_tables(T)
    xc = ctx
    for l in range(depth):
        read_ctx = l <= last_ctx_layer
        ctx_out = l < last_ctx_layer
        sh1, sc1, g1, sh2, sc2, g2 = [m[:, None, :] for m in jnp.split(mods[l, :B], MOD_CHUNKS, axis=-1)]
        if read_ctx:
            csh1, csc1, cg1, csh2, csc2, cg2 = [
                jnp.broadcast_to(m[None, None, :], (B, 1, D)) for m in jnp.split(mods[l, B], MOD_CHUNKS, axis=-1)]
        ew = (exp_w_gate, exp_w_up, exp_w_down, l)
        if l % 2 == 0:
            e = l // 2
            x, xc_new = even_layer(x, xc, (sh1, sc1, g1), (csh1, csc1, cg1), norm1_g[l],
                                   (even_w_in, e), (even_w_out, e), lower_bounds[e],
                                   hgrn_norm_g[e], diff_qnorm_g[e], diff_knorm_g[e], diff_lambda[e],
                                   diff_subln_g[e], lambda_init(l), ctx_out, tables)
        else:
            j = l // 2
            wi, wo = (conv_w_in, j), (conv_w_out, j)
            x = conv_layer(x, (sh1, sc1, g1), norm1_g[l], wi, conv_w[j], wo)
            xc_new = conv_layer(xc, (csh1, csc1, cg1), norm1_g[l], wi, conv_w[j], wo) if ctx_out else None
        streams = [(x, sh2, sc2, g2)] + ([(xc_new, csh2, csc2, cg2)] if ctx_out else [])
        outs = moe_residual(streams, norm2_g[l], router_w[l], *ew)
        x = outs[0]
        if ctx_out:
            xc = outs[1]
    return x
```

```python
import functools
import math

import numpy as np
import jax
import jax.numpy as jnp
from jax import lax
from jax.experimental import pallas as pl
from jax.experimental.pallas import tpu as pltpu

F32 = jnp.float32
BF16 = jnp.bfloat16

EPS = 1e-6
GRID_W = 64
ROPE_THETA = 10000.0
HGRN_HEAD_DIM = 128
HGRN_HEADS = 4
HGRN_WIDTH = HGRN_HEADS * HGRN_HEAD_DIM
DIFF_HEAD_DIM = 64
DIFF_HEADS = 4
DIFF_WIDTH = DIFF_HEADS * 2 * DIFF_HEAD_DIM
N_EXPERTS = 16
EC_CAPACITY = 2
MOD_CHUNKS = 6
SCAN_CHUNK = 64
SCAN_LEVELS = (32, 16, 8, 4, 2, 1)
SCAN_HEADS_PER_STEP = 2
NMM_ROW_SPLIT = 4
LANES = 128
ROW_ALIGN = 16
VMEM_LIMIT = 56 * 1024 * 1024
SCORE_BOUND_MAX = 40.0


def _cparams(*sem):
    return pltpu.CompilerParams(dimension_semantics=sem, vmem_limit_bytes=VMEM_LIMIT)


def _row_tile(t, want):
    return want if t % want == 0 else t


def _mod_kernel(s_ref, w_ref, b_ref, o_ref):
    s = s_ref[...]
    w = w_ref[0]
    s_hi = s.astype(BF16)
    s_lo = (s - s_hi.astype(F32)).astype(BF16)
    w_hi = w.astype(BF16)
    w_lo = (w - w_hi.astype(F32)).astype(BF16)
    d = lambda a, b: jnp.dot(a, b, preferred_element_type=F32)
    o_ref[0] = d(s_hi, w_hi) + d(s_hi, w_lo) + d(s_lo, w_hi) + b_ref[0]


def modulation(s, mod_w, mod_b, *, tn=512):
    R, D = s.shape
    depth, _, N = mod_w.shape
    return pl.pallas_call(
        _mod_kernel,
        grid=(depth, N // tn),
        in_specs=[pl.BlockSpec((R, D), lambda l, j: (0, 0)),
                  pl.BlockSpec((1, D, tn), lambda l, j: (l, 0, j)),
                  pl.BlockSpec((1, 1, tn), lambda l, j: (l, 0, j))],
        out_specs=pl.BlockSpec((1, R, tn), lambda l, j: (l, 0, j)),
        out_shape=jax.ShapeDtypeStruct((depth, R, N), F32),
        compiler_params=_cparams("parallel", "parallel"),
        name="modulation",
    )(s, mod_w, mod_b.reshape(depth, 1, N))


def _nmm_kernel(x_ref, g_ref, sh_ref, sc_ref, w_ref, o_ref, w_scr):
    @pl.when((pl.program_id(1) == 0) & (pl.program_id(2) == 0))
    def _():
        w_scr[...] = w_ref[0].astype(BF16)

    tm = x_ref.shape[1]
    rows = tm // NMM_ROW_SPLIT if tm % (NMM_ROW_SPLIT * ROW_ALIGN) == 0 else tm
    for r0 in range(0, tm, rows):
        x = x_ref[0, r0:r0 + rows, :]
        ms = jnp.mean(x * x, axis=-1, keepdims=True)
        h = (x * lax.rsqrt(ms + EPS)) * g_ref[...]
        h = h * (1.0 + sc_ref[0]) + sh_ref[0]
        o_ref[0, r0:r0 + rows, :] = jnp.dot(h.astype(BF16), w_scr[...],
                                            preferred_element_type=F32).astype(o_ref.dtype)


def norm_mod_matmul(x, g, shift, scale, w, *, cols=None, tm=512, out_dtype=F32):
    B, T, D = x.shape
    w, li = w
    first, N = cols if cols else (0, w.shape[2])
    tm = _row_tile(T, tm)
    tn = N
    return pl.pallas_call(
        _nmm_kernel,
        grid=(N // tn, B, T // tm),
        in_specs=[
            pl.BlockSpec((1, tm, D), lambda j, b, i: (b, i, 0)),
            pl.BlockSpec((1, D), lambda j, b, i: (0, 0)),
            pl.BlockSpec((1, 1, D), lambda j, b, i: (b, 0, 0)),
            pl.BlockSpec((1, 1, D), lambda j, b, i: (b, 0, 0)),
            pl.BlockSpec((1, D, tn), lambda j, b, i: (li, 0, first + j)),
        ],
        out_specs=pl.BlockSpec((1, tm, tn), lambda j, b, i: (b, i, j)),
        out_shape=jax.ShapeDtypeStruct((B, T, N), out_dtype),
        scratch_shapes=[pltpu.VMEM((D, tn), BF16)],
        compiler_params=_cparams("arbitrary", "arbitrary", "arbitrary"),
        name="norm_mod_matmul",
    )(x, g.reshape(1, D), shift, scale, w)


def _scan_constants():
    C = SCAN_CHUNK
    t = np.arange(C)[:, None]
    u = np.arange(C)[None, :]
    mats = [u <= t, u > t]
    masks = []
    for w in SCAN_LEVELS:
        m = (t // (2 * w)) * 2 * w + w - 1
        later = (t // w) % 2 == 1
        mats.append(np.where(later, (u > m) & (u <= t), (u > t) & (u <= m)))
        masks.append(later & ((u // w) % 2 == 0) & (u // (2 * w) == t // (2 * w)))
    masks = [t == u] + masks + [np.ones((C, C), bool)]
    a_f = np.stack(mats).astype(np.float32)
    m_f = np.stack(masks).astype(np.float32)
    a = np.stack([a_f, a_f[:, ::-1, ::-1]]).reshape(2, -1, C)
    m = np.stack([m_f, m_f[:, ::-1, ::-1]])
    m_pairs = np.concatenate([m[:, 0::2], m[:, 1::2]], axis=3)
    return np.concatenate([a, a], axis=2), m_pairs


def _scan_pair(q, z, v, lb, st, a2, mask_ref, d, later):
    C = SCAN_CHUNK
    W = HGRN_HEAD_DIM
    nl = len(SCAN_LEVELS)
    nt = lambda x, y: lax.dot_general(x, y, (((1,), (1,)), ((), ())), preferred_element_type=F32)
    nn = lambda x, y: jnp.dot(x, y, preferred_element_type=F32)
    e_abs = jnp.exp(-jnp.abs(z))
    r = 1.0 / (1.0 + e_abs)
    er = e_abs * r
    pos = z >= 0.0
    g2 = jnp.log2(lb + (1.0 - lb) * jnp.where(pos, r, er))
    k = (1.0 - lb) * jnp.where(pos, er, r)
    hi = g2.astype(BF16)
    lo = (g2 - hi.astype(F32)).astype(BF16)
    gs = jnp.concatenate([jnp.concatenate([hi[:C], hi[C:]], axis=1),
                          jnp.concatenate([lo[:C], lo[C:]], axis=1)], axis=0)
    x = nn(a2, gs)
    yield
    x = jnp.exp2(x)
    first, second = (1, 0) if d else (0, 1)
    vb = v.astype(BF16)
    sides, qt, kt, dec = [], [], [], []
    for c in (0, 1):
        qc, kc, xs = q[c * C:(c + 1) * C], k[c * C:(c + 1) * C], x[:, c * W:(c + 1) * W]
        ops = [(qc.astype(BF16), kc.astype(BF16))]
        for i in range(nl):
            qk = (jnp.where(later[i], qc, kc) * xs[(2 + i) * C:(3 + i) * C]).astype(BF16)
            ops.append((qk, qk))
        sides.append(ops)
        qt.append(qc * xs[0:C])
        kt.append(kc * xs[C:2 * C])
        dec.append(xs[0:1, :] if d else xs[C - 1:C, :])
    zero = jnp.zeros((C, W), BF16)
    sides[first].append((zero, zero))
    sides[second].append((qt[second].astype(BF16), kt[first].astype(BF16)))
    prods = []
    for ops in sides:
        pair = []
        for j in range(0, nl + 2, 2):
            (qa, ka), (qb, kb) = ops[j], ops[j + 1]
            rhs = jnp.concatenate([jnp.concatenate([ka, zero], axis=1), jnp.concatenate([zero, kb], axis=1)], axis=0)
            pair.append(nt(jnp.concatenate([qa, qb], axis=1), rhs))
        prods.append(pair)
        yield
    qt[second] = qt[second] * dec[first]
    kt[first] = kt[first] * dec[second]
    o_st = nt(jnp.concatenate([a.astype(BF16) for a in qt], axis=0), st.astype(BF16))
    upd = lax.dot_general(vb, jnp.concatenate(kt, axis=0).astype(BF16), (((0,), (0,)), ((), ())),
                          preferred_element_type=F32)
    yield
    o = [None, None]
    n_main = nl // 2
    for c, pair in enumerate(prods):
        main = mask_ref[d, 0] * pair[0]
        for j in range(1, n_main):
            main = main + mask_ref[d, j] * pair[j]
        last = mask_ref[d, n_main] * pair[n_main]
        vc, vf = vb[c * C:(c + 1) * C], vb[first * C:(first + 1) * C]
        o[c] = o_st[c * C:(c + 1) * C] + nn(jnp.concatenate([main, last], axis=1).astype(BF16),
                                            jnp.concatenate([vc, vc, vc, vf], axis=0))
    yield
    return jnp.concatenate(o, axis=0), st * (dec[0] * dec[1]) + upd


def _run_interleaved(gens):
    results = [None] * len(gens)
    live = list(range(len(gens)))
    while live:
        for i in list(live):
            try:
                next(gens[i])
            except StopIteration as done:
                results[i] = done.value
                live.remove(i)
    return results


def _scan_kernel(qf_ref, zf_ref, vf_ref, qb_ref, zb_ref, vb_ref, lbf_ref, lbb_ref, s0_ref, a_ref, mask_ref,
                 of_ref, ob_ref, sT_ref, st_scr, *, n_chunks):
    c = pl.program_id(2)
    C = SCAN_CHUNK

    @pl.when(c == 0)
    def _():
        st_scr[...] = s0_ref[0]

    W = HGRN_HEAD_DIM
    row = lax.broadcasted_iota(jnp.int32, (C, W), 0)
    later = [[(row // w) % 2 == 1 for w in SCAN_LEVELS], [((C - 1 - row) // w) % 2 == 1 for w in SCAN_LEVELS]]
    in_refs = [(qf_ref, zf_ref, vf_ref), (qb_ref, zb_ref, vb_ref)]
    lb_refs = [lbf_ref, lbb_ref]
    o_refs = [of_ref, ob_ref]
    n_pairs = n_chunks // 2
    heads = range(qf_ref.shape[2] // W)

    def body(i, carry):
        r0 = [pl.multiple_of(i * 2 * C, 2 * C), pl.multiple_of((n_pairs - 1 - i) * 2 * C, 2 * C)]
        chains = [(d, h) for h in heads for d in (0, 1)]
        ins = [[ref[0, pl.ds(r0[d], 2 * C), h * W:(h + 1) * W] for ref in in_refs[d]] for d, h in chains]
        sts = [st_scr[h, d] for d, h in chains]
        outs = _run_interleaved([
            _scan_pair(*x, lb_refs[d][:, h * W:(h + 1) * W], st, a_ref[d], mask_ref, d, later[d])
            for (d, h), x, st in zip(chains, ins, sts)])
        for (d, h), (o, st) in zip(chains, outs):
            o_refs[d][0, pl.ds(r0[d], 2 * C), h * W:(h + 1) * W] = o.astype(o_refs[d].dtype)
            st_scr[h, d] = st
        return carry

    lax.fori_loop(0, n_pairs, body, 0)

    @pl.when(c == pl.num_programs(2) - 1)
    def _():
        sT_ref[0] = st_scr[...]


def hgrn_scan(proj, lb, s0, *, tb=512):
    B, T, _ = proj.shape
    tb = _row_tile(T, tb)
    nc = T // tb
    H = HGRN_HEADS
    hd = HGRN_HEAD_DIM
    hps = SCAN_HEADS_PER_STEP
    G = H // hps
    wd = hps * hd
    fwd = lambda grp: pl.BlockSpec((1, tb, wd), lambda b, h, c: (b, c, grp * G + h))
    bwd = lambda grp: pl.BlockSpec((1, tb, wd), lambda b, h, c: (b, nc - 1 - c, grp * G + h))
    kern = functools.partial(_scan_kernel, n_chunks=tb // SCAN_CHUNK)
    a2, masks = _scan_constants()
    return pl.pallas_call(
        kern,
        grid=(B, G, nc),
        in_specs=[fwd(0), fwd(1), fwd(3), bwd(0), bwd(2), bwd(3),
                  pl.BlockSpec((1, wd), lambda b, h, c: (0, h)),
                  pl.BlockSpec((1, wd), lambda b, h, c: (0, h)),
                  pl.BlockSpec((1, hps, 2, hd, hd), lambda b, h, c: (b, h, 0, 0, 0)),
                  pl.BlockSpec(a2.shape, lambda b, h, c: (0, 0, 0)),
                  pl.BlockSpec(masks.shape, lambda b, h, c: (0, 0, 0, 0))],
        out_specs=[pl.BlockSpec((1, tb, wd), lambda b, h, c: (b, c, h)),
                   pl.BlockSpec((1, tb, wd), lambda b, h, c: (b, nc - 1 - c, h)),
                   pl.BlockSpec((1, hps, 2, hd, hd), lambda b, h, c: (b, h, 0, 0, 0))],
        out_shape=[jax.ShapeDtypeStruct((B, T, HGRN_WIDTH), BF16),
                   jax.ShapeDtypeStruct((B, T, HGRN_WIDTH), BF16),
                   jax.ShapeDtypeStruct((B, H, 2, hd, hd), F32)],
        scratch_shapes=[pltpu.VMEM((hps, 2, hd, hd), F32)],
        compiler_params=_cparams("parallel", "parallel", "arbitrary"),
        name="hgrn_scan",
    )(proj, proj, proj, proj, proj, proj, lb[0:1], lb[1:2], s0, jnp.asarray(a2, BF16), jnp.asarray(masks, F32))


def _group_mean_sq(x, gmat):
    sq = x * x
    hi = sq.astype(BF16)
    lo = (sq - hi.astype(F32)).astype(BF16)
    return (jnp.dot(hi, gmat, preferred_element_type=F32) + jnp.dot(lo, gmat, preferred_element_type=F32))


def _qkv_prep_kernel(q_ref, k_ref, v_ref, qg_ref, kg_ref, cos_ref, sin_ref, qo_ref, ko_ref, vo_ref, *, rotary):
    W = LANES
    r_i = lax.broadcasted_iota(jnp.int32, (W, W), 0) // DIFF_HEAD_DIM
    c_i = lax.broadcasted_iota(jnp.int32, (W, W), 1) // DIFF_HEAD_DIM
    gmat = jnp.where(r_i == c_i, 1.0 / DIFF_HEAD_DIM, 0.0).astype(BF16)
    lane = lax.broadcasted_iota(jnp.int32, (1, W), 1)
    first = (lane % 32) < 16

    def prep(x, g, scale):
        y = (x * lax.rsqrt(_group_mean_sq(x, gmat) + EPS)) * g
        if rotary:
            partner = jnp.where(first, pltpu.roll(y, W - 16, axis=1), pltpu.roll(y, 16, axis=1))
            y = y * cos_ref[...] + partner * sin_ref[...]
        if scale != 1.0:
            y = y * scale
        return y

    for h in range(DIFF_HEADS):
        sl = slice(h * W, (h + 1) * W)
        qo_ref[0, sl, :] = prep(q_ref[0, :, sl].astype(F32), qg_ref[...], DIFF_HEAD_DIM ** -0.5).T.astype(BF16)
        ko_ref[0, :, sl] = prep(k_ref[0, :, sl].astype(F32), kg_ref[...], 1.0).astype(BF16)
        vo_ref[0, sl, :] = v_ref[0, :, sl].astype(F32).T.astype(BF16)


def qkv_prep(proj, qg, kg, cos, sin, *, rotary, tm=512):
    B, T, _ = proj.shape
    tm = _row_tile(T, tm)
    Wd = DIFF_WIDTH
    col = lambda j: pl.BlockSpec((1, tm, Wd), lambda b, i: (b, i, j))
    vec = pl.BlockSpec((1, LANES), lambda b, i: (0, 0))
    tab = pl.BlockSpec((tm, LANES), lambda b, i: (i, 0))
    rows = pl.BlockSpec((1, tm, Wd), lambda b, i: (b, i, 0))
    cols = pl.BlockSpec((1, Wd, tm), lambda b, i: (b, 0, i))
    return pl.pallas_call(
        functools.partial(_qkv_prep_kernel, rotary=rotary),
        grid=(B, T // tm),
        in_specs=[col(1), col(2), col(3), vec, vec, tab, tab],
        out_specs=[cols, rows, cols],
        out_shape=[jax.ShapeDtypeStruct((B, Wd, T), BF16), jax.ShapeDtypeStruct((B, T, Wd), BF16),
                   jax.ShapeDtypeStruct((B, Wd, T), BF16)],
        compiler_params=_cparams("parallel", "parallel"),
        name="qkv_prep",
    )(proj, proj, proj, jnp.tile(qg, 2).reshape(1, LANES), jnp.tile(kg, 2).reshape(1, LANES), cos, sin)


def rope_tables(T):
    n = DIFF_HEAD_DIM // 2
    inv = 1.0 / (ROPE_THETA ** (jnp.arange(0, n, 2, dtype=F32) / n))
    t = jnp.arange(T)
    ang_r = (t // GRID_W).astype(F32)[:, None] * inv[None, :]
    ang_c = (t % GRID_W).astype(F32)[:, None] * inv[None, :]
    cos = jnp.concatenate([jnp.cos(ang_r)] * 2 + [jnp.cos(ang_c)] * 2, axis=-1)
    sin = jnp.concatenate([-jnp.sin(ang_r), jnp.sin(ang_r), -jnp.sin(ang_c), jnp.sin(ang_c)], axis=-1)
    return jnp.tile(cos, (1, 2)), jnp.tile(sin, (1, 2))


def _attn_tile(qt, k_ref, v_ref, lam, key_chunk, bound=None):
    tq = qt.shape[1]
    S = k_ref.shape[1]
    row = lax.broadcasted_iota(jnp.int32, (LANES, 1), 0)
    zero = jnp.zeros_like(qt)
    qq = jnp.concatenate([jnp.where(row < DIFF_HEAD_DIM, qt, zero),
                          jnp.where(row >= DIFF_HEAD_DIM, qt, zero)], axis=1)
    m = jnp.full((1, 2 * tq), -jnp.inf, F32)
    acc = [jnp.zeros((LANES + 16, tq), F32), jnp.zeros((LANES + 16, tq), F32)]
    ones = jnp.ones((16, key_chunk), BF16)
    chunks = [(c0, min(c0 + key_chunk, S)) for c0 in range(0, S, key_chunk)]
    scores = lambda c: jnp.dot(k_ref[0, c[0]:c[1], :], qq, preferred_element_type=F32)
    s_next = scores(chunks[0])
    yield
    for n, (c0, c1) in enumerate(chunks):
        s = s_next
        if n + 1 < len(chunks):
            s_next = scores(chunks[n + 1])
        vt1 = jnp.concatenate([v_ref[0, :, c0:c1], ones[:, :c1 - c0]], axis=0)
        if bound is None:
            m_new = jnp.maximum(m, jnp.max(s, axis=0, keepdims=True))
            alpha = jnp.exp(m - m_new)
            pb = jnp.exp((s - m_new).astype(BF16))
            m = m_new
        else:
            alpha = None
            pb = jnp.exp(s - bound).astype(BF16)
        for i in range(2):
            pv = jnp.dot(vt1, pb[:, i * tq:(i + 1) * tq], preferred_element_type=F32)
            acc[i] = acc[i] + pv if alpha is None else acc[i] * alpha[:, i * tq:(i + 1) * tq] + pv
        yield
    inv = [1.0 / a[LANES:LANES + 1] for a in acc]
    return acc[0][:LANES] * inv[0] - acc[1][:LANES] * (lam * inv[1])


def _diff_attn_kernel(par_ref, q_ref, k_ref, v_ref, g_ref, o_ref, *, out_scale, key_chunk, tq):
    n_tiles = q_ref.shape[2] // tq

    def run(bound):
        outs = _run_interleaved([
            _attn_tile(q_ref[0, :, i * tq:(i + 1) * tq], k_ref, v_ref, par_ref[0], key_chunk, bound)
            for i in range(n_tiles)])
        for i, o in enumerate(outs):
            ms = jnp.mean(o * o, axis=0, keepdims=True)
            o_ref[0, i * tq:(i + 1) * tq, :] = ((o * lax.rsqrt(ms + EPS)) * g_ref[...] * out_scale).T.astype(BF16)

    @pl.when(par_ref[2] > 0.5)
    def _():
        run(par_ref[1])

    @pl.when(par_ref[2] <= 0.5)
    def _():
        run(None)


def diff_attention(qt, k, vt, params, subln_g, out_scale, *, tq=256, tiles_per_step=2, key_chunk=512):
    B, Wd, T = qt.shape
    S = k.shape[1]
    tq = _row_tile(T, tq)
    ts = _row_tile(T, tq * tiles_per_step)
    return pl.pallas_call(
        functools.partial(_diff_attn_kernel, out_scale=out_scale, key_chunk=key_chunk, tq=tq),
        grid=(B, DIFF_HEADS, T // ts),
        in_specs=[pl.BlockSpec(memory_space=pltpu.SMEM),
                  pl.BlockSpec((1, LANES, ts), lambda b, h, i: (b, h, i)),
                  pl.BlockSpec((1, S, LANES), lambda b, h, i: (b, 0, h)),
                  pl.BlockSpec((1, LANES, S), lambda b, h, i: (b, h, 0)),
                  pl.BlockSpec((LANES, 1), lambda b, h, i: (0, 0))],
        out_specs=pl.BlockSpec((1, ts, LANES), lambda b, h, i: (b, i, h)),
        out_shape=jax.ShapeDtypeStruct((B, T, Wd), BF16),
        compiler_params=_cparams("parallel", "parallel", "arbitrary"),
        name="diff_attention",
    )(params, qt, k, vt, subln_g.reshape(LANES, 1))


def _even_out_kernel(x_ref, of_ref, ob_ref, gate_ref, yd_ref, hg_ref, g1_ref, w_ref, o_ref):
    acc = jnp.dot(yd_ref[0].astype(BF16), w_ref[0, HGRN_WIDTH:, :].astype(BF16), preferred_element_type=F32)
    for h in range(HGRN_HEADS):
        sl = slice(h * HGRN_HEAD_DIM, (h + 1) * HGRN_HEAD_DIM)
        o = of_ref[0, :, sl].astype(F32) + ob_ref[0, :, sl].astype(F32)
        ms = jnp.mean(o * o, axis=-1, keepdims=True)
        gate = gate_ref[0, :, sl].astype(F32)
        yh = (o * lax.rsqrt(ms + EPS)) * hg_ref[:, sl] * (gate * jax.nn.sigmoid(gate))
        acc = acc + jnp.dot(yh.astype(BF16), w_ref[0, sl, :].astype(BF16), preferred_element_type=F32)
    o_ref[0] = x_ref[0] + g1_ref[0] * acc


def even_out(x, o_f, o_b, proj, y_d, hgrn_g, g1, w_out, *, tm=512):
    B, T, D = x.shape
    w_out, li = w_out
    tm = _row_tile(T, tm)
    Wd = HGRN_WIDTH
    row = lambda w, j: pl.BlockSpec((1, tm, w), lambda b, i: (b, i, j))
    return pl.pallas_call(
        _even_out_kernel,
        grid=(B, T // tm),
        in_specs=[row(D, 0), row(Wd, 0), row(Wd, 0), row(Wd, 0), row(Wd, 0),
                  pl.BlockSpec((1, Wd), lambda b, i: (0, 0)),
                  pl.BlockSpec((1, 1, D), lambda b, i: (b, 0, 0)),
                  pl.BlockSpec((1,) + w_out.shape[1:], lambda b, i: (li, 0, 0))],
        out_specs=row(D, 0),
        out_shape=jax.ShapeDtypeStruct((B, T, D), F32),
        compiler_params=_cparams("parallel", "parallel"),
        name="even_out",
    )(x, o_f, o_b, proj, y_d, hgrn_g.reshape(1, Wd), g1, w_out)


def _conv_out_kernel(x_ref, bg_ref, cg_ref, v_ref, cp_ref, vp_ref, cn_ref, vn_ref, cw_ref, g1_ref, w_ref, o_ref):
    i = pl.program_id(1)
    n = pl.num_programs(1)
    f32 = lambda ref, *idx: ref[idx].astype(F32)
    u = f32(cg_ref, 0) * f32(v_ref, 0)
    tm = u.shape[0]
    last = ROW_ALIGN - 1
    u_prev_row = jnp.where(i > 0, f32(cp_ref, 0, slice(last, last + 1)) * f32(vp_ref, 0, slice(last, last + 1)), 0.0)
    u_next_row = jnp.where(i < n - 1, f32(cn_ref, 0, slice(0, 1)) * f32(vn_ref, 0, slice(0, 1)), 0.0)
    ridx = lax.broadcasted_iota(jnp.int32, (tm, 1), 0)
    u_prev = jnp.where(ridx == 0, u_prev_row, pltpu.roll(u, 1, axis=0))
    u_next = jnp.where(ridx == tm - 1, u_next_row, pltpu.roll(u, tm - 1, axis=0))
    y = cw_ref[0:1, :] * u_prev + cw_ref[1:2, :] * u + cw_ref[2:3, :] * u_next
    acc = jnp.dot((f32(bg_ref, 0) * y).astype(BF16), w_ref[0].astype(BF16), preferred_element_type=F32)
    o_ref[0] = x_ref[0] + g1_ref[0] * acc


def conv_out(x, proj, conv_w, g1, w_out, *, tm=512):
    B, T, D = x.shape
    w_out, li = w_out
    tm = _row_tile(T, tm)
    rt = tm // ROW_ALIGN
    last_blk = T // ROW_ALIGN - 1
    row = lambda j: pl.BlockSpec((1, tm, D), lambda b, i: (b, i, j))
    prev = lambda j: pl.BlockSpec((1, ROW_ALIGN, D), lambda b, i: (b, jnp.maximum(i * rt - 1, 0), j))
    nxt = lambda j: pl.BlockSpec((1, ROW_ALIGN, D), lambda b, i: (b, jnp.minimum((i + 1) * rt, last_blk), j))
    return pl.pallas_call(
        _conv_out_kernel,
        grid=(B, T // tm),
        in_specs=[row(0), row(0), row(1), row(2), prev(1), prev(2), nxt(1), nxt(2),
                  pl.BlockSpec((8, D), lambda b, i: (0, 0)),
                  pl.BlockSpec((1, 1, D), lambda b, i: (b, 0, 0)),
                  pl.BlockSpec((1,) + w_out.shape[1:], lambda b, i: (li, 0, 0))],
        out_specs=row(0),
        out_shape=jax.ShapeDtypeStruct((B, T, D), F32),
        compiler_params=_cparams("parallel", "parallel"),
        name="conv_out",
    )(x, proj, proj, proj, proj, proj, proj, proj,
      jnp.concatenate([conv_w, jnp.zeros((8 - conv_w.shape[0], D), conv_w.dtype)], axis=0), g1, w_out)


def _router_kernel(x_ref, g_ref, sh_ref, sc_ref, rw_ref, h_ref, lg_ref):
    x = x_ref[0]
    ms = jnp.mean(x * x, axis=-1, keepdims=True)
    h = (x * lax.rsqrt(ms + EPS)) * g_ref[...]
    h = h * (1.0 + sc_ref[0]) + sh_ref[0]
    h_ref[0] = h.astype(BF16)
    rw = rw_ref[...]
    h_hi = h.astype(BF16)
    h_lo = (h - h_hi.astype(F32)).astype(BF16)
    w_hi = rw.astype(BF16)
    w_lo = (rw - w_hi.astype(F32)).astype(BF16)
    d = lambda a, b: jnp.dot(a, b, preferred_element_type=F32)
    lg_ref[0] = d(h_hi, w_hi) + d(h_hi, w_lo) + d(h_lo, w_hi)


def norm_mod_router(x, g, shift, scale, router_w, *, tm=512):
    B, T, D = x.shape
    E = router_w.shape[1]
    tm = _row_tile(T, tm)
    return pl.pallas_call(
        _router_kernel,
        grid=(B, T // tm),
        in_specs=[pl.BlockSpec((1, tm, D), lambda b, i: (b, i, 0)),
                  pl.BlockSpec((1, D), lambda b, i: (0, 0)),
                  pl.BlockSpec((1, 1, D), lambda b, i: (b, 0, 0)),
                  pl.BlockSpec((1, 1, D), lambda b, i: (b, 0, 0)),
                  pl.BlockSpec((D, E), lambda b, i: (0, 0))],
        out_specs=[pl.BlockSpec((1, tm, D), lambda b, i: (b, i, 0)),
                   pl.BlockSpec((1, tm, E), lambda b, i: (b, i, 0))],
        out_shape=[jax.ShapeDtypeStruct((B, T, D), BF16), jax.ShapeDtypeStruct((B, T, E), F32)],
        compiler_params=_cparams("parallel", "parallel"),
        name="norm_mod_router",
    )(x, g.reshape(1, D), shift, scale, router_w)


def _lane_prefix(flags, tri_tot):
    E, N = flags.shape
    carries = [jnp.zeros((E, LANES), F32)]
    out = []
    for j in range(N // LANES):
        r = jnp.dot(flags[:, j * LANES:(j + 1) * LANES].astype(BF16), tri_tot, preferred_element_type=F32)
        out.append(r[:, :LANES] + carries[-1])
        carries.append(carries[-1] + r[:, LANES:])
    return jnp.concatenate(out, axis=1), carries


def _route_kernel(lg_ref, aff_ref, rank_ref, ts_ref, *, cap, tt):
    lg = lg_ref[0]
    E, N = lg.shape
    p = jnp.exp(lg - jnp.max(lg, axis=0, keepdims=True))
    aff = p / jnp.sum(p, axis=0, keepdims=True)
    aff_ref[0] = aff
    count = lambda m: jnp.sum(jnp.where(m, 1.0, 0.0), axis=1, keepdims=True)
    as_float = lambda i: pltpu.bitcast(i, F32)

    def refine_bits(i, thr):
        cand = thr | jnp.left_shift(jnp.int32(1), 30 - i)
        return jnp.where(count(aff >= as_float(cand)) >= cap, cand, thr)

    thr = lax.fori_loop(0, 31, refine_bits, jnp.zeros((E, 1), jnp.int32))

    def refine_mid(i, lo_hi):
        lo, hi = lo_hi
        mid = 0.5 * (lo + hi)
        up = count(aff >= mid) >= cap
        return jnp.where(up, mid, lo), jnp.where(up, hi, mid)

    lo, hi = lax.fori_loop(0, 24, refine_mid, (as_float(thr), as_float(jnp.maximum(thr + 1, 0x00800000))))
    gt = aff >= hi
    eq = (aff >= lo) & (aff < hi)
    r_i = lax.broadcasted_iota(jnp.int32, (LANES, 2 * LANES), 0)
    c_i = lax.broadcasted_iota(jnp.int32, (LANES, 2 * LANES), 1)
    tri_tot = jnp.where((r_i < c_i) | (c_i >= LANES), 1.0, 0.0).astype(BF16)
    eq_rank, _ = _lane_prefix(jnp.where(eq, 1.0, 0.0), tri_tot)
    sel = gt | (eq & (eq_rank < cap - count(gt)))
    rank, before = _lane_prefix(jnp.where(sel, 1.0, 0.0), tri_tot)
    rank_ref[0] = jnp.where(sel, rank, -1.0).astype(jnp.int32)
    lane = lax.broadcasted_iota(jnp.int32, (E, LANES), 1)
    ts = jnp.zeros((E, LANES), F32)
    for k in range(N // tt + 1):
        ts = jnp.where(lane == k, before[k * tt // LANES], ts)
    ts_ref[0] = ts.astype(jnp.int32)


def route(logits_t, cap, tt):
    B, E, N = logits_t.shape
    blk = pl.BlockSpec((1, E, N), lambda b: (b, 0, 0))
    return pl.pallas_call(
        functools.partial(_route_kernel, cap=cap, tt=tt),
        grid=(B,),
        in_specs=[blk],
        out_specs=[blk, blk, pl.BlockSpec((1, E, LANES), lambda b: (b, 0, 0))],
        out_shape=[jax.ShapeDtypeStruct((B, E, N), F32), jax.ShapeDtypeStruct((B, E, N), jnp.int32),
                   jax.ShapeDtypeStruct((B, E, LANES), jnp.int32)],
        compiler_params=_cparams("parallel"),
        name="route",
    )(logits_t)


def _window(lo, w, win, cap):
    lower = (lo // ROW_ALIGN) * ROW_ALIGN + w * win
    return pl.multiple_of(jnp.minimum(lower, cap - win), ROW_ALIGN), lower


def _extra_windows(cap, tt, win):
    return -(-(min(cap, tt) + ROW_ALIGN - 1) // win) - 1


def _moe_gather_kernel(ts_ref, rank_ref, h_ref, xe_ref, acc_ref, *, win, tt):
    b, e = pl.program_id(0), pl.program_id(1)
    N = h_ref.shape[1]
    cap = xe_ref.shape[2]
    nt = N // tt
    base = (b * pl.num_programs(1) + e) * (nt + 1)
    acc_ref[...] = jnp.zeros_like(acc_ref)
    row = lax.broadcasted_iota(jnp.int32, (win, tt), 0)

    def place(k, w):
        start, lower = _window(ts_ref[base + k], w, win, cap)
        c0 = k * tt if isinstance(k, int) else pl.multiple_of(k * tt, tt)
        rk = rank_ref[0, 0, :, pl.ds(c0, tt)]
        onehot = jnp.where(jnp.where(rk >= lower, rk, -1) == row + start, 1.0, 0.0).astype(BF16)
        acc_ref[pl.ds(start, win), :] += jnp.dot(onehot, h_ref[0, pl.ds(c0, tt), :], preferred_element_type=F32)

    span = lambda k: ts_ref[base + k + 1] - (ts_ref[base + k] // ROW_ALIGN) * ROW_ALIGN
    need = jnp.int32(0)
    for k in range(nt):
        place(k, 0)
        need = jnp.maximum(need, span(k))

    n_extra = _extra_windows(cap, tt, win)

    @pl.when(need > win)
    def _():
        def extra(i, carry):
            k, w = i // n_extra, i % n_extra + 1

            @pl.when(span(k) > w * win)
            def _():
                place(k, w)
            return carry

        lax.fori_loop(0, nt * n_extra, extra, 0)

    xe_ref[0, 0] = acc_ref[...].astype(BF16)


def moe_gather(ts_flat, rank, h, cap, *, tt):
    B, E, N = rank.shape
    D = h.shape[2]
    return pl.pallas_call(
        functools.partial(_moe_gather_kernel, win=min(LANES, cap), tt=tt),
        grid_spec=pltpu.PrefetchScalarGridSpec(
            num_scalar_prefetch=1,
            grid=(B, E),
            in_specs=[pl.BlockSpec((1, 1, 1, N), lambda b, e, ts: (b, e, 0, 0)),
                      pl.BlockSpec((1, N, D), lambda b, e, ts: (b, 0, 0))],
            out_specs=pl.BlockSpec((1, 1, cap, D), lambda b, e, ts: (b, e, 0, 0)),
            scratch_shapes=[pltpu.VMEM((cap, D), F32)]),
        out_shape=jax.ShapeDtypeStruct((B, E, cap, D), BF16),
        compiler_params=_cparams("parallel", "arbitrary"),
        name="moe_gather",
    )(ts_flat, rank.reshape(B, E, 1, N), h)


def _expert_kernel(*refs, n_streams):
    xe_refs = refs[:n_streams]
    wg_ref, wu_ref, wd_ref = refs[n_streams:n_streams + 3]
    o_refs = refs[n_streams + 3:2 * n_streams + 3]
    wg_scr, wu_scr, wd_scr = refs[2 * n_streams + 3:]

    @pl.when(pl.program_id(1) == 0)
    def _():
        wg_scr[...] = wg_ref[0, 0].astype(BF16)
        wu_scr[...] = wu_ref[0, 0].astype(BF16)
        wd_scr[...] = wd_ref[0, 0].astype(BF16)

    def swiglu(x):
        a = jnp.dot(x, wg_scr[...], preferred_element_type=F32)
        u = jnp.dot(x, wu_scr[...], preferred_element_type=F32)
        yield
        hid = (a * jax.nn.sigmoid(a)) * u
        return jnp.dot(hid.astype(BF16), wd_scr[...], preferred_element_type=F32).astype(BF16)

    xe = jnp.concatenate([r[0, 0] for r in xe_refs], axis=0)
    half = xe.shape[0] // 2
    y = jnp.concatenate(_run_interleaved([swiglu(xe[:half]), swiglu(xe[half:])]), axis=0)
    r0 = 0
    for o_ref in o_refs:
        o_ref[0, 0] = y[r0:r0 + o_ref.shape[2]]
        r0 += o_ref.shape[2]


def expert_ffn(xes, w_gate, w_up, w_down, li):
    B, E, _, D = xes[0].shape
    FF = w_gate.shape[3]
    rows = [pl.BlockSpec((1, 1, xe.shape[2], D), lambda e, b: (b, e, 0, 0)) for xe in xes]
    return pl.pallas_call(
        functools.partial(_expert_kernel, n_streams=len(xes)),
        grid=(E, B),
        in_specs=rows + [pl.BlockSpec((1, 1, D, FF), lambda e, b: (li, e, 0, 0)),
                         pl.BlockSpec((1, 1, D, FF), lambda e, b: (li, e, 0, 0)),
                         pl.BlockSpec((1, 1, FF, D), lambda e, b: (li, e, 0, 0))],
        out_specs=rows,
        out_shape=[jax.ShapeDtypeStruct(xe.shape, BF16) for xe in xes],
        scratch_shapes=[pltpu.VMEM((D, FF), BF16), pltpu.VMEM((D, FF), BF16), pltpu.VMEM((FF, D), BF16)],
        compiler_params=_cparams("arbitrary", "arbitrary"),
        name="expert_ffn",
    )(*xes, w_gate, w_up, w_down)


def _moe_combine_kernel(ts_ref, rank_ref, aff_ref, ye_ref, x_ref, g2_ref, o_ref, acc_ref, *, win, group):
    b, k = pl.program_id(0), pl.program_id(1)
    E, cap = ye_ref.shape[1], ye_ref.shape[2]
    tt = x_ref.shape[1]
    nt = pl.num_programs(1)
    col = lax.broadcasted_iota(jnp.int32, (tt, win), 1)
    lo = [ts_ref[(b * E + e) * (nt + 1) + k] for e in range(E)]
    hi = [ts_ref[(b * E + e) * (nt + 1) + k + 1] for e in range(E)]

    def contribution(w):
        total = None
        for g0 in range(0, E, group):
            lhs, rhs = [], []
            for e in range(g0, g0 + group):
                start, lower = _window(lo[e], w, win, cap)
                rk = rank_ref[0, :, e:e + 1]
                hit = jnp.where(rk >= lower, rk, -1) == col + start
                lhs.append(jnp.where(hit, aff_ref[0, :, e:e + 1], 0.0).astype(BF16))
                rhs.append(ye_ref[0, e, pl.ds(start, win), :])
            d = jnp.dot(jnp.concatenate(lhs, axis=1), jnp.concatenate(rhs, axis=0), preferred_element_type=F32)
            total = d if total is None else total + d
        return total

    acc_ref[...] = contribution(0)
    need = jnp.int32(0)
    for e in range(E):
        need = jnp.maximum(need, hi[e] - (lo[e] // ROW_ALIGN) * ROW_ALIGN)

    @pl.when(need > win)
    def _():
        def extra(w, carry):
            @pl.when(need > w * win)
            def _():
                acc_ref[...] += contribution(w)
            return carry

        lax.fori_loop(1, _extra_windows(cap, tt, win) + 1, extra, 0)

    o_ref[0] = x_ref[0] + g2_ref[0] * acc_ref[...]


def moe_combine(ts_flat, rank_t, aff_t, ye, x, g2, *, tt):
    B, N, D = x.shape
    E, cap = ye.shape[1], ye.shape[2]
    tok = lambda w: pl.BlockSpec((1, tt, w), lambda b, k, ts: (b, k, 0))
    return pl.pallas_call(
        functools.partial(_moe_combine_kernel, win=min(LANES, cap), group=4),
        grid_spec=pltpu.PrefetchScalarGridSpec(
            num_scalar_prefetch=1,
            grid=(B, N // tt),
            in_specs=[tok(E), tok(E),
                      pl.BlockSpec((1, E, cap, D), lambda b, k, ts: (b, 0, 0, 0)),
                      tok(D),
                      pl.BlockSpec((1, 1, D), lambda b, k, ts: (b, 0, 0))],
            out_specs=tok(D),
            scratch_shapes=[pltpu.VMEM((tt, D), F32)]),
        out_shape=jax.ShapeDtypeStruct((B, N, D), F32),
        compiler_params=_cparams("parallel", "arbitrary"),
        name="moe_combine",
    )(ts_flat, rank_t, aff_t, ye, x, g2)


def moe_residual(streams, g, router_w, w_gate, w_up, w_down, li):
    routed = []
    for x, shift, scale, _ in streams:
        B, N, D = x.shape
        cap = EC_CAPACITY * N // N_EXPERTS
        tt = min(4 * LANES, N)
        h, logits = norm_mod_router(x, g, shift, scale, router_w)
        aff, rank, ts = route(jnp.swapaxes(logits, 1, 2), cap, tt)
        ts_flat = ts[:, :, :N // tt + 1].reshape(-1)
        routed.append((moe_gather(ts_flat, rank, h, cap, tt=tt), ts_flat, rank, aff, tt))
    yes = expert_ffn([r[0] for r in routed], w_gate, w_up, w_down, li)
    return [moe_combine(ts_flat, jnp.swapaxes(rank, 1, 2), jnp.swapaxes(aff, 1, 2), ye, x, gate2, tt=tt)
            for (x, _, _, gate2), ye, (_, ts_flat, rank, aff, tt) in zip(streams, yes, routed)]


def lambda_init(layer):
    return 0.8 - 0.6 * math.exp(-0.3 * layer)


def even_layer(x, xc, mods, cmods, norm1_g, w_in, w_out, lb, hgrn_g, qn_g, kn_g, lam_vec, subln_g,
               lam_init, ctx_out, tables):
    sh1, sc1, g1 = mods
    csh1, csc1, cg1 = cmods
    B = x.shape[0]
    half = w_in[0].shape[2] // 2
    proj = norm_mod_matmul(x, norm1_g, sh1, sc1, w_in, cols=(0, half))
    proj2 = norm_mod_matmul(x, norm1_g, sh1, sc1, w_in, cols=(1, half), out_dtype=BF16)
    projc = norm_mod_matmul(xc, norm1_g, csh1, csc1, w_in, cols=(0, half))
    projc2 = norm_mod_matmul(xc, norm1_g, csh1, csc1, w_in, cols=(1, half), out_dtype=BF16)
    s0 = jnp.zeros((B, HGRN_HEADS, 2, HGRN_HEAD_DIM, HGRN_HEAD_DIM), F32)
    oc_f, oc_b, s_ctx = hgrn_scan(projc, lb, s0)
    o_f, o_b, _ = hgrn_scan(proj, lb, s_ctx)
    lv = lam_vec.astype(F32)
    lam = jnp.exp(jnp.sum(lv[0] * lv[1])) - jnp.exp(jnp.sum(lv[2] * lv[3])) + lam_init
    bound = 1.01 * math.sqrt(DIFF_HEAD_DIM) * jnp.max(jnp.abs(qn_g)) * jnp.max(jnp.abs(kn_g))
    lam = jnp.stack([lam, bound, (bound <= SCORE_BOUND_MAX).astype(F32)])
    cos, sin = tables
    q, k, v = qkv_prep(proj2, qn_g, kn_g, cos, sin, rotary=True)
    qc, kc, vc = qkv_prep(projc2, qn_g, kn_g, cos[:xc.shape[1]], sin[:xc.shape[1]], rotary=False)
    k_all = jnp.concatenate([k, kc], axis=1)
    v_all = jnp.concatenate([v, vc], axis=2)
    y_d = diff_attention(q, k_all, v_all, lam, subln_g, 1.0 - lam_init)
    x_new = even_out(x, o_f, o_b, proj2, y_d, hgrn_g.reshape(-1), g1, w_out)
    if not ctx_out:
        return x_new, None
    yc_d = diff_attention(qc, kc, vc, lam, subln_g, 1.0 - lam_init)
    xc_new = even_out(xc, oc_f, oc_b, projc2, yc_d, hgrn_g.reshape(-1), cg1, w_out)
    return x_new, xc_new


def conv_layer(x, mods, norm1_g, w_in, conv_w, w_out):
    sh1, sc1, g1 = mods
    proj = norm_mod_matmul(x, norm1_g, sh1, sc1, w_in, out_dtype=BF16)
    return conv_out(x, proj, conv_w, g1, w_out)


def kernel(x, c, ctx, c_ctx, mod_w, mod_b, norm1_g, norm2_g, even_w_in, even_w_out, hgrn_lb_logits, hgrn_norm_g,
           diff_qnorm_g, diff_knorm_g, diff_lambda, diff_subln_g, conv_w_in, conv_w, conv_w_out, router_w,
           exp_w_gate, exp_w_up, exp_w_down):
    depth = mod_w.shape[0]
    B, T, D = x.shape
    lb_soft = jax.nn.softmax(hgrn_lb_logits.astype(F32), axis=0)
    lower_bounds = jnp.cumsum(lb_soft, axis=0) - lb_soft[:1]
    last_ctx_layer = 2 * ((depth - 1) // 2)
    cond = jnp.concatenate([c, c_ctx[None, :], jnp.zeros((8 - (B + 1) % 8, D), F32)], axis=0)
    mods = modulation(jax.nn.silu(cond), mod_w, mod_b)
    tables = rope_tables(T)
    xc = ctx
    for l in range(depth):
        read_ctx = l <= last_ctx_layer
        ctx_out = l < last_ctx_layer
        sh1, sc1, g1, sh2, sc2, g2 = [m[:, None, :] for m in jnp.split(mods[l, :B], MOD_CHUNKS, axis=-1)]
        if read_ctx:
            csh1, csc1, cg1, csh2, csc2, cg2 = [
                jnp.broadcast_to(m[None, None, :], (B, 1, D)) for m in jnp.split(mods[l, B], MOD_CHUNKS, axis=-1)]
        ew = (exp_w_gate, exp_w_up, exp_w_down, l)
        if l % 2 == 0:
            e = l // 2
            x, xc_new = even_layer(x, xc, (sh1, sc1, g1), (csh1, csc1, cg1), norm1_g[l],
                                   (even_w_in, e), (even_w_out, e), lower_bounds[e],
                                   hgrn_norm_g[e], diff_qnorm_g[e], diff_knorm_g[e], diff_lambda[e],
                                   diff_subln_g[e], lambda_init(l), ctx_out, tables)
        else:
            j = l // 2
            wi, wo = (conv_w_in, j), (conv_w_out, j)
            x = conv_layer(x, (sh1, sc1, g1), norm1_g[l], wi, conv_w[j], wo)
            xc_new = conv_layer(xc, (csh1, csc1, cg1), norm1_g[l], wi, conv_w[j], wo) if ctx_out else None
        streams = [(x, sh2, sc2, g2)] + ([(xc_new, csh2, csc2, cg2)] if ctx_out else [])
        outs = moe_residual(streams, norm2_g[l], router_w[l], *ew)
        x = outs[0]
        if ctx_out:
            xc = outs[1]
    return x
```

```python
import functools
import math

import numpy as np
import jax
import jax.numpy as jnp
from jax import lax
from jax.experimental import pallas as pl
from jax.experimental.pallas import tpu as pltpu

F32 = jnp.float32
BF16 = jnp.bfloat16

EPS = 1e-6
GRID_W = 64
ROPE_THETA = 10000.0
HGRN_HEAD_DIM = 128
HGRN_HEADS = 4
HGRN_WIDTH = HGRN_HEADS * HGRN_HEAD_DIM
DIFF_HEAD_DIM = 64
DIFF_HEADS = 4
DIFF_WIDTH = DIFF_HEADS * 2 * DIFF_HEAD_DIM
N_EXPERTS = 16
EC_CAPACITY = 2
MOD_CHUNKS = 6
SCAN_CHUNK = 64
SCAN_LEVELS = (32, 16, 8, 4, 2, 1)
SCAN_HEADS_PER_STEP = 2
NMM_ROW_SPLIT = 4
FFN_SAMPLES_PER_STEP = 2
LANES = 128
ROW_ALIGN = 16
VMEM_LIMIT = 56 * 1024 * 1024
SCORE_BOUND_MAX = 40.0


def _cparams(*sem):
    return pltpu.CompilerParams(dimension_semantics=sem, vmem_limit_bytes=VMEM_LIMIT)


def _row_tile(t, want):
    return want if t % want == 0 else t


def _mod_kernel(s_ref, w_ref, b_ref, o_ref):
    s = s_ref[...]
    w = w_ref[0]
    s_hi = s.astype(BF16)
    s_lo = (s - s_hi.astype(F32)).astype(BF16)
    w_hi = w.astype(BF16)
    w_lo = (w - w_hi.astype(F32)).astype(BF16)
    d = lambda a, b: jnp.dot(a, b, preferred_element_type=F32)
    o_ref[0] = d(s_hi, w_hi) + d(s_hi, w_lo) + d(s_lo, w_hi) + b_ref[0]


def modulation(s, mod_w, mod_b, *, tn=2048):
    R, D = s.shape
    depth, _, N = mod_w.shape
    return pl.pallas_call(
        _mod_kernel,
        grid=(depth, N // tn),
        in_specs=[pl.BlockSpec((R, D), lambda l, j: (0, 0)),
                  pl.BlockSpec((1, D, tn), lambda l, j: (l, 0, j)),
                  pl.BlockSpec((1, 1, tn), lambda l, j: (l, 0, j))],
        out_specs=pl.BlockSpec((1, R, tn), lambda l, j: (l, 0, j)),
        out_shape=jax.ShapeDtypeStruct((depth, R, N), F32),
        compiler_params=_cparams("parallel", "parallel"),
        name="modulation",
    )(s, mod_w, mod_b.reshape(depth, 1, N))


def _nmm_kernel(x_ref, g_ref, sh_ref, sc_ref, w_ref, o_ref, w_scr):
    @pl.when((pl.program_id(1) == 0) & (pl.program_id(2) == 0))
    def _():
        w_scr[...] = w_ref[0].astype(BF16)

    tm = x_ref.shape[1]
    rows = tm // NMM_ROW_SPLIT if tm % (NMM_ROW_SPLIT * ROW_ALIGN) == 0 else tm
    for r0 in range(0, tm, rows):
        x = x_ref[0, r0:r0 + rows, :]
        ms = jnp.mean(x * x, axis=-1, keepdims=True)
        h = (x * lax.rsqrt(ms + EPS)) * g_ref[...]
        h = h * (1.0 + sc_ref[0]) + sh_ref[0]
        o_ref[0, r0:r0 + rows, :] = jnp.dot(h.astype(BF16), w_scr[...],
                                            preferred_element_type=F32).astype(o_ref.dtype)


def norm_mod_matmul(x, g, shift, scale, w, *, cols=None, tm=512, out_dtype=F32):
    B, T, D = x.shape
    w, li = w
    first, N = cols if cols else (0, w.shape[2])
    tm = _row_tile(T, tm)
    tn = N
    return pl.pallas_call(
        _nmm_kernel,
        grid=(N // tn, B, T // tm),
        in_specs=[
            pl.BlockSpec((1, tm, D), lambda j, b, i: (b, i, 0)),
            pl.BlockSpec((1, D), lambda j, b, i: (0, 0)),
            pl.BlockSpec((1, 1, D), lambda j, b, i: (b, 0, 0)),
            pl.BlockSpec((1, 1, D), lambda j, b, i: (b, 0, 0)),
            pl.BlockSpec((1, D, tn), lambda j, b, i: (li, 0, first + j)),
        ],
        out_specs=pl.BlockSpec((1, tm, tn), lambda j, b, i: (b, i, j)),
        out_shape=jax.ShapeDtypeStruct((B, T, N), out_dtype),
        scratch_shapes=[pltpu.VMEM((D, tn), BF16)],
        compiler_params=_cparams("arbitrary", "arbitrary", "arbitrary"),
        name="norm_mod_matmul",
    )(x, g.reshape(1, D), shift, scale, w)


def _scan_constants():
    C = SCAN_CHUNK
    t = np.arange(C)[:, None]
    u = np.arange(C)[None, :]
    mats = [u <= t, u > t]
    masks = []
    for w in SCAN_LEVELS:
        m = (t // (2 * w)) * 2 * w + w - 1
        later = (t // w) % 2 == 1
        mats.append(np.where(later, (u > m) & (u <= t), (u > t) & (u <= m)))
        masks.append(later & ((u // w) % 2 == 0) & (u // (2 * w) == t // (2 * w)))
    masks = [t == u] + masks + [np.ones((C, C), bool)]
    a_f = np.stack(mats).astype(np.float32)
    m_f = np.stack(masks).astype(np.float32)
    a = np.stack([a_f, a_f[:, ::-1, ::-1]]).reshape(2, -1, C)
    m = np.stack([m_f, m_f[:, ::-1, ::-1]])
    m_pairs = np.concatenate([m[:, 0::2], m[:, 1::2]], axis=3)
    return np.concatenate([a, a], axis=2), m_pairs


def _scan_pair(q, z, v, lb, st, a2, mask_ref, d, later):
    C = SCAN_CHUNK
    W = HGRN_HEAD_DIM
    nl = len(SCAN_LEVELS)
    nt = lambda x, y: lax.dot_general(x, y, (((1,), (1,)), ((), ())), preferred_element_type=F32)
    nn = lambda x, y: jnp.dot(x, y, preferred_element_type=F32)
    e_abs = jnp.exp(-jnp.abs(z))
    r = 1.0 / (1.0 + e_abs)
    er = e_abs * r
    pos = z >= 0.0
    g2 = jnp.log2(lb + (1.0 - lb) * jnp.where(pos, r, er))
    k = (1.0 - lb) * jnp.where(pos, er, r)
    hi = g2.astype(BF16)
    lo = (g2 - hi.astype(F32)).astype(BF16)
    gs = jnp.concatenate([jnp.concatenate([hi[:C], hi[C:]], axis=1),
                          jnp.concatenate([lo[:C], lo[C:]], axis=1)], axis=0)
    x = nn(a2, gs)
    yield
    x = jnp.exp2(x)
    first, second = (1, 0) if d else (0, 1)
    vb = v.astype(BF16)
    sides, qt, kt, dec = [], [], [], []
    for c in (0, 1):
        qc, kc, xs = q[c * C:(c + 1) * C], k[c * C:(c + 1) * C], x[:, c * W:(c + 1) * W]
        ops = [(qc.astype(BF16), kc.astype(BF16))]
        for i in range(nl):
            qk = (jnp.where(later[i], qc, kc) * xs[(2 + i) * C:(3 + i) * C]).astype(BF16)
            ops.append((qk, qk))
        sides.append(ops)
        qt.append(qc * xs[0:C])
        kt.append(kc * xs[C:2 * C])
        dec.append(xs[0:1, :] if d else xs[C - 1:C, :])
    zero = jnp.zeros((C, W), BF16)
    sides[first].append((zero, zero))
    sides[second].append((qt[second].astype(BF16), kt[first].astype(BF16)))
    prods = []
    for ops in sides:
        pair = []
        for j in range(0, nl + 2, 2):
            (qa, ka), (qb, kb) = ops[j], ops[j + 1]
            rhs = jnp.concatenate([jnp.concatenate([ka, zero], axis=1), jnp.concatenate([zero, kb], axis=1)], axis=0)
            pair.append(nt(jnp.concatenate([qa, qb], axis=1), rhs))
        prods.append(pair)
        yield
    qt[second] = qt[second] * dec[first]
    kt[first] = kt[first] * dec[second]
    o_st = nt(jnp.concatenate([a.astype(BF16) for a in qt], axis=0), st.astype(BF16))
    upd = lax.dot_general(vb, jnp.concatenate(kt, axis=0).astype(BF16), (((0,), (0,)), ((), ())),
                          preferred_element_type=F32)
    yield
    o = [None, None]
    n_main = nl // 2
    for c, pair in enumerate(prods):
        main = mask_ref[d, 0] * pair[0]
        for j in range(1, n_main):
            main = main + mask_ref[d, j] * pair[j]
        last = mask_ref[d, n_main] * pair[n_main]
        vc, vf = vb[c * C:(c + 1) * C], vb[first * C:(first + 1) * C]
        o[c] = o_st[c * C:(c + 1) * C] + nn(jnp.concatenate([main, last], axis=1).astype(BF16),
                                            jnp.concatenate([vc, vc, vc, vf], axis=0))
    yield
    return jnp.concatenate(o, axis=0), st * (dec[0] * dec[1]) + upd


def _run_interleaved(gens):
    results = [None] * len(gens)
    live = list(range(len(gens)))
    while live:
        for i in list(live):
            try:
                next(gens[i])
            except StopIteration as done:
                results[i] = done.value
                live.remove(i)
    return results


def _scan_kernel(qf_ref, zf_ref, vf_ref, qb_ref, zb_ref, vb_ref, lbf_ref, lbb_ref, s0_ref, a_ref, mask_ref,
                 of_ref, ob_ref, sT_ref, st_scr, *, n_chunks):
    c = pl.program_id(2)
    C = SCAN_CHUNK

    @pl.when(c == 0)
    def _():
        st_scr[...] = s0_ref[0]

    W = HGRN_HEAD_DIM
    row = lax.broadcasted_iota(jnp.int32, (C, W), 0)
    later = [[(row // w) % 2 == 1 for w in SCAN_LEVELS], [((C - 1 - row) // w) % 2 == 1 for w in SCAN_LEVELS]]
    in_refs = [(qf_ref, zf_ref, vf_ref), (qb_ref, zb_ref, vb_ref)]
    lb_refs = [lbf_ref, lbb_ref]
    o_refs = [of_ref, ob_ref]
    n_pairs = n_chunks // 2
    heads = range(qf_ref.shape[2] // W)

    def body(i, carry):
        r0 = [pl.multiple_of(i * 2 * C, 2 * C), pl.multiple_of((n_pairs - 1 - i) * 2 * C, 2 * C)]
        chains = [(d, h) for h in heads for d in (0, 1)]
        ins = [[ref[0, pl.ds(r0[d], 2 * C), h * W:(h + 1) * W] for ref in in_refs[d]] for d, h in chains]
        sts = [st_scr[h, d] for d, h in chains]
        outs = _run_interleaved([
            _scan_pair(*x, lb_refs[d][:, h * W:(h + 1) * W], st, a_ref[d], mask_ref, d, later[d])
            for (d, h), x, st in zip(chains, ins, sts)])
        for (d, h), (o, st) in zip(chains, outs):
            o_refs[d][0, pl.ds(r0[d], 2 * C), h * W:(h + 1) * W] = o.astype(o_refs[d].dtype)
            st_scr[h, d] = st
        return carry

    lax.fori_loop(0, n_pairs, body, 0)

    @pl.when(c == pl.num_programs(2) - 1)
    def _():
        sT_ref[0] = st_scr[...]


def hgrn_scan(proj, lb, s0, *, tb=512):
    B, T, _ = proj.shape
    tb = _row_tile(T, tb)
    nc = T // tb
    H = HGRN_HEADS
    hd = HGRN_HEAD_DIM
    hps = SCAN_HEADS_PER_STEP
    G = H // hps
    wd = hps * hd
    fwd = lambda grp: pl.BlockSpec((1, tb, wd), lambda b, h, c: (b, c, grp * G + h))
    bwd = lambda grp: pl.BlockSpec((1, tb, wd), lambda b, h, c: (b, nc - 1 - c, grp * G + h))
    kern = functools.partial(_scan_kernel, n_chunks=tb // SCAN_CHUNK)
    a2, masks = _scan_constants()
    return pl.pallas_call(
        kern,
        grid=(B, G, nc),
        in_specs=[fwd(0), fwd(1), fwd(3), bwd(0), bwd(2), bwd(3),
                  pl.BlockSpec((1, wd), lambda b, h, c: (0, h)),
                  pl.BlockSpec((1, wd), lambda b, h, c: (0, h)),
                  pl.BlockSpec((1, hps, 2, hd, hd), lambda b, h, c: (b, h, 0, 0, 0)),
                  pl.BlockSpec(a2.shape, lambda b, h, c: (0, 0, 0)),
                  pl.BlockSpec(masks.shape, lambda b, h, c: (0, 0, 0, 0))],
        out_specs=[pl.BlockSpec((1, tb, wd), lambda b, h, c: (b, c, h)),
                   pl.BlockSpec((1, tb, wd), lambda b, h, c: (b, nc - 1 - c, h)),
                   pl.BlockSpec((1, hps, 2, hd, hd), lambda b, h, c: (b, h, 0, 0, 0))],
        out_shape=[jax.ShapeDtypeStruct((B, T, HGRN_WIDTH), BF16),
                   jax.ShapeDtypeStruct((B, T, HGRN_WIDTH), BF16),
                   jax.ShapeDtypeStruct((B, H, 2, hd, hd), F32)],
        scratch_shapes=[pltpu.VMEM((hps, 2, hd, hd), F32)],
        compiler_params=_cparams("parallel", "parallel", "arbitrary"),
        name="hgrn_scan",
    )(proj, proj, proj, proj, proj, proj, lb[0:1], lb[1:2], s0, jnp.asarray(a2, BF16), jnp.asarray(masks, F32))


def _group_mean_sq(x, gmat):
    sq = x * x
    hi = sq.astype(BF16)
    lo = (sq - hi.astype(F32)).astype(BF16)
    return (jnp.dot(hi, gmat, preferred_element_type=F32) + jnp.dot(lo, gmat, preferred_element_type=F32))


def _qkv_prep_kernel(q_ref, k_ref, v_ref, qg_ref, kg_ref, cos_ref, sin_ref, qo_ref, ko_ref, vo_ref, *, rotary):
    W = LANES
    r_i = lax.broadcasted_iota(jnp.int32, (W, W), 0) // DIFF_HEAD_DIM
    c_i = lax.broadcasted_iota(jnp.int32, (W, W), 1) // DIFF_HEAD_DIM
    gmat = jnp.where(r_i == c_i, 1.0 / DIFF_HEAD_DIM, 0.0).astype(BF16)
    lane = lax.broadcasted_iota(jnp.int32, (1, W), 1)
    first = (lane % 32) < 16

    def prep(x, g, scale):
        y = (x * lax.rsqrt(_group_mean_sq(x, gmat) + EPS)) * g
        if rotary:
            partner = jnp.where(first, pltpu.roll(y, W - 16, axis=1), pltpu.roll(y, 16, axis=1))
            y = y * cos_ref[...] + partner * sin_ref[...]
        if scale != 1.0:
            y = y * scale
        return y

    for h in range(DIFF_HEADS):
        sl = slice(h * W, (h + 1) * W)
        qo_ref[0, sl, :] = prep(q_ref[0, :, sl].astype(F32), qg_ref[...], DIFF_HEAD_DIM ** -0.5).T.astype(BF16)
        ko_ref[0, :, sl] = prep(k_ref[0, :, sl].astype(F32), kg_ref[...], 1.0).astype(BF16)
        vo_ref[0, sl, :] = v_ref[0, :, sl].astype(F32).T.astype(BF16)


def qkv_prep(proj, qg, kg, cos, sin, *, rotary, tm=512):
    B, T, _ = proj.shape
    tm = _row_tile(T, tm)
    Wd = DIFF_WIDTH
    col = lambda j: pl.BlockSpec((1, tm, Wd), lambda b, i: (b, i, j))
    vec = pl.BlockSpec((1, LANES), lambda b, i: (0, 0))
    tab = pl.BlockSpec((tm, LANES), lambda b, i: (i, 0))
    rows = pl.BlockSpec((1, tm, Wd), lambda b, i: (b, i, 0))
    cols = pl.BlockSpec((1, Wd, tm), lambda b, i: (b, 0, i))
    return pl.pallas_call(
        functools.partial(_qkv_prep_kernel, rotary=rotary),
        grid=(B, T // tm),
        in_specs=[col(1), col(2), col(3), vec, vec, tab, tab],
        out_specs=[cols, rows, cols],
        out_shape=[jax.ShapeDtypeStruct((B, Wd, T), BF16), jax.ShapeDtypeStruct((B, T, Wd), BF16),
                   jax.ShapeDtypeStruct((B, Wd, T), BF16)],
        compiler_params=_cparams("parallel", "parallel"),
        name="qkv_prep",
    )(proj, proj, proj, jnp.tile(qg, 2).reshape(1, LANES), jnp.tile(kg, 2).reshape(1, LANES), cos, sin)


def rope_tables(T):
    n = DIFF_HEAD_DIM // 2
    inv = 1.0 / (ROPE_THETA ** (jnp.arange(0, n, 2, dtype=F32) / n))
    t = jnp.arange(T)
    ang_r = (t // GRID_W).astype(F32)[:, None] * inv[None, :]
    ang_c = (t % GRID_W).astype(F32)[:, None] * inv[None, :]
    cos = jnp.concatenate([jnp.cos(ang_r)] * 2 + [jnp.cos(ang_c)] * 2, axis=-1)
    sin = jnp.concatenate([-jnp.sin(ang_r), jnp.sin(ang_r), -jnp.sin(ang_c), jnp.sin(ang_c)], axis=-1)
    return jnp.tile(cos, (1, 2)), jnp.tile(sin, (1, 2))


def _attn_tile(qt, k_ref, v_ref, lam, key_chunk, bound=None):
    tq = qt.shape[1]
    S = k_ref.shape[1]
    row = lax.broadcasted_iota(jnp.int32, (LANES, 1), 0)
    zero = jnp.zeros_like(qt)
    qq = jnp.concatenate([jnp.where(row < DIFF_HEAD_DIM, qt, zero),
                          jnp.where(row >= DIFF_HEAD_DIM, qt, zero)], axis=1)
    m = jnp.full((1, 2 * tq), -jnp.inf, F32)
    acc = [jnp.zeros((LANES + 16, tq), F32), jnp.zeros((LANES + 16, tq), F32)]
    ones = jnp.ones((16, key_chunk), BF16)
    chunks = [(c0, min(c0 + key_chunk, S)) for c0 in range(0, S, key_chunk)]
    scores = lambda c: jnp.dot(k_ref[0, c[0]:c[1], :], qq, preferred_element_type=F32)
    s_next = scores(chunks[0])
    yield
    for n, (c0, c1) in enumerate(chunks):
        s = s_next
        if n + 1 < len(chunks):
            s_next = scores(chunks[n + 1])
        vt1 = jnp.concatenate([v_ref[0, :, c0:c1], ones[:, :c1 - c0]], axis=0)
        if bound is None:
            m_new = jnp.maximum(m, jnp.max(s, axis=0, keepdims=True))
            alpha = jnp.exp(m - m_new)
            pb = jnp.exp((s - m_new).astype(BF16))
            m = m_new
        else:
            alpha = None
            pb = jnp.exp(s - bound).astype(BF16)
        for i in range(2):
            pv = jnp.dot(vt1, pb[:, i * tq:(i + 1) * tq], preferred_element_type=F32)
            acc[i] = acc[i] + pv if alpha is None else acc[i] * alpha[:, i * tq:(i + 1) * tq] + pv
        yield
    inv = [1.0 / a[LANES:LANES + 1] for a in acc]
    return acc[0][:LANES] * inv[0] - acc[1][:LANES] * (lam * inv[1])


def _diff_attn_kernel(par_ref, q_ref, k_ref, v_ref, g_ref, o_ref, *, out_scale, key_chunk, tq):
    n_tiles = q_ref.shape[2] // tq

    def run(bound):
        outs = _run_interleaved([
            _attn_tile(q_ref[0, :, i * tq:(i + 1) * tq], k_ref, v_ref, par_ref[0], key_chunk, bound)
            for i in range(n_tiles)])
        for i, o in enumerate(outs):
            ms = jnp.mean(o * o, axis=0, keepdims=True)
            o_ref[0, i * tq:(i + 1) * tq, :] = ((o * lax.rsqrt(ms + EPS)) * g_ref[...] * out_scale).T.astype(BF16)

    @pl.when(par_ref[2] > 0.5)
    def _():
        run(par_ref[1])

    @pl.when(par_ref[2] <= 0.5)
    def _():
        run(None)


def diff_attention(qt, k, vt, params, subln_g, out_scale, *, tq=256, tiles_per_step=2, key_chunk=512):
    B, Wd, T = qt.shape
    S = k.shape[1]
    tq = _row_tile(T, tq)
    ts = _row_tile(T, tq * tiles_per_step)
    return pl.pallas_call(
        functools.partial(_diff_attn_kernel, out_scale=out_scale, key_chunk=key_chunk, tq=tq),
        grid=(B, DIFF_HEADS, T // ts),
        in_specs=[pl.BlockSpec(memory_space=pltpu.SMEM),
                  pl.BlockSpec((1, LANES, ts), lambda b, h, i: (b, h, i)),
                  pl.BlockSpec((1, S, LANES), lambda b, h, i: (b, 0, h)),
                  pl.BlockSpec((1, LANES, S), lambda b, h, i: (b, h, 0)),
                  pl.BlockSpec((LANES, 1), lambda b, h, i: (0, 0))],
        out_specs=pl.BlockSpec((1, ts, LANES), lambda b, h, i: (b, i, h)),
        out_shape=jax.ShapeDtypeStruct((B, T, Wd), BF16),
        compiler_params=_cparams("parallel", "parallel", "arbitrary"),
        name="diff_attention",
    )(params, qt, k, vt, subln_g.reshape(LANES, 1))


def _even_out_kernel(x_ref, of_ref, ob_ref, gate_ref, yd_ref, hg_ref, g1_ref, w_ref, *rest):
    router_refs, o_ref, moe_refs = rest[:-3], rest[-3], rest[-2:]
    acc = jnp.dot(yd_ref[0].astype(BF16), w_ref[0, HGRN_WIDTH:, :].astype(BF16), preferred_element_type=F32)
    for h in range(HGRN_HEADS):
        sl = slice(h * HGRN_HEAD_DIM, (h + 1) * HGRN_HEAD_DIM)
        o = of_ref[0, :, sl].astype(F32) + ob_ref[0, :, sl].astype(F32)
        ms = jnp.mean(o * o, axis=-1, keepdims=True)
        gate = gate_ref[0, :, sl].astype(F32)
        yh = (o * lax.rsqrt(ms + EPS)) * hg_ref[:, sl] * (gate * jax.nn.sigmoid(gate))
        acc = acc + jnp.dot(yh.astype(BF16), w_ref[0, sl, :].astype(BF16), preferred_element_type=F32)
    x_new = x_ref[0] + g1_ref[0] * acc
    o_ref[0] = x_new
    _router_outputs(x_new, *router_refs, *moe_refs)


def even_out(x, o_f, o_b, proj, y_d, hgrn_g, g1, w_out, router, *, tm=512):
    B, T, D = x.shape
    w_out, li = w_out
    tm = _row_tile(T, tm)
    Wd = HGRN_WIDTH
    row = lambda w, j: pl.BlockSpec((1, tm, w), lambda b, i: (b, i, j))
    r_args, r_in, r_out, r_shapes = _router_specs(B, T, D, tm, router)
    return pl.pallas_call(
        _even_out_kernel,
        grid=(B, T // tm),
        in_specs=[row(D, 0), row(Wd, 0), row(Wd, 0), row(Wd, 0), row(Wd, 0),
                  pl.BlockSpec((1, Wd), lambda b, i: (0, 0)),
                  pl.BlockSpec((1, 1, D), lambda b, i: (b, 0, 0)),
                  pl.BlockSpec((1,) + w_out.shape[1:], lambda b, i: (li, 0, 0))] + r_in,
        out_specs=[row(D, 0)] + r_out,
        out_shape=[jax.ShapeDtypeStruct((B, T, D), F32)] + r_shapes,
        compiler_params=_cparams("parallel", "parallel"),
        name="even_out",
    )(x, o_f, o_b, proj, y_d, hgrn_g.reshape(1, Wd), g1, w_out, *r_args)


def _conv_out_kernel(x_ref, bg_ref, cg_ref, v_ref, cp_ref, vp_ref, cn_ref, vn_ref, cw_ref, g1_ref, w_ref, *rest):
    router_refs, o_ref, moe_refs = rest[:-3], rest[-3], rest[-2:]
    i = pl.program_id(1)
    n = pl.num_programs(1)
    f32 = lambda ref, *idx: ref[idx].astype(F32)
    u = f32(cg_ref, 0) * f32(v_ref, 0)
    tm = u.shape[0]
    last = ROW_ALIGN - 1
    u_prev_row = jnp.where(i > 0, f32(cp_ref, 0, slice(last, last + 1)) * f32(vp_ref, 0, slice(last, last + 1)), 0.0)
    u_next_row = jnp.where(i < n - 1, f32(cn_ref, 0, slice(0, 1)) * f32(vn_ref, 0, slice(0, 1)), 0.0)
    ridx = lax.broadcasted_iota(jnp.int32, (tm, 1), 0)
    u_prev = jnp.where(ridx == 0, u_prev_row, pltpu.roll(u, 1, axis=0))
    u_next = jnp.where(ridx == tm - 1, u_next_row, pltpu.roll(u, tm - 1, axis=0))
    y = cw_ref[0:1, :] * u_prev + cw_ref[1:2, :] * u + cw_ref[2:3, :] * u_next
    acc = jnp.dot((f32(bg_ref, 0) * y).astype(BF16), w_ref[0].astype(BF16), preferred_element_type=F32)
    x_new = x_ref[0] + g1_ref[0] * acc
    o_ref[0] = x_new
    _router_outputs(x_new, *router_refs, *moe_refs)


def conv_out(x, proj, conv_w, g1, w_out, router, *, tm=512):
    B, T, D = x.shape
    w_out, li = w_out
    tm = _row_tile(T, tm)
    rt = tm // ROW_ALIGN
    last_blk = T // ROW_ALIGN - 1
    row = lambda j: pl.BlockSpec((1, tm, D), lambda b, i: (b, i, j))
    prev = lambda j: pl.BlockSpec((1, ROW_ALIGN, D), lambda b, i: (b, jnp.maximum(i * rt - 1, 0), j))
    nxt = lambda j: pl.BlockSpec((1, ROW_ALIGN, D), lambda b, i: (b, jnp.minimum((i + 1) * rt, last_blk), j))
    r_args, r_in, r_out, r_shapes = _router_specs(B, T, D, tm, router)
    return pl.pallas_call(
        _conv_out_kernel,
        grid=(B, T // tm),
        in_specs=[row(0), row(0), row(1), row(2), prev(1), prev(2), nxt(1), nxt(2),
                  pl.BlockSpec((8, D), lambda b, i: (0, 0)),
                  pl.BlockSpec((1, 1, D), lambda b, i: (b, 0, 0)),
                  pl.BlockSpec((1,) + w_out.shape[1:], lambda b, i: (li, 0, 0))] + r_in,
        out_specs=[row(0)] + r_out,
        out_shape=[jax.ShapeDtypeStruct((B, T, D), F32)] + r_shapes,
        compiler_params=_cparams("parallel", "parallel"),
        name="conv_out",
    )(x, proj, proj, proj, proj, proj, proj, proj,
      jnp.concatenate([conv_w, jnp.zeros((8 - conv_w.shape[0], D), conv_w.dtype)], axis=0), g1, w_out, *r_args)


def _router_outputs(x, g_ref, sh_ref, sc_ref, rw_ref, h_ref, lg_ref):
    ms = jnp.mean(x * x, axis=-1, keepdims=True)
    h = (x * lax.rsqrt(ms + EPS)) * g_ref[...]
    h = h * (1.0 + sc_ref[0]) + sh_ref[0]
    h_ref[0] = h.astype(BF16)
    rw = rw_ref[...]
    h_hi = h.astype(BF16)
    h_lo = (h - h_hi.astype(F32)).astype(BF16)
    w_hi = rw.astype(BF16)
    w_lo = (rw - w_hi.astype(F32)).astype(BF16)
    d = lambda a, b: jnp.dot(a, b, preferred_element_type=F32)
    lg_ref[0] = d(h_hi, w_hi) + d(h_hi, w_lo) + d(h_lo, w_hi)


def _router_specs(B, T, D, tm, router):
    g, shift, scale, router_w = router
    E = router_w.shape[1]
    vec = pl.BlockSpec((1, 1, D), lambda b, i: (b, 0, 0))
    return ((g.reshape(1, D), shift, scale, router_w),
            [pl.BlockSpec((1, D), lambda b, i: (0, 0)), vec, vec, pl.BlockSpec((D, E), lambda b, i: (0, 0))],
            [pl.BlockSpec((1, tm, D), lambda b, i: (b, i, 0)), pl.BlockSpec((1, tm, E), lambda b, i: (b, i, 0))],
            [jax.ShapeDtypeStruct((B, T, D), BF16), jax.ShapeDtypeStruct((B, T, E), F32)])


def _lane_prefix(flags, tri_tot):
    E, N = flags.shape
    carries = [jnp.zeros((E, LANES), F32)]
    out = []
    for j in range(N // LANES):
        r = jnp.dot(flags[:, j * LANES:(j + 1) * LANES].astype(BF16), tri_tot, preferred_element_type=F32)
        out.append(r[:, :LANES] + carries[-1])
        carries.append(carries[-1] + r[:, LANES:])
    return jnp.concatenate(out, axis=1), carries


def _route_kernel(lg_ref, aff_ref, rank_ref, ts_ref, *, cap, tt):
    lg = lg_ref[0]
    E, N = lg.shape
    p = jnp.exp(lg - jnp.max(lg, axis=0, keepdims=True))
    aff = p / jnp.sum(p, axis=0, keepdims=True)
    aff_ref[0] = aff
    count = lambda m: jnp.sum(jnp.where(m, 1.0, 0.0), axis=1, keepdims=True)
    as_float = lambda i: pltpu.bitcast(i, F32)

    def refine_bits(i, thr):
        cand = thr | jnp.left_shift(jnp.int32(1), 30 - i)
        return jnp.where(count(aff >= as_float(cand)) >= cap, cand, thr)

    thr = lax.fori_loop(0, 31, refine_bits, jnp.zeros((E, 1), jnp.int32))

    def refine_mid(i, lo_hi):
        lo, hi = lo_hi
        mid = 0.5 * (lo + hi)
        up = count(aff >= mid) >= cap
        return jnp.where(up, mid, lo), jnp.where(up, hi, mid)

    lo, hi = lax.fori_loop(0, 24, refine_mid, (as_float(thr), as_float(jnp.maximum(thr + 1, 0x00800000))))
    gt = aff >= hi
    eq = (aff >= lo) & (aff < hi)
    r_i = lax.broadcasted_iota(jnp.int32, (LANES, 2 * LANES), 0)
    c_i = lax.broadcasted_iota(jnp.int32, (LANES, 2 * LANES), 1)
    tri_tot = jnp.where((r_i < c_i) | (c_i >= LANES), 1.0, 0.0).astype(BF16)
    eq_rank, _ = _lane_prefix(jnp.where(eq, 1.0, 0.0), tri_tot)
    sel = gt | (eq & (eq_rank < cap - count(gt)))
    rank, before = _lane_prefix(jnp.where(sel, 1.0, 0.0), tri_tot)
    rank_ref[0] = jnp.where(sel, rank, -1.0).astype(jnp.int32)
    lane = lax.broadcasted_iota(jnp.int32, (E, LANES), 1)
    ts = jnp.zeros((E, LANES), F32)
    for k in range(N // tt + 1):
        ts = jnp.where(lane == k, before[k * tt // LANES], ts)
    ts_ref[0] = ts.astype(jnp.int32)


def route(logits_t, cap, tt):
    B, E, N = logits_t.shape
    blk = pl.BlockSpec((1, E, N), lambda b: (b, 0, 0))
    return pl.pallas_call(
        functools.partial(_route_kernel, cap=cap, tt=tt),
        grid=(B,),
        in_specs=[blk],
        out_specs=[blk, blk, pl.BlockSpec((1, E, LANES), lambda b: (b, 0, 0))],
        out_shape=[jax.ShapeDtypeStruct((B, E, N), F32), jax.ShapeDtypeStruct((B, E, N), jnp.int32),
                   jax.ShapeDtypeStruct((B, E, LANES), jnp.int32)],
        compiler_params=_cparams("parallel"),
        name="route",
    )(logits_t)


def _window(lo, w, win, cap):
    lower = (lo // ROW_ALIGN) * ROW_ALIGN + w * win
    return pl.multiple_of(jnp.minimum(lower, cap - win), ROW_ALIGN), lower


def _extra_windows(cap, tt, win):
    return -(-(min(cap, tt) + ROW_ALIGN - 1) // win) - 1


def _moe_gather_kernel(ts_ref, rank_ref, h_ref, xe_ref, acc_ref, *, win, tt):
    b, e = pl.program_id(0), pl.program_id(1)
    N = h_ref.shape[1]
    cap = xe_ref.shape[2]
    nt = N // tt
    base = (b * pl.num_programs(1) + e) * (nt + 1)
    acc_ref[...] = jnp.zeros_like(acc_ref)
    row = lax.broadcasted_iota(jnp.int32, (win, tt), 0)

    def place(k, w):
        start, lower = _window(ts_ref[base + k], w, win, cap)
        c0 = k * tt if isinstance(k, int) else pl.multiple_of(k * tt, tt)
        rk = rank_ref[0, 0, :, pl.ds(c0, tt)]
        onehot = jnp.where(jnp.where(rk >= lower, rk, -1) == row + start, 1.0, 0.0).astype(BF16)
        acc_ref[pl.ds(start, win), :] += jnp.dot(onehot, h_ref[0, pl.ds(c0, tt), :], preferred_element_type=F32)

    span = lambda k: ts_ref[base + k + 1] - (ts_ref[base + k] // ROW_ALIGN) * ROW_ALIGN
    need = jnp.int32(0)
    for k in range(nt):
        place(k, 0)
        need = jnp.maximum(need, span(k))

    n_extra = _extra_windows(cap, tt, win)

    @pl.when(need > win)
    def _():
        def extra(i, carry):
            k, w = i // n_extra, i % n_extra + 1

            @pl.when(span(k) > w * win)
            def _():
                place(k, w)
            return carry

        lax.fori_loop(0, nt * n_extra, extra, 0)

    xe_ref[0, 0] = acc_ref[...].astype(BF16)


def moe_gather(ts_flat, rank, h, cap, *, tt):
    B, E, N = rank.shape
    D = h.shape[2]
    return pl.pallas_call(
        functools.partial(_moe_gather_kernel, win=min(LANES, cap), tt=tt),
        grid_spec=pltpu.PrefetchScalarGridSpec(
            num_scalar_prefetch=1,
            grid=(B, E),
            in_specs=[pl.BlockSpec((1, 1, 1, N), lambda b, e, ts: (b, e, 0, 0)),
                      pl.BlockSpec((1, N, D), lambda b, e, ts: (b, 0, 0))],
            out_specs=pl.BlockSpec((1, 1, cap, D), lambda b, e, ts: (b, e, 0, 0)),
            scratch_shapes=[pltpu.VMEM((cap, D), F32)]),
        out_shape=jax.ShapeDtypeStruct((B, E, cap, D), BF16),
        compiler_params=_cparams("parallel", "arbitrary"),
        name="moe_gather",
    )(ts_flat, rank.reshape(B, E, 1, N), h)


def _expert_kernel(*refs, n_streams):
    xe_refs = refs[:n_streams]
    wg_ref, wu_ref, wd_ref = refs[n_streams:n_streams + 3]
    o_refs = refs[n_streams + 3:2 * n_streams + 3]
    wg_scr, wu_scr, wd_scr = refs[2 * n_streams + 3:]

    @pl.when(pl.program_id(1) == 0)
    def _():
        wg_scr[...] = wg_ref[0, 0].astype(BF16)
        wu_scr[...] = wu_ref[0, 0].astype(BF16)
        wd_scr[...] = wd_ref[0, 0].astype(BF16)

    def swiglu(x):
        a = jnp.dot(x, wg_scr[...], preferred_element_type=F32)
        u = jnp.dot(x, wu_scr[...], preferred_element_type=F32)
        yield
        hid = (a * jax.nn.sigmoid(a)) * u
        return jnp.dot(hid.astype(BF16), wd_scr[...], preferred_element_type=F32).astype(BF16)

    nb = xe_refs[0].shape[0]
    ys = _run_interleaved([swiglu(jnp.concatenate([r[s, 0] for r in xe_refs], axis=0)) for s in range(nb)])
    for s, y in enumerate(ys):
        r0 = 0
        for o_ref in o_refs:
            o_ref[s, 0] = y[r0:r0 + o_ref.shape[2]]
            r0 += o_ref.shape[2]


def expert_ffn(xes, w_gate, w_up, w_down, li):
    B, E, _, D = xes[0].shape
    FF = w_gate.shape[3]
    nb = FFN_SAMPLES_PER_STEP if B % FFN_SAMPLES_PER_STEP == 0 else 1
    rows = [pl.BlockSpec((nb, 1, xe.shape[2], D), lambda e, b: (b, e, 0, 0)) for xe in xes]
    return pl.pallas_call(
        functools.partial(_expert_kernel, n_streams=len(xes)),
        grid=(E, B // nb),
        in_specs=rows + [pl.BlockSpec((1, 1, D, FF), lambda e, b: (li, e, 0, 0)),
                         pl.BlockSpec((1, 1, D, FF), lambda e, b: (li, e, 0, 0)),
                         pl.BlockSpec((1, 1, FF, D), lambda e, b: (li, e, 0, 0))],
        out_specs=rows,
        out_shape=[jax.ShapeDtypeStruct(xe.shape, BF16) for xe in xes],
        scratch_shapes=[pltpu.VMEM((D, FF), BF16), pltpu.VMEM((D, FF), BF16), pltpu.VMEM((FF, D), BF16)],
        compiler_params=_cparams("arbitrary", "arbitrary"),
        name="expert_ffn",
    )(*xes, w_gate, w_up, w_down)


def _moe_combine_kernel(ts_ref, rank_ref, aff_ref, ye_ref, x_ref, g2_ref, o_ref, acc_ref, *, win, group):
    b, k = pl.program_id(0), pl.program_id(1)
    E, cap = ye_ref.shape[1], ye_ref.shape[2]
    tt = x_ref.shape[1]
    nt = pl.num_programs(1)
    col = lax.broadcasted_iota(jnp.int32, (tt, win), 1)
    lo = [ts_ref[(b * E + e) * (nt + 1) + k] for e in range(E)]
    hi = [ts_ref[(b * E + e) * (nt + 1) + k + 1] for e in range(E)]

    def contribution(w):
        total = None
        for g0 in range(0, E, group):
            lhs, rhs = [], []
            for e in range(g0, g0 + group):
                start, lower = _window(lo[e], w, win, cap)
                rk = rank_ref[0, :, e:e + 1]
                hit = jnp.where(rk >= lower, rk, -1) == col + start
                lhs.append(jnp.where(hit, aff_ref[0, :, e:e + 1], 0.0).astype(BF16))
                rhs.append(ye_ref[0, e, pl.ds(start, win), :])
            d = jnp.dot(jnp.concatenate(lhs, axis=1), jnp.concatenate(rhs, axis=0), preferred_element_type=F32)
            total = d if total is None else total + d
        return total

    acc_ref[...] = contribution(0)
    need = jnp.int32(0)
    for e in range(E):
        need = jnp.maximum(need, hi[e] - (lo[e] // ROW_ALIGN) * ROW_ALIGN)

    @pl.when(need > win)
    def _():
        def extra(w, carry):
            @pl.when(need > w * win)
            def _():
                acc_ref[...] += contribution(w)
            return carry

        lax.fori_loop(1, _extra_windows(cap, tt, win) + 1, extra, 0)

    o_ref[0] = x_ref[0] + g2_ref[0] * acc_ref[...]


def moe_combine(ts_flat, rank_t, aff_t, ye, x, g2, *, tt):
    B, N, D = x.shape
    E, cap = ye.shape[1], ye.shape[2]
    tok = lambda w: pl.BlockSpec((1, tt, w), lambda b, k, ts: (b, k, 0))
    return pl.pallas_call(
        functools.partial(_moe_combine_kernel, win=min(LANES, cap), group=4),
        grid_spec=pltpu.PrefetchScalarGridSpec(
            num_scalar_prefetch=1,
            grid=(B, N // tt),
            in_specs=[tok(E), tok(E),
                      pl.BlockSpec((1, E, cap, D), lambda b, k, ts: (b, 0, 0, 0)),
                      tok(D),
                      pl.BlockSpec((1, 1, D), lambda b, k, ts: (b, 0, 0))],
            out_specs=tok(D),
            scratch_shapes=[pltpu.VMEM((tt, D), F32)]),
        out_shape=jax.ShapeDtypeStruct((B, N, D), F32),
        compiler_params=_cparams("parallel", "arbitrary"),
        name="moe_combine",
    )(ts_flat, rank_t, aff_t, ye, x, g2)


def moe_residual(streams, w_gate, w_up, w_down, li):
    routed = []
    for (x, h, logits), _ in streams:
        B, N, D = x.shape
        cap = EC_CAPACITY * N // N_EXPERTS
        tt = min(4 * LANES, N)
        aff, rank, ts = route(jnp.swapaxes(logits, 1, 2), cap, tt)
        ts_flat = ts[:, :, :N // tt + 1].reshape(-1)
        routed.append((moe_gather(ts_flat, rank, h, cap, tt=tt), ts_flat, rank, aff, tt))
    yes = expert_ffn([r[0] for r in routed], w_gate, w_up, w_down, li)
    return [moe_combine(ts_flat, jnp.swapaxes(rank, 1, 2), jnp.swapaxes(aff, 1, 2), ye, x, gate2, tt=tt)
            for ((x, _, _), gate2), ye, (_, ts_flat, rank, aff, tt) in zip(streams, yes, routed)]


def lambda_init(layer):
    return 0.8 - 0.6 * math.exp(-0.3 * layer)


def even_layer(x, xc, mods, cmods, norm1_g, w_in, w_out, lb, hgrn_g, qn_g, kn_g, lam_vec, subln_g,
               lam_init, ctx_out, tables):
    sh1, sc1, g1, router = mods
    csh1, csc1, cg1, crouter = cmods
    B = x.shape[0]
    half = w_in[0].shape[2] // 2
    proj = norm_mod_matmul(x, norm1_g, sh1, sc1, w_in, cols=(0, half))
    proj2 = norm_mod_matmul(x, norm1_g, sh1, sc1, w_in, cols=(1, half), out_dtype=BF16)
    projc = norm_mod_matmul(xc, norm1_g, csh1, csc1, w_in, cols=(0, half))
    projc2 = norm_mod_matmul(xc, norm1_g, csh1, csc1, w_in, cols=(1, half), out_dtype=BF16)
    s0 = jnp.zeros((B, HGRN_HEADS, 2, HGRN_HEAD_DIM, HGRN_HEAD_DIM), F32)
    oc_f, oc_b, s_ctx = hgrn_scan(projc, lb, s0)
    o_f, o_b, _ = hgrn_scan(proj, lb, s_ctx)
    lv = lam_vec.astype(F32)
    lam = jnp.exp(jnp.sum(lv[0] * lv[1])) - jnp.exp(jnp.sum(lv[2] * lv[3])) + lam_init
    bound = 1.01 * math.sqrt(DIFF_HEAD_DIM) * jnp.max(jnp.abs(qn_g)) * jnp.max(jnp.abs(kn_g))
    lam = jnp.stack([lam, bound, (bound <= SCORE_BOUND_MAX).astype(F32)])
    cos, sin = tables
    q, k, v = qkv_prep(proj2, qn_g, kn_g, cos, sin, rotary=True)
    qc, kc, vc = qkv_prep(projc2, qn_g, kn_g, cos[:xc.shape[1]], sin[:xc.shape[1]], rotary=False)
    k_all = jnp.concatenate([k, kc], axis=1)
    v_all = jnp.concatenate([v, vc], axis=2)
    y_d = diff_attention(q, k_all, v_all, lam, subln_g, 1.0 - lam_init)
    x_new = even_out(x, o_f, o_b, proj2, y_d, hgrn_g.reshape(-1), g1, w_out, router)
    if not ctx_out:
        return x_new, None
    yc_d = diff_attention(qc, kc, vc, lam, subln_g, 1.0 - lam_init)
    xc_new = even_out(xc, oc_f, oc_b, projc2, yc_d, hgrn_g.reshape(-1), cg1, w_out, crouter)
    return x_new, xc_new


def conv_layer(x, mods, norm1_g, w_in, conv_w, w_out):
    sh1, sc1, g1, router = mods
    proj = norm_mod_matmul(x, norm1_g, sh1, sc1, w_in, out_dtype=BF16)
    return conv_out(x, proj, conv_w, g1, w_out, router)


def kernel(x, c, ctx, c_ctx, mod_w, mod_b, norm1_g, norm2_g, even_w_in, even_w_out, hgrn_lb_logits, hgrn_norm_g,
           diff_qnorm_g, diff_knorm_g, diff_lambda, diff_subln_g, conv_w_in, conv_w, conv_w_out, router_w,
           exp_w_gate, exp_w_up, exp_w_down):
    depth = mod_w.shape[0]
    B, T, D = x.shape
    lb_soft = jax.nn.softmax(hgrn_lb_logits.astype(F32), axis=0)
    lower_bounds = jnp.cumsum(lb_soft, axis=0) - lb_soft[:1]
    last_ctx_layer = 2 * ((depth - 1) // 2)
    cond = jnp.concatenate([c, c_ctx[None, :], jnp.zeros((8 - (B + 1) % 8, D), F32)], axis=0)
    mods = modulation(jax.nn.silu(cond), mod_w, mod_b)
    tables = rope_tables(T)
    xc = ctx
    for l in range(depth):
        read_ctx = l <= last_ctx_layer
        ctx_out = l < last_ctx_layer
        sh1, sc1, g1, sh2, sc2, g2 = [m[:, None, :] for m in jnp.split(mods[l, :B], MOD_CHUNKS, axis=-1)]
        if read_ctx:
            csh1, csc1, cg1, csh2, csc2, cg2 = [
                jnp.broadcast_to(m[None, None, :], (B, 1, D)) for m in jnp.split(mods[l, B], MOD_CHUNKS, axis=-1)]
        mods_l = (sh1, sc1, g1, (norm2_g[l], sh2, sc2, router_w[l]))
        cmods_l = (csh1, csc1, cg1, (norm2_g[l], csh2, csc2, router_w[l])) if read_ctx else None
        if l % 2 == 0:
            e = l // 2
            s, sc = even_layer(x, xc, mods_l, cmods_l, norm1_g[l],
                               (even_w_in, e), (even_w_out, e), lower_bounds[e],
                               hgrn_norm_g[e], diff_qnorm_g[e], diff_knorm_g[e], diff_lambda[e],
                               diff_subln_g[e], lambda_init(l), ctx_out, tables)
        else:
            j = l // 2
            wi, wo = (conv_w_in, j), (conv_w_out, j)
            s = conv_layer(x, mods_l, norm1_g[l], wi, conv_w[j], wo)
            sc = conv_layer(xc, cmods_l, norm1_g[l], wi, conv_w[j], wo) if ctx_out else None
        outs = moe_residual([(s, g2)] + ([(sc, cg2)] if ctx_out else []), exp_w_gate, exp_w_up, exp_w_down, l)
        x = outs[0]
        if ctx_out:
            xc = outs[1]
    return x
```

```python
import functools
import math

import numpy as np
import jax
import jax.numpy as jnp
from jax import lax
from jax.experimental import pallas as pl
from jax.experimental.pallas import tpu as pltpu

F32 = jnp.float32
BF16 = jnp.bfloat16

EPS = 1e-6
GRID_W = 64
ROPE_THETA = 10000.0
HGRN_HEAD_DIM = 128
HGRN_HEADS = 4
HGRN_WIDTH = HGRN_HEADS * HGRN_HEAD_DIM
DIFF_HEAD_DIM = 64
DIFF_HEADS = 4
DIFF_WIDTH = DIFF_HEADS * 2 * DIFF_HEAD_DIM
N_EXPERTS = 16
EC_CAPACITY = 2
MOD_CHUNKS = 6
SCAN_CHUNK = 64
SCAN_LEVELS = (32, 16, 8, 4, 2, 1)
SCAN_HEADS_PER_STEP = 2
TILE_ROW_SPLIT = 1
FFN_SAMPLES_PER_STEP = 2
LANES = 128
ROW_ALIGN = 16
VMEM_LIMIT = 56 * 1024 * 1024
SCORE_BOUND_MAX = 40.0


def _cparams(*sem):
    return pltpu.CompilerParams(dimension_semantics=sem, vmem_limit_bytes=VMEM_LIMIT)


def _row_tile(t, want):
    return want if t % want == 0 else t


def _row_groups(tm):
    n = tm // TILE_ROW_SPLIT
    if tm % TILE_ROW_SPLIT or n % LANES:
        n = tm
    return [(r0, n) for r0 in range(0, tm, n)]


def _mod_kernel(s_ref, w_ref, b_ref, o_ref):
    s = s_ref[...]
    w = w_ref[0]
    s_hi = s.astype(BF16)
    s_lo = (s - s_hi.astype(F32)).astype(BF16)
    w_hi = w.astype(BF16)
    w_lo = (w - w_hi.astype(F32)).astype(BF16)
    d = lambda a, b: jnp.dot(a, b, preferred_element_type=F32)
    o_ref[0] = d(s_hi, w_hi) + d(s_hi, w_lo) + d(s_lo, w_hi) + b_ref[0]


def modulation(s, mod_w, mod_b, *, tn=2048):
    R, D = s.shape
    depth, _, N = mod_w.shape
    return pl.pallas_call(
        _mod_kernel,
        grid=(depth, N // tn),
        in_specs=[pl.BlockSpec((R, D), lambda l, j: (0, 0)),
                  pl.BlockSpec((1, D, tn), lambda l, j: (l, 0, j)),
                  pl.BlockSpec((1, 1, tn), lambda l, j: (l, 0, j))],
        out_specs=pl.BlockSpec((1, R, tn), lambda l, j: (l, 0, j)),
        out_shape=jax.ShapeDtypeStruct((depth, R, N), F32),
        compiler_params=_cparams("parallel", "parallel"),
        name="modulation",
    )(s, mod_w, mod_b.reshape(depth, 1, N))


def _nmm_kernel(x_ref, g_ref, sh_ref, sc_ref, w_ref, o_ref, w_scr):
    @pl.when((pl.program_id(1) == 0) & (pl.program_id(2) == 0))
    def _():
        w_scr[...] = w_ref[0].astype(BF16)

    for r0, rows in _row_groups(x_ref.shape[1]):
        x = x_ref[0, r0:r0 + rows, :]
        ms = jnp.mean(x * x, axis=-1, keepdims=True)
        h = (x * lax.rsqrt(ms + EPS)) * g_ref[...]
        h = h * (1.0 + sc_ref[0]) + sh_ref[0]
        o_ref[0, r0:r0 + rows, :] = jnp.dot(h.astype(BF16), w_scr[...],
                                            preferred_element_type=F32).astype(o_ref.dtype)


def norm_mod_matmul(x, g, shift, scale, w, *, cols=None, tm=512, out_dtype=F32):
    B, T, D = x.shape
    w, li = w
    first, N = cols if cols else (0, w.shape[2])
    tm = _row_tile(T, tm)
    tn = N
    return pl.pallas_call(
        _nmm_kernel,
        grid=(N // tn, B, T // tm),
        in_specs=[
            pl.BlockSpec((1, tm, D), lambda j, b, i: (b, i, 0)),
            pl.BlockSpec((1, D), lambda j, b, i: (0, 0)),
            pl.BlockSpec((1, 1, D), lambda j, b, i: (b, 0, 0)),
            pl.BlockSpec((1, 1, D), lambda j, b, i: (b, 0, 0)),
            pl.BlockSpec((1, D, tn), lambda j, b, i: (li, 0, first + j)),
        ],
        out_specs=pl.BlockSpec((1, tm, tn), lambda j, b, i: (b, i, j)),
        out_shape=jax.ShapeDtypeStruct((B, T, N), out_dtype),
        scratch_shapes=[pltpu.VMEM((D, tn), BF16)],
        compiler_params=_cparams("arbitrary", "arbitrary", "arbitrary"),
        name="norm_mod_matmul",
    )(x, g.reshape(1, D), shift, scale, w)


def _scan_constants():
    C = SCAN_CHUNK
    t = np.arange(C)[:, None]
    u = np.arange(C)[None, :]
    mats = [u <= t, u > t]
    masks = []
    for w in SCAN_LEVELS:
        m = (t // (2 * w)) * 2 * w + w - 1
        later = (t // w) % 2 == 1
        mats.append(np.where(later, (u > m) & (u <= t), (u > t) & (u <= m)))
        masks.append(later & ((u // w) % 2 == 0) & (u // (2 * w) == t // (2 * w)))
    masks = [t == u] + masks + [np.ones((C, C), bool)]
    a_f = np.stack(mats).astype(np.float32)
    m_f = np.stack(masks).astype(np.float32)
    a = np.stack([a_f, a_f[:, ::-1, ::-1]]).reshape(2, -1, C)
    m = np.stack([m_f, m_f[:, ::-1, ::-1]])
    m_pairs = np.concatenate([m[:, 0::2], m[:, 1::2]], axis=3)
    return np.concatenate([a, a], axis=2), m_pairs


def _scan_pair(q, z, v, lb, st, a2, mask_ref, d, later):
    C = SCAN_CHUNK
    W = HGRN_HEAD_DIM
    nl = len(SCAN_LEVELS)
    nt = lambda x, y: lax.dot_general(x, y, (((1,), (1,)), ((), ())), preferred_element_type=F32)
    nn = lambda x, y: jnp.dot(x, y, preferred_element_type=F32)
    e_abs = jnp.exp(-jnp.abs(z))
    r = 1.0 / (1.0 + e_abs)
    er = e_abs * r
    pos = z >= 0.0
    g2 = jnp.log2(lb + (1.0 - lb) * jnp.where(pos, r, er))
    k = (1.0 - lb) * jnp.where(pos, er, r)
    hi = g2.astype(BF16)
    lo = (g2 - hi.astype(F32)).astype(BF16)
    gs = jnp.concatenate([jnp.concatenate([hi[:C], hi[C:]], axis=1),
                          jnp.concatenate([lo[:C], lo[C:]], axis=1)], axis=0)
    x = nn(a2, gs)
    yield
    x = jnp.exp2(x)
    first, second = (1, 0) if d else (0, 1)
    vb = v.astype(BF16)
    sides, qt, kt, dec = [], [], [], []
    for c in (0, 1):
        qc, kc, xs = q[c * C:(c + 1) * C], k[c * C:(c + 1) * C], x[:, c * W:(c + 1) * W]
        ops = [(qc.astype(BF16), kc.astype(BF16))]
        for i in range(nl):
            qk = (jnp.where(later[i], qc, kc) * xs[(2 + i) * C:(3 + i) * C]).astype(BF16)
            ops.append((qk, qk))
        sides.append(ops)
        qt.append(qc * xs[0:C])
        kt.append(kc * xs[C:2 * C])
        dec.append(xs[0:1, :] if d else xs[C - 1:C, :])
    zero = jnp.zeros((C, W), BF16)
    sides[first].append((zero, zero))
    sides[second].append((qt[second].astype(BF16), kt[first].astype(BF16)))
    prods = []
    for ops in sides:
        pair = []
        for j in range(0, nl + 2, 2):
            (qa, ka), (qb, kb) = ops[j], ops[j + 1]
            rhs = jnp.concatenate([jnp.concatenate([ka, zero], axis=1), jnp.concatenate([zero, kb], axis=1)], axis=0)
            pair.append(nt(jnp.concatenate([qa, qb], axis=1), rhs))
        prods.append(pair)
        yield
    qt[second] = qt[second] * dec[first]
    kt[first] = kt[first] * dec[second]
    o_st = nt(jnp.concatenate([a.astype(BF16) for a in qt], axis=0), st.astype(BF16))
    upd = lax.dot_general(vb, jnp.concatenate(kt, axis=0).astype(BF16), (((0,), (0,)), ((), ())),
                          preferred_element_type=F32)
    yield
    o = [None, None]
    n_main = nl // 2
    for c, pair in enumerate(prods):
        main = mask_ref[d, 0] * pair[0]
        for j in range(1, n_main):
            main = main + mask_ref[d, j] * pair[j]
        last = mask_ref[d, n_main] * pair[n_main]
        vc, vf = vb[c * C:(c + 1) * C], vb[first * C:(first + 1) * C]
        o[c] = o_st[c * C:(c + 1) * C] + nn(jnp.concatenate([main, last], axis=1).astype(BF16),
                                            jnp.concatenate([vc, vc, vc, vf], axis=0))
    yield
    return jnp.concatenate(o, axis=0), st * (dec[0] * dec[1]) + upd


def _run_interleaved(gens):
    results = [None] * len(gens)
    live = list(range(len(gens)))
    while live:
        for i in list(live):
            try:
                next(gens[i])
            except StopIteration as done:
                results[i] = done.value
                live.remove(i)
    return results


def _scan_kernel(qf_ref, zf_ref, vf_ref, qb_ref, zb_ref, vb_ref, lbf_ref, lbb_ref, s0_ref, a_ref, mask_ref,
                 of_ref, ob_ref, sT_ref, st_scr, *, n_chunks):
    c = pl.program_id(2)
    C = SCAN_CHUNK

    @pl.when(c == 0)
    def _():
        st_scr[...] = s0_ref[0]

    W = HGRN_HEAD_DIM
    row = lax.broadcasted_iota(jnp.int32, (C, W), 0)
    later = [[(row // w) % 2 == 1 for w in SCAN_LEVELS], [((C - 1 - row) // w) % 2 == 1 for w in SCAN_LEVELS]]
    in_refs = [(qf_ref, zf_ref, vf_ref), (qb_ref, zb_ref, vb_ref)]
    lb_refs = [lbf_ref, lbb_ref]
    o_refs = [of_ref, ob_ref]
    n_pairs = n_chunks // 2
    heads = range(qf_ref.shape[2] // W)

    def body(i, carry):
        r0 = [pl.multiple_of(i * 2 * C, 2 * C), pl.multiple_of((n_pairs - 1 - i) * 2 * C, 2 * C)]
        chains = [(d, h) for h in heads for d in (0, 1)]
        ins = [[ref[0, pl.ds(r0[d], 2 * C), h * W:(h + 1) * W] for ref in in_refs[d]] for d, h in chains]
        sts = [st_scr[h, d] for d, h in chains]
        outs = _run_interleaved([
            _scan_pair(*x, lb_refs[d][:, h * W:(h + 1) * W], st, a_ref[d], mask_ref, d, later[d])
            for (d, h), x, st in zip(chains, ins, sts)])
        for (d, h), (o, st) in zip(chains, outs):
            o_refs[d][0, pl.ds(r0[d], 2 * C), h * W:(h + 1) * W] = o.astype(o_refs[d].dtype)
            st_scr[h, d] = st
        return carry

    lax.fori_loop(0, n_pairs, body, 0)

    @pl.when(c == pl.num_programs(2) - 1)
    def _():
        sT_ref[0] = st_scr[...]


def hgrn_scan(proj, lb, s0, *, tb=512):
    B, T, _ = proj.shape
    tb = _row_tile(T, tb)
    nc = T // tb
    H = HGRN_HEADS
    hd = HGRN_HEAD_DIM
    hps = SCAN_HEADS_PER_STEP
    G = H // hps
    wd = hps * hd
    fwd = lambda grp: pl.BlockSpec((1, tb, wd), lambda b, h, c: (b, c, grp * G + h))
    bwd = lambda grp: pl.BlockSpec((1, tb, wd), lambda b, h, c: (b, nc - 1 - c, grp * G + h))
    kern = functools.partial(_scan_kernel, n_chunks=tb // SCAN_CHUNK)
    a2, masks = _scan_constants()
    return pl.pallas_call(
        kern,
        grid=(B, G, nc),
        in_specs=[fwd(0), fwd(1), fwd(3), bwd(0), bwd(2), bwd(3),
                  pl.BlockSpec((1, wd), lambda b, h, c: (0, h)),
                  pl.BlockSpec((1, wd), lambda b, h, c: (0, h)),
                  pl.BlockSpec((1, hps, 2, hd, hd), lambda b, h, c: (b, h, 0, 0, 0)),
                  pl.BlockSpec(a2.shape, lambda b, h, c: (0, 0, 0)),
                  pl.BlockSpec(masks.shape, lambda b, h, c: (0, 0, 0, 0))],
        out_specs=[pl.BlockSpec((1, tb, wd), lambda b, h, c: (b, c, h)),
                   pl.BlockSpec((1, tb, wd), lambda b, h, c: (b, nc - 1 - c, h)),
                   pl.BlockSpec((1, hps, 2, hd, hd), lambda b, h, c: (b, h, 0, 0, 0))],
        out_shape=[jax.ShapeDtypeStruct((B, T, HGRN_WIDTH), BF16),
                   jax.ShapeDtypeStruct((B, T, HGRN_WIDTH), BF16),
                   jax.ShapeDtypeStruct((B, H, 2, hd, hd), F32)],
        scratch_shapes=[pltpu.VMEM((hps, 2, hd, hd), F32)],
        compiler_params=_cparams("parallel", "parallel", "arbitrary"),
        name="hgrn_scan",
    )(proj, proj, proj, proj, proj, proj, lb[0:1], lb[1:2], s0, jnp.asarray(a2, BF16), jnp.asarray(masks, F32))


def _group_mean_sq(x, gmat):
    sq = x * x
    hi = sq.astype(BF16)
    lo = (sq - hi.astype(F32)).astype(BF16)
    return (jnp.dot(hi, gmat, preferred_element_type=F32) + jnp.dot(lo, gmat, preferred_element_type=F32))


def _qkv_prep_kernel(q_ref, k_ref, v_ref, qg_ref, kg_ref, cos_ref, sin_ref, qo_ref, ko_ref, vo_ref, *, rotary):
    W = LANES
    r_i = lax.broadcasted_iota(jnp.int32, (W, W), 0) // DIFF_HEAD_DIM
    c_i = lax.broadcasted_iota(jnp.int32, (W, W), 1) // DIFF_HEAD_DIM
    gmat = jnp.where(r_i == c_i, 1.0 / DIFF_HEAD_DIM, 0.0).astype(BF16)
    lane = lax.broadcasted_iota(jnp.int32, (1, W), 1)
    first = (lane % 32) < 16

    def prep(x, g, scale):
        y = (x * lax.rsqrt(_group_mean_sq(x, gmat) + EPS)) * g
        if rotary:
            partner = jnp.where(first, pltpu.roll(y, W - 16, axis=1), pltpu.roll(y, 16, axis=1))
            y = y * cos_ref[...] + partner * sin_ref[...]
        if scale != 1.0:
            y = y * scale
        return y

    for h in range(DIFF_HEADS):
        sl = slice(h * W, (h + 1) * W)
        qo_ref[0, sl, :] = prep(q_ref[0, :, sl].astype(F32), qg_ref[...], DIFF_HEAD_DIM ** -0.5).T.astype(BF16)
        ko_ref[0, :, sl] = prep(k_ref[0, :, sl].astype(F32), kg_ref[...], 1.0).astype(BF16)
        vo_ref[0, sl, :] = v_ref[0, :, sl].astype(F32).T.astype(BF16)


def qkv_prep(proj, qg, kg, cos, sin, *, rotary, tm=512):
    B, T, _ = proj.shape
    tm = _row_tile(T, tm)
    Wd = DIFF_WIDTH
    col = lambda j: pl.BlockSpec((1, tm, Wd), lambda b, i: (b, i, j))
    vec = pl.BlockSpec((1, LANES), lambda b, i: (0, 0))
    tab = pl.BlockSpec((tm, LANES), lambda b, i: (i, 0))
    rows = pl.BlockSpec((1, tm, Wd), lambda b, i: (b, i, 0))
    cols = pl.BlockSpec((1, Wd, tm), lambda b, i: (b, 0, i))
    return pl.pallas_call(
        functools.partial(_qkv_prep_kernel, rotary=rotary),
        grid=(B, T // tm),
        in_specs=[col(1), col(2), col(3), vec, vec, tab, tab],
        out_specs=[cols, rows, cols],
        out_shape=[jax.ShapeDtypeStruct((B, Wd, T), BF16), jax.ShapeDtypeStruct((B, T, Wd), BF16),
                   jax.ShapeDtypeStruct((B, Wd, T), BF16)],
        compiler_params=_cparams("parallel", "parallel"),
        name="qkv_prep",
    )(proj, proj, proj, jnp.tile(qg, 2).reshape(1, LANES), jnp.tile(kg, 2).reshape(1, LANES), cos, sin)


def rope_tables(T):
    n = DIFF_HEAD_DIM // 2
    inv = 1.0 / (ROPE_THETA ** (jnp.arange(0, n, 2, dtype=F32) / n))
    t = jnp.arange(T)
    ang_r = (t // GRID_W).astype(F32)[:, None] * inv[None, :]
    ang_c = (t % GRID_W).astype(F32)[:, None] * inv[None, :]
    cos = jnp.concatenate([jnp.cos(ang_r)] * 2 + [jnp.cos(ang_c)] * 2, axis=-1)
    sin = jnp.concatenate([-jnp.sin(ang_r), jnp.sin(ang_r), -jnp.sin(ang_c), jnp.sin(ang_c)], axis=-1)
    return jnp.tile(cos, (1, 2)), jnp.tile(sin, (1, 2))


def _attn_tile(qt, k_ref, v_ref, lam, key_chunk, bound=None):
    tq = qt.shape[1]
    S = k_ref.shape[1]
    row = lax.broadcasted_iota(jnp.int32, (LANES, 1), 0)
    zero = jnp.zeros_like(qt)
    qq = jnp.concatenate([jnp.where(row < DIFF_HEAD_DIM, qt, zero),
                          jnp.where(row >= DIFF_HEAD_DIM, qt, zero)], axis=1)
    m = jnp.full((1, 2 * tq), -jnp.inf, F32)
    acc = [jnp.zeros((LANES + 16, tq), F32), jnp.zeros((LANES + 16, tq), F32)]
    ones = jnp.ones((16, key_chunk), BF16)
    chunks = [(c0, min(c0 + key_chunk, S)) for c0 in range(0, S, key_chunk)]
    scores = lambda c: jnp.dot(k_ref[0, c[0]:c[1], :], qq, preferred_element_type=F32)
    s_next = scores(chunks[0])
    yield
    for n, (c0, c1) in enumerate(chunks):
        s = s_next
        if n + 1 < len(chunks):
            s_next = scores(chunks[n + 1])
        vt1 = jnp.concatenate([v_ref[0, :, c0:c1], ones[:, :c1 - c0]], axis=0)
        if bound is None:
            m_new = jnp.maximum(m, jnp.max(s, axis=0, keepdims=True))
            alpha = jnp.exp(m - m_new)
            pb = jnp.exp((s - m_new).astype(BF16))
            m = m_new
        else:
            alpha = None
            pb = jnp.exp(s - bound).astype(BF16)
        for i in range(2):
            pv = jnp.dot(vt1, pb[:, i * tq:(i + 1) * tq], preferred_element_type=F32)
            acc[i] = acc[i] + pv if alpha is None else acc[i] * alpha[:, i * tq:(i + 1) * tq] + pv
        yield
    inv = [1.0 / a[LANES:LANES + 1] for a in acc]
    return acc[0][:LANES] * inv[0] - acc[1][:LANES] * (lam * inv[1])


def _diff_attn_kernel(par_ref, q_ref, k_ref, v_ref, g_ref, o_ref, *, out_scale, key_chunk, tq):
    n_tiles = q_ref.shape[2] // tq

    def run(bound):
        outs = _run_interleaved([
            _attn_tile(q_ref[0, :, i * tq:(i + 1) * tq], k_ref, v_ref, par_ref[0], key_chunk, bound)
            for i in range(n_tiles)])
        for i, o in enumerate(outs):
            ms = jnp.mean(o * o, axis=0, keepdims=True)
            o_ref[0, i * tq:(i + 1) * tq, :] = ((o * lax.rsqrt(ms + EPS)) * g_ref[...] * out_scale).T.astype(BF16)

    @pl.when(par_ref[2] > 0.5)
    def _():
        run(par_ref[1])

    @pl.when(par_ref[2] <= 0.5)
    def _():
        run(None)


def diff_attention(qt, k, vt, params, subln_g, out_scale, *, tq=256, tiles_per_step=2, key_chunk=512):
    B, Wd, T = qt.shape
    S = k.shape[1]
    tq = _row_tile(T, tq)
    ts = _row_tile(T, tq * tiles_per_step)
    return pl.pallas_call(
        functools.partial(_diff_attn_kernel, out_scale=out_scale, key_chunk=key_chunk, tq=tq),
        grid=(B, DIFF_HEADS, T // ts),
        in_specs=[pl.BlockSpec(memory_space=pltpu.SMEM),
                  pl.BlockSpec((1, LANES, ts), lambda b, h, i: (b, h, i)),
                  pl.BlockSpec((1, S, LANES), lambda b, h, i: (b, 0, h)),
                  pl.BlockSpec((1, LANES, S), lambda b, h, i: (b, h, 0)),
                  pl.BlockSpec((LANES, 1), lambda b, h, i: (0, 0))],
        out_specs=pl.BlockSpec((1, ts, LANES), lambda b, h, i: (b, i, h)),
        out_shape=jax.ShapeDtypeStruct((B, T, Wd), BF16),
        compiler_params=_cparams("parallel", "parallel", "arbitrary"),
        name="diff_attention",
    )(params, qt, k, vt, subln_g.reshape(LANES, 1))


def _even_out_kernel(x_ref, of_ref, ob_ref, gate_ref, yd_ref, hg_ref, g1_ref, w_ref, *rest):
    router_refs, o_ref, moe_refs = rest[:-3], rest[-3], rest[-2:]
    tm = x_ref.shape[1]
    w = w_ref[0].astype(BF16)
    for r0, n in _row_groups(tm):
        rows = slice(r0, r0 + n)
        acc = jnp.dot(yd_ref[0, rows, :], w[HGRN_WIDTH:], preferred_element_type=F32)
        for h in range(HGRN_HEADS):
            sl = slice(h * HGRN_HEAD_DIM, (h + 1) * HGRN_HEAD_DIM)
            o = of_ref[0, rows, sl].astype(F32) + ob_ref[0, rows, sl].astype(F32)
            ms = jnp.mean(o * o, axis=-1, keepdims=True)
            gate = gate_ref[0, rows, sl].astype(F32)
            yh = (o * lax.rsqrt(ms + EPS)) * hg_ref[:, sl] * (gate * jax.nn.sigmoid(gate))
            acc = acc + jnp.dot(yh.astype(BF16), w[sl], preferred_element_type=F32)
        x_new = x_ref[0, rows, :] + g1_ref[0] * acc
        o_ref[0, rows, :] = x_new
        _router_outputs(x_new, r0, *router_refs, *moe_refs)


def even_out(x, o_f, o_b, proj, y_d, hgrn_g, g1, w_out, router, *, tm=512):
    B, T, D = x.shape
    w_out, li = w_out
    tm = _row_tile(T, tm)
    Wd = HGRN_WIDTH
    row = lambda w, j: pl.BlockSpec((1, tm, w), lambda b, i: (b, i, j))
    r_args, r_in, r_out, r_shapes = _router_specs(B, T, D, tm, router)
    return pl.pallas_call(
        _even_out_kernel,
        grid=(B, T // tm),
        in_specs=[row(D, 0), row(Wd, 0), row(Wd, 0), row(Wd, 0), row(Wd, 0),
                  pl.BlockSpec((1, Wd), lambda b, i: (0, 0)),
                  pl.BlockSpec((1, 1, D), lambda b, i: (b, 0, 0)),
                  pl.BlockSpec((1,) + w_out.shape[1:], lambda b, i: (li, 0, 0))] + r_in,
        out_specs=[row(D, 0)] + r_out,
        out_shape=[jax.ShapeDtypeStruct((B, T, D), F32)] + r_shapes,
        compiler_params=_cparams("parallel", "parallel"),
        name="even_out",
    )(x, o_f, o_b, proj, y_d, hgrn_g.reshape(1, Wd), g1, w_out, *r_args)


def _conv_out_kernel(x_ref, bg_ref, cg_ref, v_ref, cp_ref, vp_ref, cn_ref, vn_ref, cw_ref, g1_ref, w_ref, *rest):
    router_refs, o_ref, moe_refs = rest[:-3], rest[-3], rest[-2:]
    i = pl.program_id(1)
    n = pl.num_programs(1)
    f32 = lambda ref, *idx: ref[idx].astype(F32)
    u = f32(cg_ref, 0) * f32(v_ref, 0)
    tm = u.shape[0]
    last = ROW_ALIGN - 1
    u_prev_row = jnp.where(i > 0, f32(cp_ref, 0, slice(last, last + 1)) * f32(vp_ref, 0, slice(last, last + 1)), 0.0)
    u_next_row = jnp.where(i < n - 1, f32(cn_ref, 0, slice(0, 1)) * f32(vn_ref, 0, slice(0, 1)), 0.0)
    ridx = lax.broadcasted_iota(jnp.int32, (tm, 1), 0)
    u_prev = jnp.where(ridx == 0, u_prev_row, pltpu.roll(u, 1, axis=0))
    u_next = jnp.where(ridx == tm - 1, u_next_row, pltpu.roll(u, tm - 1, axis=0))
    y = cw_ref[0:1, :] * u_prev + cw_ref[1:2, :] * u + cw_ref[2:3, :] * u_next
    w = w_ref[0].astype(BF16)
    for r0, rows in _row_groups(tm):
        sl = slice(r0, r0 + rows)
        acc = jnp.dot((f32(bg_ref, 0, sl) * y[sl]).astype(BF16), w, preferred_element_type=F32)
        x_new = x_ref[0, sl, :] + g1_ref[0] * acc
        o_ref[0, sl, :] = x_new
        _router_outputs(x_new, r0, *router_refs, *moe_refs)


def conv_out(x, proj, conv_w, g1, w_out, router, *, tm=512):
    B, T, D = x.shape
    w_out, li = w_out
    tm = _row_tile(T, tm)
    rt = tm // ROW_ALIGN
    last_blk = T // ROW_ALIGN - 1
    row = lambda j: pl.BlockSpec((1, tm, D), lambda b, i: (b, i, j))
    prev = lambda j: pl.BlockSpec((1, ROW_ALIGN, D), lambda b, i: (b, jnp.maximum(i * rt - 1, 0), j))
    nxt = lambda j: pl.BlockSpec((1, ROW_ALIGN, D), lambda b, i: (b, jnp.minimum((i + 1) * rt, last_blk), j))
    r_args, r_in, r_out, r_shapes = _router_specs(B, T, D, tm, router)
    return pl.pallas_call(
        _conv_out_kernel,
        grid=(B, T // tm),
        in_specs=[row(0), row(0), row(1), row(2), prev(1), prev(2), nxt(1), nxt(2),
                  pl.BlockSpec((8, D), lambda b, i: (0, 0)),
                  pl.BlockSpec((1, 1, D), lambda b, i: (b, 0, 0)),
                  pl.BlockSpec((1,) + w_out.shape[1:], lambda b, i: (li, 0, 0))] + r_in,
        out_specs=[row(0)] + r_out,
        out_shape=[jax.ShapeDtypeStruct((B, T, D), F32)] + r_shapes,
        compiler_params=_cparams("parallel", "parallel"),
        name="conv_out",
    )(x, proj, proj, proj, proj, proj, proj, proj,
      jnp.concatenate([conv_w, jnp.zeros((8 - conv_w.shape[0], D), conv_w.dtype)], axis=0), g1, w_out, *r_args)


def _router_outputs(x, r0, g_ref, sh_ref, sc_ref, rwt_ref, h_ref, lg_ref):
    n = x.shape[0]
    ms = jnp.mean(x * x, axis=-1, keepdims=True)
    h = (x * lax.rsqrt(ms + EPS)) * g_ref[...]
    h = h * (1.0 + sc_ref[0]) + sh_ref[0]
    h_hi = h.astype(BF16)
    h_ref[0, r0:r0 + n, :] = h_hi
    h_lo = (h - h_hi.astype(F32)).astype(BF16)
    rwt = rwt_ref[...]
    w_hi = rwt.astype(BF16)
    w_lo = (rwt - w_hi.astype(F32)).astype(BF16)
    d = lambda a, b: lax.dot_general(a, b, (((1,), (1,)), ((), ())), preferred_element_type=F32)
    lg_ref[0, :, r0:r0 + n] = d(w_hi, h_hi) + d(w_lo, h_hi) + d(w_hi, h_lo)


def _router_specs(B, T, D, tm, router):
    g, shift, scale, router_w = router
    E = router_w.shape[1]
    vec = pl.BlockSpec((1, 1, D), lambda b, i: (b, 0, 0))
    return ((g.reshape(1, D), shift, scale, router_w.T),
            [pl.BlockSpec((1, D), lambda b, i: (0, 0)), vec, vec, pl.BlockSpec((E, D), lambda b, i: (0, 0))],
            [pl.BlockSpec((1, tm, D), lambda b, i: (b, i, 0)), pl.BlockSpec((1, E, tm), lambda b, i: (b, 0, i))],
            [jax.ShapeDtypeStruct((B, T, D), BF16), jax.ShapeDtypeStruct((B, E, T), F32)])


def _lane_prefix(flags, tri_tot):
    E, N = flags.shape
    carries = [jnp.zeros((E, LANES), F32)]
    out = []
    for j in range(N // LANES):
        r = jnp.dot(flags[:, j * LANES:(j + 1) * LANES].astype(BF16), tri_tot, preferred_element_type=F32)
        out.append(r[:, :LANES] + carries[-1])
        carries.append(carries[-1] + r[:, LANES:])
    return jnp.concatenate(out, axis=1), carries


def _route_kernel(lg_ref, aff_ref, rank_ref, ts_ref, *, cap, tt):
    lg = lg_ref[0]
    E, N = lg.shape
    p = jnp.exp(lg - jnp.max(lg, axis=0, keepdims=True))
    aff = p / jnp.sum(p, axis=0, keepdims=True)
    aff_ref[0] = aff
    count = lambda m: jnp.sum(jnp.where(m, 1.0, 0.0), axis=1, keepdims=True)
    as_float = lambda i: pltpu.bitcast(i, F32)

    def refine_bits(i, thr):
        cand = thr | jnp.left_shift(jnp.int32(1), 30 - i)
        return jnp.where(count(aff >= as_float(cand)) >= cap, cand, thr)

    thr = lax.fori_loop(0, 31, refine_bits, jnp.zeros((E, 1), jnp.int32))

    def refine_mid(i, lo_hi):
        lo, hi = lo_hi
        mid = 0.5 * (lo + hi)
        up = count(aff >= mid) >= cap
        return jnp.where(up, mid, lo), jnp.where(up, hi, mid)

    lo, hi = lax.fori_loop(0, 24, refine_mid, (as_float(thr), as_float(jnp.maximum(thr + 1, 0x00800000))))
    gt = aff >= hi
    eq = (aff >= lo) & (aff < hi)
    r_i = lax.broadcasted_iota(jnp.int32, (LANES, 2 * LANES), 0)
    c_i = lax.broadcasted_iota(jnp.int32, (LANES, 2 * LANES), 1)
    tri_tot = jnp.where((r_i < c_i) | (c_i >= LANES), 1.0, 0.0).astype(BF16)
    eq_rank, _ = _lane_prefix(jnp.where(eq, 1.0, 0.0), tri_tot)
    sel = gt | (eq & (eq_rank < cap - count(gt)))
    rank, before = _lane_prefix(jnp.where(sel, 1.0, 0.0), tri_tot)
    rank_ref[0] = jnp.where(sel, rank, -1.0).astype(jnp.int32)
    lane = lax.broadcasted_iota(jnp.int32, (E, LANES), 1)
    ts = jnp.zeros((E, LANES), F32)
    for k in range(N // tt + 1):
        ts = jnp.where(lane == k, before[k * tt // LANES], ts)
    ts_ref[0] = ts.astype(jnp.int32)


def route(logits_t, cap, tt):
    B, E, N = logits_t.shape
    blk = pl.BlockSpec((1, E, N), lambda b: (b, 0, 0))
    return pl.pallas_call(
        functools.partial(_route_kernel, cap=cap, tt=tt),
        grid=(B,),
        in_specs=[blk],
        out_specs=[blk, blk, pl.BlockSpec((1, E, LANES), lambda b: (b, 0, 0))],
        out_shape=[jax.ShapeDtypeStruct((B, E, N), F32), jax.ShapeDtypeStruct((B, E, N), jnp.int32),
                   jax.ShapeDtypeStruct((B, E, LANES), jnp.int32)],
        compiler_params=_cparams("parallel"),
        name="route",
    )(logits_t)


def _window(lo, w, win, cap):
    lower = (lo // ROW_ALIGN) * ROW_ALIGN + w * win
    return pl.multiple_of(jnp.minimum(lower, cap - win), ROW_ALIGN), lower


def _extra_windows(cap, tt, win):
    return -(-(min(cap, tt) + ROW_ALIGN - 1) // win) - 1


def _moe_gather_kernel(ts_ref, rank_ref, h_ref, xe_ref, acc_ref, *, win, tt):
    b, e = pl.program_id(0), pl.program_id(1)
    N = h_ref.shape[1]
    cap = xe_ref.shape[2]
    nt = N // tt
    base = (b * pl.num_programs(1) + e) * (nt + 1)
    acc_ref[...] = jnp.zeros_like(acc_ref)
    row = lax.broadcasted_iota(jnp.int32, (win, tt), 0)

    def place(k, w):
        start, lower = _window(ts_ref[base + k], w, win, cap)
        c0 = k * tt if isinstance(k, int) else pl.multiple_of(k * tt, tt)
        rk = rank_ref[0, 0, :, pl.ds(c0, tt)]
        onehot = jnp.where(jnp.where(rk >= lower, rk, -1) == row + start, 1.0, 0.0).astype(BF16)
        acc_ref[pl.ds(start, win), :] += jnp.dot(onehot, h_ref[0, pl.ds(c0, tt), :], preferred_element_type=F32)

    span = lambda k: ts_ref[base + k + 1] - (ts_ref[base + k] // ROW_ALIGN) * ROW_ALIGN
    need = jnp.int32(0)
    for k in range(nt):
        place(k, 0)
        need = jnp.maximum(need, span(k))

    n_extra = _extra_windows(cap, tt, win)

    @pl.when(need > win)
    def _():
        def extra(i, carry):
            k, w = i // n_extra, i % n_extra + 1

            @pl.when(span(k) > w * win)
            def _():
                place(k, w)
            return carry

        lax.fori_loop(0, nt * n_extra, extra, 0)

    xe_ref[0, 0] = acc_ref[...].astype(BF16)


def moe_gather(ts_flat, rank, h, cap, *, tt):
    B, E, N = rank.shape
    D = h.shape[2]
    return pl.pallas_call(
        functools.partial(_moe_gather_kernel, win=min(LANES, cap), tt=tt),
        grid_spec=pltpu.PrefetchScalarGridSpec(
            num_scalar_prefetch=1,
            grid=(B, E),
            in_specs=[pl.BlockSpec((1, 1, 1, N), lambda b, e, ts: (b, e, 0, 0)),
                      pl.BlockSpec((1, N, D), lambda b, e, ts: (b, 0, 0))],
            out_specs=pl.BlockSpec((1, 1, cap, D), lambda b, e, ts: (b, e, 0, 0)),
            scratch_shapes=[pltpu.VMEM((cap, D), F32)]),
        out_shape=jax.ShapeDtypeStruct((B, E, cap, D), BF16),
        compiler_params=_cparams("parallel", "arbitrary"),
        name="moe_gather",
    )(ts_flat, rank.reshape(B, E, 1, N), h)


def _expert_kernel(*refs, n_streams):
    xe_refs = refs[:n_streams]
    wg_ref, wu_ref, wd_ref = refs[n_streams:n_streams + 3]
    o_refs = refs[n_streams + 3:2 * n_streams + 3]
    wg_scr, wu_scr, wd_scr = refs[2 * n_streams + 3:]

    @pl.when(pl.program_id(1) == 0)
    def _():
        wg_scr[...] = wg_ref[0, 0].astype(BF16)
        wu_scr[...] = wu_ref[0, 0].astype(BF16)
        wd_scr[...] = wd_ref[0, 0].astype(BF16)

    def swiglu(x):
        a = jnp.dot(x, wg_scr[...], preferred_element_type=F32)
        u = jnp.dot(x, wu_scr[...], preferred_element_type=F32)
        yield
        hid = (a * jax.nn.sigmoid(a)) * u
        return jnp.dot(hid.astype(BF16), wd_scr[...], preferred_element_type=F32).astype(BF16)

    nb = xe_refs[0].shape[0]
    ys = _run_interleaved([swiglu(jnp.concatenate([r[s, 0] for r in xe_refs], axis=0)) for s in range(nb)])
    for s, y in enumerate(ys):
        r0 = 0
        for o_ref in o_refs:
            o_ref[s, 0] = y[r0:r0 + o_ref.shape[2]]
            r0 += o_ref.shape[2]


def expert_ffn(xes, w_gate, w_up, w_down, li):
    B, E, _, D = xes[0].shape
    FF = w_gate.shape[3]
    nb = FFN_SAMPLES_PER_STEP if B % FFN_SAMPLES_PER_STEP == 0 else 1
    rows = [pl.BlockSpec((nb, 1, xe.shape[2], D), lambda e, b: (b, e, 0, 0)) for xe in xes]
    return pl.pallas_call(
        functools.partial(_expert_kernel, n_streams=len(xes)),
        grid=(E, B // nb),
        in_specs=rows + [pl.BlockSpec((1, 1, D, FF), lambda e, b: (li, e, 0, 0)),
                         pl.BlockSpec((1, 1, D, FF), lambda e, b: (li, e, 0, 0)),
                         pl.BlockSpec((1, 1, FF, D), lambda e, b: (li, e, 0, 0))],
        out_specs=rows,
        out_shape=[jax.ShapeDtypeStruct(xe.shape, BF16) for xe in xes],
        scratch_shapes=[pltpu.VMEM((D, FF), BF16), pltpu.VMEM((D, FF), BF16), pltpu.VMEM((FF, D), BF16)],
        compiler_params=_cparams("arbitrary", "arbitrary"),
        name="expert_ffn",
    )(*xes, w_gate, w_up, w_down)


def _moe_combine_kernel(ts_ref, rank_ref, aff_ref, ye_ref, x_ref, g2_ref, o_ref, acc_ref, *, win, group):
    b, k = pl.program_id(0), pl.program_id(1)
    E, cap = ye_ref.shape[1], ye_ref.shape[2]
    tt = x_ref.shape[1]
    nt = pl.num_programs(1)
    col = lax.broadcasted_iota(jnp.int32, (tt, win), 1)
    lo = [ts_ref[(b * E + e) * (nt + 1) + k] for e in range(E)]
    hi = [ts_ref[(b * E + e) * (nt + 1) + k + 1] for e in range(E)]

    def contribution(w):
        total = None
        for g0 in range(0, E, group):
            lhs, rhs = [], []
            for e in range(g0, g0 + group):
                start, lower = _window(lo[e], w, win, cap)
                rk = rank_ref[0, :, e:e + 1]
                hit = jnp.where(rk >= lower, rk, -1) == col + start
                lhs.append(jnp.where(hit, aff_ref[0, :, e:e + 1], 0.0).astype(BF16))
                rhs.append(ye_ref[0, e, pl.ds(start, win), :])
            d = jnp.dot(jnp.concatenate(lhs, axis=1), jnp.concatenate(rhs, axis=0), preferred_element_type=F32)
            total = d if total is None else total + d
        return total

    acc_ref[...] = contribution(0)
    need = jnp.int32(0)
    for e in range(E):
        need = jnp.maximum(need, hi[e] - (lo[e] // ROW_ALIGN) * ROW_ALIGN)

    @pl.when(need > win)
    def _():
        def extra(w, carry):
            @pl.when(need > w * win)
            def _():
                acc_ref[...] += contribution(w)
            return carry

        lax.fori_loop(1, _extra_windows(cap, tt, win) + 1, extra, 0)

    o_ref[0] = x_ref[0] + g2_ref[0] * acc_ref[...]


def moe_combine(ts_flat, rank_t, aff_t, ye, x, g2, *, tt):
    B, N, D = x.shape
    E, cap = ye.shape[1], ye.shape[2]
    tok = lambda w: pl.BlockSpec((1, tt, w), lambda b, k, ts: (b, k, 0))
    return pl.pallas_call(
        functools.partial(_moe_combine_kernel, win=min(LANES, cap), group=4),
        grid_spec=pltpu.PrefetchScalarGridSpec(
            num_scalar_prefetch=1,
            grid=(B, N // tt),
            in_specs=[tok(E), tok(E),
                      pl.BlockSpec((1, E, cap, D), lambda b, k, ts: (b, 0, 0, 0)),
                      tok(D),
                      pl.BlockSpec((1, 1, D), lambda b, k, ts: (b, 0, 0))],
            out_specs=tok(D),
            scratch_shapes=[pltpu.VMEM((tt, D), F32)]),
        out_shape=jax.ShapeDtypeStruct((B, N, D), F32),
        compiler_params=_cparams("parallel", "arbitrary"),
        name="moe_combine",
    )(ts_flat, rank_t, aff_t, ye, x, g2)


def moe_residual(streams, w_gate, w_up, w_down, li):
    routed = []
    for (x, h, logits), _ in streams:
        B, N, D = x.shape
        cap = EC_CAPACITY * N // N_EXPERTS
        tt = min(4 * LANES, N)
        aff, rank, ts = route(logits, cap, tt)
        ts_flat = ts[:, :, :N // tt + 1].reshape(-1)
        routed.append((moe_gather(ts_flat, rank, h, cap, tt=tt), ts_flat, rank, aff, tt))
    yes = expert_ffn([r[0] for r in routed], w_gate, w_up, w_down, li)
    return [moe_combine(ts_flat, jnp.swapaxes(rank, 1, 2), jnp.swapaxes(aff, 1, 2), ye, x, gate2, tt=tt)
            for ((x, _, _), gate2), ye, (_, ts_flat, rank, aff, tt) in zip(streams, yes, routed)]


def lambda_init(layer):
    return 0.8 - 0.6 * math.exp(-0.3 * layer)


def even_layer(x, xc, mods, cmods, norm1_g, w_in, w_out, lb, hgrn_g, qn_g, kn_g, lam_vec, subln_g,
               lam_init, ctx_out, tables):
    sh1, sc1, g1, router = mods
    csh1, csc1, cg1, crouter = cmods
    B = x.shape[0]
    half = w_in[0].shape[2] // 2
    proj = norm_mod_matmul(x, norm1_g, sh1, sc1, w_in, cols=(0, half))
    proj2 = norm_mod_matmul(x, norm1_g, sh1, sc1, w_in, cols=(1, half), out_dtype=BF16)
    projc = norm_mod_matmul(xc, norm1_g, csh1, csc1, w_in, cols=(0, half))
    projc2 = norm_mod_matmul(xc, norm1_g, csh1, csc1, w_in, cols=(1, half), out_dtype=BF16)
    s0 = jnp.zeros((B, HGRN_HEADS, 2, HGRN_HEAD_DIM, HGRN_HEAD_DIM), F32)
    oc_f, oc_b, s_ctx = hgrn_scan(projc, lb, s0)
    o_f, o_b, _ = hgrn_scan(proj, lb, s_ctx)
    lv = lam_vec.astype(F32)
    lam = jnp.exp(jnp.sum(lv[0] * lv[1])) - jnp.exp(jnp.sum(lv[2] * lv[3])) + lam_init
    bound = 1.01 * math.sqrt(DIFF_HEAD_DIM) * jnp.max(jnp.abs(qn_g)) * jnp.max(jnp.abs(kn_g))
    lam = jnp.stack([lam, bound, (bound <= SCORE_BOUND_MAX).astype(F32)])
    cos, sin = tables
    q, k, v = qkv_prep(proj2, qn_g, kn_g, cos, sin, rotary=True)
    qc, kc, vc = qkv_prep(projc2, qn_g, kn_g, cos[:xc.shape[1]], sin[:xc.shape[1]], rotary=False)
    k_all = jnp.concatenate([k, kc], axis=1)
    v_all = jnp.concatenate([v, vc], axis=2)
    y_d = diff_attention(q, k_all, v_all, lam, subln_g, 1.0 - lam_init)
    x_new = even_out(x, o_f, o_b, proj2, y_d, hgrn_g.reshape(-1), g1, w_out, router)
    if not ctx_out:
        return x_new, None
    yc_d = diff_attention(qc, kc, vc, lam, subln_g, 1.0 - lam_init)
    xc_new = even_out(xc, oc_f, oc_b, projc2, yc_d, hgrn_g.reshape(-1), cg1, w_out, crouter)
    return x_new, xc_new


def conv_layer(x, mods, norm1_g, w_in, conv_w, w_out):
    sh1, sc1, g1, router = mods
    proj = norm_mod_matmul(x, norm1_g, sh1, sc1, w_in, out_dtype=BF16)
    return conv_out(x, proj, conv_w, g1, w_out, router)


def kernel(x, c, ctx, c_ctx, mod_w, mod_b, norm1_g, norm2_g, even_w_in, even_w_out, hgrn_lb_logits, hgrn_norm_g,
           diff_qnorm_g, diff_knorm_g, diff_lambda, diff_subln_g, conv_w_in, conv_w, conv_w_out, router_w,
           exp_w_gate, exp_w_up, exp_w_down):
    depth = mod_w.shape[0]
    B, T, D = x.shape
    lb_soft = jax.nn.softmax(hgrn_lb_logits.astype(F32), axis=0)
    lower_bounds = jnp.cumsum(lb_soft, axis=0) - lb_soft[:1]
    last_ctx_layer = 2 * ((depth - 1) // 2)
    cond = jnp.concatenate([c, c_ctx[None, :], jnp.zeros((8 - (B + 1) % 8, D), F32)], axis=0)
    mods = modulation(jax.nn.silu(cond), mod_w, mod_b)
    tables = rope_tables(T)
    xc = ctx
    for l in range(depth):
        read_ctx = l <= last_ctx_layer
        ctx_out = l < last_ctx_layer
        sh1, sc1, g1, sh2, sc2, g2 = [m[:, None, :] for m in jnp.split(mods[l, :B], MOD_CHUNKS, axis=-1)]
        if read_ctx:
            csh1, csc1, cg1, csh2, csc2, cg2 = [
                jnp.broadcast_to(m[None, None, :], (B, 1, D)) for m in jnp.split(mods[l, B], MOD_CHUNKS, axis=-1)]
        mods_l = (sh1, sc1, g1, (norm2_g[l], sh2, sc2, router_w[l]))
        cmods_l = (csh1, csc1, cg1, (norm2_g[l], csh2, csc2, router_w[l])) if read_ctx else None
        if l % 2 == 0:
            e = l // 2
            s, sc = even_layer(x, xc, mods_l, cmods_l, norm1_g[l],
                               (even_w_in, e), (even_w_out, e), lower_bounds[e],
                               hgrn_norm_g[e], diff_qnorm_g[e], diff_knorm_g[e], diff_lambda[e],
                               diff_subln_g[e], lambda_init(l), ctx_out, tables)
        else:
            j = l // 2
            wi, wo = (conv_w_in, j), (conv_w_out, j)
            s = conv_layer(x, mods_l, norm1_g[l], wi, conv_w[j], wo)
            sc = conv_layer(xc, cmods_l, norm1_g[l], wi, conv_w[j], wo) if ctx_out else None
        outs = moe_residual([(s, g2)] + ([(sc, cg2)] if ctx_out else []), exp_w_gate, exp_w_up, exp_w_down, l)
        x = outs[0]
        if ctx_out:
            xc = outs[1]
    return x
```

```python
import functools
import math

import numpy as np
import jax
import jax.numpy as jnp
from jax import lax
from jax.experimental import pallas as pl
from jax.experimental.pallas import tpu as pltpu

F32 = jnp.float32
BF16 = jnp.bfloat16

EPS = 1e-6
GRID_W = 64
ROPE_THETA = 10000.0
HGRN_HEAD_DIM = 128
HGRN_HEADS = 4
HGRN_WIDTH = HGRN_HEADS * HGRN_HEAD_DIM
DIFF_HEAD_DIM = 64
DIFF_HEADS = 4
DIFF_WIDTH = DIFF_HEADS * 2 * DIFF_HEAD_DIM
N_EXPERTS = 16
EC_CAPACITY = 2
MOD_CHUNKS = 6
SCAN_CHUNK = 64
SCAN_LEVELS = (32, 16, 8, 4, 2, 1)
SCAN_HEADS_PER_STEP = 2
TILE_ROW_SPLIT = 1
FFN_SAMPLES_PER_STEP = 2
LANES = 128
ROW_ALIGN = 16
VMEM_LIMIT = 56 * 1024 * 1024
SCORE_BOUND_MAX = 40.0


def _cparams(*sem):
    return pltpu.CompilerParams(dimension_semantics=sem, vmem_limit_bytes=VMEM_LIMIT)


def _row_tile(t, want):
    return want if t % want == 0 else t


def _row_groups(tm):
    n = tm // TILE_ROW_SPLIT
    if tm % TILE_ROW_SPLIT or n % LANES:
        n = tm
    return [(r0, n) for r0 in range(0, tm, n)]


def _mod_kernel(s_ref, w_ref, b_ref, o_ref):
    s = s_ref[...]
    w = w_ref[0]
    s_hi = s.astype(BF16)
    s_lo = (s - s_hi.astype(F32)).astype(BF16)
    w_hi = w.astype(BF16)
    w_lo = (w - w_hi.astype(F32)).astype(BF16)
    d = lambda a, b: jnp.dot(a, b, preferred_element_type=F32)
    o_ref[0] = d(s_hi, w_hi) + d(s_hi, w_lo) + d(s_lo, w_hi) + b_ref[0]


def modulation(s, mod_w, mod_b, *, tn=2048):
    R, D = s.shape
    depth, _, N = mod_w.shape
    return pl.pallas_call(
        _mod_kernel,
        grid=(depth, N // tn),
        in_specs=[pl.BlockSpec((R, D), lambda l, j: (0, 0)),
                  pl.BlockSpec((1, D, tn), lambda l, j: (l, 0, j)),
                  pl.BlockSpec((1, 1, tn), lambda l, j: (l, 0, j))],
        out_specs=pl.BlockSpec((1, R, tn), lambda l, j: (l, 0, j)),
        out_shape=jax.ShapeDtypeStruct((depth, R, N), F32),
        compiler_params=_cparams("parallel", "parallel"),
        name="modulation",
    )(s, mod_w, mod_b.reshape(depth, 1, N))


def _nmm_kernel(x_ref, g_ref, sh_ref, sc_ref, w_ref, o_ref, w_scr):
    @pl.when((pl.program_id(1) == 0) & (pl.program_id(2) == 0))
    def _():
        w_scr[...] = w_ref[0].astype(BF16)

    for r0, rows in _row_groups(x_ref.shape[1]):
        x = x_ref[0, r0:r0 + rows, :]
        ms = jnp.mean(x * x, axis=-1, keepdims=True)
        h = (x * lax.rsqrt(ms + EPS)) * g_ref[...]
        h = h * (1.0 + sc_ref[0]) + sh_ref[0]
        o_ref[0, r0:r0 + rows, :] = jnp.dot(h.astype(BF16), w_scr[...],
                                            preferred_element_type=F32).astype(o_ref.dtype)


def norm_mod_matmul(x, g, shift, scale, w, *, cols=None, tm=512, out_dtype=F32):
    B, T, D = x.shape
    w, li = w
    first, N = cols if cols else (0, w.shape[2])
    tm = _row_tile(T, tm)
    tn = N
    return pl.pallas_call(
        _nmm_kernel,
        grid=(N // tn, B, T // tm),
        in_specs=[
            pl.BlockSpec((1, tm, D), lambda j, b, i: (b, i, 0)),
            pl.BlockSpec((1, D), lambda j, b, i: (0, 0)),
            pl.BlockSpec((1, 1, D), lambda j, b, i: (b, 0, 0)),
            pl.BlockSpec((1, 1, D), lambda j, b, i: (b, 0, 0)),
            pl.BlockSpec((1, D, tn), lambda j, b, i: (li, 0, first + j)),
        ],
        out_specs=pl.BlockSpec((1, tm, tn), lambda j, b, i: (b, i, j)),
        out_shape=jax.ShapeDtypeStruct((B, T, N), out_dtype),
        scratch_shapes=[pltpu.VMEM((D, tn), BF16)],
        compiler_params=_cparams("arbitrary", "arbitrary", "arbitrary"),
        name="norm_mod_matmul",
    )(x, g.reshape(1, D), shift, scale, w)


def _scan_constants():
    C = SCAN_CHUNK
    t = np.arange(C)[:, None]
    u = np.arange(C)[None, :]
    mats = [u <= t, u > t]
    masks = []
    for w in SCAN_LEVELS:
        m = (t // (2 * w)) * 2 * w + w - 1
        later = (t // w) % 2 == 1
        mats.append(np.where(later, (u > m) & (u <= t), (u > t) & (u <= m)))
        masks.append(later & ((u // w) % 2 == 0) & (u // (2 * w) == t // (2 * w)))
    masks = [t == u] + masks + [np.ones((C, C), bool)]
    a_f = np.stack(mats).astype(np.float32)
    m_f = np.stack(masks).astype(np.float32)
    a = np.stack([a_f, a_f[:, ::-1, ::-1]]).reshape(2, -1, C)
    m = np.stack([m_f, m_f[:, ::-1, ::-1]])
    m_pairs = np.concatenate([m[:, 0::2], m[:, 1::2]], axis=3)
    return np.concatenate([a, a], axis=2), m_pairs


def _scan_pair(q, z, v, lb, st, a2, mask_ref, d, later):
    C = SCAN_CHUNK
    W = HGRN_HEAD_DIM
    nl = len(SCAN_LEVELS)
    nt = lambda x, y: lax.dot_general(x, y, (((1,), (1,)), ((), ())), preferred_element_type=F32)
    nn = lambda x, y: jnp.dot(x, y, preferred_element_type=F32)
    e_abs = jnp.exp(-jnp.abs(z))
    r = 1.0 / (1.0 + e_abs)
    er = e_abs * r
    pos = z >= 0.0
    g2 = jnp.log2(lb + (1.0 - lb) * jnp.where(pos, r, er))
    k = (1.0 - lb) * jnp.where(pos, er, r)
    hi = g2.astype(BF16)
    lo = (g2 - hi.astype(F32)).astype(BF16)
    gs = jnp.concatenate([jnp.concatenate([hi[:C], hi[C:]], axis=1),
                          jnp.concatenate([lo[:C], lo[C:]], axis=1)], axis=0)
    x = nn(a2, gs)
    yield
    x = jnp.exp2(x)
    first, second = (1, 0) if d else (0, 1)
    vb = v.astype(BF16)
    sides, qt, kt, dec = [], [], [], []
    for c in (0, 1):
        qc, kc, xs = q[c * C:(c + 1) * C], k[c * C:(c + 1) * C], x[:, c * W:(c + 1) * W]
        ops = [(qc.astype(BF16), kc.astype(BF16))]
        for i in range(nl):
            qk = (jnp.where(later[i], qc, kc) * xs[(2 + i) * C:(3 + i) * C]).astype(BF16)
            ops.append((qk, qk))
        sides.append(ops)
        qt.append(qc * xs[0:C])
        kt.append(kc * xs[C:2 * C])
        dec.append(xs[0:1, :] if d else xs[C - 1:C, :])
    zero = jnp.zeros((C, W), BF16)
    sides[first].append((zero, zero))
    sides[second].append((qt[second].astype(BF16), kt[first].astype(BF16)))
    prods = []
    for ops in sides:
        pair = []
        for j in range(0, nl + 2, 2):
            (qa, ka), (qb, kb) = ops[j], ops[j + 1]
            rhs = jnp.concatenate([jnp.concatenate([ka, zero], axis=1), jnp.concatenate([zero, kb], axis=1)], axis=0)
            pair.append(nt(jnp.concatenate([qa, qb], axis=1), rhs))
        prods.append(pair)
        yield
    qt[second] = qt[second] * dec[first]
    kt[first] = kt[first] * dec[second]
    o_st = nt(jnp.concatenate([a.astype(BF16) for a in qt], axis=0), st.astype(BF16))
    upd = lax.dot_general(vb, jnp.concatenate(kt, axis=0).astype(BF16), (((0,), (0,)), ((), ())),
                          preferred_element_type=F32)
    yield
    o = [None, None]
    n_main = nl // 2
    for c, pair in enumerate(prods):
        main = mask_ref[d, 0] * pair[0]
        for j in range(1, n_main):
            main = main + mask_ref[d, j] * pair[j]
        last = mask_ref[d, n_main] * pair[n_main]
        vc, vf = vb[c * C:(c + 1) * C], vb[first * C:(first + 1) * C]
        o[c] = o_st[c * C:(c + 1) * C] + nn(jnp.concatenate([main, last], axis=1).astype(BF16),
                                            jnp.concatenate([vc, vc, vc, vf], axis=0))
    yield
    return jnp.concatenate(o, axis=0), st * (dec[0] * dec[1]) + upd


def _run_interleaved(gens):
    results = [None] * len(gens)
    live = list(range(len(gens)))
    while live:
        for i in list(live):
            try:
                next(gens[i])
            except StopIteration as done:
                results[i] = done.value
                live.remove(i)
    return results


def _scan_kernel(qf_ref, zf_ref, vf_ref, qb_ref, zb_ref, vb_ref, lbf_ref, lbb_ref, s0_ref, a_ref, mask_ref,
                 of_ref, ob_ref, sT_ref, st_scr, *, n_chunks):
    c = pl.program_id(2)
    C = SCAN_CHUNK

    @pl.when(c == 0)
    def _():
        st_scr[...] = s0_ref[0]

    W = HGRN_HEAD_DIM
    row = lax.broadcasted_iota(jnp.int32, (C, W), 0)
    later = [[(row // w) % 2 == 1 for w in SCAN_LEVELS], [((C - 1 - row) // w) % 2 == 1 for w in SCAN_LEVELS]]
    in_refs = [(qf_ref, zf_ref, vf_ref), (qb_ref, zb_ref, vb_ref)]
    lb_refs = [lbf_ref, lbb_ref]
    o_refs = [of_ref, ob_ref]
    n_pairs = n_chunks // 2
    heads = range(qf_ref.shape[2] // W)

    def body(i, carry):
        r0 = [pl.multiple_of(i * 2 * C, 2 * C), pl.multiple_of((n_pairs - 1 - i) * 2 * C, 2 * C)]
        chains = [(d, h) for h in heads for d in (0, 1)]
        ins = [[ref[0, pl.ds(r0[d], 2 * C), h * W:(h + 1) * W] for ref in in_refs[d]] for d, h in chains]
        sts = [st_scr[h, d] for d, h in chains]
        outs = _run_interleaved([
            _scan_pair(*x, lb_refs[d][:, h * W:(h + 1) * W], st, a_ref[d], mask_ref, d, later[d])
            for (d, h), x, st in zip(chains, ins, sts)])
        for (d, h), (o, st) in zip(chains, outs):
            o_refs[d][0, pl.ds(r0[d], 2 * C), h * W:(h + 1) * W] = o.astype(o_refs[d].dtype)
            st_scr[h, d] = st
        return carry

    lax.fori_loop(0, n_pairs, body, 0)

    @pl.when(c == pl.num_programs(2) - 1)
    def _():
        sT_ref[0] = st_scr[...]


def hgrn_scan(proj, lb, s0, *, tb=512):
    B, T, _ = proj.shape
    tb = _row_tile(T, tb)
    nc = T // tb
    H = HGRN_HEADS
    hd = HGRN_HEAD_DIM
    hps = SCAN_HEADS_PER_STEP
    G = H // hps
    wd = hps * hd
    fwd = lambda grp: pl.BlockSpec((1, tb, wd), lambda b, h, c: (b, c, grp * G + h))
    bwd = lambda grp: pl.BlockSpec((1, tb, wd), lambda b, h, c: (b, nc - 1 - c, grp * G + h))
    kern = functools.partial(_scan_kernel, n_chunks=tb // SCAN_CHUNK)
    a2, masks = _scan_constants()
    return pl.pallas_call(
        kern,
        grid=(B, G, nc),
        in_specs=[fwd(0), fwd(1), fwd(3), bwd(0), bwd(2), bwd(3),
                  pl.BlockSpec((1, wd), lambda b, h, c: (0, h)),
                  pl.BlockSpec((1, wd), lambda b, h, c: (0, h)),
                  pl.BlockSpec((1, hps, 2, hd, hd), lambda b, h, c: (b, h, 0, 0, 0)),
                  pl.BlockSpec(a2.shape, lambda b, h, c: (0, 0, 0)),
                  pl.BlockSpec(masks.shape, lambda b, h, c: (0, 0, 0, 0))],
        out_specs=[pl.BlockSpec((1, tb, wd), lambda b, h, c: (b, c, h)),
                   pl.BlockSpec((1, tb, wd), lambda b, h, c: (b, nc - 1 - c, h)),
                   pl.BlockSpec((1, hps, 2, hd, hd), lambda b, h, c: (b, h, 0, 0, 0))],
        out_shape=[jax.ShapeDtypeStruct((B, T, HGRN_WIDTH), BF16),
                   jax.ShapeDtypeStruct((B, T, HGRN_WIDTH), BF16),
                   jax.ShapeDtypeStruct((B, H, 2, hd, hd), F32)],
        scratch_shapes=[pltpu.VMEM((hps, 2, hd, hd), F32)],
        compiler_params=_cparams("parallel", "parallel", "arbitrary"),
        name="hgrn_scan",
    )(proj, proj, proj, proj, proj, proj, lb[0:1], lb[1:2], s0, jnp.asarray(a2, BF16), jnp.asarray(masks, F32))


def _group_mean_sq(x, gmat):
    sq = x * x
    hi = sq.astype(BF16)
    lo = (sq - hi.astype(F32)).astype(BF16)
    return (jnp.dot(hi, gmat, preferred_element_type=F32) + jnp.dot(lo, gmat, preferred_element_type=F32))


def _qkv_prep_kernel(q_ref, k_ref, v_ref, qg_ref, kg_ref, cos_ref, sin_ref, qo_ref, ko_ref, vo_ref, *, rotary):
    W = LANES
    r_i = lax.broadcasted_iota(jnp.int32, (W, W), 0) // DIFF_HEAD_DIM
    c_i = lax.broadcasted_iota(jnp.int32, (W, W), 1) // DIFF_HEAD_DIM
    gmat = jnp.where(r_i == c_i, 1.0 / DIFF_HEAD_DIM, 0.0).astype(BF16)
    lane = lax.broadcasted_iota(jnp.int32, (1, W), 1)
    first = (lane % 32) < 16

    def prep(x, g, scale):
        y = (x * lax.rsqrt(_group_mean_sq(x, gmat) + EPS)) * g
        if rotary:
            partner = jnp.where(first, pltpu.roll(y, W - 16, axis=1), pltpu.roll(y, 16, axis=1))
            y = y * cos_ref[...] + partner * sin_ref[...]
        if scale != 1.0:
            y = y * scale
        return y

    for h in range(DIFF_HEADS):
        sl = slice(h * W, (h + 1) * W)
        qo_ref[0, sl, :] = prep(q_ref[0, :, sl].astype(F32), qg_ref[...], DIFF_HEAD_DIM ** -0.5).T.astype(BF16)
        ko_ref[0, :, sl] = prep(k_ref[0, :, sl].astype(F32), kg_ref[...], 1.0).astype(BF16)
        vo_ref[0, sl, :] = v_ref[0, :, sl].astype(F32).T.astype(BF16)


def qkv_prep(proj, qg, kg, cos, sin, *, rotary, tm=512):
    B, T, _ = proj.shape
    tm = _row_tile(T, tm)
    Wd = DIFF_WIDTH
    col = lambda j: pl.BlockSpec((1, tm, Wd), lambda b, i: (b, i, j))
    vec = pl.BlockSpec((1, LANES), lambda b, i: (0, 0))
    tab = pl.BlockSpec((tm, LANES), lambda b, i: (i, 0))
    rows = pl.BlockSpec((1, tm, Wd), lambda b, i: (b, i, 0))
    cols = pl.BlockSpec((1, Wd, tm), lambda b, i: (b, 0, i))
    return pl.pallas_call(
        functools.partial(_qkv_prep_kernel, rotary=rotary),
        grid=(B, T // tm),
        in_specs=[col(1), col(2), col(3), vec, vec, tab, tab],
        out_specs=[cols, rows, cols],
        out_shape=[jax.ShapeDtypeStruct((B, Wd, T), BF16), jax.ShapeDtypeStruct((B, T, Wd), BF16),
                   jax.ShapeDtypeStruct((B, Wd, T), BF16)],
        compiler_params=_cparams("parallel", "parallel"),
        name="qkv_prep",
    )(proj, proj, proj, jnp.tile(qg, 2).reshape(1, LANES), jnp.tile(kg, 2).reshape(1, LANES), cos, sin)


def rope_tables(T):
    n = DIFF_HEAD_DIM // 2
    inv = 1.0 / (ROPE_THETA ** (jnp.arange(0, n, 2, dtype=F32) / n))
    t = jnp.arange(T)
    ang_r = (t // GRID_W).astype(F32)[:, None] * inv[None, :]
    ang_c = (t % GRID_W).astype(F32)[:, None] * inv[None, :]
    cos = jnp.concatenate([jnp.cos(ang_r)] * 2 + [jnp.cos(ang_c)] * 2, axis=-1)
    sin = jnp.concatenate([-jnp.sin(ang_r), jnp.sin(ang_r), -jnp.sin(ang_c), jnp.sin(ang_c)], axis=-1)
    return jnp.tile(cos, (1, 2)), jnp.tile(sin, (1, 2))


def _attn_tile(qt, kv_refs, lam, key_chunk, bound=None):
    tq = qt.shape[1]
    row = lax.broadcasted_iota(jnp.int32, (LANES, 1), 0)
    zero = jnp.zeros_like(qt)
    qq = jnp.concatenate([jnp.where(row < DIFF_HEAD_DIM, qt, zero),
                          jnp.where(row >= DIFF_HEAD_DIM, qt, zero)], axis=1)
    m = jnp.full((1, 2 * tq), -jnp.inf, F32)
    acc = [jnp.zeros((LANES + 16, tq), F32), jnp.zeros((LANES + 16, tq), F32)]
    ones = jnp.ones((16, key_chunk), BF16)
    chunks = [(k_ref, v_ref, c0, min(c0 + key_chunk, k_ref.shape[1]))
              for k_ref, v_ref in kv_refs for c0 in range(0, k_ref.shape[1], key_chunk)]
    scores = lambda c: jnp.dot(c[0][0, c[2]:c[3], :], qq, preferred_element_type=F32)
    s_next = scores(chunks[0])
    yield
    for n, (_, v_ref, c0, c1) in enumerate(chunks):
        s = s_next
        if n + 1 < len(chunks):
            s_next = scores(chunks[n + 1])
        vt1 = jnp.concatenate([v_ref[0, :, c0:c1], ones[:, :c1 - c0]], axis=0)
        if bound is None:
            m_new = jnp.maximum(m, jnp.max(s, axis=0, keepdims=True))
            alpha = jnp.exp(m - m_new)
            pb = jnp.exp((s - m_new).astype(BF16))
            m = m_new
        else:
            alpha = None
            pb = jnp.exp(s - bound).astype(BF16)
        for i in range(2):
            pv = jnp.dot(vt1, pb[:, i * tq:(i + 1) * tq], preferred_element_type=F32)
            acc[i] = acc[i] + pv if alpha is None else acc[i] * alpha[:, i * tq:(i + 1) * tq] + pv
        yield
    inv = [1.0 / a[LANES:LANES + 1] for a in acc]
    return acc[0][:LANES] * inv[0] - acc[1][:LANES] * (lam * inv[1])


def _diff_attn_kernel(par_ref, q_ref, *rest, out_scale, key_chunk, tq):
    kv_refs, g_ref, o_ref = list(zip(rest[:-2:2], rest[1:-2:2])), rest[-2], rest[-1]
    n_tiles = q_ref.shape[2] // tq

    def run(bound):
        outs = _run_interleaved([
            _attn_tile(q_ref[0, :, i * tq:(i + 1) * tq], kv_refs, par_ref[0], key_chunk, bound)
            for i in range(n_tiles)])
        for i, o in enumerate(outs):
            ms = jnp.mean(o * o, axis=0, keepdims=True)
            o_ref[0, i * tq:(i + 1) * tq, :] = ((o * lax.rsqrt(ms + EPS)) * g_ref[...] * out_scale).T.astype(BF16)

    @pl.when(par_ref[2] > 0.5)
    def _():
        run(par_ref[1])

    @pl.when(par_ref[2] <= 0.5)
    def _():
        run(None)


def diff_attention(qt, kvs, params, subln_g, out_scale, *, tq=256, tiles_per_step=4, key_chunk=512):
    B, Wd, T = qt.shape
    tq = _row_tile(T, tq)
    ts = _row_tile(T, tq * tiles_per_step)
    kv_specs = [spec for k, _ in kvs for spec in (
        pl.BlockSpec((1, k.shape[1], LANES), lambda b, h, i: (b, 0, h)),
        pl.BlockSpec((1, LANES, k.shape[1]), lambda b, h, i: (b, h, 0)))]
    return pl.pallas_call(
        functools.partial(_diff_attn_kernel, out_scale=out_scale, key_chunk=key_chunk, tq=tq),
        grid=(B, DIFF_HEADS, T // ts),
        in_specs=[pl.BlockSpec(memory_space=pltpu.SMEM),
                  pl.BlockSpec((1, LANES, ts), lambda b, h, i: (b, h, i))] + kv_specs + [
                  pl.BlockSpec((LANES, 1), lambda b, h, i: (0, 0))],
        out_specs=pl.BlockSpec((1, ts, LANES), lambda b, h, i: (b, i, h)),
        out_shape=jax.ShapeDtypeStruct((B, T, Wd), BF16),
        compiler_params=_cparams("parallel", "parallel", "arbitrary"),
        name="diff_attention",
    )(params, qt, *[a for kv in kvs for a in kv], subln_g.reshape(LANES, 1))


def _even_out_kernel(x_ref, of_ref, ob_ref, gate_ref, yd_ref, hg_ref, g1_ref, w_ref, *rest):
    router_refs, o_ref, moe_refs = rest[:-3], rest[-3], rest[-2:]
    tm = x_ref.shape[1]
    w = w_ref[0].astype(BF16)
    for r0, n in _row_groups(tm):
        rows = slice(r0, r0 + n)
        acc = jnp.dot(yd_ref[0, rows, :], w[HGRN_WIDTH:], preferred_element_type=F32)
        for h in range(HGRN_HEADS):
            sl = slice(h * HGRN_HEAD_DIM, (h + 1) * HGRN_HEAD_DIM)
            o = of_ref[0, rows, sl].astype(F32) + ob_ref[0, rows, sl].astype(F32)
            ms = jnp.mean(o * o, axis=-1, keepdims=True)
            gate = gate_ref[0, rows, sl].astype(F32)
            yh = (o * lax.rsqrt(ms + EPS)) * hg_ref[:, sl] * (gate * jax.nn.sigmoid(gate))
            acc = acc + jnp.dot(yh.astype(BF16), w[sl], preferred_element_type=F32)
        x_new = x_ref[0, rows, :] + g1_ref[0] * acc
        o_ref[0, rows, :] = x_new
        _router_outputs(x_new, r0, *router_refs, *moe_refs)


def even_out(x, o_f, o_b, proj, y_d, hgrn_g, g1, w_out, router, *, tm=512):
    B, T, D = x.shape
    w_out, li = w_out
    tm = _row_tile(T, tm)
    Wd = HGRN_WIDTH
    row = lambda w, j: pl.BlockSpec((1, tm, w), lambda b, i: (b, i, j))
    r_args, r_in, r_out, r_shapes = _router_specs(B, T, D, tm, router)
    return pl.pallas_call(
        _even_out_kernel,
        grid=(B, T // tm),
        in_specs=[row(D, 0), row(Wd, 0), row(Wd, 0), row(Wd, 0), row(Wd, 0),
                  pl.BlockSpec((1, Wd), lambda b, i: (0, 0)),
                  pl.BlockSpec((1, 1, D), lambda b, i: (b, 0, 0)),
                  pl.BlockSpec((1,) + w_out.shape[1:], lambda b, i: (li, 0, 0))] + r_in,
        out_specs=[row(D, 0)] + r_out,
        out_shape=[jax.ShapeDtypeStruct((B, T, D), F32)] + r_shapes,
        compiler_params=_cparams("parallel", "parallel"),
        name="even_out",
    )(x, o_f, o_b, proj, y_d, hgrn_g.reshape(1, Wd), g1, w_out, *r_args)


def _conv_out_kernel(x_ref, bg_ref, cg_ref, v_ref, cp_ref, vp_ref, cn_ref, vn_ref, cw_ref, g1_ref, w_ref, *rest):
    router_refs, o_ref, moe_refs = rest[:-3], rest[-3], rest[-2:]
    i = pl.program_id(1)
    n = pl.num_programs(1)
    f32 = lambda ref, *idx: ref[idx].astype(F32)
    u = f32(cg_ref, 0) * f32(v_ref, 0)
    tm = u.shape[0]
    last = ROW_ALIGN - 1
    u_prev_row = jnp.where(i > 0, f32(cp_ref, 0, slice(last, last + 1)) * f32(vp_ref, 0, slice(last, last + 1)), 0.0)
    u_next_row = jnp.where(i < n - 1, f32(cn_ref, 0, slice(0, 1)) * f32(vn_ref, 0, slice(0, 1)), 0.0)
    ridx = lax.broadcasted_iota(jnp.int32, (tm, 1), 0)
    u_prev = jnp.where(ridx == 0, u_prev_row, pltpu.roll(u, 1, axis=0))
    u_next = jnp.where(ridx == tm - 1, u_next_row, pltpu.roll(u, tm - 1, axis=0))
    y = cw_ref[0:1, :] * u_prev + cw_ref[1:2, :] * u + cw_ref[2:3, :] * u_next
    w = w_ref[0].astype(BF16)
    for r0, rows in _row_groups(tm):
        sl = slice(r0, r0 + rows)
        acc = jnp.dot((f32(bg_ref, 0, sl) * y[sl]).astype(BF16), w, preferred_element_type=F32)
        x_new = x_ref[0, sl, :] + g1_ref[0] * acc
        o_ref[0, sl, :] = x_new
        _router_outputs(x_new, r0, *router_refs, *moe_refs)


def conv_out(x, proj, conv_w, g1, w_out, router, *, tm=512):
    B, T, D = x.shape
    w_out, li = w_out
    tm = _row_tile(T, tm)
    rt = tm // ROW_ALIGN
    last_blk = T // ROW_ALIGN - 1
    row = lambda j: pl.BlockSpec((1, tm, D), lambda b, i: (b, i, j))
    prev = lambda j: pl.BlockSpec((1, ROW_ALIGN, D), lambda b, i: (b, jnp.maximum(i * rt - 1, 0), j))
    nxt = lambda j: pl.BlockSpec((1, ROW_ALIGN, D), lambda b, i: (b, jnp.minimum((i + 1) * rt, last_blk), j))
    r_args, r_in, r_out, r_shapes = _router_specs(B, T, D, tm, router)
    return pl.pallas_call(
        _conv_out_kernel,
        grid=(B, T // tm),
        in_specs=[row(0), row(0), row(1), row(2), prev(1), prev(2), nxt(1), nxt(2),
                  pl.BlockSpec((8, D), lambda b, i: (0, 0)),
                  pl.BlockSpec((1, 1, D), lambda b, i: (b, 0, 0)),
                  pl.BlockSpec((1,) + w_out.shape[1:], lambda b, i: (li, 0, 0))] + r_in,
        out_specs=[row(0)] + r_out,
        out_shape=[jax.ShapeDtypeStruct((B, T, D), F32)] + r_shapes,
        compiler_params=_cparams("parallel", "parallel"),
        name="conv_out",
    )(x, proj, proj, proj, proj, proj, proj, proj,
      jnp.concatenate([conv_w, jnp.zeros((8 - conv_w.shape[0], D), conv_w.dtype)], axis=0), g1, w_out, *r_args)


def _router_outputs(x, r0, g_ref, sh_ref, sc_ref, rwt_ref, h_ref, lg_ref):
    n = x.shape[0]
    ms = jnp.mean(x * x, axis=-1, keepdims=True)
    h = (x * lax.rsqrt(ms + EPS)) * g_ref[...]
    h = h * (1.0 + sc_ref[0]) + sh_ref[0]
    h_hi = h.astype(BF16)
    h_ref[0, r0:r0 + n, :] = h_hi
    h_lo = (h - h_hi.astype(F32)).astype(BF16)
    rwt = rwt_ref[...]
    w_hi = rwt.astype(BF16)
    w_lo = (rwt - w_hi.astype(F32)).astype(BF16)
    d = lambda a, b: lax.dot_general(a, b, (((1,), (1,)), ((), ())), preferred_element_type=F32)
    lg_ref[0, :, r0:r0 + n] = d(w_hi, h_hi) + d(w_lo, h_hi) + d(w_hi, h_lo)


def _router_specs(B, T, D, tm, router):
    g, shift, scale, router_w = router
    E = router_w.shape[1]
    vec = pl.BlockSpec((1, 1, D), lambda b, i: (b, 0, 0))
    return ((g.reshape(1, D), shift, scale, router_w.T),
            [pl.BlockSpec((1, D), lambda b, i: (0, 0)), vec, vec, pl.BlockSpec((E, D), lambda b, i: (0, 0))],
            [pl.BlockSpec((1, tm, D), lambda b, i: (b, i, 0)), pl.BlockSpec((1, E, tm), lambda b, i: (b, 0, i))],
            [jax.ShapeDtypeStruct((B, T, D), BF16), jax.ShapeDtypeStruct((B, E, T), F32)])


def _lane_prefix(flags, tri_tot):
    E, N = flags.shape
    carries = [jnp.zeros((E, LANES), F32)]
    out = []
    for j in range(N // LANES):
        r = jnp.dot(flags[:, j * LANES:(j + 1) * LANES].astype(BF16), tri_tot, preferred_element_type=F32)
        out.append(r[:, :LANES] + carries[-1])
        carries.append(carries[-1] + r[:, LANES:])
    return jnp.concatenate(out, axis=1), carries


def _route_kernel(lg_ref, tab_ref, rank_ref, ts_ref, *, cap, tt):
    lg = lg_ref[0]
    E, N = lg.shape
    p = jnp.exp(lg - jnp.max(lg, axis=0, keepdims=True))
    aff = p / jnp.sum(p, axis=0, keepdims=True)
    count = lambda m: jnp.sum(jnp.where(m, 1.0, 0.0), axis=1, keepdims=True)
    as_float = lambda i: pltpu.bitcast(i, F32)

    def refine_bits(i, thr):
        cand = thr | jnp.left_shift(jnp.int32(1), 30 - i)
        return jnp.where(count(aff >= as_float(cand)) >= cap, cand, thr)

    thr = lax.fori_loop(0, 31, refine_bits, jnp.zeros((E, 1), jnp.int32))

    def refine_mid(i, lo_hi):
        lo, hi = lo_hi
        mid = 0.5 * (lo + hi)
        up = count(aff >= mid) >= cap
        return jnp.where(up, mid, lo), jnp.where(up, hi, mid)

    lo, hi = lax.fori_loop(0, 24, refine_mid, (as_float(thr), as_float(jnp.maximum(thr + 1, 0x00800000))))
    gt = aff >= hi
    eq = (aff >= lo) & (aff < hi)
    r_i = lax.broadcasted_iota(jnp.int32, (LANES, 2 * LANES), 0)
    c_i = lax.broadcasted_iota(jnp.int32, (LANES, 2 * LANES), 1)
    tri_tot = jnp.where((r_i < c_i) | (c_i >= LANES), 1.0, 0.0).astype(BF16)
    eq_rank, _ = _lane_prefix(jnp.where(eq, 1.0, 0.0), tri_tot)
    sel = gt | (eq & (eq_rank < cap - count(gt)))
    rank, before = _lane_prefix(jnp.where(sel, 1.0, 0.0), tri_tot)
    rank = jnp.where(sel, rank, -1.0)
    rank_ref[0] = rank.astype(jnp.int32)
    tab_ref[0] = jnp.concatenate([aff, rank, jnp.zeros((LANES - 2 * E, N), F32)], axis=0).T
    lane = lax.broadcasted_iota(jnp.int32, (E, LANES), 1)
    ts = jnp.zeros((E, LANES), F32)
    for k in range(N // tt + 1):
        ts = jnp.where(lane == k, before[k * tt // LANES], ts)
    ts_ref[0] = ts.astype(jnp.int32)


def route(logits_t, cap, tt):
    B, E, N = logits_t.shape
    blk = pl.BlockSpec((1, E, N), lambda b: (b, 0, 0))
    return pl.pallas_call(
        functools.partial(_route_kernel, cap=cap, tt=tt),
        grid=(B,),
        in_specs=[blk],
        out_specs=[pl.BlockSpec((1, N, LANES), lambda b: (b, 0, 0)), blk,
                   pl.BlockSpec((1, E, LANES), lambda b: (b, 0, 0))],
        out_shape=[jax.ShapeDtypeStruct((B, N, LANES), F32), jax.ShapeDtypeStruct((B, E, N), jnp.int32),
                   jax.ShapeDtypeStruct((B, E, LANES), jnp.int32)],
        compiler_params=_cparams("parallel"),
        name="route",
    )(logits_t)


def _window(lo, w, win, cap):
    lower = (lo // ROW_ALIGN) * ROW_ALIGN + w * win
    return pl.multiple_of(jnp.minimum(lower, cap - win), ROW_ALIGN), lower


def _extra_windows(cap, tt, win):
    return -(-(min(cap, tt) + ROW_ALIGN - 1) // win) - 1


def _moe_gather_kernel(ts_ref, rank_ref, h_ref, xe_ref, acc_ref, *, win, tt):
    b, e = pl.program_id(0), pl.program_id(1)
    N = h_ref.shape[1]
    cap = xe_ref.shape[2]
    nt = N // tt
    base = (b * pl.num_programs(1) + e) * (nt + 1)
    acc_ref[...] = jnp.zeros_like(acc_ref)
    row = lax.broadcasted_iota(jnp.int32, (win, tt), 0)

    def place(k, w):
        start, lower = _window(ts_ref[base + k], w, win, cap)
        c0 = k * tt if isinstance(k, int) else pl.multiple_of(k * tt, tt)
        rk = rank_ref[0, 0, :, pl.ds(c0, tt)]
        onehot = jnp.where(jnp.where(rk >= lower, rk, -1) == row + start, 1.0, 0.0).astype(BF16)
        acc_ref[pl.ds(start, win), :] += jnp.dot(onehot, h_ref[0, pl.ds(c0, tt), :], preferred_element_type=F32)

    span = lambda k: ts_ref[base + k + 1] - (ts_ref[base + k] // ROW_ALIGN) * ROW_ALIGN
    need = jnp.int32(0)
    for k in range(nt):
        place(k, 0)
        need = jnp.maximum(need, span(k))

    n_extra = _extra_windows(cap, tt, win)

    @pl.when(need > win)
    def _():
        def extra(i, carry):
            k, w = i // n_extra, i % n_extra + 1

            @pl.when(span(k) > w * win)
            def _():
                place(k, w)
            return carry

        lax.fori_loop(0, nt * n_extra, extra, 0)

    xe_ref[0, 0] = acc_ref[...].astype(BF16)


def moe_gather(ts_flat, rank, h, cap, *, tt):
    B, E, N = rank.shape
    D = h.shape[2]
    return pl.pallas_call(
        functools.partial(_moe_gather_kernel, win=min(LANES, cap), tt=tt),
        grid_spec=pltpu.PrefetchScalarGridSpec(
            num_scalar_prefetch=1,
            grid=(B, E),
            in_specs=[pl.BlockSpec((1, 1, 1, N), lambda b, e, ts: (b, e, 0, 0)),
                      pl.BlockSpec((1, N, D), lambda b, e, ts: (b, 0, 0))],
            out_specs=pl.BlockSpec((1, 1, cap, D), lambda b, e, ts: (b, e, 0, 0)),
            scratch_shapes=[pltpu.VMEM((cap, D), F32)]),
        out_shape=jax.ShapeDtypeStruct((B, E, cap, D), BF16),
        compiler_params=_cparams("parallel", "arbitrary"),
        name="moe_gather",
    )(ts_flat, rank.reshape(B, E, 1, N), h)


def _expert_kernel(*refs, n_streams):
    xe_refs = refs[:n_streams]
    wg_ref, wu_ref, wd_ref = refs[n_streams:n_streams + 3]
    o_refs = refs[n_streams + 3:2 * n_streams + 3]
    wg_scr, wu_scr, wd_scr = refs[2 * n_streams + 3:]

    @pl.when(pl.program_id(1) == 0)
    def _():
        wg_scr[...] = wg_ref[0, 0].astype(BF16)
        wu_scr[...] = wu_ref[0, 0].astype(BF16)
        wd_scr[...] = wd_ref[0, 0].astype(BF16)

    def swiglu(x):
        a = jnp.dot(x, wg_scr[...], preferred_element_type=F32)
        u = jnp.dot(x, wu_scr[...], preferred_element_type=F32)
        yield
        hid = (a * jax.nn.sigmoid(a)) * u
        return jnp.dot(hid.astype(BF16), wd_scr[...], preferred_element_type=F32).astype(BF16)

    nb = xe_refs[0].shape[0]
    ys = _run_interleaved([swiglu(jnp.concatenate([r[s, 0] for r in xe_refs], axis=0)) for s in range(nb)])
    for s, y in enumerate(ys):
        r0 = 0
        for o_ref in o_refs:
            o_ref[s, 0] = y[r0:r0 + o_ref.shape[2]]
            r0 += o_ref.shape[2]


def expert_ffn(xes, w_gate, w_up, w_down, li):
    B, E, _, D = xes[0].shape
    FF = w_gate.shape[3]
    nb = FFN_SAMPLES_PER_STEP if B % FFN_SAMPLES_PER_STEP == 0 else 1
    rows = [pl.BlockSpec((nb, 1, xe.shape[2], D), lambda e, b: (b, e, 0, 0)) for xe in xes]
    return pl.pallas_call(
        functools.partial(_expert_kernel, n_streams=len(xes)),
        grid=(E, B // nb),
        in_specs=rows + [pl.BlockSpec((1, 1, D, FF), lambda e, b: (li, e, 0, 0)),
                         pl.BlockSpec((1, 1, D, FF), lambda e, b: (li, e, 0, 0)),
                         pl.BlockSpec((1, 1, FF, D), lambda e, b: (li, e, 0, 0))],
        out_specs=rows,
        out_shape=[jax.ShapeDtypeStruct(xe.shape, BF16) for xe in xes],
        scratch_shapes=[pltpu.VMEM((D, FF), BF16), pltpu.VMEM((D, FF), BF16), pltpu.VMEM((FF, D), BF16)],
        compiler_params=_cparams("arbitrary", "arbitrary"),
        name="expert_ffn",
    )(*xes, w_gate, w_up, w_down)


def _moe_combine_kernel(ts_ref, tab_ref, ye_ref, x_ref, g2_ref, o_ref, acc_ref, *, win, group):
    b, k = pl.program_id(0), pl.program_id(1)
    E, cap = ye_ref.shape[1], ye_ref.shape[2]
    tt = x_ref.shape[1]
    nt = pl.num_programs(1)
    col = lax.broadcasted_iota(jnp.int32, (tt, win), 1)
    lo = [ts_ref[(b * E + e) * (nt + 1) + k] for e in range(E)]
    hi = [ts_ref[(b * E + e) * (nt + 1) + k + 1] for e in range(E)]

    def contribution(w):
        total = None
        for g0 in range(0, E, group):
            lhs, rhs = [], []
            for e in range(g0, g0 + group):
                start, lower = _window(lo[e], w, win, cap)
                rk = tab_ref[0, :, E + e:E + e + 1].astype(jnp.int32)
                hit = jnp.where(rk >= lower, rk, -1) == col + start
                lhs.append(jnp.where(hit, tab_ref[0, :, e:e + 1], 0.0).astype(BF16))
                rhs.append(ye_ref[0, e, pl.ds(start, win), :])
            d = jnp.dot(jnp.concatenate(lhs, axis=1), jnp.concatenate(rhs, axis=0), preferred_element_type=F32)
            total = d if total is None else total + d
        return total

    acc_ref[...] = contribution(0)
    need = jnp.int32(0)
    for e in range(E):
        need = jnp.maximum(need, hi[e] - (lo[e] // ROW_ALIGN) * ROW_ALIGN)

    @pl.when(need > win)
    def _():
        def extra(w, carry):
            @pl.when(need > w * win)
            def _():
                acc_ref[...] += contribution(w)
            return carry

        lax.fori_loop(1, _extra_windows(cap, tt, win) + 1, extra, 0)

    o_ref[0] = x_ref[0] + g2_ref[0] * acc_ref[...]


def moe_combine(ts_flat, table, ye, x, g2, *, tt):
    B, N, D = x.shape
    E, cap = ye.shape[1], ye.shape[2]
    tok = lambda w: pl.BlockSpec((1, tt, w), lambda b, k, ts: (b, k, 0))
    return pl.pallas_call(
        functools.partial(_moe_combine_kernel, win=min(LANES, cap), group=4),
        grid_spec=pltpu.PrefetchScalarGridSpec(
            num_scalar_prefetch=1,
            grid=(B, N // tt),
            in_specs=[tok(LANES),
                      pl.BlockSpec((1, E, cap, D), lambda b, k, ts: (b, 0, 0, 0)),
                      tok(D),
                      pl.BlockSpec((1, 1, D), lambda b, k, ts: (b, 0, 0))],
            out_specs=tok(D),
            scratch_shapes=[pltpu.VMEM((tt, D), F32)]),
        out_shape=jax.ShapeDtypeStruct((B, N, D), F32),
        compiler_params=_cparams("parallel", "arbitrary"),
        name="moe_combine",
    )(ts_flat, table, ye, x, g2)


def moe_residual(streams, w_gate, w_up, w_down, li):
    routed = []
    for (x, h, logits), _ in streams:
        B, N, D = x.shape
        cap = EC_CAPACITY * N // N_EXPERTS
        tt = min(4 * LANES, N)
        table, rank, ts = route(logits, cap, tt)
        ts_flat = ts[:, :, :N // tt + 1].reshape(-1)
        routed.append((moe_gather(ts_flat, rank, h, cap, tt=tt), ts_flat, table, tt))
    yes = expert_ffn([r[0] for r in routed], w_gate, w_up, w_down, li)
    return [moe_combine(ts_flat, table, ye, x, gate2, tt=tt)
            for ((x, _, _), gate2), ye, (_, ts_flat, table, tt) in zip(streams, yes, routed)]


def lambda_init(layer):
    return 0.8 - 0.6 * math.exp(-0.3 * layer)


def even_layer(x, xc, mods, cmods, norm1_g, w_in, w_out, lb, hgrn_g, qn_g, kn_g, lam_vec, subln_g,
               lam_init, ctx_out, tables):
    sh1, sc1, g1, router = mods
    csh1, csc1, cg1, crouter = cmods
    B = x.shape[0]
    half = w_in[0].shape[2] // 2
    proj = norm_mod_matmul(x, norm1_g, sh1, sc1, w_in, cols=(0, half))
    proj2 = norm_mod_matmul(x, norm1_g, sh1, sc1, w_in, cols=(1, half), out_dtype=BF16)
    projc = norm_mod_matmul(xc, norm1_g, csh1, csc1, w_in, cols=(0, half))
    projc2 = norm_mod_matmul(xc, norm1_g, csh1, csc1, w_in, cols=(1, half), out_dtype=BF16)
    s0 = jnp.zeros((B, HGRN_HEADS, 2, HGRN_HEAD_DIM, HGRN_HEAD_DIM), F32)
    oc_f, oc_b, s_ctx = hgrn_scan(projc, lb, s0)
    o_f, o_b, _ = hgrn_scan(proj, lb, s_ctx)
    lv = lam_vec.astype(F32)
    lam = jnp.exp(jnp.sum(lv[0] * lv[1])) - jnp.exp(jnp.sum(lv[2] * lv[3])) + lam_init
    bound = 1.01 * math.sqrt(DIFF_HEAD_DIM) * jnp.max(jnp.abs(qn_g)) * jnp.max(jnp.abs(kn_g))
    lam = jnp.stack([lam, bound, (bound <= SCORE_BOUND_MAX).astype(F32)])
    cos, sin = tables
    q, k, v = qkv_prep(proj2, qn_g, kn_g, cos, sin, rotary=True)
    qc, kc, vc = qkv_prep(projc2, qn_g, kn_g, cos[:xc.shape[1]], sin[:xc.shape[1]], rotary=False)
    y_d = diff_attention(q, [(k, v), (kc, vc)], lam, subln_g, 1.0 - lam_init)
    x_new = even_out(x, o_f, o_b, proj2, y_d, hgrn_g.reshape(-1), g1, w_out, router)
    if not ctx_out:
        return x_new, None
    yc_d = diff_attention(qc, [(kc, vc)], lam, subln_g, 1.0 - lam_init)
    xc_new = even_out(xc, oc_f, oc_b, projc2, yc_d, hgrn_g.reshape(-1), cg1, w_out, crouter)
    return x_new, xc_new


def conv_layer(x, mods, norm1_g, w_in, conv_w, w_out):
    sh1, sc1, g1, router = mods
    proj = norm_mod_matmul(x, norm1_g, sh1, sc1, w_in, out_dtype=BF16)
    return conv_out(x, proj, conv_w, g1, w_out, router)


def kernel(x, c, ctx, c_ctx, mod_w, mod_b, norm1_g, norm2_g, even_w_in, even_w_out, hgrn_lb_logits, hgrn_norm_g,
           diff_qnorm_g, diff_knorm_g, diff_lambda, diff_subln_g, conv_w_in, conv_w, conv_w_out, router_w,
           exp_w_gate, exp_w_up, exp_w_down):
    depth = mod_w.shape[0]
    B, T, D = x.shape
    lb_soft = jax.nn.softmax(hgrn_lb_logits.astype(F32), axis=0)
    lower_bounds = jnp.cumsum(lb_soft, axis=0) - lb_soft[:1]
    last_ctx_layer = 2 * ((depth - 1) // 2)
    cond = jnp.concatenate([c, c_ctx[None, :], jnp.zeros((8 - (B + 1) % 8, D), F32)], axis=0)
    mods = modulation(jax.nn.silu(cond), mod_w, mod_b)
    tables = rope_tables(T)
    xc = ctx
    for l in range(depth):
        read_ctx = l <= last_ctx_layer
        ctx_out = l < last_ctx_layer
        sh1, sc1, g1, sh2, sc2, g2 = [m[:, None, :] for m in jnp.split(mods[l, :B], MOD_CHUNKS, axis=-1)]
        if read_ctx:
            csh1, csc1, cg1, csh2, csc2, cg2 = [
                jnp.broadcast_to(m[None, None, :], (B, 1, D)) for m in jnp.split(mods[l, B], MOD_CHUNKS, axis=-1)]
        mods_l = (sh1, sc1, g1, (norm2_g[l], sh2, sc2, router_w[l]))
        cmods_l = (csh1, csc1, cg1, (norm2_g[l], csh2, csc2, router_w[l])) if read_ctx else None
        if l % 2 == 0:
            e = l // 2
            s, sc = even_layer(x, xc, mods_l, cmods_l, norm1_g[l],
                               (even_w_in, e), (even_w_out, e), lower_bounds[e],
                               hgrn_norm_g[e], diff_qnorm_g[e], diff_knorm_g[e], diff_lambda[e],
                               diff_subln_g[e], lambda_init(l), ctx_out, tables)
        else:
            j = l // 2
            wi, wo = (conv_w_in, j), (conv_w_out, j)
            s = conv_layer(x, mods_l, norm1_g[l], wi, conv_w[j], wo)
            sc = conv_layer(xc, cmods_l, norm1_g[l], wi, conv_w[j], wo) if ctx_out else None
        outs = moe_residual([(s, g2)] + ([(sc, cg2)] if ctx_out else []), exp_w_gate, exp_w_up, exp_w_down, l)
        x = outs[0]
        if ctx_out:
            xc = outs[1]
    return x
```

```python
import functools
import math

import numpy as np
import jax
import jax.numpy as jnp
from jax import lax
from jax.experimental import pallas as pl
from jax.experimental.pallas import tpu as pltpu

F32 = jnp.float32
BF16 = jnp.bfloat16

EPS = 1e-6
GRID_W = 64
ROPE_THETA = 10000.0
HGRN_HEAD_DIM = 128
HGRN_HEADS = 4
HGRN_WIDTH = HGRN_HEADS * HGRN_HEAD_DIM
DIFF_HEAD_DIM = 64
DIFF_HEADS = 4
DIFF_WIDTH = DIFF_HEADS * 2 * DIFF_HEAD_DIM
N_EXPERTS = 16
EC_CAPACITY = 2
MOD_CHUNKS = 6
SCAN_CHUNK = 64
SCAN_LEVELS = (32, 16, 8, 4, 2, 1)
SCAN_HEADS_PER_STEP = 2
TILE_ROW_SPLIT = 1
GATHER_EXPERTS_PER_STEP = 2
FFN_SAMPLES_PER_STEP = 2
LANES = 128
ROW_ALIGN = 16
VMEM_LIMIT = 56 * 1024 * 1024
SCORE_BOUND_MAX = 40.0


def _cparams(*sem):
    return pltpu.CompilerParams(dimension_semantics=sem, vmem_limit_bytes=VMEM_LIMIT)


def _row_tile(t, want):
    return want if t % want == 0 else t


def _row_groups(tm):
    n = tm // TILE_ROW_SPLIT
    if tm % TILE_ROW_SPLIT or n % LANES:
        n = tm
    return [(r0, n) for r0 in range(0, tm, n)]


def _mod_kernel(s_ref, w_ref, b_ref, o_ref):
    s = s_ref[...]
    w = w_ref[0]
    s_hi = s.astype(BF16)
    s_lo = (s - s_hi.astype(F32)).astype(BF16)
    w_hi = w.astype(BF16)
    w_lo = (w - w_hi.astype(F32)).astype(BF16)
    d = lambda a, b: jnp.dot(a, b, preferred_element_type=F32)
    o_ref[0] = d(s_hi, w_hi) + d(s_hi, w_lo) + d(s_lo, w_hi) + b_ref[0]


def modulation(s, mod_w, mod_b, *, tn=2048):
    R, D = s.shape
    depth, _, N = mod_w.shape
    return pl.pallas_call(
        _mod_kernel,
        grid=(depth, N // tn),
        in_specs=[pl.BlockSpec((R, D), lambda l, j: (0, 0)),
                  pl.BlockSpec((1, D, tn), lambda l, j: (l, 0, j)),
                  pl.BlockSpec((1, 1, tn), lambda l, j: (l, 0, j))],
        out_specs=pl.BlockSpec((1, R, tn), lambda l, j: (l, 0, j)),
        out_shape=jax.ShapeDtypeStruct((depth, R, N), F32),
        compiler_params=_cparams("parallel", "parallel"),
        name="modulation",
    )(s, mod_w, mod_b.reshape(depth, 1, N))


def _nmm_kernel(x_ref, g_ref, sh_ref, sc_ref, w_ref, o_ref, w_scr):
    @pl.when((pl.program_id(1) == 0) & (pl.program_id(2) == 0))
    def _():
        w_scr[...] = w_ref[0].astype(BF16)

    for r0, rows in _row_groups(x_ref.shape[1]):
        x = x_ref[0, r0:r0 + rows, :]
        ms = jnp.mean(x * x, axis=-1, keepdims=True)
        h = (x * lax.rsqrt(ms + EPS)) * g_ref[...]
        h = h * (1.0 + sc_ref[0]) + sh_ref[0]
        o_ref[0, r0:r0 + rows, :] = jnp.dot(h.astype(BF16), w_scr[...],
                                            preferred_element_type=F32).astype(o_ref.dtype)


def norm_mod_matmul(x, g, shift, scale, w, *, cols=None, tm=512, out_dtype=F32):
    B, T, D = x.shape
    w, li = w
    first, N = cols if cols else (0, w.shape[2])
    tm = _row_tile(T, tm)
    tn = N
    return pl.pallas_call(
        _nmm_kernel,
        grid=(N // tn, B, T // tm),
        in_specs=[
            pl.BlockSpec((1, tm, D), lambda j, b, i: (b, i, 0)),
            pl.BlockSpec((1, D), lambda j, b, i: (0, 0)),
            pl.BlockSpec((1, 1, D), lambda j, b, i: (b, 0, 0)),
            pl.BlockSpec((1, 1, D), lambda j, b, i: (b, 0, 0)),
            pl.BlockSpec((1, D, tn), lambda j, b, i: (li, 0, first + j)),
        ],
        out_specs=pl.BlockSpec((1, tm, tn), lambda j, b, i: (b, i, j)),
        out_shape=jax.ShapeDtypeStruct((B, T, N), out_dtype),
        scratch_shapes=[pltpu.VMEM((D, tn), BF16)],
        compiler_params=_cparams("arbitrary", "arbitrary", "arbitrary"),
        name="norm_mod_matmul",
    )(x, g.reshape(1, D), shift, scale, w)


def _scan_constants():
    C = SCAN_CHUNK
    t = np.arange(C)[:, None]
    u = np.arange(C)[None, :]
    mats = [u <= t, u > t]
    masks = []
    for w in SCAN_LEVELS:
        m = (t // (2 * w)) * 2 * w + w - 1
        later = (t // w) % 2 == 1
        mats.append(np.where(later, (u > m) & (u <= t), (u > t) & (u <= m)))
        masks.append(later & ((u // w) % 2 == 0) & (u // (2 * w) == t // (2 * w)))
    masks = [t == u] + masks + [np.ones((C, C), bool)]
    a_f = np.stack(mats).astype(np.float32)
    m_f = np.stack(masks).astype(np.float32)
    a = np.stack([a_f, a_f[:, ::-1, ::-1]]).reshape(2, -1, C)
    m = np.stack([m_f, m_f[:, ::-1, ::-1]])
    m_pairs = np.concatenate([m[:, 0::2], m[:, 1::2]], axis=3)
    return np.concatenate([a, a], axis=2), m_pairs


def _scan_pair(q, z, v, lb, st, a2, mask_ref, d, later):
    C = SCAN_CHUNK
    W = HGRN_HEAD_DIM
    nl = len(SCAN_LEVELS)
    nt = lambda x, y: lax.dot_general(x, y, (((1,), (1,)), ((), ())), preferred_element_type=F32)
    nn = lambda x, y: jnp.dot(x, y, preferred_element_type=F32)
    e_abs = jnp.exp(-jnp.abs(z))
    r = 1.0 / (1.0 + e_abs)
    er = e_abs * r
    pos = z >= 0.0
    g2 = jnp.log2(lb + (1.0 - lb) * jnp.where(pos, r, er))
    k = (1.0 - lb) * jnp.where(pos, er, r)
    hi = g2.astype(BF16)
    lo = (g2 - hi.astype(F32)).astype(BF16)
    gs = jnp.concatenate([jnp.concatenate([hi[:C], hi[C:]], axis=1),
                          jnp.concatenate([lo[:C], lo[C:]], axis=1)], axis=0)
    x = nn(a2, gs)
    yield
    x = jnp.exp2(x)
    first, second = (1, 0) if d else (0, 1)
    vb = v.astype(BF16)
    sides, qt, kt, dec = [], [], [], []
    for c in (0, 1):
        qc, kc, xs = q[c * C:(c + 1) * C], k[c * C:(c + 1) * C], x[:, c * W:(c + 1) * W]
        ops = [(qc.astype(BF16), kc.astype(BF16))]
        for i in range(nl):
            qk = (jnp.where(later[i], qc, kc) * xs[(2 + i) * C:(3 + i) * C]).astype(BF16)
            ops.append((qk, qk))
        sides.append(ops)
        qt.append(qc * xs[0:C])
        kt.append(kc * xs[C:2 * C])
        dec.append(xs[0:1, :] if d else xs[C - 1:C, :])
    zero = jnp.zeros((C, W), BF16)
    sides[first].append((zero, zero))
    sides[second].append((qt[second].astype(BF16), kt[first].astype(BF16)))
    prods = []
    for ops in sides:
        pair = []
        for j in range(0, nl + 2, 2):
            (qa, ka), (qb, kb) = ops[j], ops[j + 1]
            rhs = jnp.concatenate([jnp.concatenate([ka, zero], axis=1), jnp.concatenate([zero, kb], axis=1)], axis=0)
            pair.append(nt(jnp.concatenate([qa, qb], axis=1), rhs))
        prods.append(pair)
        yield
    qt[second] = qt[second] * dec[first]
    kt[first] = kt[first] * dec[second]
    o_st = nt(jnp.concatenate([a.astype(BF16) for a in qt], axis=0), st.astype(BF16))
    upd = lax.dot_general(vb, jnp.concatenate(kt, axis=0).astype(BF16), (((0,), (0,)), ((), ())),
                          preferred_element_type=F32)
    yield
    o = [None, None]
    n_main = nl // 2
    for c, pair in enumerate(prods):
        main = mask_ref[d, 0] * pair[0]
        for j in range(1, n_main):
            main = main + mask_ref[d, j] * pair[j]
        last = mask_ref[d, n_main] * pair[n_main]
        vc, vf = vb[c * C:(c + 1) * C], vb[first * C:(first + 1) * C]
        o[c] = o_st[c * C:(c + 1) * C] + nn(jnp.concatenate([main, last], axis=1).astype(BF16),
                                            jnp.concatenate([vc, vc, vc, vf], axis=0))
    yield
    return jnp.concatenate(o, axis=0), st * (dec[0] * dec[1]) + upd


def _run_interleaved(gens):
    results = [None] * len(gens)
    live = list(range(len(gens)))
    while live:
        for i in list(live):
            try:
                next(gens[i])
            except StopIteration as done:
                results[i] = done.value
                live.remove(i)
    return results


def _scan_kernel(qf_ref, zf_ref, vf_ref, qb_ref, zb_ref, vb_ref, lbf_ref, lbb_ref, s0_ref, a_ref, mask_ref,
                 of_ref, ob_ref, sT_ref, st_scr, *, n_chunks):
    c = pl.program_id(2)
    C = SCAN_CHUNK

    @pl.when(c == 0)
    def _():
        st_scr[...] = s0_ref[0]

    W = HGRN_HEAD_DIM
    row = lax.broadcasted_iota(jnp.int32, (C, W), 0)
    later = [[(row // w) % 2 == 1 for w in SCAN_LEVELS], [((C - 1 - row) // w) % 2 == 1 for w in SCAN_LEVELS]]
    in_refs = [(qf_ref, zf_ref, vf_ref), (qb_ref, zb_ref, vb_ref)]
    lb_refs = [lbf_ref, lbb_ref]
    o_refs = [of_ref, ob_ref]
    n_pairs = n_chunks // 2
    heads = range(qf_ref.shape[2] // W)

    def body(i, carry):
        r0 = [pl.multiple_of(i * 2 * C, 2 * C), pl.multiple_of((n_pairs - 1 - i) * 2 * C, 2 * C)]
        chains = [(d, h) for h in heads for d in (0, 1)]
        ins = [[ref[0, pl.ds(r0[d], 2 * C), h * W:(h + 1) * W] for ref in in_refs[d]] for d, h in chains]
        sts = [st_scr[h, d] for d, h in chains]
        outs = _run_interleaved([
            _scan_pair(*x, lb_refs[d][:, h * W:(h + 1) * W], st, a_ref[d], mask_ref, d, later[d])
            for (d, h), x, st in zip(chains, ins, sts)])
        for (d, h), (o, st) in zip(chains, outs):
            o_refs[d][0, pl.ds(r0[d], 2 * C), h * W:(h + 1) * W] = o.astype(o_refs[d].dtype)
            st_scr[h, d] = st
        return carry

    lax.fori_loop(0, n_pairs, body, 0)

    @pl.when(c == pl.num_programs(2) - 1)
    def _():
        sT_ref[0] = st_scr[...]


def hgrn_scan(proj, lb, s0, *, tb=1024):
    B, T, _ = proj.shape
    tb = _row_tile(T, tb)
    nc = T // tb
    H = HGRN_HEADS
    hd = HGRN_HEAD_DIM
    hps = SCAN_HEADS_PER_STEP
    G = H // hps
    wd = hps * hd
    fwd = lambda grp: pl.BlockSpec((1, tb, wd), lambda b, h, c: (b, c, grp * G + h))
    bwd = lambda grp: pl.BlockSpec((1, tb, wd), lambda b, h, c: (b, nc - 1 - c, grp * G + h))
    kern = functools.partial(_scan_kernel, n_chunks=tb // SCAN_CHUNK)
    a2, masks = _scan_constants()
    return pl.pallas_call(
        kern,
        grid=(B, G, nc),
        in_specs=[fwd(0), fwd(1), fwd(3), bwd(0), bwd(2), bwd(3),
                  pl.BlockSpec((1, wd), lambda b, h, c: (0, h)),
                  pl.BlockSpec((1, wd), lambda b, h, c: (0, h)),
                  pl.BlockSpec((1, hps, 2, hd, hd), lambda b, h, c: (b, h, 0, 0, 0)),
                  pl.BlockSpec(a2.shape, lambda b, h, c: (0, 0, 0)),
                  pl.BlockSpec(masks.shape, lambda b, h, c: (0, 0, 0, 0))],
        out_specs=[pl.BlockSpec((1, tb, wd), lambda b, h, c: (b, c, h)),
                   pl.BlockSpec((1, tb, wd), lambda b, h, c: (b, nc - 1 - c, h)),
                   pl.BlockSpec((1, hps, 2, hd, hd), lambda b, h, c: (b, h, 0, 0, 0))],
        out_shape=[jax.ShapeDtypeStruct((B, T, HGRN_WIDTH), BF16),
                   jax.ShapeDtypeStruct((B, T, HGRN_WIDTH), BF16),
                   jax.ShapeDtypeStruct((B, H, 2, hd, hd), F32)],
        scratch_shapes=[pltpu.VMEM((hps, 2, hd, hd), F32)],
        compiler_params=_cparams("parallel", "parallel", "arbitrary"),
        name="hgrn_scan",
    )(proj, proj, proj, proj, proj, proj, lb[0:1], lb[1:2], s0, jnp.asarray(a2, BF16), jnp.asarray(masks, F32))


def _group_mean_sq(x, gmat):
    sq = x * x
    hi = sq.astype(BF16)
    lo = (sq - hi.astype(F32)).astype(BF16)
    return (jnp.dot(hi, gmat, preferred_element_type=F32) + jnp.dot(lo, gmat, preferred_element_type=F32))


def _qkv_prep_kernel(q_ref, k_ref, v_ref, qg_ref, kg_ref, cos_ref, sin_ref, qo_ref, ko_ref, vo_ref, *, rotary):
    W = LANES
    r_i = lax.broadcasted_iota(jnp.int32, (W, W), 0) // DIFF_HEAD_DIM
    c_i = lax.broadcasted_iota(jnp.int32, (W, W), 1) // DIFF_HEAD_DIM
    gmat = jnp.where(r_i == c_i, 1.0 / DIFF_HEAD_DIM, 0.0).astype(BF16)
    lane = lax.broadcasted_iota(jnp.int32, (1, W), 1)
    first = (lane % 32) < 16

    def prep(x, g, scale):
        y = (x * lax.rsqrt(_group_mean_sq(x, gmat) + EPS)) * g
        if rotary:
            partner = jnp.where(first, pltpu.roll(y, W - 16, axis=1), pltpu.roll(y, 16, axis=1))
            y = y * cos_ref[...] + partner * sin_ref[...]
        if scale != 1.0:
            y = y * scale
        return y

    for h in range(DIFF_HEADS):
        sl = slice(h * W, (h + 1) * W)
        qo_ref[0, sl, :] = prep(q_ref[0, :, sl].astype(F32), qg_ref[...], DIFF_HEAD_DIM ** -0.5).T.astype(BF16)
        ko_ref[0, :, sl] = prep(k_ref[0, :, sl].astype(F32), kg_ref[...], 1.0).astype(BF16)
        vo_ref[0, sl, :] = v_ref[0, :, sl].astype(F32).T.astype(BF16)


def qkv_prep(proj, qg, kg, cos, sin, *, rotary, tm=1024):
    B, T, _ = proj.shape
    tm = _row_tile(T, tm)
    Wd = DIFF_WIDTH
    col = lambda j: pl.BlockSpec((1, tm, Wd), lambda b, i: (b, i, j))
    vec = pl.BlockSpec((1, LANES), lambda b, i: (0, 0))
    tab = pl.BlockSpec((tm, LANES), lambda b, i: (i, 0))
    rows = pl.BlockSpec((1, tm, Wd), lambda b, i: (b, i, 0))
    cols = pl.BlockSpec((1, Wd, tm), lambda b, i: (b, 0, i))
    return pl.pallas_call(
        functools.partial(_qkv_prep_kernel, rotary=rotary),
        grid=(B, T // tm),
        in_specs=[col(1), col(2), col(3), vec, vec, tab, tab],
        out_specs=[cols, rows, cols],
        out_shape=[jax.ShapeDtypeStruct((B, Wd, T), BF16), jax.ShapeDtypeStruct((B, T, Wd), BF16),
                   jax.ShapeDtypeStruct((B, Wd, T), BF16)],
        compiler_params=_cparams("parallel", "parallel"),
        name="qkv_prep",
    )(proj, proj, proj, jnp.tile(qg, 2).reshape(1, LANES), jnp.tile(kg, 2).reshape(1, LANES), cos, sin)


def rope_tables(T):
    n = DIFF_HEAD_DIM // 2
    inv = 1.0 / (ROPE_THETA ** (jnp.arange(0, n, 2, dtype=F32) / n))
    t = jnp.arange(T)
    ang_r = (t // GRID_W).astype(F32)[:, None] * inv[None, :]
    ang_c = (t % GRID_W).astype(F32)[:, None] * inv[None, :]
    cos = jnp.concatenate([jnp.cos(ang_r)] * 2 + [jnp.cos(ang_c)] * 2, axis=-1)
    sin = jnp.concatenate([-jnp.sin(ang_r), jnp.sin(ang_r), -jnp.sin(ang_c), jnp.sin(ang_c)], axis=-1)
    return jnp.tile(cos, (1, 2)), jnp.tile(sin, (1, 2))


def _attn_tile(qt, kv_refs, lam, key_chunk, bound=None):
    tq = qt.shape[1]
    row = lax.broadcasted_iota(jnp.int32, (LANES, 1), 0)
    zero = jnp.zeros_like(qt)
    qq = jnp.concatenate([jnp.where(row < DIFF_HEAD_DIM, qt, zero),
                          jnp.where(row >= DIFF_HEAD_DIM, qt, zero)], axis=1)
    m = jnp.full((1, 2 * tq), -jnp.inf, F32)
    acc = [jnp.zeros((LANES + 16, tq), F32), jnp.zeros((LANES + 16, tq), F32)]
    ones = jnp.ones((16, key_chunk), BF16)
    chunks = [(k_ref, v_ref, c0, min(c0 + key_chunk, k_ref.shape[1]))
              for k_ref, v_ref in kv_refs for c0 in range(0, k_ref.shape[1], key_chunk)]
    scores = lambda c: jnp.dot(c[0][0, c[2]:c[3], :], qq, preferred_element_type=F32)
    s_next = scores(chunks[0])
    yield
    for n, (_, v_ref, c0, c1) in enumerate(chunks):
        s = s_next
        if n + 1 < len(chunks):
            s_next = scores(chunks[n + 1])
        vt1 = jnp.concatenate([v_ref[0, :, c0:c1], ones[:, :c1 - c0]], axis=0)
        if bound is None:
            m_new = jnp.maximum(m, jnp.max(s, axis=0, keepdims=True))
            alpha = jnp.exp(m - m_new)
            pb = jnp.exp((s - m_new).astype(BF16))
            m = m_new
        else:
            alpha = None
            pb = jnp.exp(s - bound).astype(BF16)
        for i in range(2):
            pv = jnp.dot(vt1, pb[:, i * tq:(i + 1) * tq], preferred_element_type=F32)
            acc[i] = acc[i] + pv if alpha is None else acc[i] * alpha[:, i * tq:(i + 1) * tq] + pv
        yield
    inv = [1.0 / a[LANES:LANES + 1] for a in acc]
    return acc[0][:LANES] * inv[0] - acc[1][:LANES] * (lam * inv[1])


def _diff_attn_kernel(par_ref, q_ref, *rest, out_scale, key_chunk, tq):
    kv_refs, g_ref, o_ref = list(zip(rest[:-2:2], rest[1:-2:2])), rest[-2], rest[-1]
    n_tiles = q_ref.shape[2] // tq

    def run(bound):
        outs = _run_interleaved([
            _attn_tile(q_ref[0, :, i * tq:(i + 1) * tq], kv_refs, par_ref[0], key_chunk, bound)
            for i in range(n_tiles)])
        for i, o in enumerate(outs):
            ms = jnp.mean(o * o, axis=0, keepdims=True)
            o_ref[0, i * tq:(i + 1) * tq, :] = ((o * lax.rsqrt(ms + EPS)) * g_ref[...] * out_scale).T.astype(BF16)

    @pl.when(par_ref[2] > 0.5)
    def _():
        run(par_ref[1])

    @pl.when(par_ref[2] <= 0.5)
    def _():
        run(None)


def diff_attention(qt, kvs, params, subln_g, out_scale, *, tq=256, tiles_per_step=4, key_chunk=512):
    B, Wd, T = qt.shape
    tq = _row_tile(T, tq)
    ts = _row_tile(T, tq * tiles_per_step)
    kv_specs = [spec for k, _ in kvs for spec in (
        pl.BlockSpec((1, k.shape[1], LANES), lambda b, h, i: (b, 0, h)),
        pl.BlockSpec((1, LANES, k.shape[1]), lambda b, h, i: (b, h, 0)))]
    return pl.pallas_call(
        functools.partial(_diff_attn_kernel, out_scale=out_scale, key_chunk=key_chunk, tq=tq),
        grid=(B, DIFF_HEADS, T // ts),
        in_specs=[pl.BlockSpec(memory_space=pltpu.SMEM),
                  pl.BlockSpec((1, LANES, ts), lambda b, h, i: (b, h, i))] + kv_specs + [
                  pl.BlockSpec((LANES, 1), lambda b, h, i: (0, 0))],
        out_specs=pl.BlockSpec((1, ts, LANES), lambda b, h, i: (b, i, h)),
        out_shape=jax.ShapeDtypeStruct((B, T, Wd), BF16),
        compiler_params=_cparams("parallel", "parallel", "arbitrary"),
        name="diff_attention",
    )(params, qt, *[a for kv in kvs for a in kv], subln_g.reshape(LANES, 1))


def _even_out_kernel(x_ref, of_ref, ob_ref, gate_ref, yd_ref, hg_ref, g1_ref, w_ref, *rest):
    router_refs, o_ref, moe_refs = rest[:-3], rest[-3], rest[-2:]
    tm = x_ref.shape[1]
    w = w_ref[0].astype(BF16)
    for r0, n in _row_groups(tm):
        rows = slice(r0, r0 + n)
        acc = jnp.dot(yd_ref[0, rows, :], w[HGRN_WIDTH:], preferred_element_type=F32)
        for h in range(HGRN_HEADS):
            sl = slice(h * HGRN_HEAD_DIM, (h + 1) * HGRN_HEAD_DIM)
            o = of_ref[0, rows, sl].astype(F32) + ob_ref[0, rows, sl].astype(F32)
            ms = jnp.mean(o * o, axis=-1, keepdims=True)
            gate = gate_ref[0, rows, sl].astype(F32)
            yh = (o * lax.rsqrt(ms + EPS)) * hg_ref[:, sl] * (gate * jax.nn.sigmoid(gate))
            acc = acc + jnp.dot(yh.astype(BF16), w[sl], preferred_element_type=F32)
        x_new = x_ref[0, rows, :] + g1_ref[0] * acc
        o_ref[0, rows, :] = x_new
        _router_outputs(x_new, r0, *router_refs, *moe_refs)


def even_out(x, o_f, o_b, proj, y_d, hgrn_g, g1, w_out, router, *, tm=1024):
    B, T, D = x.shape
    w_out, li = w_out
    tm = _row_tile(T, tm)
    Wd = HGRN_WIDTH
    row = lambda w, j: pl.BlockSpec((1, tm, w), lambda b, i: (b, i, j))
    r_args, r_in, r_out, r_shapes = _router_specs(B, T, D, tm, router)
    return pl.pallas_call(
        _even_out_kernel,
        grid=(B, T // tm),
        in_specs=[row(D, 0), row(Wd, 0), row(Wd, 0), row(Wd, 0), row(Wd, 0),
                  pl.BlockSpec((1, Wd), lambda b, i: (0, 0)),
                  pl.BlockSpec((1, 1, D), lambda b, i: (b, 0, 0)),
                  pl.BlockSpec((1,) + w_out.shape[1:], lambda b, i: (li, 0, 0))] + r_in,
        out_specs=[row(D, 0)] + r_out,
        out_shape=[jax.ShapeDtypeStruct((B, T, D), F32)] + r_shapes,
        compiler_params=_cparams("parallel", "parallel"),
        name="even_out",
    )(x, o_f, o_b, proj, y_d, hgrn_g.reshape(1, Wd), g1, w_out, *r_args)


def _conv_out_kernel(x_ref, bg_ref, cg_ref, v_ref, cp_ref, vp_ref, cn_ref, vn_ref, cw_ref, g1_ref, w_ref, *rest):
    router_refs, o_ref, moe_refs = rest[:-3], rest[-3], rest[-2:]
    i = pl.program_id(1)
    n = pl.num_programs(1)
    f32 = lambda ref, *idx: ref[idx].astype(F32)
    u = f32(cg_ref, 0) * f32(v_ref, 0)
    tm = u.shape[0]
    last = ROW_ALIGN - 1
    u_prev_row = jnp.where(i > 0, f32(cp_ref, 0, slice(last, last + 1)) * f32(vp_ref, 0, slice(last, last + 1)), 0.0)
    u_next_row = jnp.where(i < n - 1, f32(cn_ref, 0, slice(0, 1)) * f32(vn_ref, 0, slice(0, 1)), 0.0)
    ridx = lax.broadcasted_iota(jnp.int32, (tm, 1), 0)
    u_prev = jnp.where(ridx == 0, u_prev_row, pltpu.roll(u, 1, axis=0))
    u_next = jnp.where(ridx == tm - 1, u_next_row, pltpu.roll(u, tm - 1, axis=0))
    y = cw_ref[0:1, :] * u_prev + cw_ref[1:2, :] * u + cw_ref[2:3, :] * u_next
    w = w_ref[0].astype(BF16)
    for r0, rows in _row_groups(tm):
        sl = slice(r0, r0 + rows)
        acc = jnp.dot((f32(bg_ref, 0, sl) * y[sl]).astype(BF16), w, preferred_element_type=F32)
        x_new = x_ref[0, sl, :] + g1_ref[0] * acc
        o_ref[0, sl, :] = x_new
        _router_outputs(x_new, r0, *router_refs, *moe_refs)


def conv_out(x, proj, conv_w, g1, w_out, router, *, tm=1024):
    B, T, D = x.shape
    w_out, li = w_out
    tm = _row_tile(T, tm)
    rt = tm // ROW_ALIGN
    last_blk = T // ROW_ALIGN - 1
    row = lambda j: pl.BlockSpec((1, tm, D), lambda b, i: (b, i, j))
    prev = lambda j: pl.BlockSpec((1, ROW_ALIGN, D), lambda b, i: (b, jnp.maximum(i * rt - 1, 0), j))
    nxt = lambda j: pl.BlockSpec((1, ROW_ALIGN, D), lambda b, i: (b, jnp.minimum((i + 1) * rt, last_blk), j))
    r_args, r_in, r_out, r_shapes = _router_specs(B, T, D, tm, router)
    return pl.pallas_call(
        _conv_out_kernel,
        grid=(B, T // tm),
        in_specs=[row(0), row(0), row(1), row(2), prev(1), prev(2), nxt(1), nxt(2),
                  pl.BlockSpec((8, D), lambda b, i: (0, 0)),
                  pl.BlockSpec((1, 1, D), lambda b, i: (b, 0, 0)),
                  pl.BlockSpec((1,) + w_out.shape[1:], lambda b, i: (li, 0, 0))] + r_in,
        out_specs=[row(0)] + r_out,
        out_shape=[jax.ShapeDtypeStruct((B, T, D), F32)] + r_shapes,
        compiler_params=_cparams("parallel", "parallel"),
        name="conv_out",
    )(x, proj, proj, proj, proj, proj, proj, proj,
      jnp.concatenate([conv_w, jnp.zeros((8 - conv_w.shape[0], D), conv_w.dtype)], axis=0), g1, w_out, *r_args)


def _router_outputs(x, r0, g_ref, sh_ref, sc_ref, rwt_ref, h_ref, lg_ref):
    n = x.shape[0]
    ms = jnp.mean(x * x, axis=-1, keepdims=True)
    h = (x * lax.rsqrt(ms + EPS)) * g_ref[...]
    h = h * (1.0 + sc_ref[0]) + sh_ref[0]
    h_hi = h.astype(BF16)
    h_ref[0, r0:r0 + n, :] = h_hi
    h_lo = (h - h_hi.astype(F32)).astype(BF16)
    rwt = rwt_ref[...]
    w_hi = rwt.astype(BF16)
    w_lo = (rwt - w_hi.astype(F32)).astype(BF16)
    d = lambda a, b: lax.dot_general(a, b, (((1,), (1,)), ((), ())), preferred_element_type=F32)
    lg_ref[0, :, r0:r0 + n] = d(w_hi, h_hi) + d(w_lo, h_hi) + d(w_hi, h_lo)


def _router_specs(B, T, D, tm, router):
    g, shift, scale, router_w = router
    E = router_w.shape[1]
    vec = pl.BlockSpec((1, 1, D), lambda b, i: (b, 0, 0))
    return ((g.reshape(1, D), shift, scale, router_w.T),
            [pl.BlockSpec((1, D), lambda b, i: (0, 0)), vec, vec, pl.BlockSpec((E, D), lambda b, i: (0, 0))],
            [pl.BlockSpec((1, tm, D), lambda b, i: (b, i, 0)), pl.BlockSpec((1, E, tm), lambda b, i: (b, 0, i))],
            [jax.ShapeDtypeStruct((B, T, D), BF16), jax.ShapeDtypeStruct((B, E, T), F32)])


def _lane_prefix(flags, tri_tot):
    E, N = flags.shape
    carries = [jnp.zeros((E, LANES), F32)]
    out = []
    for j in range(N // LANES):
        r = jnp.dot(flags[:, j * LANES:(j + 1) * LANES].astype(BF16), tri_tot, preferred_element_type=F32)
        out.append(r[:, :LANES] + carries[-1])
        carries.append(carries[-1] + r[:, LANES:])
    return jnp.concatenate(out, axis=1), carries


def _route_kernel(lg_ref, tab_ref, rank_ref, ts_ref, *, cap, tt):
    lg = lg_ref[0]
    E, N = lg.shape
    p = jnp.exp(lg - jnp.max(lg, axis=0, keepdims=True))
    aff = p / jnp.sum(p, axis=0, keepdims=True)
    count = lambda m: jnp.sum(jnp.where(m, 1.0, 0.0), axis=1, keepdims=True)
    as_float = lambda i: pltpu.bitcast(i, F32)

    def refine_bits(i, thr):
        cand = thr | jnp.left_shift(jnp.int32(1), 30 - i)
        return jnp.where(count(aff >= as_float(cand)) >= cap, cand, thr)

    thr = lax.fori_loop(0, 31, refine_bits, jnp.zeros((E, 1), jnp.int32))

    def refine_mid(i, lo_hi):
        lo, hi = lo_hi
        mid = 0.5 * (lo + hi)
        up = count(aff >= mid) >= cap
        return jnp.where(up, mid, lo), jnp.where(up, hi, mid)

    lo, hi = lax.fori_loop(0, 24, refine_mid, (as_float(thr), as_float(jnp.maximum(thr + 1, 0x00800000))))
    gt = aff >= hi
    eq = (aff >= lo) & (aff < hi)
    r_i = lax.broadcasted_iota(jnp.int32, (LANES, 2 * LANES), 0)
    c_i = lax.broadcasted_iota(jnp.int32, (LANES, 2 * LANES), 1)
    tri_tot = jnp.where((r_i < c_i) | (c_i >= LANES), 1.0, 0.0).astype(BF16)
    eq_rank, _ = _lane_prefix(jnp.where(eq, 1.0, 0.0), tri_tot)
    sel = gt | (eq & (eq_rank < cap - count(gt)))
    rank, before = _lane_prefix(jnp.where(sel, 1.0, 0.0), tri_tot)
    rank = jnp.where(sel, rank, -1.0)
    rank_ref[0] = rank.astype(jnp.int32)
    tab_ref[0] = jnp.concatenate([aff, rank, jnp.zeros((LANES - 2 * E, N), F32)], axis=0).T
    lane = lax.broadcasted_iota(jnp.int32, (E, LANES), 1)
    ts = jnp.zeros((E, LANES), F32)
    for k in range(N // tt + 1):
        ts = jnp.where(lane == k, before[k * tt // LANES], ts)
    ts_ref[0] = ts.astype(jnp.int32)


def route(logits_t, cap, tt):
    B, E, N = logits_t.shape
    blk = pl.BlockSpec((1, E, N), lambda b: (b, 0, 0))
    return pl.pallas_call(
        functools.partial(_route_kernel, cap=cap, tt=tt),
        grid=(B,),
        in_specs=[blk],
        out_specs=[pl.BlockSpec((1, N, LANES), lambda b: (b, 0, 0)), blk,
                   pl.BlockSpec((1, E, LANES), lambda b: (b, 0, 0))],
        out_shape=[jax.ShapeDtypeStruct((B, N, LANES), F32), jax.ShapeDtypeStruct((B, E, N), jnp.int32),
                   jax.ShapeDtypeStruct((B, E, LANES), jnp.int32)],
        compiler_params=_cparams("parallel"),
        name="route",
    )(logits_t)


def _window(lo, w, win, cap):
    lower = (lo // ROW_ALIGN) * ROW_ALIGN + w * win
    return pl.multiple_of(jnp.minimum(lower, cap - win), ROW_ALIGN), lower


def _extra_windows(cap, tt, win):
    return -(-(min(cap, tt) + ROW_ALIGN - 1) // win) - 1


def _moe_gather_kernel(ts_ref, rank_ref, h_ref, xe_ref, acc_ref, *, win, tt):
    b, g = pl.program_id(0), pl.program_id(1)
    N = h_ref.shape[1]
    ne, cap = xe_ref.shape[1], xe_ref.shape[2]
    nt = N // tt
    acc_ref[...] = jnp.zeros_like(acc_ref)
    row = lax.broadcasted_iota(jnp.int32, (win, tt), 0)
    n_extra = _extra_windows(cap, tt, win)
    for ee in range(ne):
        base = ((b * pl.num_programs(1) + g) * ne + ee) * (nt + 1)

        def place(k, w, ee=ee, base=base):
            start, lower = _window(ts_ref[base + k], w, win, cap)
            c0 = k * tt if isinstance(k, int) else pl.multiple_of(k * tt, tt)
            rk = rank_ref[0, ee, :, pl.ds(c0, tt)]
            onehot = jnp.where(jnp.where(rk >= lower, rk, -1) == row + start, 1.0, 0.0).astype(BF16)
            acc_ref[ee, pl.ds(start, win), :] += jnp.dot(onehot, h_ref[0, pl.ds(c0, tt), :],
                                                         preferred_element_type=F32)

        span = lambda k, base=base: ts_ref[base + k + 1] - (ts_ref[base + k] // ROW_ALIGN) * ROW_ALIGN
        need = jnp.int32(0)
        for k in range(nt):
            place(k, 0)
            need = jnp.maximum(need, span(k))

        @pl.when(need > win)
        def _(place=place, span=span):
            def extra(i, carry):
                k, w = i // n_extra, i % n_extra + 1

                @pl.when(span(k) > w * win)
                def _():
                    place(k, w)
                return carry

            lax.fori_loop(0, nt * n_extra, extra, 0)

    xe_ref[0] = acc_ref[...].astype(BF16)


def moe_gather(ts_flat, rank, h, cap, *, tt):
    B, E, N = rank.shape
    D = h.shape[2]
    ne = GATHER_EXPERTS_PER_STEP
    return pl.pallas_call(
        functools.partial(_moe_gather_kernel, win=min(LANES, cap), tt=tt),
        grid_spec=pltpu.PrefetchScalarGridSpec(
            num_scalar_prefetch=1,
            grid=(B, E // ne),
            in_specs=[pl.BlockSpec((1, ne, 1, N), lambda b, e, ts: (b, e, 0, 0)),
                      pl.BlockSpec((1, N, D), lambda b, e, ts: (b, 0, 0))],
            out_specs=pl.BlockSpec((1, ne, cap, D), lambda b, e, ts: (b, e, 0, 0)),
            scratch_shapes=[pltpu.VMEM((ne, cap, D), F32)]),
        out_shape=jax.ShapeDtypeStruct((B, E, cap, D), BF16),
        compiler_params=_cparams("parallel", "arbitrary"),
        name="moe_gather",
    )(ts_flat, rank.reshape(B, E, 1, N), h)


def _expert_kernel(*refs, n_streams):
    xe_refs = refs[:n_streams]
    wg_ref, wu_ref, wd_ref = refs[n_streams:n_streams + 3]
    o_refs = refs[n_streams + 3:2 * n_streams + 3]
    wg_scr, wu_scr, wd_scr = refs[2 * n_streams + 3:]

    @pl.when(pl.program_id(1) == 0)
    def _():
        wg_scr[...] = wg_ref[0, 0].astype(BF16)
        wu_scr[...] = wu_ref[0, 0].astype(BF16)
        wd_scr[...] = wd_ref[0, 0].astype(BF16)

    def swiglu(x):
        a = jnp.dot(x, wg_scr[...], preferred_element_type=F32)
        u = jnp.dot(x, wu_scr[...], preferred_element_type=F32)
        yield
        hid = (a * jax.nn.sigmoid(a)) * u
        return jnp.dot(hid.astype(BF16), wd_scr[...], preferred_element_type=F32).astype(BF16)

    nb = xe_refs[0].shape[0]
    ys = _run_interleaved([swiglu(jnp.concatenate([r[s, 0] for r in xe_refs], axis=0)) for s in range(nb)])
    for s, y in enumerate(ys):
        r0 = 0
        for o_ref in o_refs:
            o_ref[s, 0] = y[r0:r0 + o_ref.shape[2]]
            r0 += o_ref.shape[2]


def expert_ffn(xes, w_gate, w_up, w_down, li):
    B, E, _, D = xes[0].shape
    FF = w_gate.shape[3]
    nb = FFN_SAMPLES_PER_STEP if B % FFN_SAMPLES_PER_STEP == 0 else 1
    rows = [pl.BlockSpec((nb, 1, xe.shape[2], D), lambda e, b: (b, e, 0, 0)) for xe in xes]
    return pl.pallas_call(
        functools.partial(_expert_kernel, n_streams=len(xes)),
        grid=(E, B // nb),
        in_specs=rows + [pl.BlockSpec((1, 1, D, FF), lambda e, b: (li, e, 0, 0)),
                         pl.BlockSpec((1, 1, D, FF), lambda e, b: (li, e, 0, 0)),
                         pl.BlockSpec((1, 1, FF, D), lambda e, b: (li, e, 0, 0))],
        out_specs=rows,
        out_shape=[jax.ShapeDtypeStruct(xe.shape, BF16) for xe in xes],
        scratch_shapes=[pltpu.VMEM((D, FF), BF16), pltpu.VMEM((D, FF), BF16), pltpu.VMEM((FF, D), BF16)],
        compiler_params=_cparams("arbitrary", "arbitrary"),
        name="expert_ffn",
    )(*xes, w_gate, w_up, w_down)


def _moe_combine_kernel(ts_ref, tab_ref, ye_ref, x_ref, g2_ref, o_ref, acc_ref, *, win, group):
    b, k = pl.program_id(0), pl.program_id(1)
    E, cap = ye_ref.shape[1], ye_ref.shape[2]
    tt = x_ref.shape[1]
    nt = pl.num_programs(1)
    col = lax.broadcasted_iota(jnp.int32, (tt, win), 1)
    lo = [ts_ref[(b * E + e) * (nt + 1) + k] for e in range(E)]
    hi = [ts_ref[(b * E + e) * (nt + 1) + k + 1] for e in range(E)]

    def contribution(w):
        total = None
        for g0 in range(0, E, group):
            lhs, rhs = [], []
            for e in range(g0, g0 + group):
                start, lower = _window(lo[e], w, win, cap)
                rk = tab_ref[0, :, E + e:E + e + 1].astype(jnp.int32)
                hit = jnp.where(rk >= lower, rk, -1) == col + start
                lhs.append(jnp.where(hit, tab_ref[0, :, e:e + 1], 0.0).astype(BF16))
                rhs.append(ye_ref[0, e, pl.ds(start, win), :])
            d = jnp.dot(jnp.concatenate(lhs, axis=1), jnp.concatenate(rhs, axis=0), preferred_element_type=F32)
            total = d if total is None else total + d
        return total

    acc_ref[...] = contribution(0)
    need = jnp.int32(0)
    for e in range(E):
        need = jnp.maximum(need, hi[e] - (lo[e] // ROW_ALIGN) * ROW_ALIGN)

    @pl.when(need > win)
    def _():
        def extra(w, carry):
            @pl.when(need > w * win)
            def _():
                acc_ref[...] += contribution(w)
            return carry

        lax.fori_loop(1, _extra_windows(cap, tt, win) + 1, extra, 0)

    o_ref[0] = x_ref[0] + g2_ref[0] * acc_ref[...]


def moe_combine(ts_flat, table, ye, x, g2, *, tt):
    B, N, D = x.shape
    E, cap = ye.shape[1], ye.shape[2]
    tok = lambda w: pl.BlockSpec((1, tt, w), lambda b, k, ts: (b, k, 0))
    return pl.pallas_call(
        functools.partial(_moe_combine_kernel, win=min(LANES, cap), group=4),
        grid_spec=pltpu.PrefetchScalarGridSpec(
            num_scalar_prefetch=1,
            grid=(B, N // tt),
            in_specs=[tok(LANES),
                      pl.BlockSpec((1, E, cap, D), lambda b, k, ts: (b, 0, 0, 0)),
                      tok(D),
                      pl.BlockSpec((1, 1, D), lambda b, k, ts: (b, 0, 0))],
            out_specs=tok(D),
            scratch_shapes=[pltpu.VMEM((tt, D), F32)]),
        out_shape=jax.ShapeDtypeStruct((B, N, D), F32),
        compiler_params=_cparams("parallel", "arbitrary"),
        name="moe_combine",
    )(ts_flat, table, ye, x, g2)


def moe_residual(streams, w_gate, w_up, w_down, li):
    routed = []
    for (x, h, logits), _ in streams:
        B, N, D = x.shape
        cap = EC_CAPACITY * N // N_EXPERTS
        tt = min(4 * LANES, N)
        table, rank, ts = route(logits, cap, tt)
        ts_flat = ts[:, :, :N // tt + 1].reshape(-1)
        routed.append((moe_gather(ts_flat, rank, h, cap, tt=tt), ts_flat, table, tt))
    yes = expert_ffn([r[0] for r in routed], w_gate, w_up, w_down, li)
    return [moe_combine(ts_flat, table, ye, x, gate2, tt=tt)
            for ((x, _, _), gate2), ye, (_, ts_flat, table, tt) in zip(streams, yes, routed)]


def lambda_init(layer):
    return 0.8 - 0.6 * math.exp(-0.3 * layer)


def even_layer(x, xc, mods, cmods, norm1_g, w_in, w_out, lb, hgrn_g, qn_g, kn_g, lam_vec, subln_g,
               lam_init, ctx_out, tables):
    sh1, sc1, g1, router = mods
    csh1, csc1, cg1, crouter = cmods
    B = x.shape[0]
    half = w_in[0].shape[2] // 2
    proj = norm_mod_matmul(x, norm1_g, sh1, sc1, w_in, cols=(0, half), tm=1024)
    proj2 = norm_mod_matmul(x, norm1_g, sh1, sc1, w_in, cols=(1, half), tm=1024, out_dtype=BF16)
    projc = norm_mod_matmul(xc, norm1_g, csh1, csc1, w_in, cols=(0, half))
    projc2 = norm_mod_matmul(xc, norm1_g, csh1, csc1, w_in, cols=(1, half), out_dtype=BF16)
    s0 = jnp.zeros((B, HGRN_HEADS, 2, HGRN_HEAD_DIM, HGRN_HEAD_DIM), F32)
    oc_f, oc_b, s_ctx = hgrn_scan(projc, lb, s0)
    o_f, o_b, _ = hgrn_scan(proj, lb, s_ctx)
    lv = lam_vec.astype(F32)
    lam = jnp.exp(jnp.sum(lv[0] * lv[1])) - jnp.exp(jnp.sum(lv[2] * lv[3])) + lam_init
    bound = 1.01 * math.sqrt(DIFF_HEAD_DIM) * jnp.max(jnp.abs(qn_g)) * jnp.max(jnp.abs(kn_g))
    lam = jnp.stack([lam, bound, (bound <= SCORE_BOUND_MAX).astype(F32)])
    cos, sin = tables
    q, k, v = qkv_prep(proj2, qn_g, kn_g, cos, sin, rotary=True)
    qc, kc, vc = qkv_prep(projc2, qn_g, kn_g, cos[:xc.shape[1]], sin[:xc.shape[1]], rotary=False)
    y_d = diff_attention(q, [(k, v), (kc, vc)], lam, subln_g, 1.0 - lam_init)
    x_new = even_out(x, o_f, o_b, proj2, y_d, hgrn_g.reshape(-1), g1, w_out, router)
    if not ctx_out:
        return x_new, None
    yc_d = diff_attention(qc, [(kc, vc)], lam, subln_g, 1.0 - lam_init)
    xc_new = even_out(xc, oc_f, oc_b, projc2, yc_d, hgrn_g.reshape(-1), cg1, w_out, crouter)
    return x_new, xc_new


def conv_layer(x, mods, norm1_g, w_in, conv_w, w_out):
    sh1, sc1, g1, router = mods
    proj = norm_mod_matmul(x, norm1_g, sh1, sc1, w_in, out_dtype=BF16)
    return conv_out(x, proj, conv_w, g1, w_out, router)


def kernel(x, c, ctx, c_ctx, mod_w, mod_b, norm1_g, norm2_g, even_w_in, even_w_out, hgrn_lb_logits, hgrn_norm_g,
           diff_qnorm_g, diff_knorm_g, diff_lambda, diff_subln_g, conv_w_in, conv_w, conv_w_out, router_w,
           exp_w_gate, exp_w_up, exp_w_down):
    depth = mod_w.shape[0]
    B, T, D = x.shape
    lb_soft = jax.nn.softmax(hgrn_lb_logits.astype(F32), axis=0)
    lower_bounds = jnp.cumsum(lb_soft, axis=0) - lb_soft[:1]
    last_ctx_layer = 2 * ((depth - 1) // 2)
    cond = jnp.concatenate([c, c_ctx[None, :], jnp.zeros((8 - (B + 1) % 8, D), F32)], axis=0)
    mods = modulation(jax.nn.silu(cond), mod_w, mod_b)
    tables = rope_tables(T)
    xc = ctx
    for l in range(depth):
        read_ctx = l <= last_ctx_layer
        ctx_out = l < last_ctx_layer
        sh1, sc1, g1, sh2, sc2, g2 = [m[:, None, :] for m in jnp.split(mods[l, :B], MOD_CHUNKS, axis=-1)]
        if read_ctx:
            csh1, csc1, cg1, csh2, csc2, cg2 = [
                jnp.broadcast_to(m[None, None, :], (B, 1, D)) for m in jnp.split(mods[l, B], MOD_CHUNKS, axis=-1)]
        mods_l = (sh1, sc1, g1, (norm2_g[l], sh2, sc2, router_w[l]))
        cmods_l = (csh1, csc1, cg1, (norm2_g[l], csh2, csc2, router_w[l])) if read_ctx else None
        if l % 2 == 0:
            e = l // 2
            s, sc = even_layer(x, xc, mods_l, cmods_l, norm1_g[l],
                               (even_w_in, e), (even_w_out, e), lower_bounds[e],
                               hgrn_norm_g[e], diff_qnorm_g[e], diff_knorm_g[e], diff_lambda[e],
                               diff_subln_g[e], lambda_init(l), ctx_out, tables)
        else:
            j = l // 2
            wi, wo = (conv_w_in, j), (conv_w_out, j)
            s = conv_layer(x, mods_l, norm1_g[l], wi, conv_w[j], wo)
            sc = conv_layer(xc, cmods_l, norm1_g[l], wi, conv_w[j], wo) if ctx_out else None
        outs = moe_residual([(s, g2)] + ([(sc, cg2)] if ctx_out else []), exp_w_gate, exp_w_up, exp_w_down, l)
        x = outs[0]
        if ctx_out:
            xc = outs[1]
    return x
```

```python
import functools
import math

import numpy as np
import jax
import jax.numpy as jnp
from jax import lax
from jax.experimental import pallas as pl
from jax.experimental.pallas import tpu as pltpu

F32 = jnp.float32
BF16 = jnp.bfloat16

EPS = 1e-6
GRID_W = 64
ROPE_THETA = 10000.0
HGRN_HEAD_DIM = 128
HGRN_HEADS = 4
HGRN_WIDTH = HGRN_HEADS * HGRN_HEAD_DIM
DIFF_HEAD_DIM = 64
DIFF_HEADS = 4
DIFF_WIDTH = DIFF_HEADS * 2 * DIFF_HEAD_DIM
N_EXPERTS = 16
EC_CAPACITY = 2
MOD_CHUNKS = 6
SCAN_CHUNK = 64
SCAN_LEVELS = (32, 16, 8, 4, 2, 1)
SCAN_HEADS_PER_STEP = 2
TILE_ROW_SPLIT = 1
GATHER_EXPERTS_PER_STEP = 2
FFN_SAMPLES_PER_STEP = 2
LANES = 128
ROW_ALIGN = 16
VMEM_LIMIT = 56 * 1024 * 1024
SCORE_BOUND_MAX = 40.0


def _cparams(*sem):
    return pltpu.CompilerParams(dimension_semantics=sem, vmem_limit_bytes=VMEM_LIMIT)


def _row_tile(t, want):
    return want if t % want == 0 else t


def _row_groups(tm):
    n = tm // TILE_ROW_SPLIT
    if tm % TILE_ROW_SPLIT or n % LANES:
        n = tm
    return [(r0, n) for r0 in range(0, tm, n)]


def _mod_kernel(s_ref, w_ref, b_ref, o_ref):
    s = s_ref[...]
    w = w_ref[0]
    s_hi = s.astype(BF16)
    s_lo = (s - s_hi.astype(F32)).astype(BF16)
    w_hi = w.astype(BF16)
    w_lo = (w - w_hi.astype(F32)).astype(BF16)
    d = lambda a, b: jnp.dot(a, b, preferred_element_type=F32)
    o_ref[0] = d(s_hi, w_hi) + d(s_hi, w_lo) + d(s_lo, w_hi) + b_ref[0]


def modulation(s, mod_w, mod_b, *, tn=2048):
    R, D = s.shape
    depth, _, N = mod_w.shape
    return pl.pallas_call(
        _mod_kernel,
        grid=(depth, N // tn),
        in_specs=[pl.BlockSpec((R, D), lambda l, j: (0, 0)),
                  pl.BlockSpec((1, D, tn), lambda l, j: (l, 0, j)),
                  pl.BlockSpec((1, 1, tn), lambda l, j: (l, 0, j))],
        out_specs=pl.BlockSpec((1, R, tn), lambda l, j: (l, 0, j)),
        out_shape=jax.ShapeDtypeStruct((depth, R, N), F32),
        compiler_params=_cparams("parallel", "parallel"),
        name="modulation",
    )(s, mod_w, mod_b.reshape(depth, 1, N))


def _nmm_kernel(x_ref, g_ref, sh_ref, sc_ref, w_ref, o_ref, w_scr):
    @pl.when((pl.program_id(1) == 0) & (pl.program_id(2) == 0))
    def _():
        w_scr[...] = w_ref[0].astype(BF16)

    for r0, rows in _row_groups(x_ref.shape[1]):
        x = x_ref[0, r0:r0 + rows, :]
        ms = jnp.mean(x * x, axis=-1, keepdims=True)
        h = (x * lax.rsqrt(ms + EPS)) * g_ref[...]
        h = h * (1.0 + sc_ref[0]) + sh_ref[0]
        o_ref[0, r0:r0 + rows, :] = jnp.dot(h.astype(BF16), w_scr[...],
                                            preferred_element_type=F32).astype(o_ref.dtype)


def norm_mod_matmul(x, g, shift, scale, w, *, cols=None, tm=512, out_dtype=F32):
    B, T, D = x.shape
    w, li = w
    first, N = cols if cols else (0, w.shape[2])
    tm = _row_tile(T, tm)
    tn = N
    return pl.pallas_call(
        _nmm_kernel,
        grid=(N // tn, B, T // tm),
        in_specs=[
            pl.BlockSpec((1, tm, D), lambda j, b, i: (b, i, 0)),
            pl.BlockSpec((1, D), lambda j, b, i: (0, 0)),
            pl.BlockSpec((1, 1, D), lambda j, b, i: (b, 0, 0)),
            pl.BlockSpec((1, 1, D), lambda j, b, i: (b, 0, 0)),
            pl.BlockSpec((1, D, tn), lambda j, b, i: (li, 0, first + j)),
        ],
        out_specs=pl.BlockSpec((1, tm, tn), lambda j, b, i: (b, i, j)),
        out_shape=jax.ShapeDtypeStruct((B, T, N), out_dtype),
        scratch_shapes=[pltpu.VMEM((D, tn), BF16)],
        compiler_params=_cparams("arbitrary", "arbitrary", "arbitrary"),
        name="norm_mod_matmul",
    )(x, g.reshape(1, D), shift, scale, w)


def _scan_constants():
    C = SCAN_CHUNK
    t = np.arange(C)[:, None]
    u = np.arange(C)[None, :]
    mats = [u <= t, u > t]
    masks = []
    for w in SCAN_LEVELS:
        m = (t // (2 * w)) * 2 * w + w - 1
        later = (t // w) % 2 == 1
        mats.append(np.where(later, (u > m) & (u <= t), (u > t) & (u <= m)))
        masks.append(later & ((u // w) % 2 == 0) & (u // (2 * w) == t // (2 * w)))
    masks = [t == u] + masks + [np.ones((C, C), bool)]
    a_f = np.stack(mats).astype(np.float32)
    m_f = np.stack(masks).astype(np.float32)
    a = np.stack([a_f, a_f[:, ::-1, ::-1]]).reshape(2, -1, C)
    m = np.stack([m_f, m_f[:, ::-1, ::-1]])
    m_pairs = np.concatenate([m[:, 0::2], m[:, 1::2]], axis=3)
    return np.concatenate([a, a], axis=2), m_pairs


def _scan_pair(q, z, v, lb, st, a2, mask_ref, d, later):
    C = SCAN_CHUNK
    W = HGRN_HEAD_DIM
    nl = len(SCAN_LEVELS)
    nt = lambda x, y: lax.dot_general(x, y, (((1,), (1,)), ((), ())), preferred_element_type=F32)
    nn = lambda x, y: jnp.dot(x, y, preferred_element_type=F32)
    e_abs = jnp.exp(-jnp.abs(z))
    r = 1.0 / (1.0 + e_abs)
    er = e_abs * r
    pos = z >= 0.0
    g2 = jnp.log2(lb + (1.0 - lb) * jnp.where(pos, r, er))
    k = (1.0 - lb) * jnp.where(pos, er, r)
    hi = g2.astype(BF16)
    lo = (g2 - hi.astype(F32)).astype(BF16)
    gs = jnp.concatenate([jnp.concatenate([hi[:C], hi[C:]], axis=1),
                          jnp.concatenate([lo[:C], lo[C:]], axis=1)], axis=0)
    x = nn(a2, gs)
    yield
    x = jnp.exp2(x)
    first, second = (1, 0) if d else (0, 1)
    vb = v.astype(BF16)
    sides, qt, kt, dec = [], [], [], []
    for c in (0, 1):
        qc, kc, xs = q[c * C:(c + 1) * C], k[c * C:(c + 1) * C], x[:, c * W:(c + 1) * W]
        ops = [(qc.astype(BF16), kc.astype(BF16))]
        for i in range(nl):
            qk = (jnp.where(later[i], qc, kc) * xs[(2 + i) * C:(3 + i) * C]).astype(BF16)
            ops.append((qk, qk))
        sides.append(ops)
        qt.append(qc * xs[0:C])
        kt.append(kc * xs[C:2 * C])
        dec.append(xs[0:1, :] if d else xs[C - 1:C, :])
    zero = jnp.zeros((C, W), BF16)
    sides[first].append((zero, zero))
    sides[second].append((qt[second].astype(BF16), kt[first].astype(BF16)))
    prods = []
    for ops in sides:
        pair = []
        for j in range(0, nl + 2, 2):
            (qa, ka), (qb, kb) = ops[j], ops[j + 1]
            rhs = jnp.concatenate([jnp.concatenate([ka, zero], axis=1), jnp.concatenate([zero, kb], axis=1)], axis=0)
            pair.append(nt(jnp.concatenate([qa, qb], axis=1), rhs))
        prods.append(pair)
        yield
    qt[second] = qt[second] * dec[first]
    kt[first] = kt[first] * dec[second]
    o_st = nt(jnp.concatenate([a.astype(BF16) for a in qt], axis=0), st.astype(BF16))
    upd = lax.dot_general(vb, jnp.concatenate(kt, axis=0).astype(BF16), (((0,), (0,)), ((), ())),
                          preferred_element_type=F32)
    yield
    o = [None, None]
    n_main = nl // 2
    for c, pair in enumerate(prods):
        main = mask_ref[d, 0] * pair[0]
        for j in range(1, n_main):
            main = main + mask_ref[d, j] * pair[j]
        last = mask_ref[d, n_main] * pair[n_main]
        vc, vf = vb[c * C:(c + 1) * C], vb[first * C:(first + 1) * C]
        o[c] = o_st[c * C:(c + 1) * C] + nn(jnp.concatenate([main, last], axis=1).astype(BF16),
                                            jnp.concatenate([vc, vc, vc, vf], axis=0))
    yield
    return jnp.concatenate(o, axis=0), st * (dec[0] * dec[1]) + upd


def _run_interleaved(gens):
    results = [None] * len(gens)
    live = list(range(len(gens)))
    while live:
        for i in list(live):
            try:
                next(gens[i])
            except StopIteration as done:
                results[i] = done.value
                live.remove(i)
    return results


def _scan_kernel(qf_ref, zf_ref, vf_ref, qb_ref, zb_ref, vb_ref, lbf_ref, lbb_ref, s0_ref, a_ref, mask_ref,
                 of_ref, ob_ref, sT_ref, st_scr, *, n_chunks):
    c = pl.program_id(2)
    C = SCAN_CHUNK

    @pl.when(c == 0)
    def _():
        st_scr[...] = s0_ref[0]

    W = HGRN_HEAD_DIM
    row = lax.broadcasted_iota(jnp.int32, (C, W), 0)
    later = [[(row // w) % 2 == 1 for w in SCAN_LEVELS], [((C - 1 - row) // w) % 2 == 1 for w in SCAN_LEVELS]]
    in_refs = [(qf_ref, zf_ref, vf_ref), (qb_ref, zb_ref, vb_ref)]
    lb_refs = [lbf_ref, lbb_ref]
    o_refs = [of_ref, ob_ref]
    n_pairs = n_chunks // 2
    heads = range(qf_ref.shape[2] // W)

    def body(i, carry):
        r0 = [pl.multiple_of(i * 2 * C, 2 * C), pl.multiple_of((n_pairs - 1 - i) * 2 * C, 2 * C)]
        chains = [(d, h) for h in heads for d in (0, 1)]
        ins = [[ref[0, pl.ds(r0[d], 2 * C), h * W:(h + 1) * W] for ref in in_refs[d]] for d, h in chains]
        sts = [st_scr[h, d] for d, h in chains]
        outs = _run_interleaved([
            _scan_pair(*x, lb_refs[d][:, h * W:(h + 1) * W], st, a_ref[d], mask_ref, d, later[d])
            for (d, h), x, st in zip(chains, ins, sts)])
        for (d, h), (o, st) in zip(chains, outs):
            o_refs[d][0, pl.ds(r0[d], 2 * C), h * W:(h + 1) * W] = o.astype(o_refs[d].dtype)
            st_scr[h, d] = st
        return carry

    lax.fori_loop(0, n_pairs, body, 0)

    @pl.when(c == pl.num_programs(2) - 1)
    def _():
        sT_ref[0] = st_scr[...]


def hgrn_scan(proj, lb, s0, *, tb=1024):
    B, T, _ = proj.shape
    tb = _row_tile(T, tb)
    nc = T // tb
    H = HGRN_HEADS
    hd = HGRN_HEAD_DIM
    hps = SCAN_HEADS_PER_STEP
    G = H // hps
    wd = hps * hd
    fwd = lambda grp: pl.BlockSpec((1, tb, wd), lambda b, h, c: (b, c, grp * G + h))
    bwd = lambda grp: pl.BlockSpec((1, tb, wd), lambda b, h, c: (b, nc - 1 - c, grp * G + h))
    kern = functools.partial(_scan_kernel, n_chunks=tb // SCAN_CHUNK)
    a2, masks = _scan_constants()
    return pl.pallas_call(
        kern,
        grid=(B, G, nc),
        in_specs=[fwd(0), fwd(1), fwd(3), bwd(0), bwd(2), bwd(3),
                  pl.BlockSpec((1, wd), lambda b, h, c: (0, h)),
                  pl.BlockSpec((1, wd), lambda b, h, c: (0, h)),
                  pl.BlockSpec((1, hps, 2, hd, hd), lambda b, h, c: (b, h, 0, 0, 0)),
                  pl.BlockSpec(a2.shape, lambda b, h, c: (0, 0, 0)),
                  pl.BlockSpec(masks.shape, lambda b, h, c: (0, 0, 0, 0))],
        out_specs=[pl.BlockSpec((1, tb, wd), lambda b, h, c: (b, c, h)),
                   pl.BlockSpec((1, tb, wd), lambda b, h, c: (b, nc - 1 - c, h)),
                   pl.BlockSpec((1, hps, 2, hd, hd), lambda b, h, c: (b, h, 0, 0, 0))],
        out_shape=[jax.ShapeDtypeStruct((B, T, HGRN_WIDTH), BF16),
                   jax.ShapeDtypeStruct((B, T, HGRN_WIDTH), BF16),
                   jax.ShapeDtypeStruct((B, H, 2, hd, hd), F32)],
        scratch_shapes=[pltpu.VMEM((hps, 2, hd, hd), F32)],
        compiler_params=_cparams("parallel", "parallel", "arbitrary"),
        name="hgrn_scan",
    )(proj, proj, proj, proj, proj, proj, lb[0:1], lb[1:2], s0, jnp.asarray(a2, BF16), jnp.asarray(masks, F32))


def _group_mean_sq(x, gmat):
    sq = x * x
    hi = sq.astype(BF16)
    lo = (sq - hi.astype(F32)).astype(BF16)
    return (jnp.dot(hi, gmat, preferred_element_type=F32) + jnp.dot(lo, gmat, preferred_element_type=F32))


def _qkv_prep_kernel(q_ref, k_ref, v_ref, qg_ref, kg_ref, cos_ref, sin_ref, qo_ref, ko_ref, vo_ref, *, rotary):
    W = LANES
    r_i = lax.broadcasted_iota(jnp.int32, (W, W), 0) // DIFF_HEAD_DIM
    c_i = lax.broadcasted_iota(jnp.int32, (W, W), 1) // DIFF_HEAD_DIM
    gmat = jnp.where(r_i == c_i, 1.0 / DIFF_HEAD_DIM, 0.0).astype(BF16)
    lane = lax.broadcasted_iota(jnp.int32, (1, W), 1)
    first = (lane % 32) < 16

    def prep(x, g, scale):
        y = (x * lax.rsqrt(_group_mean_sq(x, gmat) + EPS)) * g
        if rotary:
            partner = jnp.where(first, pltpu.roll(y, W - 16, axis=1), pltpu.roll(y, 16, axis=1))
            y = y * cos_ref[...] + partner * sin_ref[...]
        if scale != 1.0:
            y = y * scale
        return y

    for h in range(DIFF_HEADS):
        sl = slice(h * W, (h + 1) * W)
        qo_ref[0, sl, :] = prep(q_ref[0, :, sl].astype(F32), qg_ref[...], DIFF_HEAD_DIM ** -0.5).T.astype(BF16)
        ko_ref[0, :, sl] = prep(k_ref[0, :, sl].astype(F32), kg_ref[...], 1.0).astype(BF16)
        vo_ref[0, sl, :] = v_ref[0, :, sl].astype(F32).T.astype(BF16)


def qkv_prep(proj, qg, kg, cos, sin, *, rotary, tm=1024):
    B, T, _ = proj.shape
    tm = _row_tile(T, tm)
    Wd = DIFF_WIDTH
    col = lambda j: pl.BlockSpec((1, tm, Wd), lambda b, i: (b, i, j))
    vec = pl.BlockSpec((1, LANES), lambda b, i: (0, 0))
    tab = pl.BlockSpec((tm, LANES), lambda b, i: (i, 0))
    rows = pl.BlockSpec((1, tm, Wd), lambda b, i: (b, i, 0))
    cols = pl.BlockSpec((1, Wd, tm), lambda b, i: (b, 0, i))
    return pl.pallas_call(
        functools.partial(_qkv_prep_kernel, rotary=rotary),
        grid=(B, T // tm),
        in_specs=[col(1), col(2), col(3), vec, vec, tab, tab],
        out_specs=[cols, rows, cols],
        out_shape=[jax.ShapeDtypeStruct((B, Wd, T), BF16), jax.ShapeDtypeStruct((B, T, Wd), BF16),
                   jax.ShapeDtypeStruct((B, Wd, T), BF16)],
        compiler_params=_cparams("parallel", "parallel"),
        name="qkv_prep",
    )(proj, proj, proj, jnp.tile(qg, 2).reshape(1, LANES), jnp.tile(kg, 2).reshape(1, LANES), cos, sin)


def rope_tables(T):
    n = DIFF_HEAD_DIM // 2
    inv = 1.0 / (ROPE_THETA ** (jnp.arange(0, n, 2, dtype=F32) / n))
    t = jnp.arange(T)
    ang_r = (t // GRID_W).astype(F32)[:, None] * inv[None, :]
    ang_c = (t % GRID_W).astype(F32)[:, None] * inv[None, :]
    cos = jnp.concatenate([jnp.cos(ang_r)] * 2 + [jnp.cos(ang_c)] * 2, axis=-1)
    sin = jnp.concatenate([-jnp.sin(ang_r), jnp.sin(ang_r), -jnp.sin(ang_c), jnp.sin(ang_c)], axis=-1)
    return jnp.tile(cos, (1, 2)), jnp.tile(sin, (1, 2))


def _attn_tile(qt, kv_refs, lam, key_chunk, bound=None):
    tq = qt.shape[1]
    row = lax.broadcasted_iota(jnp.int32, (LANES, 1), 0)
    zero = jnp.zeros_like(qt)
    qq = jnp.concatenate([jnp.where(row < DIFF_HEAD_DIM, qt, zero),
                          jnp.where(row >= DIFF_HEAD_DIM, qt, zero)], axis=1)
    m = jnp.full((1, 2 * tq), -jnp.inf, F32)
    acc = [jnp.zeros((LANES + 16, tq), F32), jnp.zeros((LANES + 16, tq), F32)]
    ones = jnp.ones((16, key_chunk), BF16)
    chunks = [(k_ref, v_ref, c0, min(c0 + key_chunk, k_ref.shape[1]))
              for k_ref, v_ref in kv_refs for c0 in range(0, k_ref.shape[1], key_chunk)]
    scores = lambda c: jnp.dot(c[0][0, c[2]:c[3], :], qq, preferred_element_type=F32)
    s_next = scores(chunks[0])
    yield
    for n, (_, v_ref, c0, c1) in enumerate(chunks):
        s = s_next
        if n + 1 < len(chunks):
            s_next = scores(chunks[n + 1])
        vt1 = jnp.concatenate([v_ref[0, :, c0:c1], ones[:, :c1 - c0]], axis=0)
        if bound is None:
            m_new = jnp.maximum(m, jnp.max(s, axis=0, keepdims=True))
            alpha = jnp.exp(m - m_new)
            pb = jnp.exp((s - m_new).astype(BF16))
            m = m_new
        else:
            alpha = None
            pb = jnp.exp(s - bound).astype(BF16)
        for i in range(2):
            pv = jnp.dot(vt1, pb[:, i * tq:(i + 1) * tq], preferred_element_type=F32)
            acc[i] = acc[i] + pv if alpha is None else acc[i] * alpha[:, i * tq:(i + 1) * tq] + pv
        yield
    inv = [1.0 / a[LANES:LANES + 1] for a in acc]
    return acc[0][:LANES] * inv[0] - acc[1][:LANES] * (lam * inv[1])


def _diff_attn_kernel(par_ref, q_ref, *rest, out_scale, key_chunk, tq):
    kv_refs, g_ref, o_ref = list(zip(rest[:-2:2], rest[1:-2:2])), rest[-2], rest[-1]
    n_tiles = q_ref.shape[2] // tq

    def run(bound):
        outs = _run_interleaved([
            _attn_tile(q_ref[0, :, i * tq:(i + 1) * tq], kv_refs, par_ref[0], key_chunk, bound)
            for i in range(n_tiles)])
        for i, o in enumerate(outs):
            ms = jnp.mean(o * o, axis=0, keepdims=True)
            o_ref[0, i * tq:(i + 1) * tq, :] = ((o * lax.rsqrt(ms + EPS)) * g_ref[...] * out_scale).T.astype(BF16)

    @pl.when(par_ref[2] > 0.5)
    def _():
        run(par_ref[1])

    @pl.when(par_ref[2] <= 0.5)
    def _():
        run(None)


def diff_attention(qt, kvs, params, subln_g, out_scale, *, tq=256, tiles_per_step=4, key_chunk=512):
    B, Wd, T = qt.shape
    tq = _row_tile(T, tq)
    ts = _row_tile(T, tq * tiles_per_step)
    kv_specs = [spec for k, _ in kvs for spec in (
        pl.BlockSpec((1, k.shape[1], LANES), lambda b, h, i: (b, 0, h)),
        pl.BlockSpec((1, LANES, k.shape[1]), lambda b, h, i: (b, h, 0)))]
    return pl.pallas_call(
        functools.partial(_diff_attn_kernel, out_scale=out_scale, key_chunk=key_chunk, tq=tq),
        grid=(B, DIFF_HEADS, T // ts),
        in_specs=[pl.BlockSpec(memory_space=pltpu.SMEM),
                  pl.BlockSpec((1, LANES, ts), lambda b, h, i: (b, h, i))] + kv_specs + [
                  pl.BlockSpec((LANES, 1), lambda b, h, i: (0, 0))],
        out_specs=pl.BlockSpec((1, ts, LANES), lambda b, h, i: (b, i, h)),
        out_shape=jax.ShapeDtypeStruct((B, T, Wd), BF16),
        compiler_params=_cparams("parallel", "parallel", "arbitrary"),
        name="diff_attention",
    )(params, qt, *[a for kv in kvs for a in kv], subln_g.reshape(LANES, 1))


def _even_out_kernel(x_ref, of_ref, ob_ref, gate_ref, yd_ref, hg_ref, g1_ref, w_ref, *rest):
    router_refs, o_ref, moe_refs = rest[:-3], rest[-3], rest[-2:]
    tm = x_ref.shape[1]
    w = w_ref[0].astype(BF16)
    for r0, n in _row_groups(tm):
        rows = slice(r0, r0 + n)
        acc = jnp.dot(yd_ref[0, rows, :], w[HGRN_WIDTH:], preferred_element_type=F32)
        for h in range(HGRN_HEADS):
            sl = slice(h * HGRN_HEAD_DIM, (h + 1) * HGRN_HEAD_DIM)
            o = of_ref[0, rows, sl].astype(F32) + ob_ref[0, rows, sl].astype(F32)
            ms = jnp.mean(o * o, axis=-1, keepdims=True)
            gate = gate_ref[0, rows, sl].astype(F32)
            yh = (o * lax.rsqrt(ms + EPS)) * hg_ref[:, sl] * (gate * jax.nn.sigmoid(gate))
            acc = acc + jnp.dot(yh.astype(BF16), w[sl], preferred_element_type=F32)
        x_new = x_ref[0, rows, :] + g1_ref[0] * acc
        o_ref[0, rows, :] = x_new
        _router_outputs(x_new, r0, *router_refs, *moe_refs)


def even_out(x, o_f, o_b, proj, y_d, hgrn_g, g1, w_out, router, *, tm=1024):
    B, T, D = x.shape
    w_out, li = w_out
    tm = _row_tile(T, tm)
    Wd = HGRN_WIDTH
    row = lambda w, j: pl.BlockSpec((1, tm, w), lambda b, i: (b, i, j))
    r_args, r_in, r_out, r_shapes = _router_specs(B, T, D, tm, router)
    return pl.pallas_call(
        _even_out_kernel,
        grid=(B, T // tm),
        in_specs=[row(D, 0), row(Wd, 0), row(Wd, 0), row(Wd, 0), row(Wd, 0),
                  pl.BlockSpec((1, Wd), lambda b, i: (0, 0)),
                  pl.BlockSpec((1, 1, D), lambda b, i: (b, 0, 0)),
                  pl.BlockSpec((1,) + w_out.shape[1:], lambda b, i: (li, 0, 0))] + r_in,
        out_specs=[row(D, 0)] + r_out,
        out_shape=[jax.ShapeDtypeStruct((B, T, D), F32)] + r_shapes,
        compiler_params=_cparams("parallel", "parallel"),
        name="even_out",
    )(x, o_f, o_b, proj, y_d, hgrn_g.reshape(1, Wd), g1, w_out, *r_args)


def _conv_out_kernel(x_ref, bg_ref, cg_ref, v_ref, cp_ref, vp_ref, cn_ref, vn_ref, cw_ref, g1_ref, w_ref, *rest):
    router_refs, o_ref, moe_refs = rest[:-3], rest[-3], rest[-2:]
    i = pl.program_id(1)
    n = pl.num_programs(1)
    f32 = lambda ref, *idx: ref[idx].astype(F32)
    u = f32(cg_ref, 0) * f32(v_ref, 0)
    tm = u.shape[0]
    last = ROW_ALIGN - 1
    u_prev_row = jnp.where(i > 0, f32(cp_ref, 0, slice(last, last + 1)) * f32(vp_ref, 0, slice(last, last + 1)), 0.0)
    u_next_row = jnp.where(i < n - 1, f32(cn_ref, 0, slice(0, 1)) * f32(vn_ref, 0, slice(0, 1)), 0.0)
    ridx = lax.broadcasted_iota(jnp.int32, (tm, 1), 0)
    u_prev = jnp.where(ridx == 0, u_prev_row, pltpu.roll(u, 1, axis=0))
    u_next = jnp.where(ridx == tm - 1, u_next_row, pltpu.roll(u, tm - 1, axis=0))
    y = cw_ref[0:1, :] * u_prev + cw_ref[1:2, :] * u + cw_ref[2:3, :] * u_next
    w = w_ref[0].astype(BF16)
    for r0, rows in _row_groups(tm):
        sl = slice(r0, r0 + rows)
        acc = jnp.dot((f32(bg_ref, 0, sl) * y[sl]).astype(BF16), w, preferred_element_type=F32)
        x_new = x_ref[0, sl, :] + g1_ref[0] * acc
        o_ref[0, sl, :] = x_new
        _router_outputs(x_new, r0, *router_refs, *moe_refs)


def conv_out(x, proj, conv_w, g1, w_out, router, *, tm=1024):
    B, T, D = x.shape
    w_out, li = w_out
    tm = _row_tile(T, tm)
    rt = tm // ROW_ALIGN
    last_blk = T // ROW_ALIGN - 1
    row = lambda j: pl.BlockSpec((1, tm, D), lambda b, i: (b, i, j))
    prev = lambda j: pl.BlockSpec((1, ROW_ALIGN, D), lambda b, i: (b, jnp.maximum(i * rt - 1, 0), j))
    nxt = lambda j: pl.BlockSpec((1, ROW_ALIGN, D), lambda b, i: (b, jnp.minimum((i + 1) * rt, last_blk), j))
    r_args, r_in, r_out, r_shapes = _router_specs(B, T, D, tm, router)
    return pl.pallas_call(
        _conv_out_kernel,
        grid=(B, T // tm),
        in_specs=[row(0), row(0), row(1), row(2), prev(1), prev(2), nxt(1), nxt(2),
                  pl.BlockSpec((8, D), lambda b, i: (0, 0)),
                  pl.BlockSpec((1, 1, D), lambda b, i: (b, 0, 0)),
                  pl.BlockSpec((1,) + w_out.shape[1:], lambda b, i: (li, 0, 0))] + r_in,
        out_specs=[row(0)] + r_out,
        out_shape=[jax.ShapeDtypeStruct((B, T, D), F32)] + r_shapes,
        compiler_params=_cparams("parallel", "parallel"),
        name="conv_out",
    )(x, proj, proj, proj, proj, proj, proj, proj,
      jnp.concatenate([conv_w, jnp.zeros((8 - conv_w.shape[0], D), conv_w.dtype)], axis=0), g1, w_out, *r_args)


def _router_outputs(x, r0, g_ref, sh_ref, sc_ref, rwt_ref, h_ref, lg_ref):
    n = x.shape[0]
    ms = jnp.mean(x * x, axis=-1, keepdims=True)
    h = (x * lax.rsqrt(ms + EPS)) * g_ref[...]
    h = h * (1.0 + sc_ref[0]) + sh_ref[0]
    h_hi = h.astype(BF16)
    h_ref[0, r0:r0 + n, :] = h_hi
    h_lo = (h - h_hi.astype(F32)).astype(BF16)
    rwt = rwt_ref[...]
    w_hi = rwt.astype(BF16)
    w_lo = (rwt - w_hi.astype(F32)).astype(BF16)
    d = lambda a, b: lax.dot_general(a, b, (((1,), (1,)), ((), ())), preferred_element_type=F32)
    lg_ref[0, :, r0:r0 + n] = d(w_hi, h_hi) + d(w_lo, h_hi) + d(w_hi, h_lo)


def _router_specs(B, T, D, tm, router):
    g, shift, scale, router_w = router
    E = router_w.shape[1]
    vec = pl.BlockSpec((1, 1, D), lambda b, i: (b, 0, 0))
    return ((g.reshape(1, D), shift, scale, router_w.T),
            [pl.BlockSpec((1, D), lambda b, i: (0, 0)), vec, vec, pl.BlockSpec((E, D), lambda b, i: (0, 0))],
            [pl.BlockSpec((1, tm, D), lambda b, i: (b, i, 0)), pl.BlockSpec((1, E, tm), lambda b, i: (b, 0, i))],
            [jax.ShapeDtypeStruct((B, T, D), BF16), jax.ShapeDtypeStruct((B, E, T), F32)])


def _lane_prefix(flags, tri_tot):
    E, N = flags.shape
    carries = [jnp.zeros((E, LANES), F32)]
    out = []
    for j in range(N // LANES):
        r = jnp.dot(flags[:, j * LANES:(j + 1) * LANES].astype(BF16), tri_tot, preferred_element_type=F32)
        out.append(r[:, :LANES] + carries[-1])
        carries.append(carries[-1] + r[:, LANES:])
    return jnp.concatenate(out, axis=1), carries


def _route_kernel(lg_ref, tab_ref, rank_ref, ts_ref, *, cap, tt):
    lg = lg_ref[...]
    B, E, N = lg.shape
    p = jnp.exp(lg - jnp.max(lg, axis=1, keepdims=True))
    R = B * E
    aff = (p / jnp.sum(p, axis=1, keepdims=True)).reshape(R, N)
    count = lambda m: jnp.sum(jnp.where(m, 1.0, 0.0), axis=1, keepdims=True)
    as_float = lambda i: pltpu.bitcast(i, F32)

    def refine_bits(i, thr):
        cand = thr | jnp.left_shift(jnp.int32(1), 30 - i)
        return jnp.where(count(aff >= as_float(cand)) >= cap, cand, thr)

    thr = lax.fori_loop(0, 31, refine_bits, jnp.zeros((R, 1), jnp.int32))

    def refine_mid(i, lo_hi):
        lo, hi = lo_hi
        mid = 0.5 * (lo + hi)
        up = count(aff >= mid) >= cap
        return jnp.where(up, mid, lo), jnp.where(up, hi, mid)

    lo, hi = lax.fori_loop(0, 24, refine_mid, (as_float(thr), as_float(jnp.maximum(thr + 1, 0x00800000))))
    gt = aff >= hi
    eq = (aff >= lo) & (aff < hi)
    r_i = lax.broadcasted_iota(jnp.int32, (LANES, 2 * LANES), 0)
    c_i = lax.broadcasted_iota(jnp.int32, (LANES, 2 * LANES), 1)
    tri_tot = jnp.where((r_i < c_i) | (c_i >= LANES), 1.0, 0.0).astype(BF16)
    eq_rank, _ = _lane_prefix(jnp.where(eq, 1.0, 0.0), tri_tot)
    sel = gt | (eq & (eq_rank < cap - count(gt)))
    rank, before = _lane_prefix(jnp.where(sel, 1.0, 0.0), tri_tot)
    rank = jnp.where(sel, rank, -1.0)
    rank_ref[...] = rank.astype(jnp.int32).reshape(B, E, N)
    pad = jnp.zeros((LANES - 2 * E, N), F32)
    for b in range(B):
        rows = slice(b * E, (b + 1) * E)
        tab_ref[b] = jnp.concatenate([aff[rows], rank[rows], pad], axis=0).T
    lane = lax.broadcasted_iota(jnp.int32, (R, LANES), 1)
    ts = jnp.zeros((R, LANES), F32)
    for k in range(N // tt + 1):
        ts = jnp.where(lane == k, before[k * tt // LANES], ts)
    ts_ref[...] = ts.astype(jnp.int32).reshape(B, E, LANES)


def route(logits_t, cap, tt):
    B, E, N = logits_t.shape
    whole = lambda *shape: pl.BlockSpec(shape, lambda i: (0,) * len(shape))
    return pl.pallas_call(
        functools.partial(_route_kernel, cap=cap, tt=tt),
        grid=(1,),
        in_specs=[whole(B, E, N)],
        out_specs=[whole(B, N, LANES), whole(B, E, N), whole(B, E, LANES)],
        out_shape=[jax.ShapeDtypeStruct((B, N, LANES), F32), jax.ShapeDtypeStruct((B, E, N), jnp.int32),
                   jax.ShapeDtypeStruct((B, E, LANES), jnp.int32)],
        compiler_params=_cparams("arbitrary"),
        name="route",
    )(logits_t)


def _window(lo, w, win, cap):
    lower = (lo // ROW_ALIGN) * ROW_ALIGN + w * win
    return pl.multiple_of(jnp.minimum(lower, cap - win), ROW_ALIGN), lower


def _extra_windows(cap, tt, win):
    return -(-(min(cap, tt) + ROW_ALIGN - 1) // win) - 1


def _moe_gather_kernel(ts_ref, rank_ref, h_ref, xe_ref, acc_ref, *, win, tt):
    b, g = pl.program_id(0), pl.program_id(1)
    N = h_ref.shape[1]
    ne, cap = xe_ref.shape[1], xe_ref.shape[2]
    nt = N // tt
    acc_ref[...] = jnp.zeros_like(acc_ref)
    row = lax.broadcasted_iota(jnp.int32, (win, tt), 0)
    n_extra = _extra_windows(cap, tt, win)
    for ee in range(ne):
        base = ((b * pl.num_programs(1) + g) * ne + ee) * (nt + 1)

        def place(k, w, ee=ee, base=base):
            start, lower = _window(ts_ref[base + k], w, win, cap)
            c0 = k * tt if isinstance(k, int) else pl.multiple_of(k * tt, tt)
            rk = rank_ref[0, ee, :, pl.ds(c0, tt)]
            onehot = jnp.where(jnp.where(rk >= lower, rk, -1) == row + start, 1.0, 0.0).astype(BF16)
            acc_ref[ee, pl.ds(start, win), :] += jnp.dot(onehot, h_ref[0, pl.ds(c0, tt), :],
                                                         preferred_element_type=F32)

        span = lambda k, base=base: ts_ref[base + k + 1] - (ts_ref[base + k] // ROW_ALIGN) * ROW_ALIGN
        need = jnp.int32(0)
        for k in range(nt):
            place(k, 0)
            need = jnp.maximum(need, span(k))

        @pl.when(need > win)
        def _(place=place, span=span):
            def extra(i, carry):
                k, w = i // n_extra, i % n_extra + 1

                @pl.when(span(k) > w * win)
                def _():
                    place(k, w)
                return carry

            lax.fori_loop(0, nt * n_extra, extra, 0)

    xe_ref[0] = acc_ref[...].astype(BF16)


def moe_gather(ts_flat, rank, h, cap, *, tt):
    B, E, N = rank.shape
    D = h.shape[2]
    ne = GATHER_EXPERTS_PER_STEP
    return pl.pallas_call(
        functools.partial(_moe_gather_kernel, win=min(LANES, cap), tt=tt),
        grid_spec=pltpu.PrefetchScalarGridSpec(
            num_scalar_prefetch=1,
            grid=(B, E // ne),
            in_specs=[pl.BlockSpec((1, ne, 1, N), lambda b, e, ts: (b, e, 0, 0)),
                      pl.BlockSpec((1, N, D), lambda b, e, ts: (b, 0, 0))],
            out_specs=pl.BlockSpec((1, ne, cap, D), lambda b, e, ts: (b, e, 0, 0)),
            scratch_shapes=[pltpu.VMEM((ne, cap, D), F32)]),
        out_shape=jax.ShapeDtypeStruct((B, E, cap, D), BF16),
        compiler_params=_cparams("parallel", "arbitrary"),
        name="moe_gather",
    )(ts_flat, rank.reshape(B, E, 1, N), h)


def _expert_kernel(*refs, n_streams):
    xe_refs = refs[:n_streams]
    wg_ref, wu_ref, wd_ref = refs[n_streams:n_streams + 3]
    o_refs = refs[n_streams + 3:2 * n_streams + 3]
    wg_scr, wu_scr, wd_scr = refs[2 * n_streams + 3:]

    @pl.when(pl.program_id(1) == 0)
    def _():
        wg_scr[...] = wg_ref[0, 0].astype(BF16)
        wu_scr[...] = wu_ref[0, 0].astype(BF16)
        wd_scr[...] = wd_ref[0, 0].astype(BF16)

    def swiglu(x):
        a = jnp.dot(x, wg_scr[...], preferred_element_type=F32)
        u = jnp.dot(x, wu_scr[...], preferred_element_type=F32)
        yield
        hid = (a * jax.nn.sigmoid(a)) * u
        return jnp.dot(hid.astype(BF16), wd_scr[...], preferred_element_type=F32).astype(BF16)

    nb = xe_refs[0].shape[0]
    ys = _run_interleaved([swiglu(jnp.concatenate([r[s, 0] for r in xe_refs], axis=0)) for s in range(nb)])
    for s, y in enumerate(ys):
        r0 = 0
        for o_ref in o_refs:
            o_ref[s, 0] = y[r0:r0 + o_ref.shape[2]]
            r0 += o_ref.shape[2]


def expert_ffn(xes, w_gate, w_up, w_down, li):
    B, E, _, D = xes[0].shape
    FF = w_gate.shape[3]
    nb = FFN_SAMPLES_PER_STEP if B % FFN_SAMPLES_PER_STEP == 0 else 1
    rows = [pl.BlockSpec((nb, 1, xe.shape[2], D), lambda e, b: (b, e, 0, 0)) for xe in xes]
    return pl.pallas_call(
        functools.partial(_expert_kernel, n_streams=len(xes)),
        grid=(E, B // nb),
        in_specs=rows + [pl.BlockSpec((1, 1, D, FF), lambda e, b: (li, e, 0, 0)),
                         pl.BlockSpec((1, 1, D, FF), lambda e, b: (li, e, 0, 0)),
                         pl.BlockSpec((1, 1, FF, D), lambda e, b: (li, e, 0, 0))],
        out_specs=rows,
        out_shape=[jax.ShapeDtypeStruct(xe.shape, BF16) for xe in xes],
        scratch_shapes=[pltpu.VMEM((D, FF), BF16), pltpu.VMEM((D, FF), BF16), pltpu.VMEM((FF, D), BF16)],
        compiler_params=_cparams("arbitrary", "arbitrary"),
        name="expert_ffn",
    )(*xes, w_gate, w_up, w_down)


def _moe_combine_kernel(ts_ref, tab_ref, ye_ref, x_ref, g2_ref, o_ref, acc_ref, *, win, group):
    b, k = pl.program_id(0), pl.program_id(1)
    E, cap = ye_ref.shape[1], ye_ref.shape[2]
    tt = x_ref.shape[1]
    nt = pl.num_programs(1)
    col = lax.broadcasted_iota(jnp.int32, (tt, win), 1)
    lo = [ts_ref[(b * E + e) * (nt + 1) + k] for e in range(E)]
    hi = [ts_ref[(b * E + e) * (nt + 1) + k + 1] for e in range(E)]

    def contribution(w):
        total = None
        for g0 in range(0, E, group):
            lhs, rhs = [], []
            for e in range(g0, g0 + group):
                start, lower = _window(lo[e], w, win, cap)
                rk = tab_ref[0, :, E + e:E + e + 1].astype(jnp.int32)
                hit = jnp.where(rk >= lower, rk, -1) == col + start
                lhs.append(jnp.where(hit, tab_ref[0, :, e:e + 1], 0.0).astype(BF16))
                rhs.append(ye_ref[0, e, pl.ds(start, win), :])
            d = jnp.dot(jnp.concatenate(lhs, axis=1), jnp.concatenate(rhs, axis=0), preferred_element_type=F32)
            total = d if total is None else total + d
        return total

    acc_ref[...] = contribution(0)
    need = jnp.int32(0)
    for e in range(E):
        need = jnp.maximum(need, hi[e] - (lo[e] // ROW_ALIGN) * ROW_ALIGN)

    @pl.when(need > win)
    def _():
        def extra(w, carry):
            @pl.when(need > w * win)
            def _():
                acc_ref[...] += contribution(w)
            return carry

        lax.fori_loop(1, _extra_windows(cap, tt, win) + 1, extra, 0)

    o_ref[0] = x_ref[0] + g2_ref[0] * acc_ref[...]


def moe_combine(ts_flat, table, ye, x, g2, *, tt):
    B, N, D = x.shape
    E, cap = ye.shape[1], ye.shape[2]
    tok = lambda w: pl.BlockSpec((1, tt, w), lambda b, k, ts: (b, k, 0))
    return pl.pallas_call(
        functools.partial(_moe_combine_kernel, win=min(LANES, cap), group=4),
        grid_spec=pltpu.PrefetchScalarGridSpec(
            num_scalar_prefetch=1,
            grid=(B, N // tt),
            in_specs=[tok(LANES),
                      pl.BlockSpec((1, E, cap, D), lambda b, k, ts: (b, 0, 0, 0)),
                      tok(D),
                      pl.BlockSpec((1, 1, D), lambda b, k, ts: (b, 0, 0))],
            out_specs=tok(D),
            scratch_shapes=[pltpu.VMEM((tt, D), F32)]),
        out_shape=jax.ShapeDtypeStruct((B, N, D), F32),
        compiler_params=_cparams("parallel", "arbitrary"),
        name="moe_combine",
    )(ts_flat, table, ye, x, g2)


def moe_residual(streams, w_gate, w_up, w_down, li):
    routed = []
    for (x, h, logits), _ in streams:
        B, N, D = x.shape
        cap = EC_CAPACITY * N // N_EXPERTS
        tt = min(4 * LANES, N)
        table, rank, ts = route(logits, cap, tt)
        ts_flat = ts[:, :, :N // tt + 1].reshape(-1)
        routed.append((moe_gather(ts_flat, rank, h, cap, tt=tt), ts_flat, table, tt))
    yes = expert_ffn([r[0] for r in routed], w_gate, w_up, w_down, li)
    return [moe_combine(ts_flat, table, ye, x, gate2, tt=tt)
            for ((x, _, _), gate2), ye, (_, ts_flat, table, tt) in zip(streams, yes, routed)]


def lambda_init(layer):
    return 0.8 - 0.6 * math.exp(-0.3 * layer)


def even_layer(x, xc, mods, cmods, norm1_g, w_in, w_out, lb, hgrn_g, qn_g, kn_g, lam_vec, subln_g,
               lam_init, ctx_out, tables):
    sh1, sc1, g1, router = mods
    csh1, csc1, cg1, crouter = cmods
    B = x.shape[0]
    half = w_in[0].shape[2] // 2
    proj = norm_mod_matmul(x, norm1_g, sh1, sc1, w_in, cols=(0, half), tm=1024)
    proj2 = norm_mod_matmul(x, norm1_g, sh1, sc1, w_in, cols=(1, half), tm=1024, out_dtype=BF16)
    projc = norm_mod_matmul(xc, norm1_g, csh1, csc1, w_in, cols=(0, half))
    projc2 = norm_mod_matmul(xc, norm1_g, csh1, csc1, w_in, cols=(1, half), out_dtype=BF16)
    s0 = jnp.zeros((B, HGRN_HEADS, 2, HGRN_HEAD_DIM, HGRN_HEAD_DIM), F32)
    oc_f, oc_b, s_ctx = hgrn_scan(projc, lb, s0)
    o_f, o_b, _ = hgrn_scan(proj, lb, s_ctx)
    lv = lam_vec.astype(F32)
    lam = jnp.exp(jnp.sum(lv[0] * lv[1])) - jnp.exp(jnp.sum(lv[2] * lv[3])) + lam_init
    bound = 1.01 * math.sqrt(DIFF_HEAD_DIM) * jnp.max(jnp.abs(qn_g)) * jnp.max(jnp.abs(kn_g))
    lam = jnp.stack([lam, bound, (bound <= SCORE_BOUND_MAX).astype(F32)])
    cos, sin = tables
    q, k, v = qkv_prep(proj2, qn_g, kn_g, cos, sin, rotary=True)
    qc, kc, vc = qkv_prep(projc2, qn_g, kn_g, cos[:xc.shape[1]], sin[:xc.shape[1]], rotary=False)
    y_d = diff_attention(q, [(k, v), (kc, vc)], lam, subln_g, 1.0 - lam_init)
    x_new = even_out(x, o_f, o_b, proj2, y_d, hgrn_g.reshape(-1), g1, w_out, router)
    if not ctx_out:
        return x_new, None
    yc_d = diff_attention(qc, [(kc, vc)], lam, subln_g, 1.0 - lam_init)
    xc_new = even_out(xc, oc_f, oc_b, projc2, yc_d, hgrn_g.reshape(-1), cg1, w_out, crouter)
    return x_new, xc_new


def conv_layer(x, mods, norm1_g, w_in, conv_w, w_out):
    sh1, sc1, g1, router = mods
    proj = norm_mod_matmul(x, norm1_g, sh1, sc1, w_in, out_dtype=BF16)
    return conv_out(x, proj, conv_w, g1, w_out, router)


def kernel(x, c, ctx, c_ctx, mod_w, mod_b, norm1_g, norm2_g, even_w_in, even_w_out, hgrn_lb_logits, hgrn_norm_g,
           diff_qnorm_g, diff_knorm_g, diff_lambda, diff_subln_g, conv_w_in, conv_w, conv_w_out, router_w,
           exp_w_gate, exp_w_up, exp_w_down):
    depth = mod_w.shape[0]
    B, T, D = x.shape
    lb_soft = jax.nn.softmax(hgrn_lb_logits.astype(F32), axis=0)
    lower_bounds = jnp.cumsum(lb_soft, axis=0) - lb_soft[:1]
    last_ctx_layer = 2 * ((depth - 1) // 2)
    cond = jnp.concatenate([c, c_ctx[None, :], jnp.zeros((8 - (B + 1) % 8, D), F32)], axis=0)
    mods = modulation(jax.nn.silu(cond), mod_w, mod_b)
    tables = rope_tables(T)
    xc = ctx
    for l in range(depth):
        read_ctx = l <= last_ctx_layer
        ctx_out = l < last_ctx_layer
        sh1, sc1, g1, sh2, sc2, g2 = [m[:, None, :] for m in jnp.split(mods[l, :B], MOD_CHUNKS, axis=-1)]
        if read_ctx:
            csh1, csc1, cg1, csh2, csc2, cg2 = [
                jnp.broadcast_to(m[None, None, :], (B, 1, D)) for m in jnp.split(mods[l, B], MOD_CHUNKS, axis=-1)]
        mods_l = (sh1, sc1, g1, (norm2_g[l], sh2, sc2, router_w[l]))
        cmods_l = (csh1, csc1, cg1, (norm2_g[l], csh2, csc2, router_w[l])) if read_ctx else None
        if l % 2 == 0:
            e = l // 2
            s, sc = even_layer(x, xc, mods_l, cmods_l, norm1_g[l],
                               (even_w_in, e), (even_w_out, e), lower_bounds[e],
                               hgrn_norm_g[e], diff_qnorm_g[e], diff_knorm_g[e], diff_lambda[e],
                               diff_subln_g[e], lambda_init(l), ctx_out, tables)
        else:
            j = l // 2
            wi, wo = (conv_w_in, j), (conv_w_out, j)
            s = conv_layer(x, mods_l, norm1_g[l], wi, conv_w[j], wo)
            sc = conv_layer(xc, cmods_l, norm1_g[l], wi, conv_w[j], wo) if ctx_out else None
        outs = moe_residual([(s, g2)] + ([(sc, cg2)] if ctx_out else []), exp_w_gate, exp_w_up, exp_w_down, l)
        x = outs[0]
        if ctx_out:
            xc = outs[1]
    return x
```

```python
import functools
import math

import numpy as np
import jax
import jax.numpy as jnp
from jax import lax
from jax.experimental import pallas as pl
from jax.experimental.pallas import tpu as pltpu

F32 = jnp.float32
BF16 = jnp.bfloat16

EPS = 1e-6
GRID_W = 64
ROPE_THETA = 10000.0
HGRN_HEAD_DIM = 128
HGRN_HEADS = 4
HGRN_WIDTH = HGRN_HEADS * HGRN_HEAD_DIM
DIFF_HEAD_DIM = 64
DIFF_HEADS = 4
DIFF_WIDTH = DIFF_HEADS * 2 * DIFF_HEAD_DIM
N_EXPERTS = 16
EC_CAPACITY = 2
MOD_CHUNKS = 6
SCAN_CHUNK = 64
SCAN_LEVELS = (32, 16, 8, 4, 2, 1)
SCAN_HEADS_PER_STEP = 2
TILE_ROW_SPLIT = 1
GATHER_EXPERTS_PER_STEP = 2
FFN_SAMPLES_PER_STEP = 4
LANES = 128
ROW_ALIGN = 16
VMEM_LIMIT = 56 * 1024 * 1024
SCORE_BOUND_MAX = 40.0


def _cparams(*sem):
    return pltpu.CompilerParams(dimension_semantics=sem, vmem_limit_bytes=VMEM_LIMIT)


def _row_tile(t, want):
    return want if t % want == 0 else t


def _row_groups(tm):
    n = tm // TILE_ROW_SPLIT
    if tm % TILE_ROW_SPLIT or n % LANES:
        n = tm
    return [(r0, n) for r0 in range(0, tm, n)]


def _mod_kernel(s_ref, w_ref, b_ref, o_ref):
    s = s_ref[...]
    w = w_ref[0]
    s_hi = s.astype(BF16)
    s_lo = (s - s_hi.astype(F32)).astype(BF16)
    w_hi = w.astype(BF16)
    w_lo = (w - w_hi.astype(F32)).astype(BF16)
    d = lambda a, b: jnp.dot(a, b, preferred_element_type=F32)
    o_ref[0] = d(s_hi, w_hi) + d(s_hi, w_lo) + d(s_lo, w_hi) + b_ref[0]


def modulation(s, mod_w, mod_b, *, tn=2048):
    R, D = s.shape
    depth, _, N = mod_w.shape
    return pl.pallas_call(
        _mod_kernel,
        grid=(depth, N // tn),
        in_specs=[pl.BlockSpec((R, D), lambda l, j: (0, 0)),
                  pl.BlockSpec((1, D, tn), lambda l, j: (l, 0, j)),
                  pl.BlockSpec((1, 1, tn), lambda l, j: (l, 0, j))],
        out_specs=pl.BlockSpec((1, R, tn), lambda l, j: (l, 0, j)),
        out_shape=jax.ShapeDtypeStruct((depth, R, N), F32),
        compiler_params=_cparams("parallel", "parallel"),
        name="modulation",
    )(s, mod_w, mod_b.reshape(depth, 1, N))


def _nmm_kernel(x_ref, g_ref, sh_ref, sc_ref, w_ref, o_ref, w_scr):
    @pl.when((pl.program_id(1) == 0) & (pl.program_id(2) == 0))
    def _():
        w_scr[...] = w_ref[0].astype(BF16)

    for r0, rows in _row_groups(x_ref.shape[1]):
        x = x_ref[0, r0:r0 + rows, :]
        ms = jnp.mean(x * x, axis=-1, keepdims=True)
        h = (x * lax.rsqrt(ms + EPS)) * g_ref[...]
        h = h * (1.0 + sc_ref[0]) + sh_ref[0]
        o_ref[0, r0:r0 + rows, :] = jnp.dot(h.astype(BF16), w_scr[...],
                                            preferred_element_type=F32).astype(o_ref.dtype)


def norm_mod_matmul(x, g, shift, scale, w, *, cols=None, tm=512, out_dtype=F32):
    B, T, D = x.shape
    w, li = w
    first, N = cols if cols else (0, w.shape[2])
    tm = _row_tile(T, tm)
    tn = N
    return pl.pallas_call(
        _nmm_kernel,
        grid=(N // tn, B, T // tm),
        in_specs=[
            pl.BlockSpec((1, tm, D), lambda j, b, i: (b, i, 0)),
            pl.BlockSpec((1, D), lambda j, b, i: (0, 0)),
            pl.BlockSpec((1, 1, D), lambda j, b, i: (b, 0, 0)),
            pl.BlockSpec((1, 1, D), lambda j, b, i: (b, 0, 0)),
            pl.BlockSpec((1, D, tn), lambda j, b, i: (li, 0, first + j)),
        ],
        out_specs=pl.BlockSpec((1, tm, tn), lambda j, b, i: (b, i, j)),
        out_shape=jax.ShapeDtypeStruct((B, T, N), out_dtype),
        scratch_shapes=[pltpu.VMEM((D, tn), BF16)],
        compiler_params=_cparams("arbitrary", "arbitrary", "arbitrary"),
        name="norm_mod_matmul",
    )(x, g.reshape(1, D), shift, scale, w)


def _scan_constants():
    C = SCAN_CHUNK
    t = np.arange(C)[:, None]
    u = np.arange(C)[None, :]
    mats = [u <= t, u > t]
    masks = []
    for w in SCAN_LEVELS:
        m = (t // (2 * w)) * 2 * w + w - 1
        later = (t // w) % 2 == 1
        mats.append(np.where(later, (u > m) & (u <= t), (u > t) & (u <= m)))
        masks.append(later & ((u // w) % 2 == 0) & (u // (2 * w) == t // (2 * w)))
    masks = [t == u] + masks + [np.ones((C, C), bool)]
    a_f = np.stack(mats).astype(np.float32)
    m_f = np.stack(masks).astype(np.float32)
    a = np.stack([a_f, a_f[:, ::-1, ::-1]]).reshape(2, -1, C)
    m = np.stack([m_f, m_f[:, ::-1, ::-1]])
    m_pairs = np.concatenate([m[:, 0::2], m[:, 1::2]], axis=3)
    return np.concatenate([a, a], axis=2), m_pairs


def _scan_pair(q, z, v, lb, st, a2, mask_ref, d, later):
    C = SCAN_CHUNK
    W = HGRN_HEAD_DIM
    nl = len(SCAN_LEVELS)
    nt = lambda x, y: lax.dot_general(x, y, (((1,), (1,)), ((), ())), preferred_element_type=F32)
    nn = lambda x, y: jnp.dot(x, y, preferred_element_type=F32)
    e_abs = jnp.exp(-jnp.abs(z))
    r = 1.0 / (1.0 + e_abs)
    er = e_abs * r
    pos = z >= 0.0
    g2 = jnp.log2(lb + (1.0 - lb) * jnp.where(pos, r, er))
    k = (1.0 - lb) * jnp.where(pos, er, r)
    hi = g2.astype(BF16)
    lo = (g2 - hi.astype(F32)).astype(BF16)
    gs = jnp.concatenate([jnp.concatenate([hi[:C], hi[C:]], axis=1),
                          jnp.concatenate([lo[:C], lo[C:]], axis=1)], axis=0)
    x = nn(a2, gs)
    yield
    x = jnp.exp2(x)
    first, second = (1, 0) if d else (0, 1)
    vb = v.astype(BF16)
    sides, qt, kt, dec = [], [], [], []
    for c in (0, 1):
        qc, kc, xs = q[c * C:(c + 1) * C], k[c * C:(c + 1) * C], x[:, c * W:(c + 1) * W]
        ops = [(qc.astype(BF16), kc.astype(BF16))]
        for i in range(nl):
            qk = (jnp.where(later[i], qc, kc) * xs[(2 + i) * C:(3 + i) * C]).astype(BF16)
            ops.append((qk, qk))
        sides.append(ops)
        qt.append(qc * xs[0:C])
        kt.append(kc * xs[C:2 * C])
        dec.append(xs[0:1, :] if d else xs[C - 1:C, :])
    zero = jnp.zeros((C, W), BF16)
    sides[first].append((zero, zero))
    sides[second].append((qt[second].astype(BF16), kt[first].astype(BF16)))
    prods = []
    for ops in sides:
        pair = []
        for j in range(0, nl + 2, 2):
            (qa, ka), (qb, kb) = ops[j], ops[j + 1]
            rhs = jnp.concatenate([jnp.concatenate([ka, zero], axis=1), jnp.concatenate([zero, kb], axis=1)], axis=0)
            pair.append(nt(jnp.concatenate([qa, qb], axis=1), rhs))
        prods.append(pair)
        yield
    qt[second] = qt[second] * dec[first]
    kt[first] = kt[first] * dec[second]
    o_st = nt(jnp.concatenate([a.astype(BF16) for a in qt], axis=0), st.astype(BF16))
    upd = lax.dot_general(vb, jnp.concatenate(kt, axis=0).astype(BF16), (((0,), (0,)), ((), ())),
                          preferred_element_type=F32)
    yield
    o = [None, None]
    n_main = nl // 2
    for c, pair in enumerate(prods):
        main = mask_ref[d, 0] * pair[0]
        for j in range(1, n_main):
            main = main + mask_ref[d, j] * pair[j]
        last = mask_ref[d, n_main] * pair[n_main]
        vc, vf = vb[c * C:(c + 1) * C], vb[first * C:(first + 1) * C]
        o[c] = o_st[c * C:(c + 1) * C] + nn(jnp.concatenate([main, last], axis=1).astype(BF16),
                                            jnp.concatenate([vc, vc, vc, vf], axis=0))
    yield
    return jnp.concatenate(o, axis=0), st * (dec[0] * dec[1]) + upd


def _run_interleaved(gens):
    results = [None] * len(gens)
    live = list(range(len(gens)))
    while live:
        for i in list(live):
            try:
                next(gens[i])
            except StopIteration as done:
                results[i] = done.value
                live.remove(i)
    return results


def _scan_kernel(qf_ref, zf_ref, vf_ref, qb_ref, zb_ref, vb_ref, lbf_ref, lbb_ref, s0_ref, a_ref, mask_ref,
                 of_ref, ob_ref, sT_ref, st_scr, *, n_chunks):
    c = pl.program_id(2)
    C = SCAN_CHUNK

    @pl.when(c == 0)
    def _():
        st_scr[...] = s0_ref[0]

    W = HGRN_HEAD_DIM
    row = lax.broadcasted_iota(jnp.int32, (C, W), 0)
    later = [[(row // w) % 2 == 1 for w in SCAN_LEVELS], [((C - 1 - row) // w) % 2 == 1 for w in SCAN_LEVELS]]
    in_refs = [(qf_ref, zf_ref, vf_ref), (qb_ref, zb_ref, vb_ref)]
    lb_refs = [lbf_ref, lbb_ref]
    o_refs = [of_ref, ob_ref]
    n_pairs = n_chunks // 2
    heads = range(qf_ref.shape[2] // W)

    def body(i, carry):
        r0 = [pl.multiple_of(i * 2 * C, 2 * C), pl.multiple_of((n_pairs - 1 - i) * 2 * C, 2 * C)]
        chains = [(d, h) for h in heads for d in (0, 1)]
        ins = [[ref[0, pl.ds(r0[d], 2 * C), h * W:(h + 1) * W] for ref in in_refs[d]] for d, h in chains]
        sts = [st_scr[h, d] for d, h in chains]
        outs = _run_interleaved([
            _scan_pair(*x, lb_refs[d][:, h * W:(h + 1) * W], st, a_ref[d], mask_ref, d, later[d])
            for (d, h), x, st in zip(chains, ins, sts)])
        for (d, h), (o, st) in zip(chains, outs):
            o_refs[d][0, pl.ds(r0[d], 2 * C), h * W:(h + 1) * W] = o.astype(o_refs[d].dtype)
            st_scr[h, d] = st
        return carry

    lax.fori_loop(0, n_pairs, body, 0)

    @pl.when(c == pl.num_programs(2) - 1)
    def _():
        sT_ref[0] = st_scr[...]


def hgrn_scan(proj, lb, s0, *, tb=1024):
    B, T, _ = proj.shape
    tb = _row_tile(T, tb)
    nc = T // tb
    H = HGRN_HEADS
    hd = HGRN_HEAD_DIM
    hps = SCAN_HEADS_PER_STEP
    G = H // hps
    wd = hps * hd
    fwd = lambda grp: pl.BlockSpec((1, tb, wd), lambda b, h, c: (b, c, grp * G + h))
    bwd = lambda grp: pl.BlockSpec((1, tb, wd), lambda b, h, c: (b, nc - 1 - c, grp * G + h))
    kern = functools.partial(_scan_kernel, n_chunks=tb // SCAN_CHUNK)
    a2, masks = _scan_constants()
    return pl.pallas_call(
        kern,
        grid=(B, G, nc),
        in_specs=[fwd(0), fwd(1), fwd(3), bwd(0), bwd(2), bwd(3),
                  pl.BlockSpec((1, wd), lambda b, h, c: (0, h)),
                  pl.BlockSpec((1, wd), lambda b, h, c: (0, h)),
                  pl.BlockSpec((1, hps, 2, hd, hd), lambda b, h, c: (b, h, 0, 0, 0)),
                  pl.BlockSpec(a2.shape, lambda b, h, c: (0, 0, 0)),
                  pl.BlockSpec(masks.shape, lambda b, h, c: (0, 0, 0, 0))],
        out_specs=[pl.BlockSpec((1, tb, wd), lambda b, h, c: (b, c, h)),
                   pl.BlockSpec((1, tb, wd), lambda b, h, c: (b, nc - 1 - c, h)),
                   pl.BlockSpec((1, hps, 2, hd, hd), lambda b, h, c: (b, h, 0, 0, 0))],
        out_shape=[jax.ShapeDtypeStruct((B, T, HGRN_WIDTH), BF16),
                   jax.ShapeDtypeStruct((B, T, HGRN_WIDTH), BF16),
                   jax.ShapeDtypeStruct((B, H, 2, hd, hd), F32)],
        scratch_shapes=[pltpu.VMEM((hps, 2, hd, hd), F32)],
        compiler_params=_cparams("parallel", "parallel", "arbitrary"),
        name="hgrn_scan",
    )(proj, proj, proj, proj, proj, proj, lb[0:1], lb[1:2], s0, jnp.asarray(a2, BF16), jnp.asarray(masks, F32))


def _group_mean_sq(x, gmat):
    sq = x * x
    hi = sq.astype(BF16)
    lo = (sq - hi.astype(F32)).astype(BF16)
    return (jnp.dot(hi, gmat, preferred_element_type=F32) + jnp.dot(lo, gmat, preferred_element_type=F32))


def _qkv_prep_kernel(q_ref, k_ref, v_ref, qg_ref, kg_ref, cos_ref, sin_ref, qo_ref, ko_ref, vo_ref, *, rotary):
    W = LANES
    r_i = lax.broadcasted_iota(jnp.int32, (W, W), 0) // DIFF_HEAD_DIM
    c_i = lax.broadcasted_iota(jnp.int32, (W, W), 1) // DIFF_HEAD_DIM
    gmat = jnp.where(r_i == c_i, 1.0 / DIFF_HEAD_DIM, 0.0).astype(BF16)
    lane = lax.broadcasted_iota(jnp.int32, (1, W), 1)
    first = (lane % 32) < 16

    def prep(x, g, scale):
        y = (x * lax.rsqrt(_group_mean_sq(x, gmat) + EPS)) * g
        if rotary:
            partner = jnp.where(first, pltpu.roll(y, W - 16, axis=1), pltpu.roll(y, 16, axis=1))
            y = y * cos_ref[...] + partner * sin_ref[...]
        if scale != 1.0:
            y = y * scale
        return y

    for h in range(DIFF_HEADS):
        sl = slice(h * W, (h + 1) * W)
        qo_ref[0, sl, :] = prep(q_ref[0, :, sl].astype(F32), qg_ref[...], DIFF_HEAD_DIM ** -0.5).T.astype(BF16)
        ko_ref[0, :, sl] = prep(k_ref[0, :, sl].astype(F32), kg_ref[...], 1.0).astype(BF16)
        vo_ref[0, sl, :] = v_ref[0, :, sl].astype(F32).T.astype(BF16)


def qkv_prep(proj, qg, kg, cos, sin, *, rotary, tm=1024):
    B, T, _ = proj.shape
    tm = _row_tile(T, tm)
    Wd = DIFF_WIDTH
    col = lambda j: pl.BlockSpec((1, tm, Wd), lambda b, i: (b, i, j))
    vec = pl.BlockSpec((1, LANES), lambda b, i: (0, 0))
    tab = pl.BlockSpec((tm, LANES), lambda b, i: (i, 0))
    rows = pl.BlockSpec((1, tm, Wd), lambda b, i: (b, i, 0))
    cols = pl.BlockSpec((1, Wd, tm), lambda b, i: (b, 0, i))
    return pl.pallas_call(
        functools.partial(_qkv_prep_kernel, rotary=rotary),
        grid=(B, T // tm),
        in_specs=[col(1), col(2), col(3), vec, vec, tab, tab],
        out_specs=[cols, rows, cols],
        out_shape=[jax.ShapeDtypeStruct((B, Wd, T), BF16), jax.ShapeDtypeStruct((B, T, Wd), BF16),
                   jax.ShapeDtypeStruct((B, Wd, T), BF16)],
        compiler_params=_cparams("parallel", "parallel"),
        name="qkv_prep",
    )(proj, proj, proj, jnp.tile(qg, 2).reshape(1, LANES), jnp.tile(kg, 2).reshape(1, LANES), cos, sin)


def rope_tables(T):
    n = DIFF_HEAD_DIM // 2
    inv = 1.0 / (ROPE_THETA ** (jnp.arange(0, n, 2, dtype=F32) / n))
    t = jnp.arange(T)
    ang_r = (t // GRID_W).astype(F32)[:, None] * inv[None, :]
    ang_c = (t % GRID_W).astype(F32)[:, None] * inv[None, :]
    cos = jnp.concatenate([jnp.cos(ang_r)] * 2 + [jnp.cos(ang_c)] * 2, axis=-1)
    sin = jnp.concatenate([-jnp.sin(ang_r), jnp.sin(ang_r), -jnp.sin(ang_c), jnp.sin(ang_c)], axis=-1)
    return jnp.tile(cos, (1, 2)), jnp.tile(sin, (1, 2))


def _attn_tile(qt, kv_refs, lam, key_chunk, bound=None):
    tq = qt.shape[1]
    row = lax.broadcasted_iota(jnp.int32, (LANES, 1), 0)
    zero = jnp.zeros_like(qt)
    qq = jnp.concatenate([jnp.where(row < DIFF_HEAD_DIM, qt, zero),
                          jnp.where(row >= DIFF_HEAD_DIM, qt, zero)], axis=1)
    m = jnp.full((1, 2 * tq), -jnp.inf, F32)
    acc = [jnp.zeros((LANES + 16, tq), F32), jnp.zeros((LANES + 16, tq), F32)]
    ones = jnp.ones((16, key_chunk), BF16)
    chunks = [(k_ref, v_ref, c0, min(c0 + key_chunk, k_ref.shape[1]))
              for k_ref, v_ref in kv_refs for c0 in range(0, k_ref.shape[1], key_chunk)]
    scores = lambda c: jnp.dot(c[0][0, c[2]:c[3], :], qq, preferred_element_type=F32)
    s_next = scores(chunks[0])
    yield
    for n, (_, v_ref, c0, c1) in enumerate(chunks):
        s = s_next
        if n + 1 < len(chunks):
            s_next = scores(chunks[n + 1])
        vt1 = jnp.concatenate([v_ref[0, :, c0:c1], ones[:, :c1 - c0]], axis=0)
        if bound is None:
            m_new = jnp.maximum(m, jnp.max(s, axis=0, keepdims=True))
            alpha = jnp.exp(m - m_new)
            pb = jnp.exp((s - m_new).astype(BF16))
            m = m_new
        else:
            alpha = None
            pb = jnp.exp(s - bound).astype(BF16)
        for i in range(2):
            pv = jnp.dot(vt1, pb[:, i * tq:(i + 1) * tq], preferred_element_type=F32)
            acc[i] = acc[i] + pv if alpha is None else acc[i] * alpha[:, i * tq:(i + 1) * tq] + pv
        yield
    inv = [1.0 / a[LANES:LANES + 1] for a in acc]
    return acc[0][:LANES] * inv[0] - acc[1][:LANES] * (lam * inv[1])


def _diff_attn_kernel(par_ref, q_ref, *rest, out_scale, key_chunk, tq):
    kv_refs, g_ref, o_ref = list(zip(rest[:-2:2], rest[1:-2:2])), rest[-2], rest[-1]
    n_tiles = q_ref.shape[2] // tq

    def run(bound):
        outs = _run_interleaved([
            _attn_tile(q_ref[0, :, i * tq:(i + 1) * tq], kv_refs, par_ref[0], key_chunk, bound)
            for i in range(n_tiles)])
        for i, o in enumerate(outs):
            ms = jnp.mean(o * o, axis=0, keepdims=True)
            o_ref[0, i * tq:(i + 1) * tq, :] = ((o * lax.rsqrt(ms + EPS)) * g_ref[...] * out_scale).T.astype(BF16)

    @pl.when(par_ref[2] > 0.5)
    def _():
        run(par_ref[1])

    @pl.when(par_ref[2] <= 0.5)
    def _():
        run(None)


def diff_attention(qt, kvs, params, subln_g, out_scale, *, tq=256, tiles_per_step=4, key_chunk=512):
    B, Wd, T = qt.shape
    tq = _row_tile(T, tq)
    ts = _row_tile(T, tq * tiles_per_step)
    kv_specs = [spec for k, _ in kvs for spec in (
        pl.BlockSpec((1, k.shape[1], LANES), lambda b, h, i: (b, 0, h)),
        pl.BlockSpec((1, LANES, k.shape[1]), lambda b, h, i: (b, h, 0)))]
    return pl.pallas_call(
        functools.partial(_diff_attn_kernel, out_scale=out_scale, key_chunk=key_chunk, tq=tq),
        grid=(B, DIFF_HEADS, T // ts),
        in_specs=[pl.BlockSpec(memory_space=pltpu.SMEM),
                  pl.BlockSpec((1, LANES, ts), lambda b, h, i: (b, h, i))] + kv_specs + [
                  pl.BlockSpec((LANES, 1), lambda b, h, i: (0, 0))],
        out_specs=pl.BlockSpec((1, ts, LANES), lambda b, h, i: (b, i, h)),
        out_shape=jax.ShapeDtypeStruct((B, T, Wd), BF16),
        compiler_params=_cparams("parallel", "parallel", "arbitrary"),
        name="diff_attention",
    )(params, qt, *[a for kv in kvs for a in kv], subln_g.reshape(LANES, 1))


def _even_out_kernel(x_ref, of_ref, ob_ref, gate_ref, yd_ref, hg_ref, g1_ref, w_ref, *rest):
    router_refs, o_ref, moe_refs = rest[:-3], rest[-3], rest[-2:]
    tm = x_ref.shape[1]
    w = w_ref[0].astype(BF16)
    for r0, n in _row_groups(tm):
        rows = slice(r0, r0 + n)
        acc = jnp.dot(yd_ref[0, rows, :], w[HGRN_WIDTH:], preferred_element_type=F32)
        for h in range(HGRN_HEADS):
            sl = slice(h * HGRN_HEAD_DIM, (h + 1) * HGRN_HEAD_DIM)
            o = of_ref[0, rows, sl].astype(F32) + ob_ref[0, rows, sl].astype(F32)
            ms = jnp.mean(o * o, axis=-1, keepdims=True)
            gate = gate_ref[0, rows, sl].astype(F32)
            yh = (o * lax.rsqrt(ms + EPS)) * hg_ref[:, sl] * (gate * jax.nn.sigmoid(gate))
            acc = acc + jnp.dot(yh.astype(BF16), w[sl], preferred_element_type=F32)
        x_new = x_ref[0, rows, :] + g1_ref[0] * acc
        o_ref[0, rows, :] = x_new
        _router_outputs(x_new, r0, *router_refs, *moe_refs)


def even_out(x, o_f, o_b, proj, y_d, hgrn_g, g1, w_out, router, *, tm=1024):
    B, T, D = x.shape
    w_out, li = w_out
    tm = _row_tile(T, tm)
    Wd = HGRN_WIDTH
    row = lambda w, j: pl.BlockSpec((1, tm, w), lambda b, i: (b, i, j))
    r_args, r_in, r_out, r_shapes = _router_specs(B, T, D, tm, router)
    return pl.pallas_call(
        _even_out_kernel,
        grid=(B, T // tm),
        in_specs=[row(D, 0), row(Wd, 0), row(Wd, 0), row(Wd, 0), row(Wd, 0),
                  pl.BlockSpec((1, Wd), lambda b, i: (0, 0)),
                  pl.BlockSpec((1, 1, D), lambda b, i: (b, 0, 0)),
                  pl.BlockSpec((1,) + w_out.shape[1:], lambda b, i: (li, 0, 0))] + r_in,
        out_specs=[row(D, 0)] + r_out,
        out_shape=[jax.ShapeDtypeStruct((B, T, D), F32)] + r_shapes,
        compiler_params=_cparams("parallel", "parallel"),
        name="even_out",
    )(x, o_f, o_b, proj, y_d, hgrn_g.reshape(1, Wd), g1, w_out, *r_args)


def _conv_out_kernel(x_ref, bg_ref, cg_ref, v_ref, cp_ref, vp_ref, cn_ref, vn_ref, cw_ref, g1_ref, w_ref, *rest):
    router_refs, o_ref, moe_refs = rest[:-3], rest[-3], rest[-2:]
    i = pl.program_id(1)
    n = pl.num_programs(1)
    f32 = lambda ref, *idx: ref[idx].astype(F32)
    u = f32(cg_ref, 0) * f32(v_ref, 0)
    tm = u.shape[0]
    last = ROW_ALIGN - 1
    u_prev_row = jnp.where(i > 0, f32(cp_ref, 0, slice(last, last + 1)) * f32(vp_ref, 0, slice(last, last + 1)), 0.0)
    u_next_row = jnp.where(i < n - 1, f32(cn_ref, 0, slice(0, 1)) * f32(vn_ref, 0, slice(0, 1)), 0.0)
    ridx = lax.broadcasted_iota(jnp.int32, (tm, 1), 0)
    u_prev = jnp.where(ridx == 0, u_prev_row, pltpu.roll(u, 1, axis=0))
    u_next = jnp.where(ridx == tm - 1, u_next_row, pltpu.roll(u, tm - 1, axis=0))
    y = cw_ref[0:1, :] * u_prev + cw_ref[1:2, :] * u + cw_ref[2:3, :] * u_next
    w = w_ref[0].astype(BF16)
    for r0, rows in _row_groups(tm):
        sl = slice(r0, r0 + rows)
        acc = jnp.dot((f32(bg_ref, 0, sl) * y[sl]).astype(BF16), w, preferred_element_type=F32)
        x_new = x_ref[0, sl, :] + g1_ref[0] * acc
        o_ref[0, sl, :] = x_new
        _router_outputs(x_new, r0, *router_refs, *moe_refs)


def conv_out(x, proj, conv_w, g1, w_out, router, *, tm=1024):
    B, T, D = x.shape
    w_out, li = w_out
    tm = _row_tile(T, tm)
    rt = tm // ROW_ALIGN
    last_blk = T // ROW_ALIGN - 1
    row = lambda j: pl.BlockSpec((1, tm, D), lambda b, i: (b, i, j))
    prev = lambda j: pl.BlockSpec((1, ROW_ALIGN, D), lambda b, i: (b, jnp.maximum(i * rt - 1, 0), j))
    nxt = lambda j: pl.BlockSpec((1, ROW_ALIGN, D), lambda b, i: (b, jnp.minimum((i + 1) * rt, last_blk), j))
    r_args, r_in, r_out, r_shapes = _router_specs(B, T, D, tm, router)
    return pl.pallas_call(
        _conv_out_kernel,
        grid=(B, T // tm),
        in_specs=[row(0), row(0), row(1), row(2), prev(1), prev(2), nxt(1), nxt(2),
                  pl.BlockSpec((8, D), lambda b, i: (0, 0)),
                  pl.BlockSpec((1, 1, D), lambda b, i: (b, 0, 0)),
                  pl.BlockSpec((1,) + w_out.shape[1:], lambda b, i: (li, 0, 0))] + r_in,
        out_specs=[row(0)] + r_out,
        out_shape=[jax.ShapeDtypeStruct((B, T, D), F32)] + r_shapes,
        compiler_params=_cparams("parallel", "parallel"),
        name="conv_out",
    )(x, proj, proj, proj, proj, proj, proj, proj,
      jnp.concatenate([conv_w, jnp.zeros((8 - conv_w.shape[0], D), conv_w.dtype)], axis=0), g1, w_out, *r_args)


def _router_outputs(x, r0, g_ref, sh_ref, sc_ref, rwt_ref, h_ref, lg_ref):
    n = x.shape[0]
    ms = jnp.mean(x * x, axis=-1, keepdims=True)
    h = (x * lax.rsqrt(ms + EPS)) * g_ref[...]
    h = h * (1.0 + sc_ref[0]) + sh_ref[0]
    h_hi = h.astype(BF16)
    h_ref[0, r0:r0 + n, :] = h_hi
    h_lo = (h - h_hi.astype(F32)).astype(BF16)
    rwt = rwt_ref[...]
    w_hi = rwt.astype(BF16)
    w_lo = (rwt - w_hi.astype(F32)).astype(BF16)
    d = lambda a, b: lax.dot_general(a, b, (((1,), (1,)), ((), ())), preferred_element_type=F32)
    lg_ref[0, :, r0:r0 + n] = d(w_hi, h_hi) + d(w_lo, h_hi) + d(w_hi, h_lo)


def _router_specs(B, T, D, tm, router):
    g, shift, scale, router_w = router
    E = router_w.shape[1]
    vec = pl.BlockSpec((1, 1, D), lambda b, i: (b, 0, 0))
    return ((g.reshape(1, D), shift, scale, router_w.T),
            [pl.BlockSpec((1, D), lambda b, i: (0, 0)), vec, vec, pl.BlockSpec((E, D), lambda b, i: (0, 0))],
            [pl.BlockSpec((1, tm, D), lambda b, i: (b, i, 0)), pl.BlockSpec((1, E, tm), lambda b, i: (b, 0, i))],
            [jax.ShapeDtypeStruct((B, T, D), BF16), jax.ShapeDtypeStruct((B, E, T), F32)])


def _lane_prefix(flags, tri_tot):
    E, N = flags.shape
    carries = [jnp.zeros((E, LANES), F32)]
    out = []
    for j in range(N // LANES):
        r = jnp.dot(flags[:, j * LANES:(j + 1) * LANES].astype(BF16), tri_tot, preferred_element_type=F32)
        out.append(r[:, :LANES] + carries[-1])
        carries.append(carries[-1] + r[:, LANES:])
    return jnp.concatenate(out, axis=1), carries


def _route_kernel(lg_ref, tab_ref, rank_ref, ts_ref, *, cap, tt):
    lg = lg_ref[...]
    B, E, N = lg.shape
    p = jnp.exp(lg - jnp.max(lg, axis=1, keepdims=True))
    R = B * E
    aff = (p / jnp.sum(p, axis=1, keepdims=True)).reshape(R, N)
    count = lambda m: jnp.sum(jnp.where(m, 1.0, 0.0), axis=1, keepdims=True)
    as_float = lambda i: pltpu.bitcast(i, F32)

    def refine_bits(i, thr):
        cand = thr | jnp.left_shift(jnp.int32(1), 30 - i)
        return jnp.where(count(aff >= as_float(cand)) >= cap, cand, thr)

    thr = lax.fori_loop(0, 31, refine_bits, jnp.zeros((R, 1), jnp.int32))

    def refine_mid(i, lo_hi):
        lo, hi = lo_hi
        mid = 0.5 * (lo + hi)
        up = count(aff >= mid) >= cap
        return jnp.where(up, mid, lo), jnp.where(up, hi, mid)

    lo, hi = lax.fori_loop(0, 24, refine_mid, (as_float(thr), as_float(jnp.maximum(thr + 1, 0x00800000))))
    gt = aff >= hi
    eq = (aff >= lo) & (aff < hi)
    r_i = lax.broadcasted_iota(jnp.int32, (LANES, 2 * LANES), 0)
    c_i = lax.broadcasted_iota(jnp.int32, (LANES, 2 * LANES), 1)
    tri_tot = jnp.where((r_i < c_i) | (c_i >= LANES), 1.0, 0.0).astype(BF16)
    eq_rank, _ = _lane_prefix(jnp.where(eq, 1.0, 0.0), tri_tot)
    sel = gt | (eq & (eq_rank < cap - count(gt)))
    rank, before = _lane_prefix(jnp.where(sel, 1.0, 0.0), tri_tot)
    rank = jnp.where(sel, rank, -1.0)
    rank_ref[...] = rank.astype(jnp.int32).reshape(B, E, N)
    pad = jnp.zeros((LANES - 2 * E, N), F32)
    for b in range(B):
        rows = slice(b * E, (b + 1) * E)
        tab_ref[b] = jnp.concatenate([aff[rows], rank[rows], pad], axis=0).T
    lane = lax.broadcasted_iota(jnp.int32, (R, LANES), 1)
    ts = jnp.zeros((R, LANES), F32)
    for k in range(N // tt + 1):
        ts = jnp.where(lane == k, before[k * tt // LANES], ts)
    ts_ref[...] = ts.astype(jnp.int32).reshape(B, E, LANES)


def route(logits_t, cap, tt):
    B, E, N = logits_t.shape
    whole = lambda *shape: pl.BlockSpec(shape, lambda i: (0,) * len(shape))
    return pl.pallas_call(
        functools.partial(_route_kernel, cap=cap, tt=tt),
        grid=(1,),
        in_specs=[whole(B, E, N)],
        out_specs=[whole(B, N, LANES), whole(B, E, N), whole(B, E, LANES)],
        out_shape=[jax.ShapeDtypeStruct((B, N, LANES), F32), jax.ShapeDtypeStruct((B, E, N), jnp.int32),
                   jax.ShapeDtypeStruct((B, E, LANES), jnp.int32)],
        compiler_params=_cparams("arbitrary"),
        name="route",
    )(logits_t)


def _window(lo, w, win, cap):
    lower = (lo // ROW_ALIGN) * ROW_ALIGN + w * win
    return pl.multiple_of(jnp.minimum(lower, cap - win), ROW_ALIGN), lower


def _extra_windows(cap, tt, win):
    return -(-(min(cap, tt) + ROW_ALIGN - 1) // win) - 1


def _moe_gather_kernel(ts_ref, rank_ref, h_ref, xe_ref, acc_ref, *, win, tt):
    b, g = pl.program_id(0), pl.program_id(1)
    N = h_ref.shape[1]
    ne, cap = xe_ref.shape[1], xe_ref.shape[2]
    nt = N // tt
    acc_ref[...] = jnp.zeros_like(acc_ref)
    row = lax.broadcasted_iota(jnp.int32, (win, tt), 0)
    n_extra = _extra_windows(cap, tt, win)
    for ee in range(ne):
        base = ((b * pl.num_programs(1) + g) * ne + ee) * (nt + 1)

        def place(k, w, ee=ee, base=base):
            start, lower = _window(ts_ref[base + k], w, win, cap)
            c0 = k * tt if isinstance(k, int) else pl.multiple_of(k * tt, tt)
            rk = rank_ref[0, ee, :, pl.ds(c0, tt)]
            onehot = jnp.where(jnp.where(rk >= lower, rk, -1) == row + start, 1.0, 0.0).astype(BF16)
            acc_ref[ee, pl.ds(start, win), :] += jnp.dot(onehot, h_ref[0, pl.ds(c0, tt), :],
                                                         preferred_element_type=F32)

        span = lambda k, base=base: ts_ref[base + k + 1] - (ts_ref[base + k] // ROW_ALIGN) * ROW_ALIGN
        need = jnp.int32(0)
        for k in range(nt):
            place(k, 0)
            need = jnp.maximum(need, span(k))

        @pl.when(need > win)
        def _(place=place, span=span):
            def extra(i, carry):
                k, w = i // n_extra, i % n_extra + 1

                @pl.when(span(k) > w * win)
                def _():
                    place(k, w)
                return carry

            lax.fori_loop(0, nt * n_extra, extra, 0)

    xe_ref[0] = acc_ref[...].astype(BF16)


def moe_gather(ts_flat, rank, h, cap, *, tt):
    B, E, N = rank.shape
    D = h.shape[2]
    ne = GATHER_EXPERTS_PER_STEP
    return pl.pallas_call(
        functools.partial(_moe_gather_kernel, win=min(LANES, cap), tt=tt),
        grid_spec=pltpu.PrefetchScalarGridSpec(
            num_scalar_prefetch=1,
            grid=(B, E // ne),
            in_specs=[pl.BlockSpec((1, ne, 1, N), lambda b, e, ts: (b, e, 0, 0)),
                      pl.BlockSpec((1, N, D), lambda b, e, ts: (b, 0, 0))],
            out_specs=pl.BlockSpec((1, ne, cap, D), lambda b, e, ts: (b, e, 0, 0)),
            scratch_shapes=[pltpu.VMEM((ne, cap, D), F32)]),
        out_shape=jax.ShapeDtypeStruct((B, E, cap, D), BF16),
        compiler_params=_cparams("parallel", "arbitrary"),
        name="moe_gather",
    )(ts_flat, rank.reshape(B, E, 1, N), h)


def _expert_kernel(*refs, n_streams):
    xe_refs = refs[:n_streams]
    wg_ref, wu_ref, wd_ref = refs[n_streams:n_streams + 3]
    o_refs = refs[n_streams + 3:2 * n_streams + 3]
    wg_scr, wu_scr, wd_scr = refs[2 * n_streams + 3:]

    @pl.when(pl.program_id(1) == 0)
    def _():
        wg_scr[...] = wg_ref[0, 0].astype(BF16)
        wu_scr[...] = wu_ref[0, 0].astype(BF16)
        wd_scr[...] = wd_ref[0, 0].astype(BF16)

    def swiglu(x):
        a = jnp.dot(x, wg_scr[...], preferred_element_type=F32)
        u = jnp.dot(x, wu_scr[...], preferred_element_type=F32)
        yield
        hid = (a * jax.nn.sigmoid(a)) * u
        return jnp.dot(hid.astype(BF16), wd_scr[...], preferred_element_type=F32).astype(BF16)

    nb = xe_refs[0].shape[0]
    ys = _run_interleaved([swiglu(jnp.concatenate([r[s, 0] for r in xe_refs], axis=0)) for s in range(nb)])
    for s, y in enumerate(ys):
        r0 = 0
        for o_ref in o_refs:
            o_ref[s, 0] = y[r0:r0 + o_ref.shape[2]]
            r0 += o_ref.shape[2]


def expert_ffn(xes, w_gate, w_up, w_down, li):
    B, E, _, D = xes[0].shape
    FF = w_gate.shape[3]
    nb = FFN_SAMPLES_PER_STEP if B % FFN_SAMPLES_PER_STEP == 0 else 1
    rows = [pl.BlockSpec((nb, 1, xe.shape[2], D), lambda e, b: (b, e, 0, 0)) for xe in xes]
    return pl.pallas_call(
        functools.partial(_expert_kernel, n_streams=len(xes)),
        grid=(E, B // nb),
        in_specs=rows + [pl.BlockSpec((1, 1, D, FF), lambda e, b: (li, e, 0, 0)),
                         pl.BlockSpec((1, 1, D, FF), lambda e, b: (li, e, 0, 0)),
                         pl.BlockSpec((1, 1, FF, D), lambda e, b: (li, e, 0, 0))],
        out_specs=rows,
        out_shape=[jax.ShapeDtypeStruct(xe.shape, BF16) for xe in xes],
        scratch_shapes=[pltpu.VMEM((D, FF), BF16), pltpu.VMEM((D, FF), BF16), pltpu.VMEM((FF, D), BF16)],
        compiler_params=_cparams("arbitrary", "arbitrary"),
        name="expert_ffn",
    )(*xes, w_gate, w_up, w_down)


def _moe_combine_kernel(ts_ref, tab_ref, ye_ref, x_ref, g2_ref, o_ref, acc_ref, *, win, group):
    b, k = pl.program_id(0), pl.program_id(1)
    E, cap = ye_ref.shape[1], ye_ref.shape[2]
    tt = x_ref.shape[1]
    nt = pl.num_programs(1)
    col = lax.broadcasted_iota(jnp.int32, (tt, win), 1)
    lo = [ts_ref[(b * E + e) * (nt + 1) + k] for e in range(E)]
    hi = [ts_ref[(b * E + e) * (nt + 1) + k + 1] for e in range(E)]

    def contribution(w):
        total = None
        for g0 in range(0, E, group):
            lhs, rhs = [], []
            for e in range(g0, g0 + group):
                start, lower = _window(lo[e], w, win, cap)
                rk = tab_ref[0, :, E + e:E + e + 1].astype(jnp.int32)
                hit = jnp.where(rk >= lower, rk, -1) == col + start
                lhs.append(jnp.where(hit, tab_ref[0, :, e:e + 1], 0.0).astype(BF16))
                rhs.append(ye_ref[0, e, pl.ds(start, win), :])
            d = jnp.dot(jnp.concatenate(lhs, axis=1), jnp.concatenate(rhs, axis=0), preferred_element_type=F32)
            total = d if total is None else total + d
        return total

    acc_ref[...] = contribution(0)
    need = jnp.int32(0)
    for e in range(E):
        need = jnp.maximum(need, hi[e] - (lo[e] // ROW_ALIGN) * ROW_ALIGN)

    @pl.when(need > win)
    def _():
        def extra(w, carry):
            @pl.when(need > w * win)
            def _():
                acc_ref[...] += contribution(w)
            return carry

        lax.fori_loop(1, _extra_windows(cap, tt, win) + 1, extra, 0)

    o_ref[0] = x_ref[0] + g2_ref[0] * acc_ref[...]


def moe_combine(ts_flat, table, ye, x, g2, *, tt):
    B, N, D = x.shape
    E, cap = ye.shape[1], ye.shape[2]
    tok = lambda w: pl.BlockSpec((1, tt, w), lambda b, k, ts: (b, k, 0))
    return pl.pallas_call(
        functools.partial(_moe_combine_kernel, win=min(LANES, cap), group=4),
        grid_spec=pltpu.PrefetchScalarGridSpec(
            num_scalar_prefetch=1,
            grid=(B, N // tt),
            in_specs=[tok(LANES),
                      pl.BlockSpec((1, E, cap, D), lambda b, k, ts: (b, 0, 0, 0)),
                      tok(D),
                      pl.BlockSpec((1, 1, D), lambda b, k, ts: (b, 0, 0))],
            out_specs=tok(D),
            scratch_shapes=[pltpu.VMEM((tt, D), F32)]),
        out_shape=jax.ShapeDtypeStruct((B, N, D), F32),
        compiler_params=_cparams("parallel", "arbitrary"),
        name="moe_combine",
    )(ts_flat, table, ye, x, g2)


def moe_residual(streams, w_gate, w_up, w_down, li):
    routed = []
    for (x, h, logits), _ in streams:
        B, N, D = x.shape
        cap = EC_CAPACITY * N // N_EXPERTS
        tt = min(4 * LANES, N)
        table, rank, ts = route(logits, cap, tt)
        ts_flat = ts[:, :, :N // tt + 1].reshape(-1)
        routed.append((moe_gather(ts_flat, rank, h, cap, tt=tt), ts_flat, table, tt))
    yes = expert_ffn([r[0] for r in routed], w_gate, w_up, w_down, li)
    return [moe_combine(ts_flat, table, ye, x, gate2, tt=tt)
            for ((x, _, _), gate2), ye, (_, ts_flat, table, tt) in zip(streams, yes, routed)]


def lambda_init(layer):
    return 0.8 - 0.6 * math.exp(-0.3 * layer)


def even_layer(x, xc, mods, cmods, norm1_g, w_in, w_out, lb, hgrn_g, qn_g, kn_g, lam_vec, subln_g,
               lam_init, ctx_out, tables):
    sh1, sc1, g1, router = mods
    csh1, csc1, cg1, crouter = cmods
    B = x.shape[0]
    half = w_in[0].shape[2] // 2
    proj = norm_mod_matmul(x, norm1_g, sh1, sc1, w_in, cols=(0, half), tm=1024)
    proj2 = norm_mod_matmul(x, norm1_g, sh1, sc1, w_in, cols=(1, half), tm=1024, out_dtype=BF16)
    projc = norm_mod_matmul(xc, norm1_g, csh1, csc1, w_in, cols=(0, half))
    projc2 = norm_mod_matmul(xc, norm1_g, csh1, csc1, w_in, cols=(1, half), out_dtype=BF16)
    s0 = jnp.zeros((B, HGRN_HEADS, 2, HGRN_HEAD_DIM, HGRN_HEAD_DIM), F32)
    oc_f, oc_b, s_ctx = hgrn_scan(projc, lb, s0)
    o_f, o_b, _ = hgrn_scan(proj, lb, s_ctx)
    lv = lam_vec.astype(F32)
    lam = jnp.exp(jnp.sum(lv[0] * lv[1])) - jnp.exp(jnp.sum(lv[2] * lv[3])) + lam_init
    bound = 1.01 * math.sqrt(DIFF_HEAD_DIM) * jnp.max(jnp.abs(qn_g)) * jnp.max(jnp.abs(kn_g))
    lam = jnp.stack([lam, bound, (bound <= SCORE_BOUND_MAX).astype(F32)])
    cos, sin = tables
    q, k, v = qkv_prep(proj2, qn_g, kn_g, cos, sin, rotary=True)
    qc, kc, vc = qkv_prep(projc2, qn_g, kn_g, cos[:xc.shape[1]], sin[:xc.shape[1]], rotary=False)
    y_d = diff_attention(q, [(k, v), (kc, vc)], lam, subln_g, 1.0 - lam_init)
    x_new = even_out(x, o_f, o_b, proj2, y_d, hgrn_g.reshape(-1), g1, w_out, router)
    if not ctx_out:
        return x_new, None
    yc_d = diff_attention(qc, [(kc, vc)], lam, subln_g, 1.0 - lam_init)
    xc_new = even_out(xc, oc_f, oc_b, projc2, yc_d, hgrn_g.reshape(-1), cg1, w_out, crouter)
    return x_new, xc_new


def conv_layer(x, mods, norm1_g, w_in, conv_w, w_out):
    sh1, sc1, g1, router = mods
    proj = norm_mod_matmul(x, norm1_g, sh1, sc1, w_in, out_dtype=BF16)
    return conv_out(x, proj, conv_w, g1, w_out, router)


def kernel(x, c, ctx, c_ctx, mod_w, mod_b, norm1_g, norm2_g, even_w_in, even_w_out, hgrn_lb_logits, hgrn_norm_g,
           diff_qnorm_g, diff_knorm_g, diff_lambda, diff_subln_g, conv_w_in, conv_w, conv_w_out, router_w,
           exp_w_gate, exp_w_up, exp_w_down):
    depth = mod_w.shape[0]
    B, T, D = x.shape
    lb_soft = jax.nn.softmax(hgrn_lb_logits.astype(F32), axis=0)
    lower_bounds = jnp.cumsum(lb_soft, axis=0) - lb_soft[:1]
    last_ctx_layer = 2 * ((depth - 1) // 2)
    cond = jnp.concatenate([c, c_ctx[None, :], jnp.zeros((8 - (B + 1) % 8, D), F32)], axis=0)
    mods = modulation(jax.nn.silu(cond), mod_w, mod_b)
    tables = rope_tables(T)
    xc = ctx
    for l in range(depth):
        read_ctx = l <= last_ctx_layer
        ctx_out = l < last_ctx_layer
        sh1, sc1, g1, sh2, sc2, g2 = [m[:, None, :] for m in jnp.split(mods[l, :B], MOD_CHUNKS, axis=-1)]
        if read_ctx:
            csh1, csc1, cg1, csh2, csc2, cg2 = [
                jnp.broadcast_to(m[None, None, :], (B, 1, D)) for m in jnp.split(mods[l, B], MOD_CHUNKS, axis=-1)]
        mods_l = (sh1, sc1, g1, (norm2_g[l], sh2, sc2, router_w[l]))
        cmods_l = (csh1, csc1, cg1, (norm2_g[l], csh2, csc2, router_w[l])) if read_ctx else None
        if l % 2 == 0:
            e = l // 2
            s, sc = even_layer(x, xc, mods_l, cmods_l, norm1_g[l],
                               (even_w_in, e), (even_w_out, e), lower_bounds[e],
                               hgrn_norm_g[e], diff_qnorm_g[e], diff_knorm_g[e], diff_lambda[e],
                               diff_subln_g[e], lambda_init(l), ctx_out, tables)
        else:
            j = l // 2
            wi, wo = (conv_w_in, j), (conv_w_out, j)
            s = conv_layer(x, mods_l, norm1_g[l], wi, conv_w[j], wo)
            sc = conv_layer(xc, cmods_l, norm1_g[l], wi, conv_w[j], wo) if ctx_out else None
        outs = moe_residual([(s, g2)] + ([(sc, cg2)] if ctx_out else []), exp_w_gate, exp_w_up, exp_w_down, l)
        x = outs[0]
        if ctx_out:
            xc = outs[1]
    return x
```

```python
import functools
import math

import numpy as np
import jax
import jax.numpy as jnp
from jax import lax
from jax.experimental import pallas as pl
from jax.experimental.pallas import tpu as pltpu

F32 = jnp.float32
BF16 = jnp.bfloat16

EPS = 1e-6
GRID_W = 64
ROPE_THETA = 10000.0
HGRN_HEAD_DIM = 128
HGRN_HEADS = 4
HGRN_WIDTH = HGRN_HEADS * HGRN_HEAD_DIM
DIFF_HEAD_DIM = 64
DIFF_HEADS = 4
DIFF_WIDTH = DIFF_HEADS * 2 * DIFF_HEAD_DIM
N_EXPERTS = 16
EC_CAPACITY = 2
MOD_CHUNKS = 6
SCAN_CHUNK = 64
SCAN_LEVELS = (32, 16, 8, 4, 2, 1)
SCAN_HEADS_PER_STEP = 2
GATHER_EXPERTS_PER_STEP = 2
FFN_SAMPLES_PER_STEP = 4
LANES = 128
ROW_ALIGN = 16
VMEM_LIMIT = 56 * 1024 * 1024
SCORE_BOUND_MAX = 40.0


def _cparams(*sem):
    return pltpu.CompilerParams(dimension_semantics=sem, vmem_limit_bytes=VMEM_LIMIT)


def _row_tile(t, want):
    return want if t % want == 0 else t


def _mod_kernel(s_ref, w_ref, b_ref, o_ref):
    s = s_ref[...]
    w = w_ref[0]
    s_hi = s.astype(BF16)
    s_lo = (s - s_hi.astype(F32)).astype(BF16)
    w_hi = w.astype(BF16)
    w_lo = (w - w_hi.astype(F32)).astype(BF16)
    d = lambda a, b: jnp.dot(a, b, preferred_element_type=F32)
    o_ref[0] = d(s_hi, w_hi) + d(s_hi, w_lo) + d(s_lo, w_hi) + b_ref[0]


def modulation(s, mod_w, mod_b, *, tn=2048):
    R, D = s.shape
    depth, _, N = mod_w.shape
    return pl.pallas_call(
        _mod_kernel,
        grid=(depth, N // tn),
        in_specs=[pl.BlockSpec((R, D), lambda l, j: (0, 0)),
                  pl.BlockSpec((1, D, tn), lambda l, j: (l, 0, j)),
                  pl.BlockSpec((1, 1, tn), lambda l, j: (l, 0, j))],
        out_specs=pl.BlockSpec((1, R, tn), lambda l, j: (l, 0, j)),
        out_shape=jax.ShapeDtypeStruct((depth, R, N), F32),
        compiler_params=_cparams("parallel", "parallel"),
        name="modulation",
    )(s, mod_w, mod_b.reshape(depth, 1, N))


def _nmm_kernel(x_ref, g_ref, sh_ref, sc_ref, w_ref, o_ref, w_scr):
    @pl.when((pl.program_id(1) == 0) & (pl.program_id(2) == 0))
    def _():
        w_scr[...] = w_ref[0].astype(BF16)

    x = x_ref[0]
    ms = jnp.mean(x * x, axis=-1, keepdims=True)
    h = (x * lax.rsqrt(ms + EPS)) * g_ref[...]
    h = h * (1.0 + sc_ref[0]) + sh_ref[0]
    o_ref[0] = jnp.dot(h.astype(BF16), w_scr[...], preferred_element_type=F32).astype(o_ref.dtype)


def norm_mod_matmul(x, g, shift, scale, w, *, cols=None, tm=512, out_dtype=F32):
    B, T, D = x.shape
    w, li = w
    first, N = cols if cols else (0, w.shape[2])
    tm = _row_tile(T, tm)
    tn = N
    return pl.pallas_call(
        _nmm_kernel,
        grid=(N // tn, B, T // tm),
        in_specs=[
            pl.BlockSpec((1, tm, D), lambda j, b, i: (b, i, 0)),
            pl.BlockSpec((1, D), lambda j, b, i: (0, 0)),
            pl.BlockSpec((1, 1, D), lambda j, b, i: (b, 0, 0)),
            pl.BlockSpec((1, 1, D), lambda j, b, i: (b, 0, 0)),
            pl.BlockSpec((1, D, tn), lambda j, b, i: (li, 0, first + j)),
        ],
        out_specs=pl.BlockSpec((1, tm, tn), lambda j, b, i: (b, i, j)),
        out_shape=jax.ShapeDtypeStruct((B, T, N), out_dtype),
        scratch_shapes=[pltpu.VMEM((D, tn), BF16)],
        compiler_params=_cparams("arbitrary", "arbitrary", "arbitrary"),
        name="norm_mod_matmul",
    )(x, g.reshape(1, D), shift, scale, w)


def _scan_constants():
    C = SCAN_CHUNK
    t = np.arange(C)[:, None]
    u = np.arange(C)[None, :]
    mats = [u <= t, u > t]
    masks = []
    for w in SCAN_LEVELS:
        m = (t // (2 * w)) * 2 * w + w - 1
        later = (t // w) % 2 == 1
        mats.append(np.where(later, (u > m) & (u <= t), (u > t) & (u <= m)))
        masks.append(later & ((u // w) % 2 == 0) & (u // (2 * w) == t // (2 * w)))
    masks = [t == u] + masks + [np.ones((C, C), bool)]
    a_f = np.stack(mats).astype(np.float32)
    m_f = np.stack(masks).astype(np.float32)
    a = np.stack([a_f, a_f[:, ::-1, ::-1]]).reshape(2, -1, C)
    m = np.stack([m_f, m_f[:, ::-1, ::-1]])
    m_pairs = np.concatenate([m[:, 0::2], m[:, 1::2]], axis=3)
    return np.concatenate([a, a], axis=2), m_pairs


def _scan_pair(q, z, v, lb, st, a2, mask_ref, d, later):
    C = SCAN_CHUNK
    W = HGRN_HEAD_DIM
    nl = len(SCAN_LEVELS)
    nt = lambda x, y: lax.dot_general(x, y, (((1,), (1,)), ((), ())), preferred_element_type=F32)
    nn = lambda x, y: jnp.dot(x, y, preferred_element_type=F32)
    e_abs = jnp.exp(-jnp.abs(z))
    r = 1.0 / (1.0 + e_abs)
    er = e_abs * r
    pos = z >= 0.0
    g2 = jnp.log2(lb + (1.0 - lb) * jnp.where(pos, r, er))
    k = (1.0 - lb) * jnp.where(pos, er, r)
    hi = g2.astype(BF16)
    lo = (g2 - hi.astype(F32)).astype(BF16)
    gs = jnp.concatenate([jnp.concatenate([hi[:C], hi[C:]], axis=1),
                          jnp.concatenate([lo[:C], lo[C:]], axis=1)], axis=0)
    x = nn(a2, gs)
    yield
    x = jnp.exp2(x)
    first, second = (1, 0) if d else (0, 1)
    vb = v.astype(BF16)
    sides, qt, kt, dec = [], [], [], []
    for c in (0, 1):
        qc, kc, xs = q[c * C:(c + 1) * C], k[c * C:(c + 1) * C], x[:, c * W:(c + 1) * W]
        ops = [(qc.astype(BF16), kc.astype(BF16))]
        for i in range(nl):
            qk = (jnp.where(later[i], qc, kc) * xs[(2 + i) * C:(3 + i) * C]).astype(BF16)
            ops.append((qk, qk))
        sides.append(ops)
        qt.append(qc * xs[0:C])
        kt.append(kc * xs[C:2 * C])
        dec.append(xs[0:1, :] if d else xs[C - 1:C, :])
    zero = jnp.zeros((C, W), BF16)
    sides[first].append((zero, zero))
    sides[second].append((qt[second].astype(BF16), kt[first].astype(BF16)))
    prods = []
    for ops in sides:
        pair = []
        for j in range(0, nl + 2, 2):
            (qa, ka), (qb, kb) = ops[j], ops[j + 1]
            rhs = jnp.concatenate([jnp.concatenate([ka, zero], axis=1), jnp.concatenate([zero, kb], axis=1)], axis=0)
            pair.append(nt(jnp.concatenate([qa, qb], axis=1), rhs))
        prods.append(pair)
        yield
    qt[second] = qt[second] * dec[first]
    kt[first] = kt[first] * dec[second]
    o_st = nt(jnp.concatenate([a.astype(BF16) for a in qt], axis=0), st.astype(BF16))
    upd = lax.dot_general(vb, jnp.concatenate(kt, axis=0).astype(BF16), (((0,), (0,)), ((), ())),
                          preferred_element_type=F32)
    yield
    o = [None, None]
    n_main = nl // 2
    for c, pair in enumerate(prods):
        main = mask_ref[d, 0] * pair[0]
        for j in range(1, n_main):
            main = main + mask_ref[d, j] * pair[j]
        last = mask_ref[d, n_main] * pair[n_main]
        vc, vf = vb[c * C:(c + 1) * C], vb[first * C:(first + 1) * C]
        o[c] = o_st[c * C:(c + 1) * C] + nn(jnp.concatenate([main, last], axis=1).astype(BF16),
                                            jnp.concatenate([vc, vc, vc, vf], axis=0))
    yield
    return jnp.concatenate(o, axis=0), st * (dec[0] * dec[1]) + upd


def _run_interleaved(gens):
    results = [None] * len(gens)
    live = list(range(len(gens)))
    while live:
        for i in list(live):
            try:
                next(gens[i])
            except StopIteration as done:
                results[i] = done.value
                live.remove(i)
    return results


def _scan_kernel(qf_ref, zf_ref, vf_ref, qb_ref, zb_ref, vb_ref, lbf_ref, lbb_ref, s0_ref, a_ref, mask_ref,
                 of_ref, ob_ref, sT_ref, st_scr, *, n_chunks):
    c = pl.program_id(2)
    C = SCAN_CHUNK

    @pl.when(c == 0)
    def _():
        st_scr[...] = s0_ref[0]

    W = HGRN_HEAD_DIM
    row = lax.broadcasted_iota(jnp.int32, (C, W), 0)
    later = [[(row // w) % 2 == 1 for w in SCAN_LEVELS], [((C - 1 - row) // w) % 2 == 1 for w in SCAN_LEVELS]]
    in_refs = [(qf_ref, zf_ref, vf_ref), (qb_ref, zb_ref, vb_ref)]
    lb_refs = [lbf_ref, lbb_ref]
    o_refs = [of_ref, ob_ref]
    n_pairs = n_chunks // 2
    heads = range(qf_ref.shape[2] // W)

    def body(i, carry):
        r0 = [pl.multiple_of(i * 2 * C, 2 * C), pl.multiple_of((n_pairs - 1 - i) * 2 * C, 2 * C)]
        chains = [(d, h) for h in heads for d in (0, 1)]
        ins = [[ref[0, pl.ds(r0[d], 2 * C), h * W:(h + 1) * W] for ref in in_refs[d]] for d, h in chains]
        sts = [st_scr[h, d] for d, h in chains]
        outs = _run_interleaved([
            _scan_pair(*x, lb_refs[d][:, h * W:(h + 1) * W], st, a_ref[d], mask_ref, d, later[d])
            for (d, h), x, st in zip(chains, ins, sts)])
        for (d, h), (o, st) in zip(chains, outs):
            o_refs[d][0, pl.ds(r0[d], 2 * C), h * W:(h + 1) * W] = o.astype(o_refs[d].dtype)
            st_scr[h, d] = st
        return carry

    lax.fori_loop(0, n_pairs, body, 0)

    @pl.when(c == pl.num_programs(2) - 1)
    def _():
        sT_ref[0] = st_scr[...]


def hgrn_scan(proj, lb, s0, *, tb=1024):
    B, T, _ = proj.shape
    tb = _row_tile(T, tb)
    nc = T // tb
    H = HGRN_HEADS
    hd = HGRN_HEAD_DIM
    hps = SCAN_HEADS_PER_STEP
    G = H // hps
    wd = hps * hd
    fwd = lambda grp: pl.BlockSpec((1, tb, wd), lambda b, h, c: (b, c, grp * G + h))
    bwd = lambda grp: pl.BlockSpec((1, tb, wd), lambda b, h, c: (b, nc - 1 - c, grp * G + h))
    kern = functools.partial(_scan_kernel, n_chunks=tb // SCAN_CHUNK)
    a2, masks = _scan_constants()
    return pl.pallas_call(
        kern,
        grid=(B, G, nc),
        in_specs=[fwd(0), fwd(1), fwd(3), bwd(0), bwd(2), bwd(3),
                  pl.BlockSpec((1, wd), lambda b, h, c: (0, h)),
                  pl.BlockSpec((1, wd), lambda b, h, c: (0, h)),
                  pl.BlockSpec((1, hps, 2, hd, hd), lambda b, h, c: (b, h, 0, 0, 0)),
                  pl.BlockSpec(a2.shape, lambda b, h, c: (0, 0, 0)),
                  pl.BlockSpec(masks.shape, lambda b, h, c: (0, 0, 0, 0))],
        out_specs=[pl.BlockSpec((1, tb, wd), lambda b, h, c: (b, c, h)),
                   pl.BlockSpec((1, tb, wd), lambda b, h, c: (b, nc - 1 - c, h)),
                   pl.BlockSpec((1, hps, 2, hd, hd), lambda b, h, c: (b, h, 0, 0, 0))],
        out_shape=[jax.ShapeDtypeStruct((B, T, HGRN_WIDTH), BF16),
                   jax.ShapeDtypeStruct((B, T, HGRN_WIDTH), BF16),
                   jax.ShapeDtypeStruct((B, H, 2, hd, hd), F32)],
        scratch_shapes=[pltpu.VMEM((hps, 2, hd, hd), F32)],
        compiler_params=_cparams("parallel", "parallel", "arbitrary"),
        name="hgrn_scan",
    )(proj, proj, proj, proj, proj, proj, lb[0:1], lb[1:2], s0, jnp.asarray(a2, BF16), jnp.asarray(masks, F32))


def _group_mean_sq(x, gmat):
    sq = x * x
    hi = sq.astype(BF16)
    lo = (sq - hi.astype(F32)).astype(BF16)
    return (jnp.dot(hi, gmat, preferred_element_type=F32) + jnp.dot(lo, gmat, preferred_element_type=F32))


def _qkv_prep_kernel(q_ref, k_ref, v_ref, qg_ref, kg_ref, cos_ref, sin_ref, qo_ref, ko_ref, vo_ref, *, rotary):
    W = LANES
    r_i = lax.broadcasted_iota(jnp.int32, (W, W), 0) // DIFF_HEAD_DIM
    c_i = lax.broadcasted_iota(jnp.int32, (W, W), 1) // DIFF_HEAD_DIM
    gmat = jnp.where(r_i == c_i, 1.0 / DIFF_HEAD_DIM, 0.0).astype(BF16)
    lane = lax.broadcasted_iota(jnp.int32, (1, W), 1)
    first = (lane % 32) < 16

    def prep(x, g, scale):
        y = (x * lax.rsqrt(_group_mean_sq(x, gmat) + EPS)) * g
        if rotary:
            partner = jnp.where(first, pltpu.roll(y, W - 16, axis=1), pltpu.roll(y, 16, axis=1))
            y = y * cos_ref[...] + partner * sin_ref[...]
        if scale != 1.0:
            y = y * scale
        return y

    for h in range(DIFF_HEADS):
        sl = slice(h * W, (h + 1) * W)
        qo_ref[0, sl, :] = prep(q_ref[0, :, sl].astype(F32), qg_ref[...], DIFF_HEAD_DIM ** -0.5).T.astype(BF16)
        ko_ref[0, :, sl] = prep(k_ref[0, :, sl].astype(F32), kg_ref[...], 1.0).astype(BF16)
        vo_ref[0, sl, :] = v_ref[0, :, sl].astype(F32).T.astype(BF16)


def qkv_prep(proj, qg, kg, cos, sin, *, rotary, tm=1024):
    B, T, _ = proj.shape
    tm = _row_tile(T, tm)
    Wd = DIFF_WIDTH
    col = lambda j: pl.BlockSpec((1, tm, Wd), lambda b, i: (b, i, j))
    vec = pl.BlockSpec((1, LANES), lambda b, i: (0, 0))
    tab = pl.BlockSpec((tm, LANES), lambda b, i: (i, 0))
    rows = pl.BlockSpec((1, tm, Wd), lambda b, i: (b, i, 0))
    cols = pl.BlockSpec((1, Wd, tm), lambda b, i: (b, 0, i))
    return pl.pallas_call(
        functools.partial(_qkv_prep_kernel, rotary=rotary),
        grid=(B, T // tm),
        in_specs=[col(1), col(2), col(3), vec, vec, tab, tab],
        out_specs=[cols, rows, cols],
        out_shape=[jax.ShapeDtypeStruct((B, Wd, T), BF16), jax.ShapeDtypeStruct((B, T, Wd), BF16),
                   jax.ShapeDtypeStruct((B, Wd, T), BF16)],
        compiler_params=_cparams("parallel", "parallel"),
        name="qkv_prep",
    )(proj, proj, proj, jnp.tile(qg, 2).reshape(1, LANES), jnp.tile(kg, 2).reshape(1, LANES), cos, sin)


def rope_tables(T):
    n = DIFF_HEAD_DIM // 2
    inv = 1.0 / (ROPE_THETA ** (jnp.arange(0, n, 2, dtype=F32) / n))
    t = jnp.arange(T)
    ang_r = (t // GRID_W).astype(F32)[:, None] * inv[None, :]
    ang_c = (t % GRID_W).astype(F32)[:, None] * inv[None, :]
    cos = jnp.concatenate([jnp.cos(ang_r)] * 2 + [jnp.cos(ang_c)] * 2, axis=-1)
    sin = jnp.concatenate([-jnp.sin(ang_r), jnp.sin(ang_r), -jnp.sin(ang_c), jnp.sin(ang_c)], axis=-1)
    return jnp.tile(cos, (1, 2)), jnp.tile(sin, (1, 2))


def _attn_tile(qt, kv_refs, lam, key_chunk, bound=None):
    tq = qt.shape[1]
    row = lax.broadcasted_iota(jnp.int32, (LANES, 1), 0)
    zero = jnp.zeros_like(qt)
    qq = jnp.concatenate([jnp.where(row < DIFF_HEAD_DIM, qt, zero),
                          jnp.where(row >= DIFF_HEAD_DIM, qt, zero)], axis=1)
    m = jnp.full((1, 2 * tq), -jnp.inf, F32)
    acc = [jnp.zeros((LANES + 16, tq), F32), jnp.zeros((LANES + 16, tq), F32)]
    ones = jnp.ones((16, key_chunk), BF16)
    chunks = [(k_ref, v_ref, c0, min(c0 + key_chunk, k_ref.shape[1]))
              for k_ref, v_ref in kv_refs for c0 in range(0, k_ref.shape[1], key_chunk)]
    scores = lambda c: jnp.dot(c[0][0, c[2]:c[3], :], qq, preferred_element_type=F32)
    s_next = scores(chunks[0])
    yield
    for n, (_, v_ref, c0, c1) in enumerate(chunks):
        s = s_next
        if n + 1 < len(chunks):
            s_next = scores(chunks[n + 1])
        vt1 = jnp.concatenate([v_ref[0, :, c0:c1], ones[:, :c1 - c0]], axis=0)
        if bound is None:
            m_new = jnp.maximum(m, jnp.max(s, axis=0, keepdims=True))
            alpha = jnp.exp(m - m_new)
            pb = jnp.exp((s - m_new).astype(BF16))
            m = m_new
        else:
            alpha = None
            pb = jnp.exp(s - bound).astype(BF16)
        for i in range(2):
            pv = jnp.dot(vt1, pb[:, i * tq:(i + 1) * tq], preferred_element_type=F32)
            acc[i] = acc[i] + pv if alpha is None else acc[i] * alpha[:, i * tq:(i + 1) * tq] + pv
        yield
    inv = [1.0 / a[LANES:LANES + 1] for a in acc]
    return acc[0][:LANES] * inv[0] - acc[1][:LANES] * (lam * inv[1])


def _diff_attn_kernel(par_ref, q_ref, *rest, out_scale, key_chunk, tq):
    kv_refs, g_ref, o_ref = list(zip(rest[:-2:2], rest[1:-2:2])), rest[-2], rest[-1]
    n_tiles = q_ref.shape[2] // tq

    def run(bound):
        outs = _run_interleaved([
            _attn_tile(q_ref[0, :, i * tq:(i + 1) * tq], kv_refs, par_ref[0], key_chunk, bound)
            for i in range(n_tiles)])
        for i, o in enumerate(outs):
            ms = jnp.mean(o * o, axis=0, keepdims=True)
            o_ref[0, i * tq:(i + 1) * tq, :] = ((o * lax.rsqrt(ms + EPS)) * g_ref[...] * out_scale).T.astype(BF16)

    @pl.when(par_ref[2] > 0.5)
    def _():
        run(par_ref[1])

    @pl.when(par_ref[2] <= 0.5)
    def _():
        run(None)


def diff_attention(qt, kvs, params, subln_g, out_scale, *, tq=256, tiles_per_step=4, key_chunk=512):
    B, Wd, T = qt.shape
    tq = _row_tile(T, tq)
    ts = _row_tile(T, tq * tiles_per_step)
    kv_specs = [spec for k, _ in kvs for spec in (
        pl.BlockSpec((1, k.shape[1], LANES), lambda b, h, i: (b, 0, h)),
        pl.BlockSpec((1, LANES, k.shape[1]), lambda b, h, i: (b, h, 0)))]
    return pl.pallas_call(
        functools.partial(_diff_attn_kernel, out_scale=out_scale, key_chunk=key_chunk, tq=tq),
        grid=(B, DIFF_HEADS, T // ts),
        in_specs=[pl.BlockSpec(memory_space=pltpu.SMEM),
                  pl.BlockSpec((1, LANES, ts), lambda b, h, i: (b, h, i))] + kv_specs + [
                  pl.BlockSpec((LANES, 1), lambda b, h, i: (0, 0))],
        out_specs=pl.BlockSpec((1, ts, LANES), lambda b, h, i: (b, i, h)),
        out_shape=jax.ShapeDtypeStruct((B, T, Wd), BF16),
        compiler_params=_cparams("parallel", "parallel", "arbitrary"),
        name="diff_attention",
    )(params, qt, *[a for kv in kvs for a in kv], subln_g.reshape(LANES, 1))


def _even_out_kernel(x_ref, of_ref, ob_ref, gate_ref, yd_ref, hg_ref, g1_ref, w_ref, *rest):
    router_refs, o_ref, moe_refs = rest[:-3], rest[-3], rest[-2:]
    w = w_ref[0].astype(BF16)
    acc = jnp.dot(yd_ref[0], w[HGRN_WIDTH:], preferred_element_type=F32)
    for h in range(HGRN_HEADS):
        sl = slice(h * HGRN_HEAD_DIM, (h + 1) * HGRN_HEAD_DIM)
        o = of_ref[0, :, sl].astype(F32) + ob_ref[0, :, sl].astype(F32)
        ms = jnp.mean(o * o, axis=-1, keepdims=True)
        gate = gate_ref[0, :, sl].astype(F32)
        yh = (o * lax.rsqrt(ms + EPS)) * hg_ref[:, sl] * (gate * jax.nn.sigmoid(gate))
        acc = acc + jnp.dot(yh.astype(BF16), w[sl], preferred_element_type=F32)
    x_new = x_ref[0] + g1_ref[0] * acc
    o_ref[0] = x_new
    _router_outputs(x_new, *router_refs, *moe_refs)


def even_out(x, o_f, o_b, proj, y_d, hgrn_g, g1, w_out, router, *, tm=1024):
    B, T, D = x.shape
    w_out, li = w_out
    tm = _row_tile(T, tm)
    Wd = HGRN_WIDTH
    row = lambda w, j: pl.BlockSpec((1, tm, w), lambda b, i: (b, i, j))
    r_args, r_in, r_out, r_shapes = _router_specs(B, T, D, tm, router)
    return pl.pallas_call(
        _even_out_kernel,
        grid=(B, T // tm),
        in_specs=[row(D, 0), row(Wd, 0), row(Wd, 0), row(Wd, 0), row(Wd, 0),
                  pl.BlockSpec((1, Wd), lambda b, i: (0, 0)),
                  pl.BlockSpec((1, 1, D), lambda b, i: (b, 0, 0)),
                  pl.BlockSpec((1,) + w_out.shape[1:], lambda b, i: (li, 0, 0))] + r_in,
        out_specs=[row(D, 0)] + r_out,
        out_shape=[jax.ShapeDtypeStruct((B, T, D), F32)] + r_shapes,
        compiler_params=_cparams("parallel", "parallel"),
        name="even_out",
    )(x, o_f, o_b, proj, y_d, hgrn_g.reshape(1, Wd), g1, w_out, *r_args)


def _conv_out_kernel(x_ref, bg_ref, cg_ref, v_ref, cp_ref, vp_ref, cn_ref, vn_ref, cw_ref, g1_ref, w_ref, *rest):
    router_refs, o_ref, moe_refs = rest[:-3], rest[-3], rest[-2:]
    i = pl.program_id(1)
    n = pl.num_programs(1)
    f32 = lambda ref, *idx: ref[idx].astype(F32)
    u = f32(cg_ref, 0) * f32(v_ref, 0)
    tm = u.shape[0]
    last = ROW_ALIGN - 1
    u_prev_row = jnp.where(i > 0, f32(cp_ref, 0, slice(last, last + 1)) * f32(vp_ref, 0, slice(last, last + 1)), 0.0)
    u_next_row = jnp.where(i < n - 1, f32(cn_ref, 0, slice(0, 1)) * f32(vn_ref, 0, slice(0, 1)), 0.0)
    ridx = lax.broadcasted_iota(jnp.int32, (tm, 1), 0)
    u_prev = jnp.where(ridx == 0, u_prev_row, pltpu.roll(u, 1, axis=0))
    u_next = jnp.where(ridx == tm - 1, u_next_row, pltpu.roll(u, tm - 1, axis=0))
    y = cw_ref[0:1, :] * u_prev + cw_ref[1:2, :] * u + cw_ref[2:3, :] * u_next
    acc = jnp.dot((f32(bg_ref, 0) * y).astype(BF16), w_ref[0].astype(BF16), preferred_element_type=F32)
    x_new = x_ref[0] + g1_ref[0] * acc
    o_ref[0] = x_new
    _router_outputs(x_new, *router_refs, *moe_refs)


def conv_out(x, proj, conv_w, g1, w_out, router, *, tm=1024):
    B, T, D = x.shape
    w_out, li = w_out
    tm = _row_tile(T, tm)
    rt = tm // ROW_ALIGN
    last_blk = T // ROW_ALIGN - 1
    row = lambda j: pl.BlockSpec((1, tm, D), lambda b, i: (b, i, j))
    prev = lambda j: pl.BlockSpec((1, ROW_ALIGN, D), lambda b, i: (b, jnp.maximum(i * rt - 1, 0), j))
    nxt = lambda j: pl.BlockSpec((1, ROW_ALIGN, D), lambda b, i: (b, jnp.minimum((i + 1) * rt, last_blk), j))
    r_args, r_in, r_out, r_shapes = _router_specs(B, T, D, tm, router)
    return pl.pallas_call(
        _conv_out_kernel,
        grid=(B, T // tm),
        in_specs=[row(0), row(0), row(1), row(2), prev(1), prev(2), nxt(1), nxt(2),
                  pl.BlockSpec((8, D), lambda b, i: (0, 0)),
                  pl.BlockSpec((1, 1, D), lambda b, i: (b, 0, 0)),
                  pl.BlockSpec((1,) + w_out.shape[1:], lambda b, i: (li, 0, 0))] + r_in,
        out_specs=[row(0)] + r_out,
        out_shape=[jax.ShapeDtypeStruct((B, T, D), F32)] + r_shapes,
        compiler_params=_cparams("parallel", "parallel"),
        name="conv_out",
    )(x, proj, proj, proj, proj, proj, proj, proj,
      jnp.concatenate([conv_w, jnp.zeros((8 - conv_w.shape[0], D), conv_w.dtype)], axis=0), g1, w_out, *r_args)


def _router_outputs(x, g_ref, sh_ref, sc_ref, rwt_ref, h_ref, lg_ref):
    ms = jnp.mean(x * x, axis=-1, keepdims=True)
    h = (x * lax.rsqrt(ms + EPS)) * g_ref[...]
    h = h * (1.0 + sc_ref[0]) + sh_ref[0]
    h_hi = h.astype(BF16)
    h_ref[0] = h_hi
    h_lo = (h - h_hi.astype(F32)).astype(BF16)
    rwt = rwt_ref[...]
    w_hi = rwt.astype(BF16)
    w_lo = (rwt - w_hi.astype(F32)).astype(BF16)
    d = lambda a, b: lax.dot_general(a, b, (((1,), (1,)), ((), ())), preferred_element_type=F32)
    lg_ref[0] = d(w_hi, h_hi) + d(w_lo, h_hi) + d(w_hi, h_lo)


def _router_specs(B, T, D, tm, router):
    g, shift, scale, router_w = router
    E = router_w.shape[1]
    vec = pl.BlockSpec((1, 1, D), lambda b, i: (b, 0, 0))
    return ((g.reshape(1, D), shift, scale, router_w.T),
            [pl.BlockSpec((1, D), lambda b, i: (0, 0)), vec, vec, pl.BlockSpec((E, D), lambda b, i: (0, 0))],
            [pl.BlockSpec((1, tm, D), lambda b, i: (b, i, 0)), pl.BlockSpec((1, E, tm), lambda b, i: (b, 0, i))],
            [jax.ShapeDtypeStruct((B, T, D), BF16), jax.ShapeDtypeStruct((B, E, T), F32)])


def _lane_prefix(flags, tri_tot):
    E, N = flags.shape
    carries = [jnp.zeros((E, LANES), F32)]
    out = []
    for j in range(N // LANES):
        r = jnp.dot(flags[:, j * LANES:(j + 1) * LANES].astype(BF16), tri_tot, preferred_element_type=F32)
        out.append(r[:, :LANES] + carries[-1])
        carries.append(carries[-1] + r[:, LANES:])
    return jnp.concatenate(out, axis=1), carries


def _route_kernel(lg_ref, tab_ref, rank_ref, ts_ref, *, cap, tt):
    lg = lg_ref[...]
    B, E, N = lg.shape
    p = jnp.exp(lg - jnp.max(lg, axis=1, keepdims=True))
    R = B * E
    aff = (p / jnp.sum(p, axis=1, keepdims=True)).reshape(R, N)
    count = lambda m: jnp.sum(jnp.where(m, 1.0, 0.0), axis=1, keepdims=True)
    as_float = lambda i: pltpu.bitcast(i, F32)

    def refine_bits(i, thr):
        cand = thr | jnp.left_shift(jnp.int32(1), 30 - i)
        return jnp.where(count(aff >= as_float(cand)) >= cap, cand, thr)

    thr = lax.fori_loop(0, 31, refine_bits, jnp.zeros((R, 1), jnp.int32))

    def refine_mid(i, lo_hi):
        lo, hi = lo_hi
        mid = 0.5 * (lo + hi)
        up = count(aff >= mid) >= cap
        return jnp.where(up, mid, lo), jnp.where(up, hi, mid)

    lo, hi = lax.fori_loop(0, 24, refine_mid, (as_float(thr), as_float(jnp.maximum(thr + 1, 0x00800000))))
    gt = aff >= hi
    eq = (aff >= lo) & (aff < hi)
    r_i = lax.broadcasted_iota(jnp.int32, (LANES, 2 * LANES), 0)
    c_i = lax.broadcasted_iota(jnp.int32, (LANES, 2 * LANES), 1)
    tri_tot = jnp.where((r_i < c_i) | (c_i >= LANES), 1.0, 0.0).astype(BF16)
    eq_rank, _ = _lane_prefix(jnp.where(eq, 1.0, 0.0), tri_tot)
    sel = gt | (eq & (eq_rank < cap - count(gt)))
    rank, before = _lane_prefix(jnp.where(sel, 1.0, 0.0), tri_tot)
    rank = jnp.where(sel, rank, -1.0)
    rank_ref[...] = rank.astype(jnp.int32).reshape(B, E, N)
    pad = jnp.zeros((LANES - 2 * E, N), F32)
    for b in range(B):
        rows = slice(b * E, (b + 1) * E)
        tab_ref[b] = jnp.concatenate([aff[rows], rank[rows], pad], axis=0).T
    lane = lax.broadcasted_iota(jnp.int32, (R, LANES), 1)
    ts = jnp.zeros((R, LANES), F32)
    for k in range(N // tt + 1):
        ts = jnp.where(lane == k, before[k * tt // LANES], ts)
    ts_ref[...] = ts.astype(jnp.int32).reshape(B, E, LANES)


def route(logits_t, cap, tt):
    B, E, N = logits_t.shape
    whole = lambda *shape: pl.BlockSpec(shape, lambda i: (0,) * len(shape))
    return pl.pallas_call(
        functools.partial(_route_kernel, cap=cap, tt=tt),
        grid=(1,),
        in_specs=[whole(B, E, N)],
        out_specs=[whole(B, N, LANES), whole(B, E, N), whole(B, E, LANES)],
        out_shape=[jax.ShapeDtypeStruct((B, N, LANES), F32), jax.ShapeDtypeStruct((B, E, N), jnp.int32),
                   jax.ShapeDtypeStruct((B, E, LANES), jnp.int32)],
        compiler_params=_cparams("arbitrary"),
        name="route",
    )(logits_t)


def _window(lo, w, win, cap):
    lower = (lo // ROW_ALIGN) * ROW_ALIGN + w * win
    return pl.multiple_of(jnp.minimum(lower, cap - win), ROW_ALIGN), lower


def _extra_windows(cap, tt, win):
    return -(-(min(cap, tt) + ROW_ALIGN - 1) // win) - 1


def _moe_gather_kernel(ts_ref, rank_ref, h_ref, xe_ref, acc_ref, *, win, tt):
    b, g = pl.program_id(0), pl.program_id(1)
    N = h_ref.shape[1]
    ne, cap = xe_ref.shape[1], xe_ref.shape[2]
    nt = N // tt
    acc_ref[...] = jnp.zeros_like(acc_ref)
    row = lax.broadcasted_iota(jnp.int32, (win, tt), 0)
    n_extra = _extra_windows(cap, tt, win)
    for ee in range(ne):
        base = ((b * pl.num_programs(1) + g) * ne + ee) * (nt + 1)

        def place(k, w, ee=ee, base=base):
            start, lower = _window(ts_ref[base + k], w, win, cap)
            c0 = k * tt if isinstance(k, int) else pl.multiple_of(k * tt, tt)
            rk = rank_ref[0, ee, :, pl.ds(c0, tt)]
            onehot = jnp.where(jnp.where(rk >= lower, rk, -1) == row + start, 1.0, 0.0).astype(BF16)
            acc_ref[ee, pl.ds(start, win), :] += jnp.dot(onehot, h_ref[0, pl.ds(c0, tt), :],
                                                         preferred_element_type=F32)

        span = lambda k, base=base: ts_ref[base + k + 1] - (ts_ref[base + k] // ROW_ALIGN) * ROW_ALIGN
        need = jnp.int32(0)
        for k in range(nt):
            place(k, 0)
            need = jnp.maximum(need, span(k))

        @pl.when(need > win)
        def _(place=place, span=span):
            def extra(i, carry):
                k, w = i // n_extra, i % n_extra + 1

                @pl.when(span(k) > w * win)
                def _():
                    place(k, w)
                return carry

            lax.fori_loop(0, nt * n_extra, extra, 0)

    xe_ref[0] = acc_ref[...].astype(BF16)


def moe_gather(ts_flat, rank, h, cap, *, tt):
    B, E, N = rank.shape
    D = h.shape[2]
    ne = GATHER_EXPERTS_PER_STEP
    return pl.pallas_call(
        functools.partial(_moe_gather_kernel, win=min(LANES, cap), tt=tt),
        grid_spec=pltpu.PrefetchScalarGridSpec(
            num_scalar_prefetch=1,
            grid=(B, E // ne),
            in_specs=[pl.BlockSpec((1, ne, 1, N), lambda b, e, ts: (b, e, 0, 0)),
                      pl.BlockSpec((1, N, D), lambda b, e, ts: (b, 0, 0))],
            out_specs=pl.BlockSpec((1, ne, cap, D), lambda b, e, ts: (b, e, 0, 0)),
            scratch_shapes=[pltpu.VMEM((ne, cap, D), F32)]),
        out_shape=jax.ShapeDtypeStruct((B, E, cap, D), BF16),
        compiler_params=_cparams("parallel", "arbitrary"),
        name="moe_gather",
    )(ts_flat, rank.reshape(B, E, 1, N), h)


def _expert_kernel(*refs, n_streams):
    xe_refs = refs[:n_streams]
    wg_ref, wu_ref, wd_ref = refs[n_streams:n_streams + 3]
    o_refs = refs[n_streams + 3:2 * n_streams + 3]
    wg_scr, wu_scr, wd_scr = refs[2 * n_streams + 3:]

    @pl.when(pl.program_id(1) == 0)
    def _():
        wg_scr[...] = wg_ref[0, 0].astype(BF16)
        wu_scr[...] = wu_ref[0, 0].astype(BF16)
        wd_scr[...] = wd_ref[0, 0].astype(BF16)

    def swiglu(x):
        a = jnp.dot(x, wg_scr[...], preferred_element_type=F32)
        u = jnp.dot(x, wu_scr[...], preferred_element_type=F32)
        yield
        hid = (a * jax.nn.sigmoid(a)) * u
        return jnp.dot(hid.astype(BF16), wd_scr[...], preferred_element_type=F32).astype(BF16)

    nb = xe_refs[0].shape[0]
    ys = _run_interleaved([swiglu(jnp.concatenate([r[s, 0] for r in xe_refs], axis=0)) for s in range(nb)])
    for s, y in enumerate(ys):
        r0 = 0
        for o_ref in o_refs:
            o_ref[s, 0] = y[r0:r0 + o_ref.shape[2]]
            r0 += o_ref.shape[2]


def expert_ffn(xes, w_gate, w_up, w_down, li):
    B, E, _, D = xes[0].shape
    FF = w_gate.shape[3]
    nb = FFN_SAMPLES_PER_STEP if B % FFN_SAMPLES_PER_STEP == 0 else 1
    rows = [pl.BlockSpec((nb, 1, xe.shape[2], D), lambda e, b: (b, e, 0, 0)) for xe in xes]
    return pl.pallas_call(
        functools.partial(_expert_kernel, n_streams=len(xes)),
        grid=(E, B // nb),
        in_specs=rows + [pl.BlockSpec((1, 1, D, FF), lambda e, b: (li, e, 0, 0)),
                         pl.BlockSpec((1, 1, D, FF), lambda e, b: (li, e, 0, 0)),
                         pl.BlockSpec((1, 1, FF, D), lambda e, b: (li, e, 0, 0))],
        out_specs=rows,
        out_shape=[jax.ShapeDtypeStruct(xe.shape, BF16) for xe in xes],
        scratch_shapes=[pltpu.VMEM((D, FF), BF16), pltpu.VMEM((D, FF), BF16), pltpu.VMEM((FF, D), BF16)],
        compiler_params=_cparams("arbitrary", "arbitrary"),
        name="expert_ffn",
    )(*xes, w_gate, w_up, w_down)


def _moe_combine_kernel(ts_ref, tab_ref, ye_ref, x_ref, g2_ref, o_ref, acc_ref, *, win, group):
    b, k = pl.program_id(0), pl.program_id(1)
    E, cap = ye_ref.shape[1], ye_ref.shape[2]
    tt = x_ref.shape[1]
    nt = pl.num_programs(1)
    col = lax.broadcasted_iota(jnp.int32, (tt, win), 1)
    lo = [ts_ref[(b * E + e) * (nt + 1) + k] for e in range(E)]
    hi = [ts_ref[(b * E + e) * (nt + 1) + k + 1] for e in range(E)]

    def contribution(w):
        total = None
        for g0 in range(0, E, group):
            lhs, rhs = [], []
            for e in range(g0, g0 + group):
                start, lower = _window(lo[e], w, win, cap)
                rk = tab_ref[0, :, E + e:E + e + 1].astype(jnp.int32)
                hit = jnp.where(rk >= lower, rk, -1) == col + start
                lhs.append(jnp.where(hit, tab_ref[0, :, e:e + 1], 0.0).astype(BF16))
                rhs.append(ye_ref[0, e, pl.ds(start, win), :])
            d = jnp.dot(jnp.concatenate(lhs, axis=1), jnp.concatenate(rhs, axis=0), preferred_element_type=F32)
            total = d if total is None else total + d
        return total

    acc_ref[...] = contribution(0)
    need = jnp.int32(0)
    for e in range(E):
        need = jnp.maximum(need, hi[e] - (lo[e] // ROW_ALIGN) * ROW_ALIGN)

    @pl.when(need > win)
    def _():
        def extra(w, carry):
            @pl.when(need > w * win)
            def _():
                acc_ref[...] += contribution(w)
            return carry

        lax.fori_loop(1, _extra_windows(cap, tt, win) + 1, extra, 0)

    o_ref[0] = x_ref[0] + g2_ref[0] * acc_ref[...]


def moe_combine(ts_flat, table, ye, x, g2, *, tt):
    B, N, D = x.shape
    E, cap = ye.shape[1], ye.shape[2]
    tok = lambda w: pl.BlockSpec((1, tt, w), lambda b, k, ts: (b, k, 0))
    return pl.pallas_call(
        functools.partial(_moe_combine_kernel, win=min(LANES, cap), group=4),
        grid_spec=pltpu.PrefetchScalarGridSpec(
            num_scalar_prefetch=1,
            grid=(B, N // tt),
            in_specs=[tok(LANES),
                      pl.BlockSpec((1, E, cap, D), lambda b, k, ts: (b, 0, 0, 0)),
                      tok(D),
                      pl.BlockSpec((1, 1, D), lambda b, k, ts: (b, 0, 0))],
            out_specs=tok(D),
            scratch_shapes=[pltpu.VMEM((tt, D), F32)]),
        out_shape=jax.ShapeDtypeStruct((B, N, D), F32),
        compiler_params=_cparams("parallel", "arbitrary"),
        name="moe_combine",
    )(ts_flat, table, ye, x, g2)


def moe_residual(streams, w_gate, w_up, w_down, li):
    routed = []
    for (x, h, logits), _ in streams:
        B, N, D = x.shape
        cap = EC_CAPACITY * N // N_EXPERTS
        tt = min(4 * LANES, N)
        table, rank, ts = route(logits, cap, tt)
        ts_flat = ts[:, :, :N // tt + 1].reshape(-1)
        routed.append((moe_gather(ts_flat, rank, h, cap, tt=tt), ts_flat, table, tt))
    yes = expert_ffn([r[0] for r in routed], w_gate, w_up, w_down, li)
    return [moe_combine(ts_flat, table, ye, x, gate2, tt=tt)
            for ((x, _, _), gate2), ye, (_, ts_flat, table, tt) in zip(streams, yes, routed)]


def lambda_init(layer):
    return 0.8 - 0.6 * math.exp(-0.3 * layer)


def even_layer(x, xc, mods, cmods, norm1_g, w_in, w_out, lb, hgrn_g, qn_g, kn_g, lam_vec, subln_g,
               lam_init, ctx_out, tables):
    sh1, sc1, g1, router = mods
    csh1, csc1, cg1, crouter = cmods
    B = x.shape[0]
    half = w_in[0].shape[2] // 2
    proj = norm_mod_matmul(x, norm1_g, sh1, sc1, w_in, cols=(0, half), tm=1024)
    proj2 = norm_mod_matmul(x, norm1_g, sh1, sc1, w_in, cols=(1, half), tm=1024, out_dtype=BF16)
    projc = norm_mod_matmul(xc, norm1_g, csh1, csc1, w_in, cols=(0, half))
    projc2 = norm_mod_matmul(xc, norm1_g, csh1, csc1, w_in, cols=(1, half), out_dtype=BF16)
    s0 = jnp.zeros((B, HGRN_HEADS, 2, HGRN_HEAD_DIM, HGRN_HEAD_DIM), F32)
    oc_f, oc_b, s_ctx = hgrn_scan(projc, lb, s0)
    o_f, o_b, _ = hgrn_scan(proj, lb, s_ctx)
    lv = lam_vec.astype(F32)
    lam = jnp.exp(jnp.sum(lv[0] * lv[1])) - jnp.exp(jnp.sum(lv[2] * lv[3])) + lam_init
    bound = 1.01 * math.sqrt(DIFF_HEAD_DIM) * jnp.max(jnp.abs(qn_g)) * jnp.max(jnp.abs(kn_g))
    lam = jnp.stack([lam, bound, (bound <= SCORE_BOUND_MAX).astype(F32)])
    cos, sin = tables
    q, k, v = qkv_prep(proj2, qn_g, kn_g, cos, sin, rotary=True)
    qc, kc, vc = qkv_prep(projc2, qn_g, kn_g, cos[:xc.shape[1]], sin[:xc.shape[1]], rotary=False)
    y_d = diff_attention(q, [(k, v), (kc, vc)], lam, subln_g, 1.0 - lam_init)
    x_new = even_out(x, o_f, o_b, proj2, y_d, hgrn_g.reshape(-1), g1, w_out, router)
    if not ctx_out:
        return x_new, None
    yc_d = diff_attention(qc, [(kc, vc)], lam, subln_g, 1.0 - lam_init)
    xc_new = even_out(xc, oc_f, oc_b, projc2, yc_d, hgrn_g.reshape(-1), cg1, w_out, crouter)
    return x_new, xc_new


def conv_layer(x, mods, norm1_g, w_in, conv_w, w_out):
    sh1, sc1, g1, router = mods
    proj = norm_mod_matmul(x, norm1_g, sh1, sc1, w_in, out_dtype=BF16)
    return conv_out(x, proj, conv_w, g1, w_out, router)


def kernel(x, c, ctx, c_ctx, mod_w, mod_b, norm1_g, norm2_g, even_w_in, even_w_out, hgrn_lb_logits, hgrn_norm_g,
           diff_qnorm_g, diff_knorm_g, diff_lambda, diff_subln_g, conv_w_in, conv_w, conv_w_out, router_w,
           exp_w_gate, exp_w_up, exp_w_down):
    depth = mod_w.shape[0]
    B, T, D = x.shape
    lb_soft = jax.nn.softmax(hgrn_lb_logits.astype(F32), axis=0)
    lower_bounds = jnp.cumsum(lb_soft, axis=0) - lb_soft[:1]
    last_ctx_layer = 2 * ((depth - 1) // 2)
    cond = jnp.concatenate([c, c_ctx[None, :], jnp.zeros((8 - (B + 1) % 8, D), F32)], axis=0)
    mods = modulation(jax.nn.silu(cond), mod_w, mod_b)
    tables = rope_tables(T)
    xc = ctx
    for l in range(depth):
        read_ctx = l <= last_ctx_layer
        ctx_out = l < last_ctx_layer
        sh1, sc1, g1, sh2, sc2, g2 = [m[:, None, :] for m in jnp.split(mods[l, :B], MOD_CHUNKS, axis=-1)]
        if read_ctx:
            csh1, csc1, cg1, csh2, csc2, cg2 = [
                jnp.broadcast_to(m[None, None, :], (B, 1, D)) for m in jnp.split(mods[l, B], MOD_CHUNKS, axis=-1)]
        mods_l = (sh1, sc1, g1, (norm2_g[l], sh2, sc2, router_w[l]))
        cmods_l = (csh1, csc1, cg1, (norm2_g[l], csh2, csc2, router_w[l])) if read_ctx else None
        if l % 2 == 0:
            e = l // 2
            s, sc = even_layer(x, xc, mods_l, cmods_l, norm1_g[l],
                               (even_w_in, e), (even_w_out, e), lower_bounds[e],
                               hgrn_norm_g[e], diff_qnorm_g[e], diff_knorm_g[e], diff_lambda[e],
                               diff_subln_g[e], lambda_init(l), ctx_out, tables)
        else:
            j = l // 2
            wi, wo = (conv_w_in, j), (conv_w_out, j)
            s = conv_layer(x, mods_l, norm1_g[l], wi, conv_w[j], wo)
            sc = conv_layer(xc, cmods_l, norm1_g[l], wi, conv_w[j], wo) if ctx_out else None
        outs = moe_residual([(s, g2)] + ([(sc, cg2)] if ctx_out else []), exp_w_gate, exp_w_up, exp_w_down, l)
        x = outs[0]
        if ctx_out:
            xc = outs[1]
    return x
```

```python
import functools
import math

import numpy as np
import jax
import jax.numpy as jnp
from jax import lax
from jax.experimental import pallas as pl
from jax.experimental.pallas import tpu as pltpu

F32 = jnp.float32
BF16 = jnp.bfloat16

EPS = 1e-6
GRID_W = 64
ROPE_THETA = 10000.0
HGRN_HEAD_DIM = 128
HGRN_HEADS = 4
HGRN_WIDTH = HGRN_HEADS * HGRN_HEAD_DIM
DIFF_HEAD_DIM = 64
DIFF_HEADS = 4
DIFF_WIDTH = DIFF_HEADS * 2 * DIFF_HEAD_DIM
N_EXPERTS = 16
EC_CAPACITY = 2
MOD_CHUNKS = 6
SCAN_CHUNK = 64
SCAN_LEVELS = (32, 16, 8, 4, 2, 1)
SCAN_HEADS_PER_STEP = 2
GATHER_EXPERTS_PER_STEP = 2
FFN_SAMPLES_PER_STEP = 4
LANES = 128
ROW_ALIGN = 16
VMEM_LIMIT = 56 * 1024 * 1024
SCORE_BOUND_MAX = 40.0


def _cparams(*sem):
    return pltpu.CompilerParams(dimension_semantics=sem, vmem_limit_bytes=VMEM_LIMIT)


def _row_tile(t, want):
    return want if t % want == 0 else t


def _mod_kernel(s_ref, w_ref, b_ref, o_ref):
    s = s_ref[...]
    w = w_ref[0]
    s_hi = s.astype(BF16)
    s_lo = (s - s_hi.astype(F32)).astype(BF16)
    w_hi = w.astype(BF16)
    w_lo = (w - w_hi.astype(F32)).astype(BF16)
    d = lambda a, b: jnp.dot(a, b, preferred_element_type=F32)
    o_ref[0] = d(s_hi, w_hi) + d(s_hi, w_lo) + d(s_lo, w_hi) + b_ref[0]


def modulation(s, mod_w, mod_b, *, tn=2048):
    R, D = s.shape
    depth, _, N = mod_w.shape
    return pl.pallas_call(
        _mod_kernel,
        grid=(depth, N // tn),
        in_specs=[pl.BlockSpec((R, D), lambda l, j: (0, 0)),
                  pl.BlockSpec((1, D, tn), lambda l, j: (l, 0, j)),
                  pl.BlockSpec((1, 1, tn), lambda l, j: (l, 0, j))],
        out_specs=pl.BlockSpec((1, R, tn), lambda l, j: (l, 0, j)),
        out_shape=jax.ShapeDtypeStruct((depth, R, N), F32),
        compiler_params=_cparams("parallel", "parallel"),
        name="modulation",
    )(s, mod_w, mod_b.reshape(depth, 1, N))


def _nmm_kernel(x_ref, g_ref, sh_ref, sc_ref, w_ref, o_ref, w_scr):
    @pl.when((pl.program_id(1) == 0) & (pl.program_id(2) == 0))
    def _():
        w_scr[...] = w_ref[0].astype(BF16)

    x = x_ref[0]
    ms = jnp.mean(x * x, axis=-1, keepdims=True)
    h = (x * lax.rsqrt(ms + EPS)) * g_ref[...]
    h = h * (1.0 + sc_ref[0]) + sh_ref[0]
    o_ref[0] = jnp.dot(h.astype(BF16), w_scr[...], preferred_element_type=F32).astype(o_ref.dtype)


def norm_mod_matmul(x, g, shift, scale, w, *, cols=None, tm=512, out_dtype=F32):
    B, T, D = x.shape
    w, li = w
    first, N = cols if cols else (0, w.shape[2])
    tm = _row_tile(T, tm)
    tn = N
    return pl.pallas_call(
        _nmm_kernel,
        grid=(N // tn, B, T // tm),
        in_specs=[
            pl.BlockSpec((1, tm, D), lambda j, b, i: (b, i, 0)),
            pl.BlockSpec((1, D), lambda j, b, i: (0, 0)),
            pl.BlockSpec((1, 1, D), lambda j, b, i: (b, 0, 0)),
            pl.BlockSpec((1, 1, D), lambda j, b, i: (b, 0, 0)),
            pl.BlockSpec((1, D, tn), lambda j, b, i: (li, 0, first + j)),
        ],
        out_specs=pl.BlockSpec((1, tm, tn), lambda j, b, i: (b, i, j)),
        out_shape=jax.ShapeDtypeStruct((B, T, N), out_dtype),
        scratch_shapes=[pltpu.VMEM((D, tn), BF16)],
        compiler_params=_cparams("arbitrary", "arbitrary", "arbitrary"),
        name="norm_mod_matmul",
    )(x, g.reshape(1, D), shift, scale, w)


def _scan_constants():
    C = SCAN_CHUNK
    t = np.arange(C)[:, None]
    u = np.arange(C)[None, :]
    mats = [u <= t, u > t]
    masks = []
    for w in SCAN_LEVELS:
        m = (t // (2 * w)) * 2 * w + w - 1
        later = (t // w) % 2 == 1
        mats.append(np.where(later, (u > m) & (u <= t), (u > t) & (u <= m)))
        masks.append(later & ((u // w) % 2 == 0) & (u // (2 * w) == t // (2 * w)))
    masks = [t == u] + masks + [np.ones((C, C), bool)]
    a_f = np.stack(mats).astype(np.float32)
    m_f = np.stack(masks).astype(np.float32)
    a = np.stack([a_f, a_f[:, ::-1, ::-1]]).reshape(2, -1, C)
    m = np.stack([m_f, m_f[:, ::-1, ::-1]])
    m_pairs = np.concatenate([m[:, 0::2], m[:, 1::2]], axis=3)
    return np.concatenate([a, a], axis=2), m_pairs


def _scan_pair(q, z, v, lb, st, a2, mask_ref, d, later):
    C = SCAN_CHUNK
    W = HGRN_HEAD_DIM
    nl = len(SCAN_LEVELS)
    nt = lambda x, y: lax.dot_general(x, y, (((1,), (1,)), ((), ())), preferred_element_type=F32)
    nn = lambda x, y: jnp.dot(x, y, preferred_element_type=F32)
    e_abs = jnp.exp(-jnp.abs(z))
    r = 1.0 / (1.0 + e_abs)
    er = e_abs * r
    pos = z >= 0.0
    g2 = jnp.log2(lb + (1.0 - lb) * jnp.where(pos, r, er))
    k = (1.0 - lb) * jnp.where(pos, er, r)
    hi = g2.astype(BF16)
    lo = (g2 - hi.astype(F32)).astype(BF16)
    gs = jnp.concatenate([jnp.concatenate([hi[:C], hi[C:]], axis=1),
                          jnp.concatenate([lo[:C], lo[C:]], axis=1)], axis=0)
    x = nn(a2, gs)
    yield
    x = jnp.exp2(x)
    first, second = (1, 0) if d else (0, 1)
    vb = v.astype(BF16)
    sides, qt, kt, dec = [], [], [], []
    for c in (0, 1):
        qc, kc, xs = q[c * C:(c + 1) * C], k[c * C:(c + 1) * C], x[:, c * W:(c + 1) * W]
        ops = [(qc.astype(BF16), kc.astype(BF16))]
        for i in range(nl):
            qk = (jnp.where(later[i], qc, kc) * xs[(2 + i) * C:(3 + i) * C]).astype(BF16)
            ops.append((qk, qk))
        sides.append(ops)
        qt.append(qc * xs[0:C])
        kt.append(kc * xs[C:2 * C])
        dec.append(xs[0:1, :] if d else xs[C - 1:C, :])
    zero = jnp.zeros((C, W), BF16)
    sides[first].append((zero, zero))
    sides[second].append((qt[second].astype(BF16), kt[first].astype(BF16)))
    prods = []
    for ops in sides:
        pair = []
        for j in range(0, nl + 2, 2):
            (qa, ka), (qb, kb) = ops[j], ops[j + 1]
            rhs = jnp.concatenate([jnp.concatenate([ka, zero], axis=1), jnp.concatenate([zero, kb], axis=1)], axis=0)
            pair.append(nt(jnp.concatenate([qa, qb], axis=1), rhs))
        prods.append(pair)
        yield
    qt[second] = qt[second] * dec[first]
    kt[first] = kt[first] * dec[second]
    o_st = nt(jnp.concatenate([a.astype(BF16) for a in qt], axis=0), st.astype(BF16))
    upd = lax.dot_general(vb, jnp.concatenate(kt, axis=0).astype(BF16), (((0,), (0,)), ((), ())),
                          preferred_element_type=F32)
    yield
    o = [None, None]
    n_main = nl // 2
    for c, pair in enumerate(prods):
        main = mask_ref[d, 0] * pair[0]
        for j in range(1, n_main):
            main = main + mask_ref[d, j] * pair[j]
        last = mask_ref[d, n_main] * pair[n_main]
        vc, vf = vb[c * C:(c + 1) * C], vb[first * C:(first + 1) * C]
        o[c] = o_st[c * C:(c + 1) * C] + nn(jnp.concatenate([main, last], axis=1).astype(BF16),
                                            jnp.concatenate([vc, vc, vc, vf], axis=0))
    yield
    return jnp.concatenate(o, axis=0), st * (dec[0] * dec[1]) + upd


def _run_interleaved(gens):
    results = [None] * len(gens)
    live = list(range(len(gens)))
    while live:
        for i in list(live):
            try:
                next(gens[i])
            except StopIteration as done:
                results[i] = done.value
                live.remove(i)
    return results


def _scan_kernel(qf_ref, zf_ref, vf_ref, qb_ref, zb_ref, vb_ref, lbf_ref, lbb_ref, s0_ref, a_ref, mask_ref,
                 of_ref, ob_ref, sT_ref, st_scr, *, n_chunks):
    c = pl.program_id(2)
    C = SCAN_CHUNK

    @pl.when(c == 0)
    def _():
        st_scr[...] = s0_ref[0]

    W = HGRN_HEAD_DIM
    row = lax.broadcasted_iota(jnp.int32, (C, W), 0)
    later = [[(row // w) % 2 == 1 for w in SCAN_LEVELS], [((C - 1 - row) // w) % 2 == 1 for w in SCAN_LEVELS]]
    in_refs = [(qf_ref, zf_ref, vf_ref), (qb_ref, zb_ref, vb_ref)]
    lb_refs = [lbf_ref, lbb_ref]
    o_refs = [of_ref, ob_ref]
    n_pairs = n_chunks // 2
    heads = range(qf_ref.shape[2] // W)

    def body(i, carry):
        r0 = [pl.multiple_of(i * 2 * C, 2 * C), pl.multiple_of((n_pairs - 1 - i) * 2 * C, 2 * C)]
        chains = [(d, h) for h in heads for d in (0, 1)]
        ins = [[ref[0, pl.ds(r0[d], 2 * C), h * W:(h + 1) * W] for ref in in_refs[d]] for d, h in chains]
        sts = [st_scr[h, d] for d, h in chains]
        outs = _run_interleaved([
            _scan_pair(*x, lb_refs[d][:, h * W:(h + 1) * W], st, a_ref[d], mask_ref, d, later[d])
            for (d, h), x, st in zip(chains, ins, sts)])
        for (d, h), (o, st) in zip(chains, outs):
            o_refs[d][0, pl.ds(r0[d], 2 * C), h * W:(h + 1) * W] = o.astype(o_refs[d].dtype)
            st_scr[h, d] = st
        return carry

    lax.fori_loop(0, n_pairs, body, 0)

    @pl.when(c == pl.num_programs(2) - 1)
    def _():
        sT_ref[0] = st_scr[...]


def hgrn_scan(proj, lb, s0, *, tb=2048):
    B, T, _ = proj.shape
    tb = _row_tile(T, tb)
    nc = T // tb
    H = HGRN_HEADS
    hd = HGRN_HEAD_DIM
    hps = SCAN_HEADS_PER_STEP
    G = H // hps
    wd = hps * hd
    fwd = lambda grp: pl.BlockSpec((1, tb, wd), lambda b, h, c: (b, c, grp * G + h))
    bwd = lambda grp: pl.BlockSpec((1, tb, wd), lambda b, h, c: (b, nc - 1 - c, grp * G + h))
    kern = functools.partial(_scan_kernel, n_chunks=tb // SCAN_CHUNK)
    a2, masks = _scan_constants()
    return pl.pallas_call(
        kern,
        grid=(B, G, nc),
        in_specs=[fwd(0), fwd(1), fwd(3), bwd(0), bwd(2), bwd(3),
                  pl.BlockSpec((1, wd), lambda b, h, c: (0, h)),
                  pl.BlockSpec((1, wd), lambda b, h, c: (0, h)),
                  pl.BlockSpec((1, hps, 2, hd, hd), lambda b, h, c: (b, h, 0, 0, 0)),
                  pl.BlockSpec(a2.shape, lambda b, h, c: (0, 0, 0)),
                  pl.BlockSpec(masks.shape, lambda b, h, c: (0, 0, 0, 0))],
        out_specs=[pl.BlockSpec((1, tb, wd), lambda b, h, c: (b, c, h)),
                   pl.BlockSpec((1, tb, wd), lambda b, h, c: (b, nc - 1 - c, h)),
                   pl.BlockSpec((1, hps, 2, hd, hd), lambda b, h, c: (b, h, 0, 0, 0))],
        out_shape=[jax.ShapeDtypeStruct((B, T, HGRN_WIDTH), BF16),
                   jax.ShapeDtypeStruct((B, T, HGRN_WIDTH), BF16),
                   jax.ShapeDtypeStruct((B, H, 2, hd, hd), F32)],
        scratch_shapes=[pltpu.VMEM((hps, 2, hd, hd), F32)],
        compiler_params=_cparams("parallel", "parallel", "arbitrary"),
        name="hgrn_scan",
    )(proj, proj, proj, proj, proj, proj, lb[0:1], lb[1:2], s0, jnp.asarray(a2, BF16), jnp.asarray(masks, F32))


def _group_mean_sq(x, gmat):
    sq = x * x
    hi = sq.astype(BF16)
    lo = (sq - hi.astype(F32)).astype(BF16)
    return (jnp.dot(hi, gmat, preferred_element_type=F32) + jnp.dot(lo, gmat, preferred_element_type=F32))


def _qkv_prep_kernel(q_ref, k_ref, v_ref, qg_ref, kg_ref, cos_ref, sin_ref, qo_ref, ko_ref, vo_ref, *, rotary):
    W = LANES
    r_i = lax.broadcasted_iota(jnp.int32, (W, W), 0) // DIFF_HEAD_DIM
    c_i = lax.broadcasted_iota(jnp.int32, (W, W), 1) // DIFF_HEAD_DIM
    gmat = jnp.where(r_i == c_i, 1.0 / DIFF_HEAD_DIM, 0.0).astype(BF16)
    lane = lax.broadcasted_iota(jnp.int32, (1, W), 1)
    first = (lane % 32) < 16

    def prep(x, g, scale):
        y = (x * lax.rsqrt(_group_mean_sq(x, gmat) + EPS)) * g
        if rotary:
            partner = jnp.where(first, pltpu.roll(y, W - 16, axis=1), pltpu.roll(y, 16, axis=1))
            y = y * cos_ref[...] + partner * sin_ref[...]
        if scale != 1.0:
            y = y * scale
        return y

    for h in range(DIFF_HEADS):
        sl = slice(h * W, (h + 1) * W)
        qo_ref[0, sl, :] = prep(q_ref[0, :, sl].astype(F32), qg_ref[...], DIFF_HEAD_DIM ** -0.5).T.astype(BF16)
        ko_ref[0, :, sl] = prep(k_ref[0, :, sl].astype(F32), kg_ref[...], 1.0).astype(BF16)
        vo_ref[0, sl, :] = v_ref[0, :, sl].astype(F32).T.astype(BF16)


def qkv_prep(proj, qg, kg, cos, sin, *, rotary, tm=1024):
    B, T, _ = proj.shape
    tm = _row_tile(T, tm)
    Wd = DIFF_WIDTH
    col = lambda j: pl.BlockSpec((1, tm, Wd), lambda b, i: (b, i, j))
    vec = pl.BlockSpec((1, LANES), lambda b, i: (0, 0))
    tab = pl.BlockSpec((tm, LANES), lambda b, i: (i, 0))
    rows = pl.BlockSpec((1, tm, Wd), lambda b, i: (b, i, 0))
    cols = pl.BlockSpec((1, Wd, tm), lambda b, i: (b, 0, i))
    return pl.pallas_call(
        functools.partial(_qkv_prep_kernel, rotary=rotary),
        grid=(B, T // tm),
        in_specs=[col(1), col(2), col(3), vec, vec, tab, tab],
        out_specs=[cols, rows, cols],
        out_shape=[jax.ShapeDtypeStruct((B, Wd, T), BF16), jax.ShapeDtypeStruct((B, T, Wd), BF16),
                   jax.ShapeDtypeStruct((B, Wd, T), BF16)],
        compiler_params=_cparams("parallel", "parallel"),
        name="qkv_prep",
    )(proj, proj, proj, jnp.tile(qg, 2).reshape(1, LANES), jnp.tile(kg, 2).reshape(1, LANES), cos, sin)


def rope_tables(T):
    n = DIFF_HEAD_DIM // 2
    inv = 1.0 / (ROPE_THETA ** (jnp.arange(0, n, 2, dtype=F32) / n))
    t = jnp.arange(T)
    ang_r = (t // GRID_W).astype(F32)[:, None] * inv[None, :]
    ang_c = (t % GRID_W).astype(F32)[:, None] * inv[None, :]
    cos = jnp.concatenate([jnp.cos(ang_r)] * 2 + [jnp.cos(ang_c)] * 2, axis=-1)
    sin = jnp.concatenate([-jnp.sin(ang_r), jnp.sin(ang_r), -jnp.sin(ang_c), jnp.sin(ang_c)], axis=-1)
    return jnp.tile(cos, (1, 2)), jnp.tile(sin, (1, 2))


def _attn_tile(qt, kv_refs, lam, key_chunk, bound=None):
    tq = qt.shape[1]
    row = lax.broadcasted_iota(jnp.int32, (LANES, 1), 0)
    zero = jnp.zeros_like(qt)
    qq = jnp.concatenate([jnp.where(row < DIFF_HEAD_DIM, qt, zero),
                          jnp.where(row >= DIFF_HEAD_DIM, qt, zero)], axis=1)
    m = jnp.full((1, 2 * tq), -jnp.inf, F32)
    acc = [jnp.zeros((LANES + 16, tq), F32), jnp.zeros((LANES + 16, tq), F32)]
    ones = jnp.ones((16, key_chunk), BF16)
    chunks = [(k_ref, v_ref, c0, min(c0 + key_chunk, k_ref.shape[1]))
              for k_ref, v_ref in kv_refs for c0 in range(0, k_ref.shape[1], key_chunk)]
    scores = lambda c: jnp.dot(c[0][0, c[2]:c[3], :], qq, preferred_element_type=F32)
    s_next = scores(chunks[0])
    yield
    for n, (_, v_ref, c0, c1) in enumerate(chunks):
        s = s_next
        if n + 1 < len(chunks):
            s_next = scores(chunks[n + 1])
        vt1 = jnp.concatenate([v_ref[0, :, c0:c1], ones[:, :c1 - c0]], axis=0)
        if bound is None:
            m_new = jnp.maximum(m, jnp.max(s, axis=0, keepdims=True))
            alpha = jnp.exp(m - m_new)
            pb = jnp.exp((s - m_new).astype(BF16))
            m = m_new
        else:
            alpha = None
            pb = jnp.exp(s - bound).astype(BF16)
        for i in range(2):
            pv = jnp.dot(vt1, pb[:, i * tq:(i + 1) * tq], preferred_element_type=F32)
            acc[i] = acc[i] + pv if alpha is None else acc[i] * alpha[:, i * tq:(i + 1) * tq] + pv
        yield
    inv = [1.0 / a[LANES:LANES + 1] for a in acc]
    return acc[0][:LANES] * inv[0] - acc[1][:LANES] * (lam * inv[1])


def _diff_attn_kernel(par_ref, q_ref, *rest, out_scale, key_chunk, tq):
    kv_refs, g_ref, o_ref = list(zip(rest[:-2:2], rest[1:-2:2])), rest[-2], rest[-1]
    n_tiles = q_ref.shape[2] // tq

    def run(bound):
        outs = _run_interleaved([
            _attn_tile(q_ref[0, :, i * tq:(i + 1) * tq], kv_refs, par_ref[0], key_chunk, bound)
            for i in range(n_tiles)])
        for i, o in enumerate(outs):
            ms = jnp.mean(o * o, axis=0, keepdims=True)
            o_ref[0, i * tq:(i + 1) * tq, :] = ((o * lax.rsqrt(ms + EPS)) * g_ref[...] * out_scale).T.astype(BF16)

    @pl.when(par_ref[2] > 0.5)
    def _():
        run(par_ref[1])

    @pl.when(par_ref[2] <= 0.5)
    def _():
        run(None)


def diff_attention(qt, kvs, params, subln_g, out_scale, *, tq=256, tiles_per_step=4, key_chunk=512):
    B, Wd, T = qt.shape
    tq = _row_tile(T, tq)
    ts = _row_tile(T, tq * tiles_per_step)
    kv_specs = [spec for k, _ in kvs for spec in (
        pl.BlockSpec((1, k.shape[1], LANES), lambda b, h, i: (b, 0, h)),
        pl.BlockSpec((1, LANES, k.shape[1]), lambda b, h, i: (b, h, 0)))]
    return pl.pallas_call(
        functools.partial(_diff_attn_kernel, out_scale=out_scale, key_chunk=key_chunk, tq=tq),
        grid=(B, DIFF_HEADS, T // ts),
        in_specs=[pl.BlockSpec(memory_space=pltpu.SMEM),
                  pl.BlockSpec((1, LANES, ts), lambda b, h, i: (b, h, i))] + kv_specs + [
                  pl.BlockSpec((LANES, 1), lambda b, h, i: (0, 0))],
        out_specs=pl.BlockSpec((1, ts, LANES), lambda b, h, i: (b, i, h)),
        out_shape=jax.ShapeDtypeStruct((B, T, Wd), BF16),
        compiler_params=_cparams("parallel", "parallel", "arbitrary"),
        name="diff_attention",
    )(params, qt, *[a for kv in kvs for a in kv], subln_g.reshape(LANES, 1))


def _even_out_kernel(x_ref, of_ref, ob_ref, gate_ref, yd_ref, hg_ref, g1_ref, w_ref, *rest):
    router_refs, o_ref, moe_refs = rest[:-3], rest[-3], rest[-2:]
    w = w_ref[0].astype(BF16)
    acc = jnp.dot(yd_ref[0], w[HGRN_WIDTH:], preferred_element_type=F32)
    for h in range(HGRN_HEADS):
        sl = slice(h * HGRN_HEAD_DIM, (h + 1) * HGRN_HEAD_DIM)
        o = of_ref[0, :, sl].astype(F32) + ob_ref[0, :, sl].astype(F32)
        ms = jnp.mean(o * o, axis=-1, keepdims=True)
        gate = gate_ref[0, :, sl].astype(F32)
        yh = (o * lax.rsqrt(ms + EPS)) * hg_ref[:, sl] * (gate * jax.nn.sigmoid(gate))
        acc = acc + jnp.dot(yh.astype(BF16), w[sl], preferred_element_type=F32)
    x_new = x_ref[0] + g1_ref[0] * acc
    o_ref[0] = x_new
    _router_outputs(x_new, *router_refs, *moe_refs)


def even_out(x, o_f, o_b, proj, y_d, hgrn_g, g1, w_out, router, *, tm=1024):
    B, T, D = x.shape
    w_out, li = w_out
    tm = _row_tile(T, tm)
    Wd = HGRN_WIDTH
    row = lambda w, j: pl.BlockSpec((1, tm, w), lambda b, i: (b, i, j))
    r_args, r_in, r_out, r_shapes = _router_specs(B, T, D, tm, router)
    return pl.pallas_call(
        _even_out_kernel,
        grid=(B, T // tm),
        in_specs=[row(D, 0), row(Wd, 0), row(Wd, 0), row(Wd, 0), row(Wd, 0),
                  pl.BlockSpec((1, Wd), lambda b, i: (0, 0)),
                  pl.BlockSpec((1, 1, D), lambda b, i: (b, 0, 0)),
                  pl.BlockSpec((1,) + w_out.shape[1:], lambda b, i: (li, 0, 0))] + r_in,
        out_specs=[row(D, 0)] + r_out,
        out_shape=[jax.ShapeDtypeStruct((B, T, D), F32)] + r_shapes,
        compiler_params=_cparams("parallel", "parallel"),
        name="even_out",
    )(x, o_f, o_b, proj, y_d, hgrn_g.reshape(1, Wd), g1, w_out, *r_args)


def _conv_out_kernel(x_ref, bg_ref, cg_ref, v_ref, cp_ref, vp_ref, cn_ref, vn_ref, cw_ref, g1_ref, w_ref, *rest):
    router_refs, o_ref, moe_refs = rest[:-3], rest[-3], rest[-2:]
    i = pl.program_id(1)
    n = pl.num_programs(1)
    f32 = lambda ref, *idx: ref[idx].astype(F32)
    u = f32(cg_ref, 0) * f32(v_ref, 0)
    tm = u.shape[0]
    last = ROW_ALIGN - 1
    u_prev_row = jnp.where(i > 0, f32(cp_ref, 0, slice(last, last + 1)) * f32(vp_ref, 0, slice(last, last + 1)), 0.0)
    u_next_row = jnp.where(i < n - 1, f32(cn_ref, 0, slice(0, 1)) * f32(vn_ref, 0, slice(0, 1)), 0.0)
    ridx = lax.broadcasted_iota(jnp.int32, (tm, 1), 0)
    u_prev = jnp.where(ridx == 0, u_prev_row, pltpu.roll(u, 1, axis=0))
    u_next = jnp.where(ridx == tm - 1, u_next_row, pltpu.roll(u, tm - 1, axis=0))
    y = cw_ref[0:1, :] * u_prev + cw_ref[1:2, :] * u + cw_ref[2:3, :] * u_next
    acc = jnp.dot((f32(bg_ref, 0) * y).astype(BF16), w_ref[0].astype(BF16), preferred_element_type=F32)
    x_new = x_ref[0] + g1_ref[0] * acc
    o_ref[0] = x_new
    _router_outputs(x_new, *router_refs, *moe_refs)


def conv_out(x, proj, conv_w, g1, w_out, router, *, tm=1024):
    B, T, D = x.shape
    w_out, li = w_out
    tm = _row_tile(T, tm)
    rt = tm // ROW_ALIGN
    last_blk = T // ROW_ALIGN - 1
    row = lambda j: pl.BlockSpec((1, tm, D), lambda b, i: (b, i, j))
    prev = lambda j: pl.BlockSpec((1, ROW_ALIGN, D), lambda b, i: (b, jnp.maximum(i * rt - 1, 0), j))
    nxt = lambda j: pl.BlockSpec((1, ROW_ALIGN, D), lambda b, i: (b, jnp.minimum((i + 1) * rt, last_blk), j))
    r_args, r_in, r_out, r_shapes = _router_specs(B, T, D, tm, router)
    return pl.pallas_call(
        _conv_out_kernel,
        grid=(B, T // tm),
        in_specs=[row(0), row(0), row(1), row(2), prev(1), prev(2), nxt(1), nxt(2),
                  pl.BlockSpec((8, D), lambda b, i: (0, 0)),
                  pl.BlockSpec((1, 1, D), lambda b, i: (b, 0, 0)),
                  pl.BlockSpec((1,) + w_out.shape[1:], lambda b, i: (li, 0, 0))] + r_in,
        out_specs=[row(0)] + r_out,
        out_shape=[jax.ShapeDtypeStruct((B, T, D), F32)] + r_shapes,
        compiler_params=_cparams("parallel", "parallel"),
        name="conv_out",
    )(x, proj, proj, proj, proj, proj, proj, proj,
      jnp.concatenate([conv_w, jnp.zeros((8 - conv_w.shape[0], D), conv_w.dtype)], axis=0), g1, w_out, *r_args)


def _router_outputs(x, g_ref, sh_ref, sc_ref, rwt_ref, h_ref, lg_ref):
    ms = jnp.mean(x * x, axis=-1, keepdims=True)
    h = (x * lax.rsqrt(ms + EPS)) * g_ref[...]
    h = h * (1.0 + sc_ref[0]) + sh_ref[0]
    h_hi = h.astype(BF16)
    h_ref[0] = h_hi
    h_lo = (h - h_hi.astype(F32)).astype(BF16)
    rwt = rwt_ref[...]
    w_hi = rwt.astype(BF16)
    w_lo = (rwt - w_hi.astype(F32)).astype(BF16)
    d = lambda a, b: lax.dot_general(a, b, (((1,), (1,)), ((), ())), preferred_element_type=F32)
    lg_ref[0] = d(w_hi, h_hi) + d(w_lo, h_hi) + d(w_hi, h_lo)


def _router_specs(B, T, D, tm, router):
    g, shift, scale, router_w = router
    E = router_w.shape[1]
    vec = pl.BlockSpec((1, 1, D), lambda b, i: (b, 0, 0))
    return ((g.reshape(1, D), shift, scale, router_w.T),
            [pl.BlockSpec((1, D), lambda b, i: (0, 0)), vec, vec, pl.BlockSpec((E, D), lambda b, i: (0, 0))],
            [pl.BlockSpec((1, tm, D), lambda b, i: (b, i, 0)), pl.BlockSpec((1, E, tm), lambda b, i: (b, 0, i))],
            [jax.ShapeDtypeStruct((B, T, D), BF16), jax.ShapeDtypeStruct((B, E, T), F32)])


def _lane_prefix(flags, tri_tot):
    E, N = flags.shape
    carries = [jnp.zeros((E, LANES), F32)]
    out = []
    for j in range(N // LANES):
        r = jnp.dot(flags[:, j * LANES:(j + 1) * LANES].astype(BF16), tri_tot, preferred_element_type=F32)
        out.append(r[:, :LANES] + carries[-1])
        carries.append(carries[-1] + r[:, LANES:])
    return jnp.concatenate(out, axis=1), carries


def _route_kernel(lg_ref, tab_ref, rank_ref, ts_ref, *, cap, tt):
    lg = lg_ref[...]
    B, E, N = lg.shape
    p = jnp.exp(lg - jnp.max(lg, axis=1, keepdims=True))
    R = B * E
    aff = (p / jnp.sum(p, axis=1, keepdims=True)).reshape(R, N)
    count = lambda m: jnp.sum(jnp.where(m, 1.0, 0.0), axis=1, keepdims=True)
    as_float = lambda i: pltpu.bitcast(i, F32)

    def refine_bits(i, thr):
        cand = thr | jnp.left_shift(jnp.int32(1), 30 - i)
        return jnp.where(count(aff >= as_float(cand)) >= cap, cand, thr)

    thr = lax.fori_loop(0, 31, refine_bits, jnp.zeros((R, 1), jnp.int32))

    def refine_mid(i, lo_hi):
        lo, hi = lo_hi
        mid = 0.5 * (lo + hi)
        up = count(aff >= mid) >= cap
        return jnp.where(up, mid, lo), jnp.where(up, hi, mid)

    lo, hi = lax.fori_loop(0, 24, refine_mid, (as_float(thr), as_float(jnp.maximum(thr + 1, 0x00800000))))
    gt = aff >= hi
    eq = (aff >= lo) & (aff < hi)
    r_i = lax.broadcasted_iota(jnp.int32, (LANES, 2 * LANES), 0)
    c_i = lax.broadcasted_iota(jnp.int32, (LANES, 2 * LANES), 1)
    tri_tot = jnp.where((r_i < c_i) | (c_i >= LANES), 1.0, 0.0).astype(BF16)
    eq_rank, _ = _lane_prefix(jnp.where(eq, 1.0, 0.0), tri_tot)
    sel = gt | (eq & (eq_rank < cap - count(gt)))
    rank, before = _lane_prefix(jnp.where(sel, 1.0, 0.0), tri_tot)
    rank = jnp.where(sel, rank, -1.0)
    rank_ref[...] = rank.astype(jnp.int32).reshape(B, E, N)
    pad = jnp.zeros((LANES - 2 * E, N), F32)
    for b in range(B):
        rows = slice(b * E, (b + 1) * E)
        tab_ref[b] = jnp.concatenate([aff[rows], rank[rows], pad], axis=0).T
    lane = lax.broadcasted_iota(jnp.int32, (R, LANES), 1)
    ts = jnp.zeros((R, LANES), F32)
    for k in range(N // tt + 1):
        ts = jnp.where(lane == k, before[k * tt // LANES], ts)
    ts_ref[...] = ts.astype(jnp.int32).reshape(B, E, LANES)


def route(logits_t, cap, tt):
    B, E, N = logits_t.shape
    whole = lambda *shape: pl.BlockSpec(shape, lambda i: (0,) * len(shape))
    return pl.pallas_call(
        functools.partial(_route_kernel, cap=cap, tt=tt),
        grid=(1,),
        in_specs=[whole(B, E, N)],
        out_specs=[whole(B, N, LANES), whole(B, E, N), whole(B, E, LANES)],
        out_shape=[jax.ShapeDtypeStruct((B, N, LANES), F32), jax.ShapeDtypeStruct((B, E, N), jnp.int32),
                   jax.ShapeDtypeStruct((B, E, LANES), jnp.int32)],
        compiler_params=_cparams("arbitrary"),
        name="route",
    )(logits_t)


def _window(lo, w, win, cap):
    lower = (lo // ROW_ALIGN) * ROW_ALIGN + w * win
    return pl.multiple_of(jnp.minimum(lower, cap - win), ROW_ALIGN), lower


def _extra_windows(cap, tt, win):
    return -(-(min(cap, tt) + ROW_ALIGN - 1) // win) - 1


def _moe_gather_kernel(ts_ref, rank_ref, h_ref, xe_ref, acc_ref, *, win, tt):
    b, g = pl.program_id(0), pl.program_id(1)
    N = h_ref.shape[1]
    ne, cap = xe_ref.shape[1], xe_ref.shape[2]
    nt = N // tt
    acc_ref[...] = jnp.zeros_like(acc_ref)
    row = lax.broadcasted_iota(jnp.int32, (win, tt), 0)
    n_extra = _extra_windows(cap, tt, win)
    for ee in range(ne):
        base = ((b * pl.num_programs(1) + g) * ne + ee) * (nt + 1)

        def place(k, w, ee=ee, base=base):
            start, lower = _window(ts_ref[base + k], w, win, cap)
            c0 = k * tt if isinstance(k, int) else pl.multiple_of(k * tt, tt)
            rk = rank_ref[0, ee, :, pl.ds(c0, tt)]
            onehot = jnp.where(jnp.where(rk >= lower, rk, -1) == row + start, 1.0, 0.0).astype(BF16)
            acc_ref[ee, pl.ds(start, win), :] += jnp.dot(onehot, h_ref[0, pl.ds(c0, tt), :],
                                                         preferred_element_type=F32)

        span = lambda k, base=base: ts_ref[base + k + 1] - (ts_ref[base + k] // ROW_ALIGN) * ROW_ALIGN
        need = jnp.int32(0)
        for k in range(nt):
            place(k, 0)
            need = jnp.maximum(need, span(k))

        @pl.when(need > win)
        def _(place=place, span=span):
            def extra(i, carry):
                k, w = i // n_extra, i % n_extra + 1

                @pl.when(span(k) > w * win)
                def _():
                    place(k, w)
                return carry

            lax.fori_loop(0, nt * n_extra, extra, 0)

    xe_ref[0] = acc_ref[...].astype(BF16)


def moe_gather(ts_flat, rank, h, cap, *, tt):
    B, E, N = rank.shape
    D = h.shape[2]
    ne = GATHER_EXPERTS_PER_STEP
    return pl.pallas_call(
        functools.partial(_moe_gather_kernel, win=min(LANES, cap), tt=tt),
        grid_spec=pltpu.PrefetchScalarGridSpec(
            num_scalar_prefetch=1,
            grid=(B, E // ne),
            in_specs=[pl.BlockSpec((1, ne, 1, N), lambda b, e, ts: (b, e, 0, 0)),
                      pl.BlockSpec((1, N, D), lambda b, e, ts: (b, 0, 0))],
            out_specs=pl.BlockSpec((1, ne, cap, D), lambda b, e, ts: (b, e, 0, 0)),
            scratch_shapes=[pltpu.VMEM((ne, cap, D), F32)]),
        out_shape=jax.ShapeDtypeStruct((B, E, cap, D), BF16),
        compiler_params=_cparams("parallel", "arbitrary"),
        name="moe_gather",
    )(ts_flat, rank.reshape(B, E, 1, N), h)


def _expert_kernel(*refs, n_streams):
    xe_refs = refs[:n_streams]
    wg_ref, wu_ref, wd_ref = refs[n_streams:n_streams + 3]
    o_refs = refs[n_streams + 3:2 * n_streams + 3]
    wg_scr, wu_scr, wd_scr = refs[2 * n_streams + 3:]

    @pl.when(pl.program_id(1) == 0)
    def _():
        wg_scr[...] = wg_ref[0, 0].astype(BF16)
        wu_scr[...] = wu_ref[0, 0].astype(BF16)
        wd_scr[...] = wd_ref[0, 0].astype(BF16)

    def swiglu(x):
        a = jnp.dot(x, wg_scr[...], preferred_element_type=F32)
        u = jnp.dot(x, wu_scr[...], preferred_element_type=F32)
        yield
        hid = (a * jax.nn.sigmoid(a)) * u
        return jnp.dot(hid.astype(BF16), wd_scr[...], preferred_element_type=F32).astype(BF16)

    nb = xe_refs[0].shape[0]
    ys = _run_interleaved([swiglu(jnp.concatenate([r[s, 0] for r in xe_refs], axis=0)) for s in range(nb)])
    for s, y in enumerate(ys):
        r0 = 0
        for o_ref in o_refs:
            o_ref[s, 0] = y[r0:r0 + o_ref.shape[2]]
            r0 += o_ref.shape[2]


def expert_ffn(xes, w_gate, w_up, w_down, li):
    B, E, _, D = xes[0].shape
    FF = w_gate.shape[3]
    nb = FFN_SAMPLES_PER_STEP if B % FFN_SAMPLES_PER_STEP == 0 else 1
    rows = [pl.BlockSpec((nb, 1, xe.shape[2], D), lambda e, b: (b, e, 0, 0)) for xe in xes]
    return pl.pallas_call(
        functools.partial(_expert_kernel, n_streams=len(xes)),
        grid=(E, B // nb),
        in_specs=rows + [pl.BlockSpec((1, 1, D, FF), lambda e, b: (li, e, 0, 0)),
                         pl.BlockSpec((1, 1, D, FF), lambda e, b: (li, e, 0, 0)),
                         pl.BlockSpec((1, 1, FF, D), lambda e, b: (li, e, 0, 0))],
        out_specs=rows,
        out_shape=[jax.ShapeDtypeStruct(xe.shape, BF16) for xe in xes],
        scratch_shapes=[pltpu.VMEM((D, FF), BF16), pltpu.VMEM((D, FF), BF16), pltpu.VMEM((FF, D), BF16)],
        compiler_params=_cparams("arbitrary", "arbitrary"),
        name="expert_ffn",
    )(*xes, w_gate, w_up, w_down)


def _moe_combine_kernel(ts_ref, tab_ref, ye_ref, x_ref, g2_ref, o_ref, acc_ref, *, win, group):
    b, k = pl.program_id(0), pl.program_id(1)
    E, cap = ye_ref.shape[1], ye_ref.shape[2]
    tt = x_ref.shape[1]
    nt = pl.num_programs(1)
    col = lax.broadcasted_iota(jnp.int32, (tt, win), 1)
    lo = [ts_ref[(b * E + e) * (nt + 1) + k] for e in range(E)]
    hi = [ts_ref[(b * E + e) * (nt + 1) + k + 1] for e in range(E)]

    def contribution(w):
        total = None
        for g0 in range(0, E, group):
            lhs, rhs = [], []
            for e in range(g0, g0 + group):
                start, lower = _window(lo[e], w, win, cap)
                rk = tab_ref[0, :, E + e:E + e + 1].astype(jnp.int32)
                hit = jnp.where(rk >= lower, rk, -1) == col + start
                lhs.append(jnp.where(hit, tab_ref[0, :, e:e + 1], 0.0).astype(BF16))
                rhs.append(ye_ref[0, e, pl.ds(start, win), :])
            d = jnp.dot(jnp.concatenate(lhs, axis=1), jnp.concatenate(rhs, axis=0), preferred_element_type=F32)
            total = d if total is None else total + d
        return total

    acc_ref[...] = contribution(0)
    need = jnp.int32(0)
    for e in range(E):
        need = jnp.maximum(need, hi[e] - (lo[e] // ROW_ALIGN) * ROW_ALIGN)

    @pl.when(need > win)
    def _():
        def extra(w, carry):
            @pl.when(need > w * win)
            def _():
                acc_ref[...] += contribution(w)
            return carry

        lax.fori_loop(1, _extra_windows(cap, tt, win) + 1, extra, 0)

    o_ref[0] = x_ref[0] + g2_ref[0] * acc_ref[...]


def moe_combine(ts_flat, table, ye, x, g2, *, tt):
    B, N, D = x.shape
    E, cap = ye.shape[1], ye.shape[2]
    tok = lambda w: pl.BlockSpec((1, tt, w), lambda b, k, ts: (b, k, 0))
    return pl.pallas_call(
        functools.partial(_moe_combine_kernel, win=min(LANES, cap), group=4),
        grid_spec=pltpu.PrefetchScalarGridSpec(
            num_scalar_prefetch=1,
            grid=(B, N // tt),
            in_specs=[tok(LANES),
                      pl.BlockSpec((1, E, cap, D), lambda b, k, ts: (b, 0, 0, 0)),
                      tok(D),
                      pl.BlockSpec((1, 1, D), lambda b, k, ts: (b, 0, 0))],
            out_specs=tok(D),
            scratch_shapes=[pltpu.VMEM((tt, D), F32)]),
        out_shape=jax.ShapeDtypeStruct((B, N, D), F32),
        compiler_params=_cparams("parallel", "arbitrary"),
        name="moe_combine",
    )(ts_flat, table, ye, x, g2)


def moe_residual(streams, w_gate, w_up, w_down, li):
    routed = []
    for (x, h, logits), _ in streams:
        B, N, D = x.shape
        cap = EC_CAPACITY * N // N_EXPERTS
        tt = min(4 * LANES, N)
        table, rank, ts = route(logits, cap, tt)
        ts_flat = ts[:, :, :N // tt + 1].reshape(-1)
        routed.append((moe_gather(ts_flat, rank, h, cap, tt=tt), ts_flat, table, tt))
    yes = expert_ffn([r[0] for r in routed], w_gate, w_up, w_down, li)
    return [moe_combine(ts_flat, table, ye, x, gate2, tt=tt)
            for ((x, _, _), gate2), ye, (_, ts_flat, table, tt) in zip(streams, yes, routed)]


def lambda_init(layer):
    return 0.8 - 0.6 * math.exp(-0.3 * layer)


def even_layer(x, xc, mods, cmods, norm1_g, w_in, w_out, lb, hgrn_g, qn_g, kn_g, lam_vec, subln_g,
               lam_init, ctx_out, tables):
    sh1, sc1, g1, router = mods
    csh1, csc1, cg1, crouter = cmods
    B = x.shape[0]
    half = w_in[0].shape[2] // 2
    proj = norm_mod_matmul(x, norm1_g, sh1, sc1, w_in, cols=(0, half), tm=1024)
    proj2 = norm_mod_matmul(x, norm1_g, sh1, sc1, w_in, cols=(1, half), tm=1024, out_dtype=BF16)
    projc = norm_mod_matmul(xc, norm1_g, csh1, csc1, w_in, cols=(0, half))
    projc2 = norm_mod_matmul(xc, norm1_g, csh1, csc1, w_in, cols=(1, half), out_dtype=BF16)
    s0 = jnp.zeros((B, HGRN_HEADS, 2, HGRN_HEAD_DIM, HGRN_HEAD_DIM), F32)
    oc_f, oc_b, s_ctx = hgrn_scan(projc, lb, s0)
    o_f, o_b, _ = hgrn_scan(proj, lb, s_ctx)
    lv = lam_vec.astype(F32)
    lam = jnp.exp(jnp.sum(lv[0] * lv[1])) - jnp.exp(jnp.sum(lv[2] * lv[3])) + lam_init
    bound = 1.01 * math.sqrt(DIFF_HEAD_DIM) * jnp.max(jnp.abs(qn_g)) * jnp.max(jnp.abs(kn_g))
    lam = jnp.stack([lam, bound, (bound <= SCORE_BOUND_MAX).astype(F32)])
    cos, sin = tables
    q, k, v = qkv_prep(proj2, qn_g, kn_g, cos, sin, rotary=True)
    qc, kc, vc = qkv_prep(projc2, qn_g, kn_g, cos[:xc.shape[1]], sin[:xc.shape[1]], rotary=False)
    y_d = diff_attention(q, [(k, v), (kc, vc)], lam, subln_g, 1.0 - lam_init)
    x_new = even_out(x, o_f, o_b, proj2, y_d, hgrn_g.reshape(-1), g1, w_out, router)
    if not ctx_out:
        return x_new, None
    yc_d = diff_attention(qc, [(kc, vc)], lam, subln_g, 1.0 - lam_init)
    xc_new = even_out(xc, oc_f, oc_b, projc2, yc_d, hgrn_g.reshape(-1), cg1, w_out, crouter)
    return x_new, xc_new


def conv_layer(x, mods, norm1_g, w_in, conv_w, w_out):
    sh1, sc1, g1, router = mods
    proj = norm_mod_matmul(x, norm1_g, sh1, sc1, w_in, tm=1024, out_dtype=BF16)
    return conv_out(x, proj, conv_w, g1, w_out, router)


def kernel(x, c, ctx, c_ctx, mod_w, mod_b, norm1_g, norm2_g, even_w_in, even_w_out, hgrn_lb_logits, hgrn_norm_g,
           diff_qnorm_g, diff_knorm_g, diff_lambda, diff_subln_g, conv_w_in, conv_w, conv_w_out, router_w,
           exp_w_gate, exp_w_up, exp_w_down):
    depth = mod_w.shape[0]
    B, T, D = x.shape
    lb_soft = jax.nn.softmax(hgrn_lb_logits.astype(F32), axis=0)
    lower_bounds = jnp.cumsum(lb_soft, axis=0) - lb_soft[:1]
    last_ctx_layer = 2 * ((depth - 1) // 2)
    cond = jnp.concatenate([c, c_ctx[None, :], jnp.zeros((8 - (B + 1) % 8, D), F32)], axis=0)
    mods = modulation(jax.nn.silu(cond), mod_w, mod_b)
    tables = rope_tables(T)
    xc = ctx
    for l in range(depth):
        read_ctx = l <= last_ctx_layer
        ctx_out = l < last_ctx_layer
        sh1, sc1, g1, sh2, sc2, g2 = [m[:, None, :] for m in jnp.split(mods[l, :B], MOD_CHUNKS, axis=-1)]
        if read_ctx:
            csh1, csc1, cg1, csh2, csc2, cg2 = [
                jnp.broadcast_to(m[None, None, :], (B, 1, D)) for m in jnp.split(mods[l, B], MOD_CHUNKS, axis=-1)]
        mods_l = (sh1, sc1, g1, (norm2_g[l], sh2, sc2, router_w[l]))
        cmods_l = (csh1, csc1, cg1, (norm2_g[l], csh2, csc2, router_w[l])) if read_ctx else None
        if l % 2 == 0:
            e = l // 2
            s, sc = even_layer(x, xc, mods_l, cmods_l, norm1_g[l],
                               (even_w_in, e), (even_w_out, e), lower_bounds[e],
                               hgrn_norm_g[e], diff_qnorm_g[e], diff_knorm_g[e], diff_lambda[e],
                               diff_subln_g[e], lambda_init(l), ctx_out, tables)
        else:
            j = l // 2
            wi, wo = (conv_w_in, j), (conv_w_out, j)
            s = conv_layer(x, mods_l, norm1_g[l], wi, conv_w[j], wo)
            sc = conv_layer(xc, cmods_l, norm1_g[l], wi, conv_w[j], wo) if ctx_out else None
        outs = moe_residual([(s, g2)] + ([(sc, cg2)] if ctx_out else []), exp_w_gate, exp_w_up, exp_w_down, l)
        x = outs[0]
        if ctx_out:
            xc = outs[1]
    return x
```

```python
import functools
import math

import numpy as np
import jax
import jax.numpy as jnp
from jax import lax
from jax.experimental import pallas as pl
from jax.experimental.pallas import tpu as pltpu

F32 = jnp.float32
BF16 = jnp.bfloat16

EPS = 1e-6
GRID_W = 64
ROPE_THETA = 10000.0
HGRN_HEAD_DIM = 128
HGRN_HEADS = 4
HGRN_WIDTH = HGRN_HEADS * HGRN_HEAD_DIM
DIFF_HEAD_DIM = 64
DIFF_HEADS = 4
DIFF_WIDTH = DIFF_HEADS * 2 * DIFF_HEAD_DIM
N_EXPERTS = 16
EC_CAPACITY = 2
MOD_CHUNKS = 6
SCAN_CHUNK = 64
SCAN_LEVELS = (32, 16, 8, 4, 2, 1)
SCAN_HEADS_PER_STEP = 2
GATHER_EXPERTS_PER_STEP = 4
FFN_SAMPLES_PER_STEP = 4
LANES = 128
ROW_ALIGN = 16
VMEM_LIMIT = 56 * 1024 * 1024
SCORE_BOUND_MAX = 40.0


def _cparams(*sem):
    return pltpu.CompilerParams(dimension_semantics=sem, vmem_limit_bytes=VMEM_LIMIT)


def _row_tile(t, want):
    return want if t % want == 0 else t


def _mod_kernel(s_ref, w_ref, b_ref, o_ref):
    s = s_ref[...]
    w = w_ref[0]
    s_hi = s.astype(BF16)
    s_lo = (s - s_hi.astype(F32)).astype(BF16)
    w_hi = w.astype(BF16)
    w_lo = (w - w_hi.astype(F32)).astype(BF16)
    d = lambda a, b: jnp.dot(a, b, preferred_element_type=F32)
    o_ref[0] = d(s_hi, w_hi) + d(s_hi, w_lo) + d(s_lo, w_hi) + b_ref[0]


def modulation(s, mod_w, mod_b, *, tn=2048):
    R, D = s.shape
    depth, _, N = mod_w.shape
    return pl.pallas_call(
        _mod_kernel,
        grid=(depth, N // tn),
        in_specs=[pl.BlockSpec((R, D), lambda l, j: (0, 0)),
                  pl.BlockSpec((1, D, tn), lambda l, j: (l, 0, j)),
                  pl.BlockSpec((1, 1, tn), lambda l, j: (l, 0, j))],
        out_specs=pl.BlockSpec((1, R, tn), lambda l, j: (l, 0, j)),
        out_shape=jax.ShapeDtypeStruct((depth, R, N), F32),
        compiler_params=_cparams("parallel", "parallel"),
        name="modulation",
    )(s, mod_w, mod_b.reshape(depth, 1, N))


def _nmm_kernel(x_ref, g_ref, sh_ref, sc_ref, w_ref, o_ref, w_scr):
    @pl.when((pl.program_id(1) == 0) & (pl.program_id(2) == 0))
    def _():
        w_scr[...] = w_ref[0].astype(BF16)

    x = x_ref[0]
    ms = jnp.mean(x * x, axis=-1, keepdims=True)
    h = (x * lax.rsqrt(ms + EPS)) * g_ref[...]
    h = h * (1.0 + sc_ref[0]) + sh_ref[0]
    o_ref[0] = jnp.dot(h.astype(BF16), w_scr[...], preferred_element_type=F32).astype(o_ref.dtype)


def norm_mod_matmul(x, g, shift, scale, w, *, cols=None, tm=512, out_dtype=F32):
    B, T, D = x.shape
    w, li = w
    first, N = cols if cols else (0, w.shape[2])
    tm = _row_tile(T, tm)
    tn = N
    return pl.pallas_call(
        _nmm_kernel,
        grid=(N // tn, B, T // tm),
        in_specs=[
            pl.BlockSpec((1, tm, D), lambda j, b, i: (b, i, 0)),
            pl.BlockSpec((1, D), lambda j, b, i: (0, 0)),
            pl.BlockSpec((1, 1, D), lambda j, b, i: (b, 0, 0)),
            pl.BlockSpec((1, 1, D), lambda j, b, i: (b, 0, 0)),
            pl.BlockSpec((1, D, tn), lambda j, b, i: (li, 0, first + j)),
        ],
        out_specs=pl.BlockSpec((1, tm, tn), lambda j, b, i: (b, i, j)),
        out_shape=jax.ShapeDtypeStruct((B, T, N), out_dtype),
        scratch_shapes=[pltpu.VMEM((D, tn), BF16)],
        compiler_params=_cparams("arbitrary", "arbitrary", "arbitrary"),
        name="norm_mod_matmul",
    )(x, g.reshape(1, D), shift, scale, w)


def _scan_constants():
    C = SCAN_CHUNK
    t = np.arange(C)[:, None]
    u = np.arange(C)[None, :]
    mats = [u <= t, u > t]
    masks = []
    for w in SCAN_LEVELS:
        m = (t // (2 * w)) * 2 * w + w - 1
        later = (t // w) % 2 == 1
        mats.append(np.where(later, (u > m) & (u <= t), (u > t) & (u <= m)))
        masks.append(later & ((u // w) % 2 == 0) & (u // (2 * w) == t // (2 * w)))
    masks = [t == u] + masks + [np.ones((C, C), bool)]
    a_f = np.stack(mats).astype(np.float32)
    m_f = np.stack(masks).astype(np.float32)
    a = np.stack([a_f, a_f[:, ::-1, ::-1]]).reshape(2, -1, C)
    m = np.stack([m_f, m_f[:, ::-1, ::-1]])
    m_pairs = np.concatenate([m[:, 0::2], m[:, 1::2]], axis=3)
    return np.concatenate([a, a], axis=2), m_pairs


def _scan_pair(q, z, v, lb, st, a2, mask_ref, d, later):
    C = SCAN_CHUNK
    W = HGRN_HEAD_DIM
    nl = len(SCAN_LEVELS)
    nt = lambda x, y: lax.dot_general(x, y, (((1,), (1,)), ((), ())), preferred_element_type=F32)
    nn = lambda x, y: jnp.dot(x, y, preferred_element_type=F32)
    e_abs = jnp.exp(-jnp.abs(z))
    r = 1.0 / (1.0 + e_abs)
    er = e_abs * r
    pos = z >= 0.0
    g2 = jnp.log2(lb + (1.0 - lb) * jnp.where(pos, r, er))
    k = (1.0 - lb) * jnp.where(pos, er, r)
    hi = g2.astype(BF16)
    lo = (g2 - hi.astype(F32)).astype(BF16)
    gs = jnp.concatenate([jnp.concatenate([hi[:C], hi[C:]], axis=1),
                          jnp.concatenate([lo[:C], lo[C:]], axis=1)], axis=0)
    x = nn(a2, gs)
    yield
    x = jnp.exp2(x)
    first, second = (1, 0) if d else (0, 1)
    vb = v.astype(BF16)
    sides, qt, kt, dec = [], [], [], []
    for c in (0, 1):
        qc, kc, xs = q[c * C:(c + 1) * C], k[c * C:(c + 1) * C], x[:, c * W:(c + 1) * W]
        ops = [(qc.astype(BF16), kc.astype(BF16))]
        for i in range(nl):
            qk = (jnp.where(later[i], qc, kc) * xs[(2 + i) * C:(3 + i) * C]).astype(BF16)
            ops.append((qk, qk))
        sides.append(ops)
        qt.append(qc * xs[0:C])
        kt.append(kc * xs[C:2 * C])
        dec.append(xs[0:1, :] if d else xs[C - 1:C, :])
    zero = jnp.zeros((C, W), BF16)
    sides[first].append((zero, zero))
    sides[second].append((qt[second].astype(BF16), kt[first].astype(BF16)))
    prods = []
    for ops in sides:
        pair = []
        for j in range(0, nl + 2, 2):
            (qa, ka), (qb, kb) = ops[j], ops[j + 1]
            rhs = jnp.concatenate([jnp.concatenate([ka, zero], axis=1), jnp.concatenate([zero, kb], axis=1)], axis=0)
            pair.append(nt(jnp.concatenate([qa, qb], axis=1), rhs))
        prods.append(pair)
        yield
    qt[second] = qt[second] * dec[first]
    kt[first] = kt[first] * dec[second]
    o_st = nt(jnp.concatenate([a.astype(BF16) for a in qt], axis=0), st.astype(BF16))
    upd = lax.dot_general(vb, jnp.concatenate(kt, axis=0).astype(BF16), (((0,), (0,)), ((), ())),
                          preferred_element_type=F32)
    yield
    o = [None, None]
    n_main = nl // 2
    for c, pair in enumerate(prods):
        main = mask_ref[d, 0] * pair[0]
        for j in range(1, n_main):
            main = main + mask_ref[d, j] * pair[j]
        last = mask_ref[d, n_main] * pair[n_main]
        vc, vf = vb[c * C:(c + 1) * C], vb[first * C:(first + 1) * C]
        o[c] = o_st[c * C:(c + 1) * C] + nn(jnp.concatenate([main, last], axis=1).astype(BF16),
                                            jnp.concatenate([vc, vc, vc, vf], axis=0))
    yield
    return jnp.concatenate(o, axis=0), st * (dec[0] * dec[1]) + upd


def _run_interleaved(gens):
    results = [None] * len(gens)
    live = list(range(len(gens)))
    while live:
        for i in list(live):
            try:
                next(gens[i])
            except StopIteration as done:
                results[i] = done.value
                live.remove(i)
    return results


def _scan_kernel(qf_ref, zf_ref, vf_ref, qb_ref, zb_ref, vb_ref, lbf_ref, lbb_ref, s0_ref, a_ref, mask_ref,
                 of_ref, ob_ref, sT_ref, st_scr, *, n_chunks):
    c = pl.program_id(2)
    C = SCAN_CHUNK

    @pl.when(c == 0)
    def _():
        st_scr[...] = s0_ref[0]

    W = HGRN_HEAD_DIM
    row = lax.broadcasted_iota(jnp.int32, (C, W), 0)
    later = [[(row // w) % 2 == 1 for w in SCAN_LEVELS], [((C - 1 - row) // w) % 2 == 1 for w in SCAN_LEVELS]]
    in_refs = [(qf_ref, zf_ref, vf_ref), (qb_ref, zb_ref, vb_ref)]
    lb_refs = [lbf_ref, lbb_ref]
    o_refs = [of_ref, ob_ref]
    n_pairs = n_chunks // 2
    heads = range(qf_ref.shape[2] // W)

    def body(i, carry):
        r0 = [pl.multiple_of(i * 2 * C, 2 * C), pl.multiple_of((n_pairs - 1 - i) * 2 * C, 2 * C)]
        chains = [(d, h) for h in heads for d in (0, 1)]
        ins = [[ref[0, pl.ds(r0[d], 2 * C), h * W:(h + 1) * W] for ref in in_refs[d]] for d, h in chains]
        sts = [st_scr[h, d] for d, h in chains]
        outs = _run_interleaved([
            _scan_pair(*x, lb_refs[d][:, h * W:(h + 1) * W], st, a_ref[d], mask_ref, d, later[d])
            for (d, h), x, st in zip(chains, ins, sts)])
        for (d, h), (o, st) in zip(chains, outs):
            o_refs[d][0, pl.ds(r0[d], 2 * C), h * W:(h + 1) * W] = o.astype(o_refs[d].dtype)
            st_scr[h, d] = st
        return carry

    lax.fori_loop(0, n_pairs, body, 0)

    @pl.when(c == pl.num_programs(2) - 1)
    def _():
        sT_ref[0] = st_scr[...]


def hgrn_scan(proj, lb, s0, *, tb=2048):
    B, T, _ = proj.shape
    tb = _row_tile(T, tb)
    nc = T // tb
    H = HGRN_HEADS
    hd = HGRN_HEAD_DIM
    hps = SCAN_HEADS_PER_STEP
    G = H // hps
    wd = hps * hd
    fwd = lambda grp: pl.BlockSpec((1, tb, wd), lambda b, h, c: (b, c, grp * G + h))
    bwd = lambda grp: pl.BlockSpec((1, tb, wd), lambda b, h, c: (b, nc - 1 - c, grp * G + h))
    kern = functools.partial(_scan_kernel, n_chunks=tb // SCAN_CHUNK)
    a2, masks = _scan_constants()
    return pl.pallas_call(
        kern,
        grid=(B, G, nc),
        in_specs=[fwd(0), fwd(1), fwd(3), bwd(0), bwd(2), bwd(3),
                  pl.BlockSpec((1, wd), lambda b, h, c: (0, h)),
                  pl.BlockSpec((1, wd), lambda b, h, c: (0, h)),
                  pl.BlockSpec((1, hps, 2, hd, hd), lambda b, h, c: (b, h, 0, 0, 0)),
                  pl.BlockSpec(a2.shape, lambda b, h, c: (0, 0, 0)),
                  pl.BlockSpec(masks.shape, lambda b, h, c: (0, 0, 0, 0))],
        out_specs=[pl.BlockSpec((1, tb, wd), lambda b, h, c: (b, c, h)),
                   pl.BlockSpec((1, tb, wd), lambda b, h, c: (b, nc - 1 - c, h)),
                   pl.BlockSpec((1, hps, 2, hd, hd), lambda b, h, c: (b, h, 0, 0, 0))],
        out_shape=[jax.ShapeDtypeStruct((B, T, HGRN_WIDTH), BF16),
                   jax.ShapeDtypeStruct((B, T, HGRN_WIDTH), BF16),
                   jax.ShapeDtypeStruct((B, H, 2, hd, hd), F32)],
        scratch_shapes=[pltpu.VMEM((hps, 2, hd, hd), F32)],
        compiler_params=_cparams("parallel", "parallel", "arbitrary"),
        name="hgrn_scan",
    )(proj, proj, proj, proj, proj, proj, lb[0:1], lb[1:2], s0, jnp.asarray(a2, BF16), jnp.asarray(masks, F32))


def _group_mean_sq(x, gmat):
    sq = x * x
    hi = sq.astype(BF16)
    lo = (sq - hi.astype(F32)).astype(BF16)
    return (jnp.dot(hi, gmat, preferred_element_type=F32) + jnp.dot(lo, gmat, preferred_element_type=F32))


def _qkv_prep_kernel(q_ref, k_ref, v_ref, qg_ref, kg_ref, cos_ref, sin_ref, qo_ref, ko_ref, vo_ref, *, rotary):
    W = LANES
    r_i = lax.broadcasted_iota(jnp.int32, (W, W), 0) // DIFF_HEAD_DIM
    c_i = lax.broadcasted_iota(jnp.int32, (W, W), 1) // DIFF_HEAD_DIM
    gmat = jnp.where(r_i == c_i, 1.0 / DIFF_HEAD_DIM, 0.0).astype(BF16)
    lane = lax.broadcasted_iota(jnp.int32, (1, W), 1)
    first = (lane % 32) < 16

    def prep(x, g, scale):
        y = (x * lax.rsqrt(_group_mean_sq(x, gmat) + EPS)) * g
        if rotary:
            partner = jnp.where(first, pltpu.roll(y, W - 16, axis=1), pltpu.roll(y, 16, axis=1))
            y = y * cos_ref[...] + partner * sin_ref[...]
        if scale != 1.0:
            y = y * scale
        return y

    for h in range(DIFF_HEADS):
        sl = slice(h * W, (h + 1) * W)
        qo_ref[0, sl, :] = prep(q_ref[0, :, sl].astype(F32), qg_ref[...], DIFF_HEAD_DIM ** -0.5).T.astype(BF16)
        ko_ref[0, :, sl] = prep(k_ref[0, :, sl].astype(F32), kg_ref[...], 1.0).astype(BF16)
        vo_ref[0, sl, :] = v_ref[0, :, sl].astype(F32).T.astype(BF16)


def qkv_prep(proj, qg, kg, cos, sin, *, rotary, tm=1024):
    B, T, _ = proj.shape
    tm = _row_tile(T, tm)
    Wd = DIFF_WIDTH
    col = lambda j: pl.BlockSpec((1, tm, Wd), lambda b, i: (b, i, j))
    vec = pl.BlockSpec((1, LANES), lambda b, i: (0, 0))
    tab = pl.BlockSpec((tm, LANES), lambda b, i: (i, 0))
    rows = pl.BlockSpec((1, tm, Wd), lambda b, i: (b, i, 0))
    cols = pl.BlockSpec((1, Wd, tm), lambda b, i: (b, 0, i))
    return pl.pallas_call(
        functools.partial(_qkv_prep_kernel, rotary=rotary),
        grid=(B, T // tm),
        in_specs=[col(1), col(2), col(3), vec, vec, tab, tab],
        out_specs=[cols, rows, cols],
        out_shape=[jax.ShapeDtypeStruct((B, Wd, T), BF16), jax.ShapeDtypeStruct((B, T, Wd), BF16),
                   jax.ShapeDtypeStruct((B, Wd, T), BF16)],
        compiler_params=_cparams("parallel", "parallel"),
        name="qkv_prep",
    )(proj, proj, proj, jnp.tile(qg, 2).reshape(1, LANES), jnp.tile(kg, 2).reshape(1, LANES), cos, sin)


def rope_tables(T):
    n = DIFF_HEAD_DIM // 2
    inv = 1.0 / (ROPE_THETA ** (jnp.arange(0, n, 2, dtype=F32) / n))
    t = jnp.arange(T)
    ang_r = (t // GRID_W).astype(F32)[:, None] * inv[None, :]
    ang_c = (t % GRID_W).astype(F32)[:, None] * inv[None, :]
    cos = jnp.concatenate([jnp.cos(ang_r)] * 2 + [jnp.cos(ang_c)] * 2, axis=-1)
    sin = jnp.concatenate([-jnp.sin(ang_r), jnp.sin(ang_r), -jnp.sin(ang_c), jnp.sin(ang_c)], axis=-1)
    return jnp.tile(cos, (1, 2)), jnp.tile(sin, (1, 2))


def _attn_tile(qt, kv_refs, lam, key_chunk, bound=None):
    tq = qt.shape[1]
    row = lax.broadcasted_iota(jnp.int32, (LANES, 1), 0)
    zero = jnp.zeros_like(qt)
    qq = jnp.concatenate([jnp.where(row < DIFF_HEAD_DIM, qt, zero),
                          jnp.where(row >= DIFF_HEAD_DIM, qt, zero)], axis=1)
    m = jnp.full((1, 2 * tq), -jnp.inf, F32)
    acc = [jnp.zeros((LANES + 16, tq), F32), jnp.zeros((LANES + 16, tq), F32)]
    ones = jnp.ones((16, key_chunk), BF16)
    chunks = [(k_ref, v_ref, c0, min(c0 + key_chunk, k_ref.shape[1]))
              for k_ref, v_ref in kv_refs for c0 in range(0, k_ref.shape[1], key_chunk)]
    scores = lambda c: jnp.dot(c[0][0, c[2]:c[3], :], qq, preferred_element_type=F32)
    s_next = scores(chunks[0])
    yield
    for n, (_, v_ref, c0, c1) in enumerate(chunks):
        s = s_next
        if n + 1 < len(chunks):
            s_next = scores(chunks[n + 1])
        vt1 = jnp.concatenate([v_ref[0, :, c0:c1], ones[:, :c1 - c0]], axis=0)
        if bound is None:
            m_new = jnp.maximum(m, jnp.max(s, axis=0, keepdims=True))
            alpha = jnp.exp(m - m_new)
            pb = jnp.exp((s - m_new).astype(BF16))
            m = m_new
        else:
            alpha = None
            pb = jnp.exp(s - bound).astype(BF16)
        for i in range(2):
            pv = jnp.dot(vt1, pb[:, i * tq:(i + 1) * tq], preferred_element_type=F32)
            acc[i] = acc[i] + pv if alpha is None else acc[i] * alpha[:, i * tq:(i + 1) * tq] + pv
        yield
    inv = [1.0 / a[LANES:LANES + 1] for a in acc]
    return acc[0][:LANES] * inv[0] - acc[1][:LANES] * (lam * inv[1])


def _diff_attn_kernel(par_ref, q_ref, *rest, out_scale, key_chunk, tq):
    kv_refs, g_ref, o_ref = list(zip(rest[:-2:2], rest[1:-2:2])), rest[-2], rest[-1]
    n_tiles = q_ref.shape[2] // tq

    def run(bound):
        outs = _run_interleaved([
            _attn_tile(q_ref[0, :, i * tq:(i + 1) * tq], kv_refs, par_ref[0], key_chunk, bound)
            for i in range(n_tiles)])
        for i, o in enumerate(outs):
            ms = jnp.mean(o * o, axis=0, keepdims=True)
            o_ref[0, i * tq:(i + 1) * tq, :] = ((o * lax.rsqrt(ms + EPS)) * g_ref[...] * out_scale).T.astype(BF16)

    @pl.when(par_ref[2] > 0.5)
    def _():
        run(par_ref[1])

    @pl.when(par_ref[2] <= 0.5)
    def _():
        run(None)


def diff_attention(qt, kvs, params, subln_g, out_scale, *, tq=256, tiles_per_step=4, key_chunk=512):
    B, Wd, T = qt.shape
    tq = _row_tile(T, tq)
    ts = _row_tile(T, tq * tiles_per_step)
    kv_specs = [spec for k, _ in kvs for spec in (
        pl.BlockSpec((1, k.shape[1], LANES), lambda b, h, i: (b, 0, h)),
        pl.BlockSpec((1, LANES, k.shape[1]), lambda b, h, i: (b, h, 0)))]
    return pl.pallas_call(
        functools.partial(_diff_attn_kernel, out_scale=out_scale, key_chunk=key_chunk, tq=tq),
        grid=(B, DIFF_HEADS, T // ts),
        in_specs=[pl.BlockSpec(memory_space=pltpu.SMEM),
                  pl.BlockSpec((1, LANES, ts), lambda b, h, i: (b, h, i))] + kv_specs + [
                  pl.BlockSpec((LANES, 1), lambda b, h, i: (0, 0))],
        out_specs=pl.BlockSpec((1, ts, LANES), lambda b, h, i: (b, i, h)),
        out_shape=jax.ShapeDtypeStruct((B, T, Wd), BF16),
        compiler_params=_cparams("parallel", "parallel", "arbitrary"),
        name="diff_attention",
    )(params, qt, *[a for kv in kvs for a in kv], subln_g.reshape(LANES, 1))


def _even_out_kernel(x_ref, of_ref, ob_ref, gate_ref, yd_ref, hg_ref, g1_ref, w_ref, *rest):
    router_refs, o_ref, moe_refs = rest[:-3], rest[-3], rest[-2:]
    w = w_ref[0].astype(BF16)
    acc = jnp.dot(yd_ref[0], w[HGRN_WIDTH:], preferred_element_type=F32)
    for h in range(HGRN_HEADS):
        sl = slice(h * HGRN_HEAD_DIM, (h + 1) * HGRN_HEAD_DIM)
        o = of_ref[0, :, sl].astype(F32) + ob_ref[0, :, sl].astype(F32)
        ms = jnp.mean(o * o, axis=-1, keepdims=True)
        gate = gate_ref[0, :, sl].astype(F32)
        yh = (o * lax.rsqrt(ms + EPS)) * hg_ref[:, sl] * (gate * jax.nn.sigmoid(gate))
        acc = acc + jnp.dot(yh.astype(BF16), w[sl], preferred_element_type=F32)
    x_new = x_ref[0] + g1_ref[0] * acc
    o_ref[0] = x_new
    _router_outputs(x_new, *router_refs, *moe_refs)


def even_out(x, o_f, o_b, proj, y_d, hgrn_g, g1, w_out, router, *, tm=1024):
    B, T, D = x.shape
    w_out, li = w_out
    tm = _row_tile(T, tm)
    Wd = HGRN_WIDTH
    row = lambda w, j: pl.BlockSpec((1, tm, w), lambda b, i: (b, i, j))
    r_args, r_in, r_out, r_shapes = _router_specs(B, T, D, tm, router)
    return pl.pallas_call(
        _even_out_kernel,
        grid=(B, T // tm),
        in_specs=[row(D, 0), row(Wd, 0), row(Wd, 0), row(Wd, 0), row(Wd, 0),
                  pl.BlockSpec((1, Wd), lambda b, i: (0, 0)),
                  pl.BlockSpec((1, 1, D), lambda b, i: (b, 0, 0)),
                  pl.BlockSpec((1,) + w_out.shape[1:], lambda b, i: (li, 0, 0))] + r_in,
        out_specs=[row(D, 0)] + r_out,
        out_shape=[jax.ShapeDtypeStruct((B, T, D), F32)] + r_shapes,
        compiler_params=_cparams("parallel", "parallel"),
        name="even_out",
    )(x, o_f, o_b, proj, y_d, hgrn_g.reshape(1, Wd), g1, w_out, *r_args)


def _conv_out_kernel(x_ref, bg_ref, cg_ref, v_ref, cp_ref, vp_ref, cn_ref, vn_ref, cw_ref, g1_ref, w_ref, *rest):
    router_refs, o_ref, moe_refs = rest[:-3], rest[-3], rest[-2:]
    i = pl.program_id(1)
    n = pl.num_programs(1)
    f32 = lambda ref, *idx: ref[idx].astype(F32)
    u = f32(cg_ref, 0) * f32(v_ref, 0)
    tm = u.shape[0]
    last = ROW_ALIGN - 1
    u_prev_row = jnp.where(i > 0, f32(cp_ref, 0, slice(last, last + 1)) * f32(vp_ref, 0, slice(last, last + 1)), 0.0)
    u_next_row = jnp.where(i < n - 1, f32(cn_ref, 0, slice(0, 1)) * f32(vn_ref, 0, slice(0, 1)), 0.0)
    ridx = lax.broadcasted_iota(jnp.int32, (tm, 1), 0)
    u_prev = jnp.where(ridx == 0, u_prev_row, pltpu.roll(u, 1, axis=0))
    u_next = jnp.where(ridx == tm - 1, u_next_row, pltpu.roll(u, tm - 1, axis=0))
    y = cw_ref[0:1, :] * u_prev + cw_ref[1:2, :] * u + cw_ref[2:3, :] * u_next
    acc = jnp.dot((f32(bg_ref, 0) * y).astype(BF16), w_ref[0].astype(BF16), preferred_element_type=F32)
    x_new = x_ref[0] + g1_ref[0] * acc
    o_ref[0] = x_new
    _router_outputs(x_new, *router_refs, *moe_refs)


def conv_out(x, proj, conv_w, g1, w_out, router, *, tm=1024):
    B, T, D = x.shape
    w_out, li = w_out
    tm = _row_tile(T, tm)
    rt = tm // ROW_ALIGN
    last_blk = T // ROW_ALIGN - 1
    row = lambda j: pl.BlockSpec((1, tm, D), lambda b, i: (b, i, j))
    prev = lambda j: pl.BlockSpec((1, ROW_ALIGN, D), lambda b, i: (b, jnp.maximum(i * rt - 1, 0), j))
    nxt = lambda j: pl.BlockSpec((1, ROW_ALIGN, D), lambda b, i: (b, jnp.minimum((i + 1) * rt, last_blk), j))
    r_args, r_in, r_out, r_shapes = _router_specs(B, T, D, tm, router)
    return pl.pallas_call(
        _conv_out_kernel,
        grid=(B, T // tm),
        in_specs=[row(0), row(0), row(1), row(2), prev(1), prev(2), nxt(1), nxt(2),
                  pl.BlockSpec((8, D), lambda b, i: (0, 0)),
                  pl.BlockSpec((1, 1, D), lambda b, i: (b, 0, 0)),
                  pl.BlockSpec((1,) + w_out.shape[1:], lambda b, i: (li, 0, 0))] + r_in,
        out_specs=[row(0)] + r_out,
        out_shape=[jax.ShapeDtypeStruct((B, T, D), F32)] + r_shapes,
        compiler_params=_cparams("parallel", "parallel"),
        name="conv_out",
    )(x, proj, proj, proj, proj, proj, proj, proj,
      jnp.concatenate([conv_w, jnp.zeros((8 - conv_w.shape[0], D), conv_w.dtype)], axis=0), g1, w_out, *r_args)


def _router_outputs(x, g_ref, sh_ref, sc_ref, rwt_ref, h_ref, lg_ref):
    ms = jnp.mean(x * x, axis=-1, keepdims=True)
    h = (x * lax.rsqrt(ms + EPS)) * g_ref[...]
    h = h * (1.0 + sc_ref[0]) + sh_ref[0]
    h_hi = h.astype(BF16)
    h_ref[0] = h_hi
    h_lo = (h - h_hi.astype(F32)).astype(BF16)
    rwt = rwt_ref[...]
    w_hi = rwt.astype(BF16)
    w_lo = (rwt - w_hi.astype(F32)).astype(BF16)
    d = lambda a, b: lax.dot_general(a, b, (((1,), (1,)), ((), ())), preferred_element_type=F32)
    lg_ref[0] = d(w_hi, h_hi) + d(w_lo, h_hi) + d(w_hi, h_lo)


def _router_specs(B, T, D, tm, router):
    g, shift, scale, router_w = router
    E = router_w.shape[1]
    vec = pl.BlockSpec((1, 1, D), lambda b, i: (b, 0, 0))
    return ((g.reshape(1, D), shift, scale, router_w.T),
            [pl.BlockSpec((1, D), lambda b, i: (0, 0)), vec, vec, pl.BlockSpec((E, D), lambda b, i: (0, 0))],
            [pl.BlockSpec((1, tm, D), lambda b, i: (b, i, 0)), pl.BlockSpec((1, E, tm), lambda b, i: (b, 0, i))],
            [jax.ShapeDtypeStruct((B, T, D), BF16), jax.ShapeDtypeStruct((B, E, T), F32)])


def _lane_prefix(flags, tri_tot):
    E, N = flags.shape
    carries = [jnp.zeros((E, LANES), F32)]
    out = []
    for j in range(N // LANES):
        r = jnp.dot(flags[:, j * LANES:(j + 1) * LANES].astype(BF16), tri_tot, preferred_element_type=F32)
        out.append(r[:, :LANES] + carries[-1])
        carries.append(carries[-1] + r[:, LANES:])
    return jnp.concatenate(out, axis=1), carries


def _route_kernel(lg_ref, tab_ref, rank_ref, ts_ref, *, cap, tt):
    lg = lg_ref[...]
    B, E, N = lg.shape
    p = jnp.exp(lg - jnp.max(lg, axis=1, keepdims=True))
    R = B * E
    aff = (p / jnp.sum(p, axis=1, keepdims=True)).reshape(R, N)
    count = lambda m: jnp.sum(jnp.where(m, 1.0, 0.0), axis=1, keepdims=True)
    as_float = lambda i: pltpu.bitcast(i, F32)

    def refine_bits(i, thr):
        cand = thr | jnp.left_shift(jnp.int32(1), 30 - i)
        return jnp.where(count(aff >= as_float(cand)) >= cap, cand, thr)

    thr = lax.fori_loop(0, 31, refine_bits, jnp.zeros((R, 1), jnp.int32))

    def refine_mid(i, lo_hi):
        lo, hi = lo_hi
        mid = 0.5 * (lo + hi)
        up = count(aff >= mid) >= cap
        return jnp.where(up, mid, lo), jnp.where(up, hi, mid)

    lo, hi = lax.fori_loop(0, 24, refine_mid, (as_float(thr), as_float(jnp.maximum(thr + 1, 0x00800000))))
    gt = aff >= hi
    eq = (aff >= lo) & (aff < hi)
    r_i = lax.broadcasted_iota(jnp.int32, (LANES, 2 * LANES), 0)
    c_i = lax.broadcasted_iota(jnp.int32, (LANES, 2 * LANES), 1)
    tri_tot = jnp.where((r_i < c_i) | (c_i >= LANES), 1.0, 0.0).astype(BF16)
    eq_rank, _ = _lane_prefix(jnp.where(eq, 1.0, 0.0), tri_tot)
    sel = gt | (eq & (eq_rank < cap - count(gt)))
    rank, before = _lane_prefix(jnp.where(sel, 1.0, 0.0), tri_tot)
    rank = jnp.where(sel, rank, -1.0)
    rank_ref[...] = rank.astype(jnp.int32).reshape(B, E, N)
    pad = jnp.zeros((LANES - 2 * E, N), F32)
    for b in range(B):
        rows = slice(b * E, (b + 1) * E)
        tab_ref[b] = jnp.concatenate([aff[rows], rank[rows], pad], axis=0).T
    lane = lax.broadcasted_iota(jnp.int32, (R, LANES), 1)
    ts = jnp.zeros((R, LANES), F32)
    for k in range(N // tt + 1):
        ts = jnp.where(lane == k, before[k * tt // LANES], ts)
    ts_ref[...] = ts.astype(jnp.int32).reshape(B, E, LANES)


def route(logits_t, cap, tt):
    B, E, N = logits_t.shape
    whole = lambda *shape: pl.BlockSpec(shape, lambda i: (0,) * len(shape))
    return pl.pallas_call(
        functools.partial(_route_kernel, cap=cap, tt=tt),
        grid=(1,),
        in_specs=[whole(B, E, N)],
        out_specs=[whole(B, N, LANES), whole(B, E, N), whole(B, E, LANES)],
        out_shape=[jax.ShapeDtypeStruct((B, N, LANES), F32), jax.ShapeDtypeStruct((B, E, N), jnp.int32),
                   jax.ShapeDtypeStruct((B, E, LANES), jnp.int32)],
        compiler_params=_cparams("arbitrary"),
        name="route",
    )(logits_t)


def _window(lo, w, win, cap):
    lower = (lo // ROW_ALIGN) * ROW_ALIGN + w * win
    return pl.multiple_of(jnp.minimum(lower, cap - win), ROW_ALIGN), lower


def _extra_windows(cap, tt, win):
    return -(-(min(cap, tt) + ROW_ALIGN - 1) // win) - 1


def _moe_gather_kernel(ts_ref, rank_ref, h_ref, xe_ref, acc_ref, *, win, tt):
    b, g = pl.program_id(0), pl.program_id(1)
    N = h_ref.shape[1]
    ne, cap = xe_ref.shape[1], xe_ref.shape[2]
    nt = N // tt
    acc_ref[...] = jnp.zeros_like(acc_ref)
    row = lax.broadcasted_iota(jnp.int32, (win, tt), 0)
    n_extra = _extra_windows(cap, tt, win)
    for ee in range(ne):
        base = ((b * pl.num_programs(1) + g) * ne + ee) * (nt + 1)

        def place(k, w, ee=ee, base=base):
            start, lower = _window(ts_ref[base + k], w, win, cap)
            c0 = k * tt if isinstance(k, int) else pl.multiple_of(k * tt, tt)
            rk = rank_ref[0, ee, :, pl.ds(c0, tt)]
            onehot = jnp.where(jnp.where(rk >= lower, rk, -1) == row + start, 1.0, 0.0).astype(BF16)
            acc_ref[ee, pl.ds(start, win), :] += jnp.dot(onehot, h_ref[0, pl.ds(c0, tt), :],
                                                         preferred_element_type=F32)

        span = lambda k, base=base: ts_ref[base + k + 1] - (ts_ref[base + k] // ROW_ALIGN) * ROW_ALIGN
        need = jnp.int32(0)
        for k in range(nt):
            place(k, 0)
            need = jnp.maximum(need, span(k))

        @pl.when(need > win)
        def _(place=place, span=span):
            def extra(i, carry):
                k, w = i // n_extra, i % n_extra + 1

                @pl.when(span(k) > w * win)
                def _():
                    place(k, w)
                return carry

            lax.fori_loop(0, nt * n_extra, extra, 0)

    xe_ref[0] = acc_ref[...].astype(BF16)


def moe_gather(ts_flat, rank, h, cap, *, tt):
    B, E, N = rank.shape
    D = h.shape[2]
    ne = GATHER_EXPERTS_PER_STEP
    return pl.pallas_call(
        functools.partial(_moe_gather_kernel, win=min(LANES, cap), tt=tt),
        grid_spec=pltpu.PrefetchScalarGridSpec(
            num_scalar_prefetch=1,
            grid=(B, E // ne),
            in_specs=[pl.BlockSpec((1, ne, 1, N), lambda b, e, ts: (b, e, 0, 0)),
                      pl.BlockSpec((1, N, D), lambda b, e, ts: (b, 0, 0))],
            out_specs=pl.BlockSpec((1, ne, cap, D), lambda b, e, ts: (b, e, 0, 0)),
            scratch_shapes=[pltpu.VMEM((ne, cap, D), F32)]),
        out_shape=jax.ShapeDtypeStruct((B, E, cap, D), BF16),
        compiler_params=_cparams("parallel", "arbitrary"),
        name="moe_gather",
    )(ts_flat, rank.reshape(B, E, 1, N), h)


def _expert_kernel(*refs, n_streams):
    xe_refs = refs[:n_streams]
    wg_ref, wu_ref, wd_ref = refs[n_streams:n_streams + 3]
    o_refs = refs[n_streams + 3:2 * n_streams + 3]
    wg_scr, wu_scr, wd_scr = refs[2 * n_streams + 3:]

    @pl.when(pl.program_id(1) == 0)
    def _():
        wg_scr[...] = wg_ref[0, 0].astype(BF16)
        wu_scr[...] = wu_ref[0, 0].astype(BF16)
        wd_scr[...] = wd_ref[0, 0].astype(BF16)

    def swiglu(x):
        a = jnp.dot(x, wg_scr[...], preferred_element_type=F32)
        u = jnp.dot(x, wu_scr[...], preferred_element_type=F32)
        yield
        hid = (a * jax.nn.sigmoid(a)) * u
        return jnp.dot(hid.astype(BF16), wd_scr[...], preferred_element_type=F32).astype(BF16)

    nb = xe_refs[0].shape[0]
    ys = _run_interleaved([swiglu(jnp.concatenate([r[s, 0] for r in xe_refs], axis=0)) for s in range(nb)])
    for s, y in enumerate(ys):
        r0 = 0
        for o_ref in o_refs:
            o_ref[s, 0] = y[r0:r0 + o_ref.shape[2]]
            r0 += o_ref.shape[2]


def expert_ffn(xes, w_gate, w_up, w_down, li):
    B, E, _, D = xes[0].shape
    FF = w_gate.shape[3]
    nb = FFN_SAMPLES_PER_STEP if B % FFN_SAMPLES_PER_STEP == 0 else 1
    rows = [pl.BlockSpec((nb, 1, xe.shape[2], D), lambda e, b: (b, e, 0, 0)) for xe in xes]
    return pl.pallas_call(
        functools.partial(_expert_kernel, n_streams=len(xes)),
        grid=(E, B // nb),
        in_specs=rows + [pl.BlockSpec((1, 1, D, FF), lambda e, b: (li, e, 0, 0)),
                         pl.BlockSpec((1, 1, D, FF), lambda e, b: (li, e, 0, 0)),
                         pl.BlockSpec((1, 1, FF, D), lambda e, b: (li, e, 0, 0))],
        out_specs=rows,
        out_shape=[jax.ShapeDtypeStruct(xe.shape, BF16) for xe in xes],
        scratch_shapes=[pltpu.VMEM((D, FF), BF16), pltpu.VMEM((D, FF), BF16), pltpu.VMEM((FF, D), BF16)],
        compiler_params=_cparams("arbitrary", "arbitrary"),
        name="expert_ffn",
    )(*xes, w_gate, w_up, w_down)


def _moe_combine_kernel(ts_ref, tab_ref, ye_ref, x_ref, g2_ref, o_ref, acc_ref, *, win, group):
    b, k = pl.program_id(0), pl.program_id(1)
    E, cap = ye_ref.shape[1], ye_ref.shape[2]
    tt = x_ref.shape[1]
    nt = pl.num_programs(1)
    col = lax.broadcasted_iota(jnp.int32, (tt, win), 1)
    lo = [ts_ref[(b * E + e) * (nt + 1) + k] for e in range(E)]
    hi = [ts_ref[(b * E + e) * (nt + 1) + k + 1] for e in range(E)]

    def contribution(w):
        total = None
        for g0 in range(0, E, group):
            lhs, rhs = [], []
            for e in range(g0, g0 + group):
                start, lower = _window(lo[e], w, win, cap)
                rk = tab_ref[0, :, E + e:E + e + 1].astype(jnp.int32)
                hit = jnp.where(rk >= lower, rk, -1) == col + start
                lhs.append(jnp.where(hit, tab_ref[0, :, e:e + 1], 0.0).astype(BF16))
                rhs.append(ye_ref[0, e, pl.ds(start, win), :])
            d = jnp.dot(jnp.concatenate(lhs, axis=1), jnp.concatenate(rhs, axis=0), preferred_element_type=F32)
            total = d if total is None else total + d
        return total

    acc_ref[...] = contribution(0)
    need = jnp.int32(0)
    for e in range(E):
        need = jnp.maximum(need, hi[e] - (lo[e] // ROW_ALIGN) * ROW_ALIGN)

    @pl.when(need > win)
    def _():
        def extra(w, carry):
            @pl.when(need > w * win)
            def _():
                acc_ref[...] += contribution(w)
            return carry

        lax.fori_loop(1, _extra_windows(cap, tt, win) + 1, extra, 0)

    o_ref[0] = x_ref[0] + g2_ref[0] * acc_ref[...]


def moe_combine(ts_flat, table, ye, x, g2, *, tt):
    B, N, D = x.shape
    E, cap = ye.shape[1], ye.shape[2]
    tok = lambda w: pl.BlockSpec((1, tt, w), lambda b, k, ts: (b, k, 0))
    return pl.pallas_call(
        functools.partial(_moe_combine_kernel, win=min(LANES, cap), group=4),
        grid_spec=pltpu.PrefetchScalarGridSpec(
            num_scalar_prefetch=1,
            grid=(B, N // tt),
            in_specs=[tok(LANES),
                      pl.BlockSpec((1, E, cap, D), lambda b, k, ts: (b, 0, 0, 0)),
                      tok(D),
                      pl.BlockSpec((1, 1, D), lambda b, k, ts: (b, 0, 0))],
            out_specs=tok(D),
            scratch_shapes=[pltpu.VMEM((tt, D), F32)]),
        out_shape=jax.ShapeDtypeStruct((B, N, D), F32),
        compiler_params=_cparams("parallel", "arbitrary"),
        name="moe_combine",
    )(ts_flat, table, ye, x, g2)


def moe_residual(streams, w_gate, w_up, w_down, li):
    routed = []
    for (x, h, logits), _ in streams:
        B, N, D = x.shape
        cap = EC_CAPACITY * N // N_EXPERTS
        tt = min(4 * LANES, N)
        table, rank, ts = route(logits, cap, tt)
        ts_flat = ts[:, :, :N // tt + 1].reshape(-1)
        routed.append((moe_gather(ts_flat, rank, h, cap, tt=tt), ts_flat, table, tt))
    yes = expert_ffn([r[0] for r in routed], w_gate, w_up, w_down, li)
    return [moe_combine(ts_flat, table, ye, x, gate2, tt=tt)
            for ((x, _, _), gate2), ye, (_, ts_flat, table, tt) in zip(streams, yes, routed)]


def lambda_init(layer):
    return 0.8 - 0.6 * math.exp(-0.3 * layer)


def even_layer(x, xc, mods, cmods, norm1_g, w_in, w_out, lb, hgrn_g, qn_g, kn_g, lam_vec, subln_g,
               lam_init, ctx_out, tables):
    sh1, sc1, g1, router = mods
    csh1, csc1, cg1, crouter = cmods
    B = x.shape[0]
    half = w_in[0].shape[2] // 2
    proj = norm_mod_matmul(x, norm1_g, sh1, sc1, w_in, cols=(0, half), tm=1024)
    proj2 = norm_mod_matmul(x, norm1_g, sh1, sc1, w_in, cols=(1, half), tm=1024, out_dtype=BF16)
    projc = norm_mod_matmul(xc, norm1_g, csh1, csc1, w_in, cols=(0, half))
    projc2 = norm_mod_matmul(xc, norm1_g, csh1, csc1, w_in, cols=(1, half), out_dtype=BF16)
    s0 = jnp.zeros((B, HGRN_HEADS, 2, HGRN_HEAD_DIM, HGRN_HEAD_DIM), F32)
    oc_f, oc_b, s_ctx = hgrn_scan(projc, lb, s0)
    o_f, o_b, _ = hgrn_scan(proj, lb, s_ctx)
    lv = lam_vec.astype(F32)
    lam = jnp.exp(jnp.sum(lv[0] * lv[1])) - jnp.exp(jnp.sum(lv[2] * lv[3])) + lam_init
    bound = 1.01 * math.sqrt(DIFF_HEAD_DIM) * jnp.max(jnp.abs(qn_g)) * jnp.max(jnp.abs(kn_g))
    lam = jnp.stack([lam, bound, (bound <= SCORE_BOUND_MAX).astype(F32)])
    cos, sin = tables
    q, k, v = qkv_prep(proj2, qn_g, kn_g, cos, sin, rotary=True)
    qc, kc, vc = qkv_prep(projc2, qn_g, kn_g, cos[:xc.shape[1]], sin[:xc.shape[1]], rotary=False)
    y_d = diff_attention(q, [(k, v), (kc, vc)], lam, subln_g, 1.0 - lam_init)
    x_new = even_out(x, o_f, o_b, proj2, y_d, hgrn_g.reshape(-1), g1, w_out, router)
    if not ctx_out:
        return x_new, None
    yc_d = diff_attention(qc, [(kc, vc)], lam, subln_g, 1.0 - lam_init)
    xc_new = even_out(xc, oc_f, oc_b, projc2, yc_d, hgrn_g.reshape(-1), cg1, w_out, crouter)
    return x_new, xc_new


def conv_layer(x, mods, norm1_g, w_in, conv_w, w_out):
    sh1, sc1, g1, router = mods
    proj = norm_mod_matmul(x, norm1_g, sh1, sc1, w_in, tm=1024, out_dtype=BF16)
    return conv_out(x, proj, conv_w, g1, w_out, router)


def kernel(x, c, ctx, c_ctx, mod_w, mod_b, norm1_g, norm2_g, even_w_in, even_w_out, hgrn_lb_logits, hgrn_norm_g,
           diff_qnorm_g, diff_knorm_g, diff_lambda, diff_subln_g, conv_w_in, conv_w, conv_w_out, router_w,
           exp_w_gate, exp_w_up, exp_w_down):
    depth = mod_w.shape[0]
    B, T, D = x.shape
    lb_soft = jax.nn.softmax(hgrn_lb_logits.astype(F32), axis=0)
    lower_bounds = jnp.cumsum(lb_soft, axis=0) - lb_soft[:1]
    last_ctx_layer = 2 * ((depth - 1) // 2)
    cond = jnp.concatenate([c, c_ctx[None, :], jnp.zeros((8 - (B + 1) % 8, D), F32)], axis=0)
    mods = modulation(jax.nn.silu(cond), mod_w, mod_b)
    tables = rope_tables(T)
    xc = ctx
    for l in range(depth):
        read_ctx = l <= last_ctx_layer
        ctx_out = l < last_ctx_layer
        sh1, sc1, g1, sh2, sc2, g2 = [m[:, None, :] for m in jnp.split(mods[l, :B], MOD_CHUNKS, axis=-1)]
        if read_ctx:
            csh1, csc1, cg1, csh2, csc2, cg2 = [
                jnp.broadcast_to(m[None, None, :], (B, 1, D)) for m in jnp.split(mods[l, B], MOD_CHUNKS, axis=-1)]
        mods_l = (sh1, sc1, g1, (norm2_g[l], sh2, sc2, router_w[l]))
        cmods_l = (csh1, csc1, cg1, (norm2_g[l], csh2, csc2, router_w[l])) if read_ctx else None
        if l % 2 == 0:
            e = l // 2
            s, sc = even_layer(x, xc, mods_l, cmods_l, norm1_g[l],
                               (even_w_in, e), (even_w_out, e), lower_bounds[e],
                               hgrn_norm_g[e], diff_qnorm_g[e], diff_knorm_g[e], diff_lambda[e],
                               diff_subln_g[e], lambda_init(l), ctx_out, tables)
        else:
            j = l // 2
            wi, wo = (conv_w_in, j), (conv_w_out, j)
            s = conv_layer(x, mods_l, norm1_g[l], wi, conv_w[j], wo)
            sc = conv_layer(xc, cmods_l, norm1_g[l], wi, conv_w[j], wo) if ctx_out else None
        outs = moe_residual([(s, g2)] + ([(sc, cg2)] if ctx_out else []), exp_w_gate, exp_w_up, exp_w_down, l)
        x = outs[0]
        if ctx_out:
            xc = outs[1]
    return x
```
